```python
import math
import jax, jax.numpy as jnp
from jax import lax
import numpy as np

D_MODEL = 1024
BATCH = 8
SEQ = 8192
DEPTH = 1

HEAD_DIM = 64
ATTN_WIDTH = D_MODEL // 2
CONV_WIDTH = D_MODEL - ATTN_WIDTH
N_ATTN_HEADS = ATTN_WIDTH // HEAD_DIM
CONV_GROUPS = CONV_WIDTH // HEAD_DIM
CONV_KERNEL = 31
D_FF = 4 * D_MODEL
Q_BLOCK = 128
LN_EPS = 1e-5
DEEPNORM_ALPHA = (2.0 * DEPTH) ** 0.25
DEEPNORM_BETA = (8.0 * DEPTH) ** -0.25
IN_SPLITS = [ATTN_WIDTH, 2 * ATTN_WIDTH, 3 * ATTN_WIDTH, 3 * ATTN_WIDTH + N_ATTN_HEADS,
             3 * ATTN_WIDTH + N_ATTN_HEADS + CONV_WIDTH]
N_IN_COLS = 3 * ATTN_WIDTH + N_ATTN_HEADS + 2 * CONV_WIDTH

kernel_name = "hymba_fox_conformer_deepnorm_adaln_block"


def _layernorm(x, g, b):
    xf = x.astype(jnp.float32)
    mu = jnp.mean(xf, axis=-1, keepdims=True)
    var = jnp.mean(jnp.square(xf - mu), axis=-1, keepdims=True)
    return ((xf - mu) * lax.rsqrt(var + LN_EPS)).astype(x.dtype) * g + b


def _rmsnorm(x, g):
    xf = x.astype(jnp.float32)
    return (xf * lax.rsqrt(jnp.mean(xf * xf, axis=-1, keepdims=True) + LN_EPS)).astype(x.dtype) * g


def _forgetting_attention(q, k, v, log_f):
    b, h, s, dh = q.shape
    n_blk = s // Q_BLOCK
    cum = jnp.cumsum(log_f, axis=-1)
    q_blocks = (q * (dh ** -0.5)).reshape(b, h, n_blk, Q_BLOCK, dh).transpose(2, 0, 1, 3, 4)
    cq_blocks = cum.reshape(b, h, n_blk, Q_BLOCK).transpose(2, 0, 1, 3)
    k_pos = jnp.arange(s)

    def one_block(args):
        qb, cqb, blk = args
        q_pos = blk * Q_BLOCK + jnp.arange(Q_BLOCK)
        logits = jnp.einsum('bhqd,bhkd->bhqk', qb, k).astype(jnp.float32)
        logits = logits + cqb[..., :, None] - cum[..., None, :]
        causal = k_pos[None, :] <= q_pos[:, None]
        logits = jnp.where(causal, logits, -jnp.inf)
        p = jax.nn.softmax(logits, axis=-1)
        return jnp.einsum('bhqk,bhkd->bhqd', p.astype(v.dtype), v)

    out = lax.map(one_block, (q_blocks, cq_blocks, jnp.arange(n_blk)))
    return out.transpose(1, 2, 0, 3, 4).reshape(b, h, s, dh)


def _conformer_conv(a, gate, w_dw, b_dw, gn_g, gn_b):
    u = a * jax.nn.sigmoid(gate)
    y = lax.conv_general_dilated(u, w_dw, window_strides=(1,), padding=((CONV_KERNEL - 1, 0),),
                                 dimension_numbers=('NWC', 'WIO', 'NWC'),
                                 feature_group_count=CONV_WIDTH) + b_dw
    bsz, seq, ch = y.shape
    yg = y.reshape(bsz, seq, CONV_GROUPS, ch // CONV_GROUPS).astype(jnp.float32)
    mu = jnp.mean(yg, axis=-1, keepdims=True)
    var = jnp.mean(jnp.square(yg - mu), axis=-1, keepdims=True)
    yn = ((yg - mu) * lax.rsqrt(var + LN_EPS)).reshape(bsz, seq, ch).astype(y.dtype) * gn_g + gn_b
    return jax.nn.silu(yn)


def _fwd_setup_inputs(seed: int = 0) -> dict:
    key = jax.random.key(seed)
    ks = jax.random.split(key, 24)
    nrm = jax.random.normal
    d = D_MODEL
    x = nrm(ks[0], (BATCH, SEQ, d), jnp.float32)
    c = nrm(ks[1], (BATCH, d), jnp.float32)
    w_ada = nrm(ks[2], (DEPTH, d, 6 * d), jnp.float32) * (0.1 * d ** -0.5)
    b_ada = nrm(ks[3], (DEPTH, 6 * d), jnp.float32) * 0.02
    col_scale = jnp.concatenate([
        jnp.ones((2 * ATTN_WIDTH,), jnp.float32),
        jnp.full((ATTN_WIDTH,), DEEPNORM_BETA, jnp.float32),
        jnp.ones((N_ATTN_HEADS,), jnp.float32),
        jnp.full((CONV_WIDTH,), DEEPNORM_BETA, jnp.float32),
        jnp.ones((CONV_WIDTH,), jnp.float32)])
    w_in = nrm(ks[4], (DEPTH, d, N_IN_COLS), jnp.float32) * (d ** -0.5) * col_scale
    b_forget = jax.random.uniform(ks[5], (DEPTH, N_ATTN_HEADS), jnp.float32, minval=2.0, maxval=6.0)
    w_dw = nrm(ks[6], (DEPTH, CONV_KERNEL, 1, CONV_WIDTH), jnp.float32) * (CONV_KERNEL ** -0.5)
    b_dw = nrm(ks[7], (DEPTH, CONV_WIDTH), jnp.float32) * 0.02
    gn_g = 1.0 + 0.02 * nrm(ks[8], (DEPTH, CONV_WIDTH), jnp.float32)
    gn_b = 0.02 * nrm(ks[9], (DEPTH, CONV_WIDTH), jnp.float32)
    g_attn_out = 1.0 + 0.02 * nrm(ks[10], (DEPTH, ATTN_WIDTH), jnp.float32)
    g_conv_out = 1.0 + 0.02 * nrm(ks[11], (DEPTH, CONV_WIDTH), jnp.float32)
    w_out = nrm(ks[12], (DEPTH, d, d), jnp.float32) * (d ** -0.5) * DEEPNORM_BETA
    ln1_g = 1.0 + 0.02 * nrm(ks[13], (DEPTH, d), jnp.float32)
    ln1_b = 0.02 * nrm(ks[14], (DEPTH, d), jnp.float32)
    w_ff1 = nrm(ks[15], (DEPTH, d, D_FF), jnp.float32) * (d ** -0.5) * DEEPNORM_BETA
    w_ff2 = nrm(ks[16], (DEPTH, D_FF, d), jnp.float32) * (D_FF ** -0.5) * DEEPNORM_BETA
    ln2_g = 1.0 + 0.02 * nrm(ks[17], (DEPTH, d), jnp.float32)
    ln2_b = 0.02 * nrm(ks[18], (DEPTH, d), jnp.float32)
    return {"x": x, "c": c, "w_ada": w_ada, "b_ada": b_ada, "w_in": w_in, "b_forget": b_forget,
            "w_dw": w_dw, "b_dw": b_dw, "gn_g": gn_g, "gn_b": gn_b, "g_attn_out": g_attn_out,
            "g_conv_out": g_conv_out, "w_out": w_out, "ln1_g": ln1_g, "ln1_b": ln1_b,
            "w_ff1": w_ff1, "w_ff2": w_ff2, "ln2_g": ln2_g, "ln2_b": ln2_b}


def _fwd_reference(x, c, w_ada, b_ada, w_in, b_forget, w_dw, b_dw, gn_g, gn_b, g_attn_out,
              g_conv_out, w_out, ln1_g, ln1_b, w_ff1, w_ff2, ln2_g, ln2_b):
    bsz, seq, _ = x.shape
    for layer in range(DEPTH):
        ada = jax.nn.silu(c) @ w_ada[layer] + b_ada[layer]
        sh1, sc1, gt1, sh2, sc2, gt2 = jnp.split(ada[:, None, :], 6, axis=-1)

        u = x * (1 + sc1) + sh1
        proj = u @ w_in[layer]
        q, k, v, f_logit, a, g = jnp.split(proj, IN_SPLITS, axis=-1)

        def heads(t):
            return t.reshape(bsz, seq, N_ATTN_HEADS, HEAD_DIM).transpose(0, 2, 1, 3)

        log_f = jax.nn.log_sigmoid((f_logit + b_forget[layer]).astype(jnp.float32)).transpose(0, 2, 1)
        attn = _forgetting_attention(heads(q), heads(k), heads(v), log_f)
        attn = attn.transpose(0, 2, 1, 3).reshape(bsz, seq, ATTN_WIDTH)
        conv = _conformer_conv(a, g, w_dw[layer], b_dw[layer], gn_g[layer], gn_b[layer])

        mixed = jnp.concatenate([_rmsnorm(attn, g_attn_out[layer]),
                                 _rmsnorm(conv, g_conv_out[layer])], axis=-1) @ w_out[layer]
        x = _layernorm(DEEPNORM_ALPHA * x + (1 + gt1) * mixed, ln1_g[layer], ln1_b[layer])

        u2 = x * (1 + sc2) + sh2
        hid = jnp.square(jax.nn.relu(u2 @ w_ff1[layer]))
        ff = hid @ w_ff2[layer]
        x = _layernorm(DEEPNORM_ALPHA * x + (1 + gt2) * ff, ln2_g[layer], ln2_b[layer])
    return x


import jax as _jax
import jax.numpy as _jnp

TWIN_FORMAT = 'train_step'
FWD_PARAMS = ['x', 'c', 'w_ada', 'b_ada', 'w_in', 'b_forget', 'w_dw', 'b_dw', 'gn_g', 'gn_b', 'g_attn_out', 'g_conv_out', 'w_out', 'ln1_g', 'ln1_b', 'w_ff1', 'w_ff2', 'ln2_g', 'ln2_b']
TWIN_WEIGHTS = ['w_ada', 'b_ada', 'w_in', 'b_forget', 'w_dw', 'b_dw', 'gn_g', 'gn_b', 'g_attn_out', 'g_conv_out', 'w_out', 'ln1_g', 'ln1_b', 'w_ff1', 'w_ff2', 'ln2_g', 'ln2_b']
TWIN_DIFF_INPUT = 'x'
TWIN_INPUTS = ['x', 'c', 'w_ada', 'b_ada', 'w_in', 'b_forget', 'w_dw', 'b_dw', 'gn_g', 'gn_b', 'g_attn_out', 'g_conv_out', 'w_out', 'ln1_g', 'ln1_b', 'w_ff1', 'w_ff2', 'ln2_g', 'ln2_b', 'loss_target', 'm_w_ada', 'm_b_ada', 'm_w_in', 'm_b_forget', 'm_w_dw', 'm_b_dw', 'm_gn_g', 'm_gn_b', 'm_g_attn_out', 'm_g_conv_out', 'm_w_out', 'm_ln1_g', 'm_ln1_b', 'm_w_ff1', 'm_w_ff2', 'm_ln2_g', 'm_ln2_b', 'v_w_ada', 'v_b_ada', 'v_w_in', 'v_b_forget', 'v_w_dw', 'v_b_dw', 'v_gn_g', 'v_gn_b', 'v_g_attn_out', 'v_g_conv_out', 'v_w_out', 'v_ln1_g', 'v_ln1_b', 'v_w_ff1', 'v_w_ff2', 'v_ln2_g', 'v_ln2_b']
TWIN_OUTPUTS = ['loss', 'grad_x', 'grad_w_ada', 'grad_b_ada', 'grad_w_in', 'grad_b_forget', 'grad_w_dw', 'grad_b_dw', 'grad_gn_g', 'grad_gn_b', 'grad_g_attn_out', 'grad_g_conv_out', 'grad_w_out', 'grad_ln1_g', 'grad_ln1_b', 'grad_w_ff1', 'grad_w_ff2', 'grad_ln2_g', 'grad_ln2_b', 'delta_w_ada', 'delta_b_ada', 'delta_w_in', 'delta_b_forget', 'delta_w_dw', 'delta_b_dw', 'delta_gn_g', 'delta_gn_b', 'delta_g_attn_out', 'delta_g_conv_out', 'delta_w_out', 'delta_ln1_g', 'delta_ln1_b', 'delta_w_ff1', 'delta_w_ff2', 'delta_ln2_g', 'delta_ln2_b', 'new_m_w_ada', 'new_m_b_ada', 'new_m_w_in', 'new_m_b_forget', 'new_m_w_dw', 'new_m_b_dw', 'new_m_gn_g', 'new_m_gn_b', 'new_m_g_attn_out', 'new_m_g_conv_out', 'new_m_w_out', 'new_m_ln1_g', 'new_m_ln1_b', 'new_m_w_ff1', 'new_m_w_ff2', 'new_m_ln2_g', 'new_m_ln2_b', 'new_v_w_ada', 'new_v_b_ada', 'new_v_w_in', 'new_v_b_forget', 'new_v_w_dw', 'new_v_b_dw', 'new_v_gn_g', 'new_v_gn_b', 'new_v_g_attn_out', 'new_v_g_conv_out', 'new_v_w_out', 'new_v_ln1_g', 'new_v_ln1_b', 'new_v_w_ff1', 'new_v_w_ff2', 'new_v_ln2_g', 'new_v_ln2_b']
TWIN_LEAF_KINDS = {'loss': 'loss', 'grad_x': 'grad_x', 'grad_w_ada': 'grad_w', 'grad_b_ada': 'grad_w', 'grad_w_in': 'grad_w', 'grad_b_forget': 'grad_w', 'grad_w_dw': 'grad_w', 'grad_b_dw': 'grad_w', 'grad_gn_g': 'grad_w', 'grad_gn_b': 'grad_w', 'grad_g_attn_out': 'grad_w', 'grad_g_conv_out': 'grad_w', 'grad_w_out': 'grad_w', 'grad_ln1_g': 'grad_w', 'grad_ln1_b': 'grad_w', 'grad_w_ff1': 'grad_w', 'grad_w_ff2': 'grad_w', 'grad_ln2_g': 'grad_w', 'grad_ln2_b': 'grad_w', 'delta_w_ada': 'delta_w', 'delta_b_ada': 'delta_w', 'delta_w_in': 'delta_w', 'delta_b_forget': 'delta_w', 'delta_w_dw': 'delta_w', 'delta_b_dw': 'delta_w', 'delta_gn_g': 'delta_w', 'delta_gn_b': 'delta_w', 'delta_g_attn_out': 'delta_w', 'delta_g_conv_out': 'delta_w', 'delta_w_out': 'delta_w', 'delta_ln1_g': 'delta_w', 'delta_ln1_b': 'delta_w', 'delta_w_ff1': 'delta_w', 'delta_w_ff2': 'delta_w', 'delta_ln2_g': 'delta_w', 'delta_ln2_b': 'delta_w', 'new_m_w_ada': 'new_m', 'new_m_b_ada': 'new_m', 'new_m_w_in': 'new_m', 'new_m_b_forget': 'new_m', 'new_m_w_dw': 'new_m', 'new_m_b_dw': 'new_m', 'new_m_gn_g': 'new_m', 'new_m_gn_b': 'new_m', 'new_m_g_attn_out': 'new_m', 'new_m_g_conv_out': 'new_m', 'new_m_w_out': 'new_m', 'new_m_ln1_g': 'new_m', 'new_m_ln1_b': 'new_m', 'new_m_w_ff1': 'new_m', 'new_m_w_ff2': 'new_m', 'new_m_ln2_g': 'new_m', 'new_m_ln2_b': 'new_m', 'new_v_w_ada': 'new_v', 'new_v_b_ada': 'new_v', 'new_v_w_in': 'new_v', 'new_v_b_forget': 'new_v', 'new_v_w_dw': 'new_v', 'new_v_b_dw': 'new_v', 'new_v_gn_g': 'new_v', 'new_v_gn_b': 'new_v', 'new_v_g_attn_out': 'new_v', 'new_v_g_conv_out': 'new_v', 'new_v_w_out': 'new_v', 'new_v_ln1_g': 'new_v', 'new_v_ln1_b': 'new_v', 'new_v_w_ff1': 'new_v', 'new_v_w_ff2': 'new_v', 'new_v_ln2_g': 'new_v', 'new_v_ln2_b': 'new_v'}


def _forward(args):
    return _fwd_reference(*[args[k] for k in FWD_PARAMS])


def _output_shape():
    def fwd():
        inp = _fwd_setup_inputs(0)
        return _fwd_reference(*[inp[k] for k in FWD_PARAMS])
    out = _jax.eval_shape(fwd)
    return out.shape, out.dtype

N_MICROBATCH = 1
ADAM_LR = 0.001
ADAM_B1 = 0.9
ADAM_B2 = 0.999
ADAM_EPS = 1e-08
ADAM_WD = 0.01
ADAM_STEP = 10
PER_EXAMPLE_BATCH_AXIS = {'x': 0, 'c': 0, 'loss_target': 0}
SHARED_INPUTS = []
_WEIGHT_DTYPES = {'w_ada': _jnp.float32, 'b_ada': _jnp.float32, 'w_in': _jnp.float32, 'b_forget': _jnp.float32, 'w_dw': _jnp.float32, 'b_dw': _jnp.float32, 'gn_g': _jnp.float32, 'gn_b': _jnp.float32, 'g_attn_out': _jnp.float32, 'g_conv_out': _jnp.float32, 'w_out': _jnp.float32, 'ln1_g': _jnp.float32, 'ln1_b': _jnp.float32, 'w_ff1': _jnp.float32, 'w_ff2': _jnp.float32, 'ln2_g': _jnp.float32, 'ln2_b': _jnp.float32}
MOMENT_SCALE = {'w_ada': 2.498126e-01, 'b_ada': 8.707381e-01, 'w_in': 1.463148e-01, 'b_forget': 1.218506e+00, 'w_dw': 1.233842e-01, 'b_dw': 1.012589e+00, 'gn_g': 2.664516e-01, 'gn_b': 3.792402e-01, 'g_attn_out': 1.238047e-01, 'g_conv_out': 1.751928e-01, 'w_out': 2.514379e-01, 'ln1_g': 2.126671e+00, 'ln1_b': 1.076399e+00, 'w_ff1': 5.375648e-02, 'w_ff2': 1.521081e-01, 'ln2_g': 6.412104e+01, 'ln2_b': 1.022952e+01}


def _to_microbatches(a, axis):
    t = _jnp.moveaxis(a, axis, 0)
    t = t.reshape((N_MICROBATCH, t.shape[0] // N_MICROBATCH) + t.shape[1:])
    return _jnp.moveaxis(t, 1, axis + 1)


def setup_inputs(seed: int = 0) -> dict:
    inp = _fwd_setup_inputs(seed)
    key = _jax.random.fold_in(_jax.random.key(seed), 7919)
    shape, _ = _output_shape()
    out = dict(inp)
    out["loss_target"] = _jax.random.normal(_jax.random.fold_in(key, 0), shape, _jnp.float32)
    for i, name in enumerate(TWIN_WEIGHTS):
        w = inp[name].astype(_jnp.float32)
        if MOMENT_SCALE is None:
            s = _jnp.sqrt(_jnp.mean(_jnp.square(w)) + 1e-30)
        else:
            s = MOMENT_SCALE[name]
        km, kv = _jax.random.split(_jax.random.fold_in(key, i + 1))
        out[name] = w
        out["m_" + name] = s * _jax.random.normal(km, w.shape, _jnp.float32)
        out["v_" + name] = (s * s) * _jax.random.uniform(kv, w.shape, _jnp.float32, 0.5, 1.5)
    if N_MICROBATCH > 1:
        for name, axis in PER_EXAMPLE_BATCH_AXIS.items():
            out[name] = _to_microbatches(out[name], axis)
    return {'x': out['x'], 'c': out['c'], 'w_ada': out['w_ada'], 'b_ada': out['b_ada'], 'w_in': out['w_in'], 'b_forget': out['b_forget'], 'w_dw': out['w_dw'], 'b_dw': out['b_dw'], 'gn_g': out['gn_g'], 'gn_b': out['gn_b'], 'g_attn_out': out['g_attn_out'], 'g_conv_out': out['g_conv_out'], 'w_out': out['w_out'], 'ln1_g': out['ln1_g'], 'ln1_b': out['ln1_b'], 'w_ff1': out['w_ff1'], 'w_ff2': out['w_ff2'], 'ln2_g': out['ln2_g'], 'ln2_b': out['ln2_b'], 'loss_target': out['loss_target'], 'm_w_ada': out['m_w_ada'], 'm_b_ada': out['m_b_ada'], 'm_w_in': out['m_w_in'], 'm_b_forget': out['m_b_forget'], 'm_w_dw': out['m_w_dw'], 'm_b_dw': out['m_b_dw'], 'm_gn_g': out['m_gn_g'], 'm_gn_b': out['m_gn_b'], 'm_g_attn_out': out['m_g_attn_out'], 'm_g_conv_out': out['m_g_conv_out'], 'm_w_out': out['m_w_out'], 'm_ln1_g': out['m_ln1_g'], 'm_ln1_b': out['m_ln1_b'], 'm_w_ff1': out['m_w_ff1'], 'm_w_ff2': out['m_w_ff2'], 'm_ln2_g': out['m_ln2_g'], 'm_ln2_b': out['m_ln2_b'], 'v_w_ada': out['v_w_ada'], 'v_b_ada': out['v_b_ada'], 'v_w_in': out['v_w_in'], 'v_b_forget': out['v_b_forget'], 'v_w_dw': out['v_w_dw'], 'v_b_dw': out['v_b_dw'], 'v_gn_g': out['v_gn_g'], 'v_gn_b': out['v_gn_b'], 'v_g_attn_out': out['v_g_attn_out'], 'v_g_conv_out': out['v_g_conv_out'], 'v_w_out': out['v_w_out'], 'v_ln1_g': out['v_ln1_g'], 'v_ln1_b': out['v_ln1_b'], 'v_w_ff1': out['v_w_ff1'], 'v_w_ff2': out['v_w_ff2'], 'v_ln2_g': out['v_ln2_g'], 'v_ln2_b': out['v_ln2_b']}


def _loss(weights, diff, rest, loss_target):
    with _jax.named_scope("forward"):
        args = {**rest, TWIN_DIFF_INPUT: diff, **{k: w.astype(_WEIGHT_DTYPES[k]) for k, w in weights.items()}}
        y = _forward(args)
    with _jax.named_scope("loss_head"):
        err = _jnp.square(y.astype(_jnp.float32) - loss_target)
        return 0.5 * _jnp.sum(_jnp.mean(err, axis=-1)) if err.ndim else 0.5 * err


def _adamw(w, g, m, v):
    m = ADAM_B1 * m + (1.0 - ADAM_B1) * g
    v = ADAM_B2 * v + (1.0 - ADAM_B2) * _jnp.square(g)
    m_hat = m / (1.0 - ADAM_B1 ** ADAM_STEP)
    v_hat = v / (1.0 - ADAM_B2 ** ADAM_STEP)
    delta = -ADAM_LR * (m_hat / (_jnp.sqrt(v_hat) + ADAM_EPS) + ADAM_WD * w)
    return delta, m, v


def reference(x, c, w_ada, b_ada, w_in, b_forget, w_dw, b_dw, gn_g, gn_b, g_attn_out, g_conv_out, w_out, ln1_g, ln1_b, w_ff1, w_ff2, ln2_g, ln2_b, loss_target, m_w_ada, m_b_ada, m_w_in, m_b_forget, m_w_dw, m_b_dw, m_gn_g, m_gn_b, m_g_attn_out, m_g_conv_out, m_w_out, m_ln1_g, m_ln1_b, m_w_ff1, m_w_ff2, m_ln2_g, m_ln2_b, v_w_ada, v_b_ada, v_w_in, v_b_forget, v_w_dw, v_b_dw, v_gn_g, v_gn_b, v_g_attn_out, v_g_conv_out, v_w_out, v_ln1_g, v_ln1_b, v_w_ff1, v_w_ff2, v_ln2_g, v_ln2_b):
    given = dict(x=x, c=c, w_ada=w_ada, b_ada=b_ada, w_in=w_in, b_forget=b_forget, w_dw=w_dw, b_dw=b_dw, gn_g=gn_g, gn_b=gn_b, g_attn_out=g_attn_out, g_conv_out=g_conv_out, w_out=w_out, ln1_g=ln1_g, ln1_b=ln1_b, w_ff1=w_ff1, w_ff2=w_ff2, ln2_g=ln2_g, ln2_b=ln2_b, loss_target=loss_target, m_w_ada=m_w_ada, m_b_ada=m_b_ada, m_w_in=m_w_in, m_b_forget=m_b_forget, m_w_dw=m_w_dw, m_b_dw=m_b_dw, m_gn_g=m_gn_g, m_gn_b=m_gn_b, m_g_attn_out=m_g_attn_out, m_g_conv_out=m_g_conv_out, m_w_out=m_w_out, m_ln1_g=m_ln1_g, m_ln1_b=m_ln1_b, m_w_ff1=m_w_ff1, m_w_ff2=m_w_ff2, m_ln2_g=m_ln2_g, m_ln2_b=m_ln2_b, v_w_ada=v_w_ada, v_b_ada=v_b_ada, v_w_in=v_w_in, v_b_forget=v_b_forget, v_w_dw=v_w_dw, v_b_dw=v_b_dw, v_gn_g=v_gn_g, v_gn_b=v_gn_b, v_g_attn_out=v_g_attn_out, v_g_conv_out=v_g_conv_out, v_w_out=v_w_out, v_ln1_g=v_ln1_g, v_ln1_b=v_ln1_b, v_w_ff1=v_w_ff1, v_w_ff2=v_w_ff2, v_ln2_g=v_ln2_g, v_ln2_b=v_ln2_b)
    weights = {n: given[n] for n in TWIN_WEIGHTS}
    shared = {n: given[n] for n in SHARED_INPUTS}
    per_example = {n: given[n] for n in ['x', 'c']}
    grad_fn = _jax.value_and_grad(_loss, argnums=(0, 1))

    def one_microbatch(ex, loss_target):
        ex = dict(ex)
        diff = ex.pop(TWIN_DIFF_INPUT)
        return grad_fn(weights, diff, {**shared, **ex}, loss_target)

    if N_MICROBATCH == 1:
        loss, (grad_w, grad_x) = one_microbatch(per_example, given["loss_target"])
    else:
        def body(carry, xs):
            loss_sum, grad_sum = carry
            l_k, (gw_k, gx_k) = one_microbatch(xs[0], xs[1])
            with _jax.named_scope("update"):
                return (loss_sum + l_k, _jax.tree.map(_jnp.add, grad_sum, gw_k)), gx_k

        init = (_jnp.zeros((), _jnp.float32), _jax.tree.map(_jnp.zeros_like, weights))
        (loss, grad_w), grad_x = _jax.lax.scan(body, init, (per_example, given["loss_target"]))
    with _jax.named_scope("update"):
        delta_w, new_m, new_v = {}, {}, {}
        for n in TWIN_WEIGHTS:
            delta_w[n], new_m[n], new_v[n] = _adamw(weights[n], grad_w[n], given["m_" + n], given["v_" + n])
    return (loss, grad_x, *[grad_w[n] for n in TWIN_WEIGHTS], *[delta_w[n] for n in TWIN_WEIGHTS],
            *[new_m[n] for n in TWIN_WEIGHTS], *[new_v[n] for n in TWIN_WEIGHTS])
```

```python
import jax
import jax.numpy as jnp
from jax import lax
from jax.experimental import pallas as pl
from jax.experimental.pallas import tpu as pltpu

F32 = jnp.float32
BF16 = jnp.bfloat16

D_MODEL = 1024
ATTN_W = 512
CONV_W = 512
N_HEADS = 8
CONV_K = 31
D_FF = 4096
N_DEV = 8
N_IN = 3 * ATTN_W + N_HEADS + 2 * CONV_W
IN_SHARD = N_IN // N_DEV
IN_SHARD_PAD = 384
ADA_SHARD = 6 * D_MODEL // N_DEV
FF_SHARD = D_FF // N_DEV
OUT_SHARD = D_MODEL // N_DEV
DW_SHARD = CONV_W // N_DEV
HALO = 32
LN_EPS = 1e-5
ALPHA = 2.0 ** 0.25
ADAM_LR, ADAM_B1, ADAM_B2, ADAM_EPS, ADAM_WD, ADAM_STEP = 0.001, 0.9, 0.999, 1e-08, 0.01, 10

ROW_TILE = 512
FF_TILE = 256
ATT_TILE = 512
VMEM_BIG = 56 * 1024 * 1024
VMEM_MID = 40 * 1024 * 1024

MESH = pl.DeviceIdType.MESH
VMEM_WHOLE = pl.BlockSpec(memory_space=pltpu.VMEM)
ANY_SPACE = pl.BlockSpec(memory_space=pl.ANY)


def _dot(a, b):
    return jnp.dot(a, b, preferred_element_type=F32)


def _dot_nt(a, b):
    return lax.dot_general(a, b, (((1,), (1,)), ((), ())), preferred_element_type=F32)


def _dot_tn(a, b):
    return lax.dot_general(a, b, (((0,), (0,)), ((), ())), preferred_element_type=F32)


def _dot3(xf, g_ref):
    g = g_ref[...]
    hi = xf.astype(BF16)
    r1 = xf - hi.astype(F32)
    mid = r1.astype(BF16)
    lo = (r1 - mid.astype(F32)).astype(BF16)
    return _dot(hi, g) + _dot(mid, g) + _dot(lo, g)


def _sigmoid(x):
    return 1.0 / (1.0 + jnp.exp(-x))


def _rowsum(x):
    return jnp.sum(x, axis=0, keepdims=True)


def _ln_norm(z):
    mu = jnp.mean(z, axis=1, keepdims=True)
    zc = z - mu
    var = jnp.mean(zc * zc, axis=1, keepdims=True)
    rstd = lax.rsqrt(var + LN_EPS)
    return zc * rstd, rstd


def _ln_bwd(dxh, xh, rstd):
    m1 = jnp.mean(dxh, axis=1, keepdims=True)
    m2 = jnp.mean(dxh * xh, axis=1, keepdims=True)
    return rstd * (dxh - m1 - xh * m2)


def _adamw(w, g, m, v):
    m2 = ADAM_B1 * m + (1.0 - ADAM_B1) * g
    v2 = ADAM_B2 * v + (1.0 - ADAM_B2) * (g * g)
    m_hat = m2 / (1.0 - ADAM_B1 ** ADAM_STEP)
    v_hat = v2 / (1.0 - ADAM_B2 ** ADAM_STEP)
    delta = -ADAM_LR * (m_hat / (jnp.sqrt(v_hat) + ADAM_EPS) + ADAM_WD * w)
    return delta, m2, v2


def _params(sem=None, vmem=None):
    kw = {}
    if sem is not None:
        kw["dimension_semantics"] = sem
    if vmem is not None:
        kw["vmem_limit_bytes"] = vmem
    return pltpu.CompilerParams(**kw)


def _coords():
    return lax.axis_index("x"), lax.axis_index("y"), lax.axis_index("c")


def _flip(v, bit):
    return 1 - v if bit else v


def _acc_rows(ref, r, val):
    ref[r:r + 1, :] = ref[r:r + 1, :] + val


def _gather_small(v, name):
    r, n = v.shape

    def body(v_ref, out_ref, send_sems, recv_sems, local_sem):
        x, y, c = _coords()
        me = 4 * x + 2 * y + c
        mine = pltpu.make_async_copy(v_ref, out_ref.at[me], local_sem)
        mine.start()
        sends = []
        for k in range(1, N_DEV):
            tgt = (_flip(x, (k >> 2) & 1), _flip(y, (k >> 1) & 1), _flip(c, k & 1))
            cp = pltpu.make_async_remote_copy(
                src_ref=v_ref, dst_ref=out_ref.at[me], send_sem=send_sems.at[k - 1],
                recv_sem=recv_sems.at[k - 1], device_id=tgt, device_id_type=MESH)
            cp.start()
            sends.append(cp)
        for k in range(1, N_DEV):
            px, py, pc = _flip(x, (k >> 2) & 1), _flip(y, (k >> 1) & 1), _flip(c, k & 1)
            peer = 4 * px + 2 * py + pc
            pltpu.make_async_remote_copy(
                src_ref=v_ref, dst_ref=out_ref.at[peer], send_sem=send_sems.at[k - 1],
                recv_sem=recv_sems.at[k - 1], device_id=(px, py, pc), device_id_type=MESH).wait_recv()
        for cp in sends:
            cp.wait_send()
        mine.wait()

    return pl.pallas_call(
        body, name=name,
        out_shape=jax.ShapeDtypeStruct((N_DEV, r, n), v.dtype),
        in_specs=[VMEM_WHOLE], out_specs=VMEM_WHOLE,
        scratch_shapes=[pltpu.SemaphoreType.DMA((N_DEV - 1,)), pltpu.SemaphoreType.DMA((N_DEV - 1,)),
                        pltpu.SemaphoreType.DMA],
    )(v)


def _gather_weights(shards):
    n = len(shards)

    def body(*refs):
        ins, outs = refs[:n], refs[n:2 * n]
        send_sems, recv_sems, local_sems = refs[2 * n:]
        x, y, c = _coords()
        me, sibling = (x, y, c), (x, y, 1 - c)
        chips = [(1 - x, y), (x, 1 - y), (1 - x, 1 - y)]

        def slot(a, pos):
            return outs[a].at[4 * pos[0] + 2 * pos[1] + pos[2]]

        def copy(a, k, block, to, src=None):
            return pltpu.make_async_remote_copy(
                src_ref=slot(a, block) if src is None else src, dst_ref=slot(a, block),
                send_sem=send_sems.at[7 * a + k], recv_sem=recv_sems.at[7 * a + k],
                device_id=to, device_id_type=MESH)

        started = []
        local = []
        for a in range(n):
            mine = pltpu.make_async_copy(ins[a], slot(a, me), local_sems.at[a])
            mine.start()
            local.append(mine)
        for j, chip in enumerate(chips):
            for a in range(n):
                cp = copy(a, 1 + j, me, (*chip, c), src=ins[a])
                cp.start()
                started.append(cp)
        for a in range(n):
            cp = copy(a, 0, me, sibling, src=ins[a])
            cp.start()
            started.append(cp)
        for j, chip in enumerate(chips):
            for a in range(n):
                copy(a, 1 + j, (*chip, c), me).wait_recv()
                cp = copy(a, 4 + j, (*chip, c), sibling)
                cp.start()
                started.append(cp)
        for a in range(n):
            copy(a, 0, sibling, me).wait_recv()
        for j, chip in enumerate(chips):
            for a in range(n):
                copy(a, 4 + j, (*chip, 1 - c), me).wait_recv()
        for cp in started:
            cp.wait_send()
        for mine in local:
            mine.wait()

    return pl.pallas_call(
        body, name="gather_weights",
        out_shape=[jax.ShapeDtypeStruct((N_DEV,) + s.shape, s.dtype) for s in shards],
        in_specs=[ANY_SPACE] * n, out_specs=[ANY_SPACE] * n,
        scratch_shapes=[pltpu.SemaphoreType.DMA((7 * n,)), pltpu.SemaphoreType.DMA((7 * n,)),
                        pltpu.SemaphoreType.DMA((n,))],
    )(*shards)


def _scatter_grads(grads):
    n = len(grads)

    def body(*refs):
        ins, outs = refs[:n], refs[n:2 * n]
        send_sems, recv_sems, local_sems = refs[2 * n:]
        x, y, c = _coords()
        me = 4 * x + 2 * y + c
        local = []
        for a in range(n):
            mine = pltpu.make_async_copy(ins[a].at[me], outs[a].at[me], local_sems.at[a])
            mine.start()
            local.append(mine)
        sends = []
        for k in range(1, N_DEV):
            px, py, pc = _flip(x, (k >> 2) & 1), _flip(y, (k >> 1) & 1), _flip(c, k & 1)
            peer = 4 * px + 2 * py + pc
            for a in range(n):
                cp = pltpu.make_async_remote_copy(
                    src_ref=ins[a].at[peer], dst_ref=outs[a].at[me],
                    send_sem=send_sems.at[7 * a + k - 1], recv_sem=recv_sems.at[7 * a + k - 1],
                    device_id=(px, py, pc), device_id_type=MESH)
                cp.start()
                sends.append(cp)
        for k in range(1, N_DEV):
            px, py, pc = _flip(x, (k >> 2) & 1), _flip(y, (k >> 1) & 1), _flip(c, k & 1)
            peer = 4 * px + 2 * py + pc
            for a in range(n):
                pltpu.make_async_remote_copy(
                    src_ref=ins[a].at[peer], dst_ref=outs[a].at[peer],
                    send_sem=send_sems.at[7 * a + k - 1], recv_sem=recv_sems.at[7 * a + k - 1],
                    device_id=(px, py, pc), device_id_type=MESH).wait_recv()
        for cp in sends:
            cp.wait_send()
        for mine in local:
            mine.wait()

    return pl.pallas_call(
        body, name="scatter_grads",
        out_shape=[jax.ShapeDtypeStruct(g.shape, g.dtype) for g in grads],
        in_specs=[ANY_SPACE] * n, out_specs=[ANY_SPACE] * n,
        scratch_shapes=[pltpu.SemaphoreType.DMA((7 * n,)), pltpu.SemaphoreType.DMA((7 * n,)),
                        pltpu.SemaphoreType.DMA((n,))],
    )(*grads)


def _ada_part(c_all, w_ada, b_shard):
    def body(c_ref, w_ref, b_ref, o_ref):
        cv = c_ref[...]
        sc = (cv * _sigmoid(cv)).astype(BF16)
        o_ref[...] = _dot(sc, w_ref[...].astype(BF16)) + b_ref[...]

    return pl.pallas_call(
        body, name="ada_part", out_shape=jax.ShapeDtypeStruct((N_DEV, ADA_SHARD), F32),
        in_specs=[VMEM_WHOLE] * 3, out_specs=VMEM_WHOLE,
        compiler_params=_params(vmem=VMEM_MID),
    )(c_all, w_ada, b_shard)


def _in_proj(x, mod, wqkv, wag, wft, tm):
    s = x.shape[0]

    def body(x_ref, mod_ref, wqkv_ref, wag_ref, wft_ref, qkv_ref, ag_ref, ft_ref, u_ref):
        u = x_ref[...] * (1.0 + mod_ref[1:2, :]) + mod_ref[0:1, :]
        ub = u.astype(BF16)
        u_ref[...] = ub
        qkv_ref[...] = _dot(ub, wqkv_ref[...]).astype(BF16)
        ag_ref[...] = _dot(ub, wag_ref[...])
        ft_ref[...] = _dot_nt(wft_ref[...], ub)

    return pl.pallas_call(
        body, name="in_proj", grid=(s // tm,),
        in_specs=[pl.BlockSpec((tm, D_MODEL), lambda i: (i, 0)), pl.BlockSpec((8, D_MODEL), lambda i: (0, 0)),
                  VMEM_WHOLE, VMEM_WHOLE, VMEM_WHOLE],
        out_specs=[pl.BlockSpec((tm, 3 * ATTN_W), lambda i: (i, 0)),
                   pl.BlockSpec((tm, 2 * CONV_W), lambda i: (i, 0)),
                   pl.BlockSpec((8, tm), lambda i: (0, i)),
                   pl.BlockSpec((tm, D_MODEL), lambda i: (i, 0))],
        out_shape=[jax.ShapeDtypeStruct((s, 3 * ATTN_W), BF16), jax.ShapeDtypeStruct((s, 2 * CONV_W), F32),
                   jax.ShapeDtypeStruct((8, s), F32), jax.ShapeDtypeStruct((s, D_MODEL), BF16)],
        compiler_params=_params(("parallel",), VMEM_MID),
    )(x, mod, wqkv, wag, wft)


def _fgate(ft, bcol):
    s = ft.shape[1]

    def body(f_ref, b_ref, cum_ref, sneg_ref):
        z = f_ref[...] + b_ref[...]
        e = jnp.exp(-jnp.abs(z))
        l1p = jnp.where(e < 1e-2, e * (1.0 - e * (0.5 - e * (1.0 / 3.0))), jnp.log(1.0 + e))
        logf = jnp.minimum(z, 0.0) - l1p
        r = 1.0 / (1.0 + e)
        sneg_ref[...] = jnp.where(z >= 0, e * r, r)
        lane = lax.broadcasted_iota(jnp.int32, (8, s), 1)
        acc = logf
        sh = 1
        while sh < s:
            acc = acc + jnp.where(lane >= sh, pltpu.roll(acc, sh, axis=1), 0.0)
            sh *= 2
        cum_ref[...] = acc

    return pl.pallas_call(
        body, name="fgate", out_shape=[jax.ShapeDtypeStruct((8, s), F32)] * 2,
        in_specs=[VMEM_WHOLE] * 2, out_specs=[VMEM_WHOLE] * 2,
    )(ft, bcol)


def _attn_fwd(qkv, cum_t, tb):
    s = qkv.shape[0]
    nq = s // tb

    def body(q_ref, k_ref, v_ref, cum_ref, o_ref, lse_ref):
        hp = pl.program_id(0)
        qi = pl.program_id(1)
        q2 = q_ref[...]
        lane = lax.broadcasted_iota(jnp.int32, (1, 128), 1)
        causal = (lax.broadcasted_iota(jnp.int32, (tb, tb), 1) <= lax.broadcasted_iota(jnp.int32, (tb, tb), 0))
        o_tot = jnp.zeros((tb, 128), F32)
        for hh in range(2):
            hmask = (lane >= 64) if hh else (lane < 64)
            qm = jnp.where(hmask, q2, jnp.zeros_like(q2)) * 0.125

            def step(kb, carry, masked, hh=hh, hmask=hmask, qm=qm):
                m, l, acc = carry
                k0 = pl.multiple_of(kb * tb, tb)
                k2 = k_ref[pl.ds(k0, tb), :]
                v2 = v_ref[pl.ds(k0, tb), :]
                cs = cum_ref[pl.ds(2 * hp + hh, 1), pl.ds(k0, tb)]
                sc = _dot_nt(qm, k2) - cs
                if masked:
                    sc = jnp.where(causal, sc, -jnp.inf)
                m_new = jnp.maximum(m, jnp.max(sc, axis=1, keepdims=True))
                p = jnp.exp(sc - m_new)
                corr = jnp.exp(m - m_new)
                l_new = l * corr + jnp.sum(p, axis=1, keepdims=True)
                vm = jnp.where(hmask, v2, jnp.zeros_like(v2))
                acc_new = acc * corr + _dot(p.astype(BF16), vm)
                return m_new, l_new, acc_new

            init = (jnp.full((tb, 1), -jnp.inf, F32), jnp.zeros((tb, 1), F32), jnp.zeros((tb, 128), F32))
            carry = lax.fori_loop(0, qi, lambda kb, cr: step(kb, cr, False), init)
            m, l, acc = step(qi, carry, True)
            o_tot = o_tot + acc / l
            lse_ref[:, hh:hh + 1] = m + jnp.log(l)
        o_ref[...] = o_tot

    return pl.pallas_call(
        body, name="attn_fwd", grid=(4, nq),
        in_specs=[pl.BlockSpec((tb, 128), lambda h, i: (i, h)),
                  pl.BlockSpec((s, 128), lambda h, i: (0, 4 + h)),
                  pl.BlockSpec((s, 128), lambda h, i: (0, 8 + h)),
                  pl.BlockSpec((8, s), lambda h, i: (0, 0))],
        out_specs=[pl.BlockSpec((tb, 128), lambda h, i: (i, h)),
                   pl.BlockSpec((None, tb, 2), lambda h, i: (h, i, 0))],
        out_shape=[jax.ShapeDtypeStruct((s, ATTN_W), F32), jax.ShapeDtypeStruct((4, s, 2), F32)],
        compiler_params=_params(("parallel", "parallel"), VMEM_MID),
    )(qkv, qkv, qkv, cum_t)


def _gn_fwd(y, vec_ref, g_ref):
    mean = _dot3(y, g_ref)
    yc = y - mean
    var = _dot3(yc * yc, g_ref)
    rstd = lax.rsqrt(var + LN_EPS)
    yh = yc * rstd
    yn = yh * vec_ref[1:2, :] + vec_ref[2:3, :]
    return yh, rstd, yn


def _conv_fwd(ag, wdw, vecs, gmat, tm):
    s = ag.shape[0]
    hb = tm // HALO

    def body(cur_ref, halo_ref, w_ref, vec_ref, g_ref, ypre_ref, conv_ref, buf):
        i = pl.program_id(0)
        ug = cur_ref[:, 0:CONV_W] * _sigmoid(cur_ref[:, CONV_W:2 * CONV_W])
        ugh = halo_ref[:, 0:CONV_W] * _sigmoid(halo_ref[:, CONV_W:2 * CONV_W])
        buf[0:HALO, :] = jnp.where(i > 0, ugh, 0.0)
        buf[HALO:HALO + tm, :] = ug
        y = jnp.broadcast_to(vec_ref[0:1, :], (tm, CONV_W))
        for j in range(CONV_K):
            y = y + w_ref[j:j + 1, :] * buf[pl.ds(HALO - (CONV_K - 1) + j, tm), :]
        ypre_ref[...] = y
        _, _, yn = _gn_fwd(y, vec_ref, g_ref)
        conv_ref[...] = yn * _sigmoid(yn)

    return pl.pallas_call(
        body, name="conv_fwd", grid=(s // tm,),
        in_specs=[pl.BlockSpec((tm, 2 * CONV_W), lambda i: (i, 0)),
                  pl.BlockSpec((HALO, 2 * CONV_W), lambda i: (jnp.maximum(i * hb - 1, 0), 0)),
                  pl.BlockSpec((32, CONV_W), lambda i: (0, 0)),
                  pl.BlockSpec((8, CONV_W), lambda i: (0, 0)),
                  pl.BlockSpec((CONV_W, CONV_W), lambda i: (0, 0))],
        out_specs=[pl.BlockSpec((tm, CONV_W), lambda i: (i, 0))] * 2,
        out_shape=[jax.ShapeDtypeStruct((s, CONV_W), F32)] * 2,
        scratch_shapes=[pltpu.VMEM((HALO + tm, CONV_W), F32)],
        compiler_params=_params(("parallel",), VMEM_MID),
    )(ag, ag, wdw, vecs, gmat)


def _mix_fwd(o, conv, x, w_out, gvec, mod, lnv, tm):
    s = x.shape[0]

    def body(o_ref, c_ref, x_ref, w_ref, g_ref, mod_ref, ln_ref, cat_ref, mixed_ref, z1_ref, x1_ref):
        ov = o_ref[...]
        cv = c_ref[...]
        ra = ov * lax.rsqrt(jnp.mean(ov * ov, axis=1, keepdims=True) + LN_EPS) * g_ref[0:1, :]
        rc = cv * lax.rsqrt(jnp.mean(cv * cv, axis=1, keepdims=True) + LN_EPS) * g_ref[1:2, :]
        rab = ra.astype(BF16)
        rcb = rc.astype(BF16)
        cat_ref[:, 0:ATTN_W] = rab
        cat_ref[:, ATTN_W:D_MODEL] = rcb
        mixed = _dot(rab, w_ref[0:ATTN_W, :]) + _dot(rcb, w_ref[ATTN_W:D_MODEL, :])
        mixed_ref[...] = mixed
        z1 = ALPHA * x_ref[...] + (1.0 + mod_ref[2:3, :]) * mixed
        z1_ref[...] = z1
        xh, _ = _ln_norm(z1)
        x1_ref[...] = xh * ln_ref[0:1, :] + ln_ref[1:2, :]

    row = lambda w: pl.BlockSpec((tm, w), lambda i: (i, 0))
    return pl.pallas_call(
        body, name="mix_fwd", grid=(s // tm,),
        in_specs=[row(ATTN_W), row(CONV_W), row(D_MODEL), VMEM_WHOLE,
                  pl.BlockSpec((8, ATTN_W), lambda i: (0, 0)), pl.BlockSpec((8, D_MODEL), lambda i: (0, 0)),
                  pl.BlockSpec((8, D_MODEL), lambda i: (0, 0))],
        out_specs=[row(D_MODEL)] * 4,
        out_shape=[jax.ShapeDtypeStruct((s, D_MODEL), BF16)] + [jax.ShapeDtypeStruct((s, D_MODEL), F32)] * 3,
        compiler_params=_params(("parallel",), VMEM_MID),
    )(o, conv, x, w_out, gvec, mod, lnv)


def _ff_fwd(x1, tgt, w1, w2, mod, lnv, tm):
    s = x1.shape[0]

    def body(x1_ref, t_ref, w1_ref, w2_ref, mod_ref, ln_ref, u2_ref, hid_ref, dff_ref, dxa_ref, st_ref):
        i = pl.program_id(0)
        x1v = x1_ref[...]
        u2 = (x1v * (1.0 + mod_ref[4:5, :]) + mod_ref[3:4, :]).astype(BF16)
        u2_ref[...] = u2
        ff = jnp.zeros((tm, D_MODEL), F32)
        for d in range(N_DEV):
            h = _dot(u2, w1_ref[d])
            hr = jnp.maximum(h, 0.0)
            hb = (hr * hr).astype(BF16)
            hid_ref[:, d * FF_SHARD:(d + 1) * FF_SHARD] = hb
            ff = ff + _dot(hb, w2_ref[d * FF_SHARD:(d + 1) * FF_SHARD, :])
        gate = 1.0 + mod_ref[5:6, :]
        z2 = ALPHA * x1v + gate * ff
        xh, rstd = _ln_norm(z2)
        y = xh * ln_ref[2:3, :] + ln_ref[3:4, :]
        diff = y - t_ref[...]
        dy = diff * (1.0 / D_MODEL)
        dz2 = _ln_bwd(dy * ln_ref[2:3, :], xh, rstd)
        dff_ref[...] = (gate * dz2).astype(BF16)
        dxa_ref[...] = ALPHA * dz2

        @pl.when(i == 0)
        def _():
            st_ref[...] = jnp.zeros_like(st_ref)

        _acc_rows(st_ref, 0, _rowsum(dy * xh))
        _acc_rows(st_ref, 1, _rowsum(dy))
        _acc_rows(st_ref, 2, _rowsum(dz2 * ff))
        sq = _rowsum(jnp.sum(diff * diff, axis=1, keepdims=True))
        _acc_rows(st_ref, 3, jnp.broadcast_to(sq * (0.5 / D_MODEL), (1, D_MODEL)))

    row = lambda w: pl.BlockSpec((tm, w), lambda i: (i, 0))
    vec = pl.BlockSpec((8, D_MODEL), lambda i: (0, 0))
    return pl.pallas_call(
        body, name="ff_fwd", grid=(s // tm,),
        in_specs=[row(D_MODEL), row(D_MODEL), VMEM_WHOLE, VMEM_WHOLE, vec, vec],
        out_specs=[row(D_MODEL), row(D_FF), row(D_MODEL), row(D_MODEL), vec],
        out_shape=[jax.ShapeDtypeStruct((s, D_MODEL), BF16), jax.ShapeDtypeStruct((s, D_FF), BF16),
                   jax.ShapeDtypeStruct((s, D_MODEL), BF16), jax.ShapeDtypeStruct((s, D_MODEL), F32),
                   jax.ShapeDtypeStruct((8, D_MODEL), F32)],
        compiler_params=_params(("arbitrary",), VMEM_BIG),
    )(x1, tgt, w1, w2, mod, lnv)


def _ff_bwd(dff, hid, w1, w2, dxa, x1, z1, mixed, mod, lnv, tm):
    s = x1.shape[0]

    def body(dff_ref, hid_ref, w1_ref, w2_ref, dxa_ref, x1_ref, z1_ref, mixed_ref, mod_ref, ln_ref,
             dh_ref, dxb_ref, dmix_ref, st_ref):
        i = pl.program_id(0)
        dffv = dff_ref[...]
        du2 = jnp.zeros((tm, D_MODEL), F32)
        for d in range(N_DEV):
            cols = slice(d * FF_SHARD, (d + 1) * FF_SHARD)
            dhid = _dot_nt(dffv, w2_ref[cols, :])
            dh = (dhid * (2.0 * jnp.sqrt(hid_ref[:, cols].astype(F32)))).astype(BF16)
            dh_ref[:, cols] = dh
            du2 = du2 + _dot_nt(dh, w1_ref[d])
        x1v = x1_ref[...]
        dx1 = dxa_ref[...] + du2 * (1.0 + mod_ref[4:5, :])
        xh, rstd = _ln_norm(z1_ref[...])
        dz1 = _ln_bwd(dx1 * ln_ref[0:1, :], xh, rstd)
        dxb_ref[...] = ALPHA * dz1
        dmix_ref[...] = ((1.0 + mod_ref[2:3, :]) * dz1).astype(BF16)

        @pl.when(i == 0)
        def _():
            st_ref[...] = jnp.zeros_like(st_ref)

        _acc_rows(st_ref, 0, _rowsum(du2 * x1v))
        _acc_rows(st_ref, 1, _rowsum(du2))
        _acc_rows(st_ref, 2, _rowsum(dx1 * xh))
        _acc_rows(st_ref, 3, _rowsum(dx1))
        _acc_rows(st_ref, 4, _rowsum(dz1 * mixed_ref[...]))

    row = lambda w: pl.BlockSpec((tm, w), lambda i: (i, 0))
    vec = pl.BlockSpec((8, D_MODEL), lambda i: (0, 0))
    return pl.pallas_call(
        body, name="ff_bwd", grid=(s // tm,),
        in_specs=[row(D_MODEL), row(D_FF), VMEM_WHOLE, VMEM_WHOLE, row(D_MODEL), row(D_MODEL), row(D_MODEL),
                  row(D_MODEL), vec, vec],
        out_specs=[row(D_FF), row(D_MODEL), row(D_MODEL), vec],
        out_shape=[jax.ShapeDtypeStruct((s, D_FF), BF16), jax.ShapeDtypeStruct((s, D_MODEL), F32),
                   jax.ShapeDtypeStruct((s, D_MODEL), BF16), jax.ShapeDtypeStruct((8, D_MODEL), F32)],
        compiler_params=_params(("arbitrary",), VMEM_BIG),
    )(dff, hid, w1, w2, dxa, x1, z1, mixed, mod, lnv)


def _mix_bwd(dmix, w_out, o, conv, gvec, tm):
    s = o.shape[0]

    def body(dm_ref, w_ref, o_ref, c_ref, g_ref, do_ref, dc_ref, st_ref):
        i = pl.program_id(0)
        dmv = dm_ref[...]

        @pl.when(i == 0)
        def _():
            st_ref[...] = jnp.zeros_like(st_ref)

        for part, (src, dst) in enumerate(((o_ref, do_ref), (c_ref, dc_ref))):
            dr = _dot_nt(dmv, w_ref[part * ATTN_W:(part + 1) * ATTN_W, :])
            v = src[...]
            rr = lax.rsqrt(jnp.mean(v * v, axis=1, keepdims=True) + LN_EPS)
            vh = v * rr
            _acc_rows(st_ref, part, _rowsum(dr * vh))
            t = dr * g_ref[part:part + 1, :]
            dst[...] = rr * (t - vh * jnp.mean(t * vh, axis=1, keepdims=True))

    row = lambda w: pl.BlockSpec((tm, w), lambda i: (i, 0))
    vec = pl.BlockSpec((8, ATTN_W), lambda i: (0, 0))
    return pl.pallas_call(
        body, name="mix_bwd", grid=(s // tm,),
        in_specs=[row(D_MODEL), VMEM_WHOLE, row(ATTN_W), row(CONV_W), vec],
        out_specs=[row(ATTN_W), row(CONV_W), vec],
        out_shape=[jax.ShapeDtypeStruct((s, ATTN_W), F32), jax.ShapeDtypeStruct((s, CONV_W), F32),
                   jax.ShapeDtypeStruct((8, ATTN_W), F32)],
        compiler_params=_params(("arbitrary",), VMEM_MID),
    )(dmix, w_out, o, conv, gvec)


def _conv_bwd(dconv, ypre, ag, wdw, vecs, gmat, tm):
    s = ag.shape[0]
    hb = tm // HALO
    nt = s // tm
    last_halo = s // HALO - 1

    def body(dc_ref, dch_ref, yp_ref, yph_ref, cur_ref, halo_ref, w_ref, vec_ref, g_ref,
             dag_ref, st_ref, dw_ref, bufu, bufd):
        i = pl.program_id(0)

        def dyc_of(yp, dc):
            yh, rstd, yn = _gn_fwd(yp, vec_ref, g_ref)
            sg = _sigmoid(yn)
            dyn = dc * (sg * (1.0 + yn * (1.0 - sg)))
            dyh = dyn * vec_ref[1:2, :]
            dyc = rstd * (dyh - _dot3(dyh, g_ref) - yh * _dot3(dyh * yh, g_ref))
            return dyc, dyn, yh

        dyc, dyn, yh = dyc_of(yp_ref[...], dc_ref[...])
        dych, _, _ = dyc_of(yph_ref[...], dch_ref[...])
        bufd[0:tm, :] = dyc
        bufd[tm:tm + HALO, :] = jnp.where(i < nt - 1, dych, 0.0)

        av = cur_ref[:, 0:CONV_W]
        sg_g = _sigmoid(cur_ref[:, CONV_W:2 * CONV_W])
        ugh = halo_ref[:, 0:CONV_W] * _sigmoid(halo_ref[:, CONV_W:2 * CONV_W])
        bufu[0:HALO, :] = jnp.where(i > 0, ugh, 0.0)
        bufu[HALO:HALO + tm, :] = av * sg_g

        @pl.when(i == 0)
        def _():
            st_ref[...] = jnp.zeros_like(st_ref)
            dw_ref[...] = jnp.zeros_like(dw_ref)

        _acc_rows(st_ref, 0, _rowsum(dyc))
        _acc_rows(st_ref, 1, _rowsum(dyn * yh))
        _acc_rows(st_ref, 2, _rowsum(dyn))

        dug = jnp.zeros((tm, CONV_W), F32)
        for j in range(CONV_K):
            dug = dug + w_ref[j:j + 1, :] * bufd[pl.ds(CONV_K - 1 - j, tm), :]
            _acc_rows(dw_ref, j, _rowsum(dyc * bufu[pl.ds(HALO - (CONV_K - 1) + j, tm), :]))
        dag_ref[:, 0:CONV_W] = (dug * sg_g).astype(BF16)
        dag_ref[:, CONV_W:2 * CONV_W] = (dug * av * sg_g * (1.0 - sg_g)).astype(BF16)

    row = lambda w: pl.BlockSpec((tm, w), lambda i: (i, 0))
    nxt = lambda w: pl.BlockSpec((HALO, w), lambda i: (jnp.minimum((i + 1) * hb, last_halo), 0))
    return pl.pallas_call(
        body, name="conv_bwd", grid=(nt,),
        in_specs=[row(CONV_W), nxt(CONV_W), row(CONV_W), nxt(CONV_W), row(2 * CONV_W),
                  pl.BlockSpec((HALO, 2 * CONV_W), lambda i: (jnp.maximum(i * hb - 1, 0), 0)),
                  pl.BlockSpec((32, CONV_W), lambda i: (0, 0)),
                  pl.BlockSpec((8, CONV_W), lambda i: (0, 0)),
                  pl.BlockSpec((CONV_W, CONV_W), lambda i: (0, 0))],
        out_specs=[row(2 * CONV_W), pl.BlockSpec((8, CONV_W), lambda i: (0, 0)),
                   pl.BlockSpec((32, CONV_W), lambda i: (0, 0))],
        out_shape=[jax.ShapeDtypeStruct((s, 2 * CONV_W), BF16), jax.ShapeDtypeStruct((8, CONV_W), F32),
                   jax.ShapeDtypeStruct((32, CONV_W), F32)],
        scratch_shapes=[pltpu.VMEM((HALO + tm, CONV_W), F32), pltpu.VMEM((tm + HALO, CONV_W), F32)],
        compiler_params=_params(("arbitrary",), VMEM_MID),
    )(dconv, dconv, ypre, ypre, ag, ag, wdw, vecs, gmat)


def _attn_bwd(qkv, cum_t, o, do, lse, tb):
    s = qkv.shape[0]
    nq = s // tb

    def body(q_ref, k_ref, v_ref, cum_ref, o_ref, do_ref, lse_ref, dq_ref, dk_ref, dv_ref, dcum_ref, drow_ref):
        hp = pl.program_id(0)
        qi = pl.program_id(1)

        @pl.when(qi == 0)
        def _():
            dk_ref[...] = jnp.zeros_like(dk_ref)
            dv_ref[...] = jnp.zeros_like(dv_ref)
            dcum_ref[...] = jnp.zeros_like(dcum_ref)

        q2 = q_ref[...]
        ov = o_ref[...]
        dov = do_ref[...]
        dob = dov.astype(BF16)
        lane = lax.broadcasted_iota(jnp.int32, (1, 128), 1)
        causal = (lax.broadcasted_iota(jnp.int32, (tb, tb), 1) <= lax.broadcasted_iota(jnp.int32, (tb, tb), 0))
        dq_tot = jnp.zeros((tb, 128), F32)
        for hh in range(2):
            hmask = (lane >= 64) if hh else (lane < 64)
            qm = jnp.where(hmask, q2, jnp.zeros_like(q2)) * 0.125
            dom = jnp.where(hmask, dob, jnp.zeros_like(dob))
            delta = jnp.sum(dom.astype(F32) * ov, axis=1, keepdims=True)
            lse_h = lse_ref[:, hh:hh + 1]

            def step(kb, carry, masked, hh=hh, hmask=hmask, qm=qm, dom=dom, delta=delta, lse_h=lse_h):
                dq, rs = carry
                k0 = pl.multiple_of(kb * tb, tb)
                k2 = k_ref[pl.ds(k0, tb), :]
                v2 = v_ref[pl.ds(k0, tb), :]
                cs = cum_ref[pl.ds(2 * hp + hh, 1), pl.ds(k0, tb)]
                sc = _dot_nt(qm, k2) - cs
                p = jnp.exp(sc - lse_h)
                if masked:
                    p = jnp.where(causal, p, 0.0)
                dp = _dot_nt(dom, v2)
                ds = p * (dp - delta)
                dcum_ref[pl.ds(hh, 1), pl.ds(k0, tb)] = dcum_ref[pl.ds(hh, 1), pl.ds(k0, tb)] - _rowsum(ds)
                pb = p.astype(BF16)
                dsb = ds.astype(BF16)
                dv_ref[pl.ds(k0, tb), :] = dv_ref[pl.ds(k0, tb), :] + _dot_tn(pb, dom)
                dk_ref[pl.ds(k0, tb), :] = dk_ref[pl.ds(k0, tb), :] + _dot_tn(dsb, qm)
                km = jnp.where(hmask, k2, jnp.zeros_like(k2))
                return dq + _dot(dsb, km), rs + jnp.sum(ds, axis=1, keepdims=True)

            init = (jnp.zeros((tb, 128), F32), jnp.zeros((tb, 1), F32))
            carry = lax.fori_loop(0, qi, lambda kb, a: step(kb, a, False), init)
            dq, rs = step(qi, carry, True)
            dq_tot = dq_tot + dq
            drow_ref[:, hh:hh + 1] = rs
        dq_ref[...] = (dq_tot * 0.125).astype(BF16)

    return pl.pallas_call(
        body, name="attn_bwd", grid=(4, nq),
        in_specs=[pl.BlockSpec((tb, 128), lambda h, i: (i, h)),
                  pl.BlockSpec((s, 128), lambda h, i: (0, 4 + h)),
                  pl.BlockSpec((s, 128), lambda h, i: (0, 8 + h)),
                  pl.BlockSpec((8, s), lambda h, i: (0, 0)),
                  pl.BlockSpec((tb, 128), lambda h, i: (i, h)),
                  pl.BlockSpec((tb, 128), lambda h, i: (i, h)),
                  pl.BlockSpec((None, tb, 2), lambda h, i: (h, i, 0))],
        out_specs=[pl.BlockSpec((tb, 128), lambda h, i: (i, h)),
                   pl.BlockSpec((s, 128), lambda h, i: (0, h)),
                   pl.BlockSpec((s, 128), lambda h, i: (0, h)),
                   pl.BlockSpec((None, 8, s), lambda h, i: (h, 0, 0)),
                   pl.BlockSpec((None, tb, 2), lambda h, i: (h, i, 0))],
        out_shape=[jax.ShapeDtypeStruct((s, ATTN_W), BF16), jax.ShapeDtypeStruct((s, ATTN_W), F32),
                   jax.ShapeDtypeStruct((s, ATTN_W), F32), jax.ShapeDtypeStruct((4, 8, s), F32),
                   jax.ShapeDtypeStruct((4, s, 2), F32)],
        compiler_params=_params(("arbitrary", "arbitrary"), VMEM_BIG),
    )(qkv, qkv, qkv, cum_t, o, do, lse)


def _fgate_bwd(dcum_t, drow_t, sneg_t):
    s = dcum_t.shape[1]

    def body(dc_ref, dr_ref, sn_ref, df_ref, db_ref):
        lane = lax.broadcasted_iota(jnp.int32, (8, s), 1)
        acc = dc_ref[...] + dr_ref[...]
        sh = 1
        while sh < s:
            acc = acc + jnp.where(lane < s - sh, pltpu.roll(acc, s - sh, axis=1), 0.0)
            sh *= 2
        df = acc * sn_ref[...]
        df_ref[...] = df
        db_ref[...] = jnp.broadcast_to(jnp.sum(df, axis=1, keepdims=True), (8, 128))

    return pl.pallas_call(
        body, name="fgate_bwd",
        out_shape=[jax.ShapeDtypeStruct((8, s), F32), jax.ShapeDtypeStruct((8, 128), F32)],
        in_specs=[VMEM_WHOLE] * 3, out_specs=[VMEM_WHOLE] * 2,
    )(dcum_t, drow_t, sneg_t)


def _in_proj_bwd(dq, dk, dv, dag, dft, wqkv, wag, wft, x, dxb, mod, tm):
    s = x.shape[0]

    def body(dq_ref, dk_ref, dv_ref, dag_ref, dft_ref, wqkv_ref, wag_ref, wft_ref, x_ref, dxb_ref, mod_ref,
             dqkv_ref, dx_ref, st_ref):
        i = pl.program_id(0)
        dqb = dq_ref[...]
        dkb = dk_ref[...].astype(BF16)
        dvb = dv_ref[...].astype(BF16)
        dqkv_ref[:, 0:ATTN_W] = dqb
        dqkv_ref[:, ATTN_W:2 * ATTN_W] = dkb
        dqkv_ref[:, 2 * ATTN_W:3 * ATTN_W] = dvb
        du = _dot_nt(dqb, wqkv_ref[:, 0:ATTN_W])
        du = du + _dot_nt(dkb, wqkv_ref[:, ATTN_W:2 * ATTN_W])
        du = du + _dot_nt(dvb, wqkv_ref[:, 2 * ATTN_W:3 * ATTN_W])
        du = du + _dot_nt(dag_ref[...], wag_ref[...])
        du = du + _dot_tn(dft_ref[...].astype(BF16), wft_ref[...])
        xv = x_ref[...]
        dx_ref[...] = dxb_ref[...] + du * (1.0 + mod_ref[1:2, :])

        @pl.when(i == 0)
        def _():
            st_ref[...] = jnp.zeros_like(st_ref)

        _acc_rows(st_ref, 0, _rowsum(du * xv))
        _acc_rows(st_ref, 1, _rowsum(du))

    row = lambda w: pl.BlockSpec((tm, w), lambda i: (i, 0))
    vec = pl.BlockSpec((8, D_MODEL), lambda i: (0, 0))
    return pl.pallas_call(
        body, name="in_proj_bwd", grid=(s // tm,),
        in_specs=[row(ATTN_W), row(ATTN_W), row(ATTN_W), row(2 * CONV_W), pl.BlockSpec((8, tm), lambda i: (0, i)),
                  VMEM_WHOLE, VMEM_WHOLE, VMEM_WHOLE, row(D_MODEL), row(D_MODEL), vec],
        out_specs=[row(3 * ATTN_W), row(D_MODEL), vec],
        out_shape=[jax.ShapeDtypeStruct((s, 3 * ATTN_W), BF16), jax.ShapeDtypeStruct((s, D_MODEL), F32),
                   jax.ShapeDtypeStruct((8, D_MODEL), F32)],
        compiler_params=_params(("arbitrary",), VMEM_MID),
    )(dq, dk, dv, dag, dft, wqkv, wag, wft, x, dxb, mod)


def _wgrad_tn(a, b, name, tm, tn, tk, out_shape, out_spec):
    s, m = a.shape
    n = b.shape[1]
    nk = s // tk

    def body(a_ref, b_ref, o_ref, acc_ref):
        k = pl.program_id(2)

        @pl.when(k == 0)
        def _():
            acc_ref[...] = jnp.zeros_like(acc_ref)

        acc_ref[...] += _dot_tn(a_ref[...], b_ref[...])

        @pl.when(k == nk - 1)
        def _():
            o_ref[...] = acc_ref[...].astype(o_ref.dtype)

    return pl.pallas_call(
        body, name=name, grid=(m // tm, n // tn, nk),
        in_specs=[pl.BlockSpec((tk, tm), lambda i, j, k: (k, i)), pl.BlockSpec((tk, tn), lambda i, j, k: (k, j))],
        out_specs=out_spec, out_shape=out_shape,
        scratch_shapes=[pltpu.VMEM((tm, tn), F32)],
        compiler_params=_params(("parallel", "parallel", "arbitrary"), VMEM_MID),
    )(a, b)


def _wgrad_f(dft, u, tk):
    s = u.shape[0]
    nk = s // tk

    def body(d_ref, u_ref, o_ref, acc_ref):
        k = pl.program_id(0)

        @pl.when(k == 0)
        def _():
            acc_ref[...] = jnp.zeros_like(acc_ref)

        acc_ref[...] += _dot(d_ref[...].astype(BF16), u_ref[...])

        @pl.when(k == nk - 1)
        def _():
            o_ref[...] = acc_ref[...].astype(BF16)

    return pl.pallas_call(
        body, name="wgrad_f", grid=(nk,),
        in_specs=[pl.BlockSpec((8, tk), lambda k: (0, k)), pl.BlockSpec((tk, D_MODEL), lambda k: (k, 0))],
        out_specs=pl.BlockSpec((8, D_MODEL), lambda k: (0, 0)),
        out_shape=jax.ShapeDtypeStruct((8, D_MODEL), BF16),
        scratch_shapes=[pltpu.VMEM((8, D_MODEL), F32)],
        compiler_params=_params(("arbitrary",)),
    )(dft, u)


def _adamw_big(recv, w, m, v, name, tr):
    r, cdim = w.shape

    def body(g_ref, w_ref, m_ref, v_ref, go_ref, d_ref, mo_ref, vo_ref):
        g = g_ref[0].astype(F32)
        for k in range(1, N_DEV):
            g = g + g_ref[k].astype(F32)
        delta, m2, v2 = _adamw(w_ref[...], g, m_ref[...], v_ref[...])
        go_ref[...] = g
        d_ref[...] = delta
        mo_ref[...] = m2
        vo_ref[...] = v2

    blk = pl.BlockSpec((tr, cdim), lambda i: (i, 0))
    return pl.pallas_call(
        body, name=name, grid=(r // tr,),
        in_specs=[pl.BlockSpec((N_DEV, tr, cdim), lambda i: (0, i, 0)), blk, blk, blk],
        out_specs=[blk] * 4, out_shape=[jax.ShapeDtypeStruct((r, cdim), F32)] * 4,
        compiler_params=_params(("parallel",), VMEM_MID),
    )(recv, w, m, v)


def _adamw_ada(c_all, dada, w, m, v, tr):
    r, cdim = w.shape

    def body(c_ref, d_ref, w_ref, m_ref, v_ref, go_ref, dl_ref, mo_ref, vo_ref):
        cv = c_ref[...]
        sc = (cv * _sigmoid(cv)).astype(BF16)
        g = _dot_tn(sc, d_ref[...].astype(BF16))
        delta, m2, v2 = _adamw(w_ref[...], g, m_ref[...], v_ref[...])
        go_ref[...] = g
        dl_ref[...] = delta
        mo_ref[...] = m2
        vo_ref[...] = v2

    blk = pl.BlockSpec((tr, cdim), lambda i: (i, 0))
    return pl.pallas_call(
        body, name="adamw_ada", grid=(r // tr,),
        in_specs=[pl.BlockSpec((N_DEV, tr), lambda i: (0, i)), pl.BlockSpec((N_DEV, cdim), lambda i: (0, 0)),
                  blk, blk, blk],
        out_specs=[blk] * 4, out_shape=[jax.ShapeDtypeStruct((r, cdim), F32)] * 4,
        compiler_params=_params(("parallel",), VMEM_MID),
    )(c_all, dada, w, m, v)


def _adamw_small(parts, w, m, v):
    n = w.shape[1]

    def body(g_ref, w_ref, m_ref, v_ref, go_ref, d_ref, mo_ref, vo_ref):
        g = g_ref[0]
        for k in range(1, N_DEV):
            g = g + g_ref[k]
        delta, m2, v2 = _adamw(w_ref[...], g, m_ref[...], v_ref[...])
        go_ref[...] = g
        d_ref[...] = delta
        mo_ref[...] = m2
        vo_ref[...] = v2

    return pl.pallas_call(
        body, name="adamw_small", out_shape=[jax.ShapeDtypeStruct((1, n), F32)] * 4,
        in_specs=[VMEM_WHOLE] * 4, out_specs=[VMEM_WHOLE] * 4,
    )(parts, w, m, v)


def _pad_cols(a, n):
    return jnp.pad(a, ((0, 0), (0, n - a.shape[1])))


def _pad_rows(a, n):
    return jnp.pad(a, ((0, n - a.shape[0]), (0, 0)))


def kernel(x, c, w_ada, b_ada, w_in, b_forget, w_dw, b_dw, gn_g, gn_b, g_attn_out, g_conv_out, w_out, ln1_g, ln1_b, w_ff1, w_ff2, ln2_g, ln2_b, loss_target, m_w_ada, m_b_ada, m_w_in, m_b_forget, m_w_dw, m_b_dw, m_gn_g, m_gn_b, m_g_attn_out, m_g_conv_out, m_w_out, m_ln1_g, m_ln1_b, m_w_ff1, m_w_ff2, m_ln2_g, m_ln2_b, v_w_ada, v_b_ada, v_w_in, v_b_forget, v_w_dw, v_b_dw, v_gn_g, v_gn_b, v_g_attn_out, v_g_conv_out, v_w_out, v_ln1_g, v_ln1_b, v_w_ff1, v_w_ff2, v_ln2_g, v_ln2_b):
    s = x.shape[1]
    tm = min(ROW_TILE, s)
    tb = min(ATT_TILE, s)
    me = 4 * lax.axis_index("x") + 2 * lax.axis_index("y") + lax.axis_index("c")
    xs = x[0]
    tgt = loss_target[0]

    dw_n = CONV_K * DW_SHARD
    pack1 = jnp.concatenate([c, w_dw[0].reshape(1, dw_n), jnp.zeros((1, 2048 - dw_n), F32)], axis=1)
    g1 = _gather_small(pack1, "gather_c")[:, 0, :]
    c_all = g1[:, :D_MODEL]
    wdw_full = g1[:, D_MODEL:D_MODEL + dw_n].reshape(N_DEV, CONV_K, DW_SHARD)
    wdw_full = _pad_rows(wdw_full.transpose(1, 0, 2).reshape(CONV_K, CONV_W), 32)

    b_shard = lax.dynamic_slice(b_ada, (0, me * ADA_SHARD), (1, ADA_SHARD))
    ada_all = _gather_small(_ada_part(c_all, w_ada[0], b_shard), "gather_ada")
    ada = lax.dynamic_index_in_dim(ada_all, me, axis=1, keepdims=False).reshape(6, D_MODEL)
    mod = _pad_rows(ada, 8)

    gw_in, gw_out, gw_ff1, gw_ff2 = _gather_weights([
        _pad_cols(w_in[0], IN_SHARD_PAD).astype(BF16), w_out[0].astype(BF16),
        w_ff1[0].astype(BF16), w_ff2[0].astype(BF16)])
    w_in_full = gw_in[:, :, :IN_SHARD].transpose(1, 0, 2).reshape(D_MODEL, N_IN)
    wqkv = w_in_full[:, :3 * ATTN_W]
    wft = w_in_full[:, 3 * ATTN_W:3 * ATTN_W + N_HEADS].T
    wag = w_in_full[:, 3 * ATTN_W + N_HEADS:]
    w_out_full = gw_out.reshape(D_MODEL, D_MODEL)
    w2_full = gw_ff2.reshape(D_FF, D_MODEL)

    qkv, ag, ft, u = _in_proj(xs, mod, wqkv, wag, wft, tm)
    cum_t, sneg_t = _fgate(ft, b_forget.reshape(N_HEADS, 1))
    o, lse = _attn_fwd(qkv, cum_t, tb)
    cvec = _pad_rows(jnp.concatenate([b_dw, gn_g, gn_b], axis=0), 8)
    grp = jnp.arange(CONV_W) // 64
    gmat = jnp.where(grp[:, None] == grp[None, :], 1.0 / 64.0, 0.0).astype(BF16)
    ypre, conv = _conv_fwd(ag, wdw_full, cvec, gmat, tm)
    gvec = _pad_rows(jnp.concatenate([g_attn_out, g_conv_out], axis=0), 8)
    lnv = _pad_rows(jnp.concatenate([ln1_g, ln1_b, ln2_g, ln2_b], axis=0), 8)
    cat, mixed, z1, x1 = _mix_fwd(o, conv, xs, w_out_full, gvec, mod, lnv, tm)
    tf = min(FF_TILE, s)
    u2, hid, dff, dxa, st_ff = _ff_fwd(x1, tgt, gw_ff1, w2_full, mod, lnv, tf)

    dh, dxb, dmix, st_fb = _ff_bwd(dff, hid, gw_ff1, w2_full, dxa, x1, z1, mixed, mod, lnv, tf)
    do, dconv, st_mix = _mix_bwd(dmix, w_out_full, o, conv, gvec, tm)
    dag, st_conv, dwdw = _conv_bwd(dconv, ypre, ag, wdw_full, cvec, gmat, tm)
    dq, dk, dv, dcum, drow = _attn_bwd(qkv, cum_t, o, do, lse, tb)
    dft, dbf = _fgate_bwd(dcum[:, :2, :].reshape(N_HEADS, s), drow.transpose(0, 2, 1).reshape(N_HEADS, s), sneg_t)
    dqkv, grad_x, st_in = _in_proj_bwd(dq, dk, dv, dag, dft, wqkv, wag, wft, xs, dxb, mod, tm)

    tk = min(512, s)
    g_ff2 = _wgrad_tn(hid, dff, "wgrad_ff2", 512, 1024, tk,
                      jax.ShapeDtypeStruct((D_FF, D_MODEL), BF16),
                      pl.BlockSpec((512, 1024), lambda i, j, k: (i, j))).reshape(N_DEV, FF_SHARD, D_MODEL)
    g_ff1 = _wgrad_tn(u2, dh, "wgrad_ff1", 1024, 512, tk,
                      jax.ShapeDtypeStruct((N_DEV, D_MODEL, FF_SHARD), BF16),
                      pl.BlockSpec((None, 1024, 512), lambda i, j, k: (j, i, 0)))
    g_out = _wgrad_tn(cat, dmix, "wgrad_out", 512, 1024, tk,
                      jax.ShapeDtypeStruct((D_MODEL, D_MODEL), BF16),
                      pl.BlockSpec((512, 1024), lambda i, j, k: (i, j))).reshape(N_DEV, OUT_SHARD, D_MODEL)
    g_qkv = _wgrad_tn(u, dqkv, "wgrad_qkv", 1024, 512, tk,
                      jax.ShapeDtypeStruct((D_MODEL, 3 * ATTN_W), BF16),
                      pl.BlockSpec((1024, 512), lambda i, j, k: (i, j)))
    g_ag = _wgrad_tn(u, dag, "wgrad_ag", 1024, 512, tk,
                     jax.ShapeDtypeStruct((D_MODEL, 2 * CONV_W), BF16),
                     pl.BlockSpec((1024, 512), lambda i, j, k: (i, j)))
    g_ft = _wgrad_f(dft, u, tk)
    g_in = jnp.concatenate([g_qkv, g_ft.T, g_ag], axis=1)
    g_in = jnp.pad(g_in.reshape(D_MODEL, N_DEV, IN_SHARD).transpose(1, 0, 2),
                   ((0, 0), (0, 0), (0, IN_SHARD_PAD - IN_SHARD)))

    r_in, r_out, r_ff1, r_ff2 = _scatter_grads([g_in, g_out, g_ff1, g_ff2])

    dada = jnp.concatenate([st_in[1:2], st_in[0:1], st_fb[4:5], st_fb[1:2], st_fb[0:1], st_ff[2:3]], axis=1)
    loss_part = st_ff[3:4, 0:1]
    pack2 = jnp.concatenate([
        dada,
        _pad_cols(dbf[:, 0].reshape(1, N_HEADS), 128),
        dwdw[:CONV_K].reshape(1, CONV_K * CONV_W),
        st_conv[0:1], st_conv[1:2], st_conv[2:3], st_mix[0:1], st_mix[1:2],
        st_fb[2:3], st_fb[3:4], st_ff[0:1], st_ff[1:2],
        _pad_cols(loss_part, 128)], axis=1)
    g2 = _gather_small(pack2, "gather_small_grads")[:, 0, :]
    o_bf = 6 * D_MODEL
    o_dw = o_bf + 128
    o_v = o_dw + CONV_K * CONV_W
    o_ln = o_v + 5 * CONV_W
    o_loss = o_ln + 4 * D_MODEL
    dw_parts = lax.dynamic_slice_in_dim(
        g2[:, o_dw:o_v].reshape(N_DEV, CONV_K, N_DEV, DW_SHARD), me, 1, axis=2).reshape(N_DEV, dw_n)
    parts = jnp.concatenate([
        g2[:, :o_dw], _pad_cols(dw_parts, 2048), g2[:, o_v:o_loss + 128]], axis=1)[:, None, :]

    def pack_w(b_ada_, b_forget_, w_dw_, smalls):
        return jnp.concatenate([b_ada_, _pad_cols(b_forget_, 128), _pad_cols(w_dw_[0].reshape(1, dw_n), 2048)]
                               + smalls + [jnp.zeros((1, 128), F32)], axis=1)

    pw = pack_w(b_ada, b_forget, w_dw, [b_dw, gn_g, gn_b, g_attn_out, g_conv_out, ln1_g, ln1_b, ln2_g, ln2_b])
    pm = pack_w(m_b_ada, m_b_forget, m_w_dw, [m_b_dw, m_gn_g, m_gn_b, m_g_attn_out, m_g_conv_out,
                                              m_ln1_g, m_ln1_b, m_ln2_g, m_ln2_b])
    pv = pack_w(v_b_ada, v_b_forget, v_w_dw, [v_b_dw, v_gn_g, v_gn_b, v_g_attn_out, v_g_conv_out,
                                              v_ln1_g, v_ln1_b, v_ln2_g, v_ln2_b])
    small = _adamw_small(parts, pw, pm, pv)

    p_bf = 6 * D_MODEL
    p_dw = p_bf + 128
    p_v = p_dw + 2048
    p_ln = p_v + 5 * CONV_W
    p_loss = p_ln + 4 * D_MODEL

    def unpack(t):
        outs = {"b_ada": t[:, :p_bf], "b_forget": t[:, p_bf:p_bf + N_HEADS],
                "w_dw": t[:, p_dw:p_dw + dw_n].reshape(1, CONV_K, 1, DW_SHARD)}
        for k, nm in enumerate(["b_dw", "gn_g", "gn_b", "g_attn_out", "g_conv_out"]):
            outs[nm] = t[:, p_v + k * CONV_W:p_v + (k + 1) * CONV_W]
        for k, nm in enumerate(["ln1_g", "ln1_b", "ln2_g", "ln2_b"]):
            outs[nm] = t[:, p_ln + k * D_MODEL:p_ln + (k + 1) * D_MODEL]
        return outs

    sm = [unpack(t) for t in small]
    loss = small[0][0, p_loss]

    dada_all = g2[:, :6 * D_MODEL]
    dada_shard = lax.dynamic_slice_in_dim(dada_all, me * ADA_SHARD, ADA_SHARD, axis=1)
    big = {
        "w_ada": _adamw_ada(c_all, dada_shard, w_ada[0], m_w_ada[0], v_w_ada[0], 256),
        "w_in": [t[:, :IN_SHARD] for t in _adamw_big(
            r_in, _pad_cols(w_in[0], IN_SHARD_PAD), _pad_cols(m_w_in[0], IN_SHARD_PAD),
            _pad_cols(v_w_in[0], IN_SHARD_PAD), "adamw_in", 256)],
        "w_out": _adamw_big(r_out, w_out[0], m_w_out[0], v_w_out[0], "adamw_out", 128),
        "w_ff1": _adamw_big(r_ff1, w_ff1[0], m_w_ff1[0], v_w_ff1[0], "adamw_ff1", 256),
        "w_ff2": _adamw_big(r_ff2, w_ff2[0], m_w_ff2[0], v_w_ff2[0], "adamw_ff2", 256),
    }

    names = ["w_ada", "b_ada", "w_in", "b_forget", "w_dw", "b_dw", "gn_g", "gn_b", "g_attn_out", "g_conv_out",
             "w_out", "ln1_g", "ln1_b", "w_ff1", "w_ff2", "ln2_g", "ln2_b"]

    def leaf(kind, nm):
        if nm in big:
            return big[nm][kind][None]
        return sm[kind][nm]

    outs = [loss, grad_x[None]]
    for kind in range(4):
        outs += [leaf(kind, nm) for nm in names]
    return tuple(outs)
```

```python
import jax
import jax.numpy as jnp
from jax import lax
from jax.experimental import pallas as pl
from jax.experimental.pallas import tpu as pltpu

F32 = jnp.float32
BF16 = jnp.bfloat16

D_MODEL = 1024
ATTN_W = 512
CONV_W = 512
N_HEADS = 8
CONV_K = 31
D_FF = 4096
N_DEV = 8
N_IN = 3 * ATTN_W + N_HEADS + 2 * CONV_W
IN_SHARD = N_IN // N_DEV
IN_SHARD_PAD = 384
ADA_SHARD = 6 * D_MODEL // N_DEV
FF_SHARD = D_FF // N_DEV
OUT_SHARD = D_MODEL // N_DEV
DW_SHARD = CONV_W // N_DEV
HALO = 32
LN_EPS = 1e-5
ALPHA = 2.0 ** 0.25
ADAM_LR, ADAM_B1, ADAM_B2, ADAM_EPS, ADAM_WD, ADAM_STEP = 0.001, 0.9, 0.999, 1e-08, 0.01, 10

ROW_TILE = 512
FF_TILE = 256
ATT_TILE = 512
VMEM_BIG = 56 * 1024 * 1024
VMEM_MID = 40 * 1024 * 1024

MESH = pl.DeviceIdType.MESH
VMEM_WHOLE = pl.BlockSpec(memory_space=pltpu.VMEM)
ANY_SPACE = pl.BlockSpec(memory_space=pl.ANY)


def _dot(a, b):
    return jnp.dot(a, b, preferred_element_type=F32)


def _dot_nt(a, b):
    return lax.dot_general(a, b, (((1,), (1,)), ((), ())), preferred_element_type=F32)


def _dot_tn(a, b):
    return lax.dot_general(a, b, (((0,), (0,)), ((), ())), preferred_element_type=F32)


def _dot3(xf, g_ref):
    g = g_ref[...]
    hi = xf.astype(BF16)
    r1 = xf - hi.astype(F32)
    mid = r1.astype(BF16)
    lo = (r1 - mid.astype(F32)).astype(BF16)
    return _dot(hi, g) + _dot(mid, g) + _dot(lo, g)


def _sigmoid(x):
    return 1.0 / (1.0 + jnp.exp(-x))


def _rowsum(x):
    return jnp.sum(x, axis=0, keepdims=True)


def _ln_norm(z):
    mu = jnp.mean(z, axis=1, keepdims=True)
    zc = z - mu
    var = jnp.mean(zc * zc, axis=1, keepdims=True)
    rstd = lax.rsqrt(var + LN_EPS)
    return zc * rstd, rstd


def _ln_bwd(dxh, xh, rstd):
    m1 = jnp.mean(dxh, axis=1, keepdims=True)
    m2 = jnp.mean(dxh * xh, axis=1, keepdims=True)
    return rstd * (dxh - m1 - xh * m2)


def _adamw(w, g, m, v):
    m2 = ADAM_B1 * m + (1.0 - ADAM_B1) * g
    v2 = ADAM_B2 * v + (1.0 - ADAM_B2) * (g * g)
    m_hat = m2 / (1.0 - ADAM_B1 ** ADAM_STEP)
    v_hat = v2 / (1.0 - ADAM_B2 ** ADAM_STEP)
    delta = -ADAM_LR * (m_hat / (jnp.sqrt(v_hat) + ADAM_EPS) + ADAM_WD * w)
    return delta, m2, v2


def _params(sem=None, vmem=None):
    kw = {}
    if sem is not None:
        kw["dimension_semantics"] = sem
    if vmem is not None:
        kw["vmem_limit_bytes"] = vmem
    return pltpu.CompilerParams(**kw)


def _coords():
    return lax.axis_index("x"), lax.axis_index("y"), lax.axis_index("c")


def _flip(v, bit):
    return 1 - v if bit else v


def _acc_rows(ref, r, val):
    ref[r:r + 1, :] = ref[r:r + 1, :] + val


def _gather_small(v, name):
    r, n = v.shape

    def body(v_ref, out_ref, send_sems, recv_sems, local_sem):
        x, y, c = _coords()
        me = 4 * x + 2 * y + c
        mine = pltpu.make_async_copy(v_ref, out_ref.at[me], local_sem)
        mine.start()
        sends = []
        for k in range(1, N_DEV):
            tgt = (_flip(x, (k >> 2) & 1), _flip(y, (k >> 1) & 1), _flip(c, k & 1))
            cp = pltpu.make_async_remote_copy(
                src_ref=v_ref, dst_ref=out_ref.at[me], send_sem=send_sems.at[k - 1],
                recv_sem=recv_sems.at[k - 1], device_id=tgt, device_id_type=MESH)
            cp.start()
            sends.append(cp)
        for k in range(1, N_DEV):
            px, py, pc = _flip(x, (k >> 2) & 1), _flip(y, (k >> 1) & 1), _flip(c, k & 1)
            peer = 4 * px + 2 * py + pc
            pltpu.make_async_remote_copy(
                src_ref=v_ref, dst_ref=out_ref.at[peer], send_sem=send_sems.at[k - 1],
                recv_sem=recv_sems.at[k - 1], device_id=(px, py, pc), device_id_type=MESH).wait_recv()
        for cp in sends:
            cp.wait_send()
        mine.wait()

    return pl.pallas_call(
        body, name=name,
        out_shape=jax.ShapeDtypeStruct((N_DEV, r, n), v.dtype),
        in_specs=[VMEM_WHOLE], out_specs=VMEM_WHOLE,
        scratch_shapes=[pltpu.SemaphoreType.DMA((N_DEV - 1,)), pltpu.SemaphoreType.DMA((N_DEV - 1,)),
                        pltpu.SemaphoreType.DMA],
    )(v)


def _gather_weights(shards):
    n = len(shards)

    def body(*refs):
        ins, outs = refs[:n], refs[n:2 * n]
        send_sems, recv_sems, local_sems = refs[2 * n:]
        x, y, c = _coords()
        me, sibling = (x, y, c), (x, y, 1 - c)
        chips = [(1 - x, y), (x, 1 - y), (1 - x, 1 - y)]

        def slot(a, pos):
            return outs[a].at[4 * pos[0] + 2 * pos[1] + pos[2]]

        def copy(a, k, block, to, src=None):
            return pltpu.make_async_remote_copy(
                src_ref=slot(a, block) if src is None else src, dst_ref=slot(a, block),
                send_sem=send_sems.at[7 * a + k], recv_sem=recv_sems.at[7 * a + k],
                device_id=to, device_id_type=MESH)

        started = []
        local = []
        for a in range(n):
            mine = pltpu.make_async_copy(ins[a], slot(a, me), local_sems.at[a])
            mine.start()
            local.append(mine)
        for j, chip in enumerate(chips):
            for a in range(n):
                cp = copy(a, 1 + j, me, (*chip, c), src=ins[a])
                cp.start()
                started.append(cp)
        for a in range(n):
            cp = copy(a, 0, me, sibling, src=ins[a])
            cp.start()
            started.append(cp)
        for j, chip in enumerate(chips):
            for a in range(n):
                copy(a, 1 + j, (*chip, c), me).wait_recv()
                cp = copy(a, 4 + j, (*chip, c), sibling)
                cp.start()
                started.append(cp)
        for a in range(n):
            copy(a, 0, sibling, me).wait_recv()
        for j, chip in enumerate(chips):
            for a in range(n):
                copy(a, 4 + j, (*chip, 1 - c), me).wait_recv()
        for cp in started:
            cp.wait_send()
        for mine in local:
            mine.wait()

    return pl.pallas_call(
        body, name="gather_weights",
        out_shape=[jax.ShapeDtypeStruct((N_DEV,) + s.shape, s.dtype) for s in shards],
        in_specs=[ANY_SPACE] * n, out_specs=[ANY_SPACE] * n,
        scratch_shapes=[pltpu.SemaphoreType.DMA((7 * n,)), pltpu.SemaphoreType.DMA((7 * n,)),
                        pltpu.SemaphoreType.DMA((n,))],
    )(*shards)


def _scatter_grads(grads):
    n = len(grads)

    def body(*refs):
        ins, outs = refs[:n], refs[n:2 * n]
        send_sems, recv_sems, local_sems = refs[2 * n:]
        x, y, c = _coords()
        me = 4 * x + 2 * y + c
        local = []
        for a in range(n):
            mine = pltpu.make_async_copy(ins[a].at[me], outs[a].at[me], local_sems.at[a])
            mine.start()
            local.append(mine)
        sends = []
        for k in range(1, N_DEV):
            px, py, pc = _flip(x, (k >> 2) & 1), _flip(y, (k >> 1) & 1), _flip(c, k & 1)
            peer = 4 * px + 2 * py + pc
            for a in range(n):
                cp = pltpu.make_async_remote_copy(
                    src_ref=ins[a].at[peer], dst_ref=outs[a].at[me],
                    send_sem=send_sems.at[7 * a + k - 1], recv_sem=recv_sems.at[7 * a + k - 1],
                    device_id=(px, py, pc), device_id_type=MESH)
                cp.start()
                sends.append(cp)
        for k in range(1, N_DEV):
            px, py, pc = _flip(x, (k >> 2) & 1), _flip(y, (k >> 1) & 1), _flip(c, k & 1)
            peer = 4 * px + 2 * py + pc
            for a in range(n):
                pltpu.make_async_remote_copy(
                    src_ref=ins[a].at[peer], dst_ref=outs[a].at[peer],
                    send_sem=send_sems.at[7 * a + k - 1], recv_sem=recv_sems.at[7 * a + k - 1],
                    device_id=(px, py, pc), device_id_type=MESH).wait_recv()
        for cp in sends:
            cp.wait_send()
        for mine in local:
            mine.wait()

    return pl.pallas_call(
        body, name="scatter_grads",
        out_shape=[jax.ShapeDtypeStruct(g.shape, g.dtype) for g in grads],
        in_specs=[ANY_SPACE] * n, out_specs=[ANY_SPACE] * n,
        scratch_shapes=[pltpu.SemaphoreType.DMA((7 * n,)), pltpu.SemaphoreType.DMA((7 * n,)),
                        pltpu.SemaphoreType.DMA((n,))],
    )(*grads)


def _ada_part(c_all, w_ada, b_shard):
    def body(c_ref, w_ref, b_ref, o_ref):
        cv = c_ref[...]
        sc = (cv * _sigmoid(cv)).astype(BF16)
        o_ref[...] = _dot(sc, w_ref[...].astype(BF16)) + b_ref[...]

    return pl.pallas_call(
        body, name="ada_part", out_shape=jax.ShapeDtypeStruct((N_DEV, ADA_SHARD), F32),
        in_specs=[VMEM_WHOLE] * 3, out_specs=VMEM_WHOLE,
        compiler_params=_params(vmem=VMEM_MID),
    )(c_all, w_ada, b_shard)


def _in_proj(x, mod, wqkv, wag, wft, tm):
    s = x.shape[0]

    def body(x_ref, mod_ref, wqkv_ref, wag_ref, wft_ref, qkv_ref, ag_ref, ft_ref, u_ref):
        u = x_ref[...] * (1.0 + mod_ref[1:2, :]) + mod_ref[0:1, :]
        ub = u.astype(BF16)
        u_ref[...] = ub
        qkv_ref[...] = _dot(ub, wqkv_ref[...]).astype(BF16)
        ag_ref[...] = _dot(ub, wag_ref[...])
        ft_ref[...] = _dot_nt(wft_ref[...], ub)

    return pl.pallas_call(
        body, name="in_proj", grid=(s // tm,),
        in_specs=[pl.BlockSpec((tm, D_MODEL), lambda i: (i, 0)), pl.BlockSpec((8, D_MODEL), lambda i: (0, 0)),
                  VMEM_WHOLE, VMEM_WHOLE, VMEM_WHOLE],
        out_specs=[pl.BlockSpec((tm, 3 * ATTN_W), lambda i: (i, 0)),
                   pl.BlockSpec((tm, 2 * CONV_W), lambda i: (i, 0)),
                   pl.BlockSpec((8, tm), lambda i: (0, i)),
                   pl.BlockSpec((tm, D_MODEL), lambda i: (i, 0))],
        out_shape=[jax.ShapeDtypeStruct((s, 3 * ATTN_W), BF16), jax.ShapeDtypeStruct((s, 2 * CONV_W), F32),
                   jax.ShapeDtypeStruct((8, s), F32), jax.ShapeDtypeStruct((s, D_MODEL), BF16)],
        compiler_params=_params(("parallel",), VMEM_MID),
    )(x, mod, wqkv, wag, wft)


def _fgate(ft, bcol):
    s = ft.shape[1]

    def body(f_ref, b_ref, cum_ref, sneg_ref):
        z = f_ref[...] + b_ref[...]
        e = jnp.exp(-jnp.abs(z))
        l1p = jnp.where(e < 1e-2, e * (1.0 - e * (0.5 - e * (1.0 / 3.0))), jnp.log(1.0 + e))
        logf = jnp.minimum(z, 0.0) - l1p
        r = 1.0 / (1.0 + e)
        sneg_ref[...] = jnp.where(z >= 0, e * r, r)
        lane = lax.broadcasted_iota(jnp.int32, (8, s), 1)
        acc = logf
        sh = 1
        while sh < s:
            acc = acc + jnp.where(lane >= sh, pltpu.roll(acc, sh, axis=1), 0.0)
            sh *= 2
        cum_ref[...] = acc

    return pl.pallas_call(
        body, name="fgate", out_shape=[jax.ShapeDtypeStruct((8, s), F32)] * 2,
        in_specs=[VMEM_WHOLE] * 2, out_specs=[VMEM_WHOLE] * 2,
    )(ft, bcol)


def _attn_fwd(qkv, cum_t, tb):
    s = qkv.shape[0]
    nq = s // tb

    def body(q_ref, k_ref, v_ref, cum_ref, o_ref, lse_ref):
        hp = pl.program_id(0)
        qi = pl.program_id(1)
        q2 = q_ref[...]
        lane = lax.broadcasted_iota(jnp.int32, (1, 128), 1)
        causal = (lax.broadcasted_iota(jnp.int32, (tb, tb), 1) <= lax.broadcasted_iota(jnp.int32, (tb, tb), 0))
        hmask = (lane < 64, lane >= 64)
        aux = (64, 0)
        qm = [jnp.where(hmask[h], q2, jnp.zeros_like(q2)) * 0.125 for h in range(2)]

        def step(kb, carry, masked):
            k0 = pl.multiple_of(kb * tb, tb)
            k2 = k_ref[pl.ds(k0, tb), :]
            v2 = v_ref[pl.ds(k0, tb), :]
            out = []
            for h in range(2):
                m, acc = carry[2 * h], carry[2 * h + 1]
                cs = cum_ref[pl.ds(2 * hp + h, 1), pl.ds(k0, tb)]
                sc = _dot_nt(qm[h], k2) - cs
                if masked:
                    sc = jnp.where(causal, sc, -jnp.inf)
                m_new = jnp.maximum(m, jnp.max(sc, axis=1, keepdims=True))
                p = jnp.exp(sc - m_new)
                corr = jnp.exp(m - m_new)
                va = jnp.where(hmask[h], v2, jnp.where(lane == aux[h], 1.0, 0.0).astype(v2.dtype))
                out += [m_new, acc * corr + _dot(p.astype(BF16), va)]
            return tuple(out)

        init = (jnp.full((tb, 1), -jnp.inf, F32), jnp.zeros((tb, 128), F32)) * 2
        carry = lax.fori_loop(0, qi, lambda kb, cr: step(kb, cr, False), init)
        m0, acc0, m1, acc1 = step(qi, carry, True)
        l0 = acc0[:, aux[0]:aux[0] + 1]
        l1 = acc1[:, aux[1]:aux[1] + 1]
        o_ref[...] = jnp.where(hmask[0], acc0 / l0, acc1 / l1)
        lse_ref[:, 0:1] = m0 + jnp.log(l0)
        lse_ref[:, 1:2] = m1 + jnp.log(l1)

    return pl.pallas_call(
        body, name="attn_fwd", grid=(4, nq),
        in_specs=[pl.BlockSpec((tb, 128), lambda h, i: (i, h)),
                  pl.BlockSpec((s, 128), lambda h, i: (0, 4 + h)),
                  pl.BlockSpec((s, 128), lambda h, i: (0, 8 + h)),
                  pl.BlockSpec((8, s), lambda h, i: (0, 0))],
        out_specs=[pl.BlockSpec((tb, 128), lambda h, i: (i, h)),
                   pl.BlockSpec((None, tb, 2), lambda h, i: (h, i, 0))],
        out_shape=[jax.ShapeDtypeStruct((s, ATTN_W), F32), jax.ShapeDtypeStruct((4, s, 2), F32)],
        compiler_params=_params(("parallel", "parallel"), VMEM_MID),
    )(qkv, qkv, qkv, cum_t)


def _gn_fwd(y, vec_ref, g_ref):
    mean = _dot3(y, g_ref)
    yc = y - mean
    var = _dot3(yc * yc, g_ref)
    rstd = lax.rsqrt(var + LN_EPS)
    yh = yc * rstd
    yn = yh * vec_ref[1:2, :] + vec_ref[2:3, :]
    return yh, rstd, yn


def _conv_fwd(ag, wdw, vecs, gmat, tm):
    s = ag.shape[0]
    hb = tm // HALO

    def body(cur_ref, halo_ref, w_ref, vec_ref, g_ref, ypre_ref, conv_ref, buf):
        i = pl.program_id(0)
        ug = cur_ref[:, 0:CONV_W] * _sigmoid(cur_ref[:, CONV_W:2 * CONV_W])
        ugh = halo_ref[:, 0:CONV_W] * _sigmoid(halo_ref[:, CONV_W:2 * CONV_W])
        buf[0:HALO, :] = jnp.where(i > 0, ugh, 0.0)
        buf[HALO:HALO + tm, :] = ug
        y = jnp.broadcast_to(vec_ref[0:1, :], (tm, CONV_W))
        for j in range(CONV_K):
            y = y + w_ref[j:j + 1, :] * buf[pl.ds(HALO - (CONV_K - 1) + j, tm), :]
        ypre_ref[...] = y
        _, _, yn = _gn_fwd(y, vec_ref, g_ref)
        conv_ref[...] = yn * _sigmoid(yn)

    return pl.pallas_call(
        body, name="conv_fwd", grid=(s // tm,),
        in_specs=[pl.BlockSpec((tm, 2 * CONV_W), lambda i: (i, 0)),
                  pl.BlockSpec((HALO, 2 * CONV_W), lambda i: (jnp.maximum(i * hb - 1, 0), 0)),
                  pl.BlockSpec((32, CONV_W), lambda i: (0, 0)),
                  pl.BlockSpec((8, CONV_W), lambda i: (0, 0)),
                  pl.BlockSpec((CONV_W, CONV_W), lambda i: (0, 0))],
        out_specs=[pl.BlockSpec((tm, CONV_W), lambda i: (i, 0))] * 2,
        out_shape=[jax.ShapeDtypeStruct((s, CONV_W), F32)] * 2,
        scratch_shapes=[pltpu.VMEM((HALO + tm, CONV_W), F32)],
        compiler_params=_params(("parallel",), VMEM_MID),
    )(ag, ag, wdw, vecs, gmat)


def _mix_fwd(o, conv, x, w_out, gvec, mod, lnv, tm):
    s = x.shape[0]

    def body(o_ref, c_ref, x_ref, w_ref, g_ref, mod_ref, ln_ref, cat_ref, mixed_ref, z1_ref, x1_ref):
        ov = o_ref[...]
        cv = c_ref[...]
        ra = ov * lax.rsqrt(jnp.mean(ov * ov, axis=1, keepdims=True) + LN_EPS) * g_ref[0:1, :]
        rc = cv * lax.rsqrt(jnp.mean(cv * cv, axis=1, keepdims=True) + LN_EPS) * g_ref[1:2, :]
        rab = ra.astype(BF16)
        rcb = rc.astype(BF16)
        cat_ref[:, 0:ATTN_W] = rab
        cat_ref[:, ATTN_W:D_MODEL] = rcb
        mixed = _dot(rab, w_ref[0:ATTN_W, :]) + _dot(rcb, w_ref[ATTN_W:D_MODEL, :])
        mixed_ref[...] = mixed
        z1 = ALPHA * x_ref[...] + (1.0 + mod_ref[2:3, :]) * mixed
        z1_ref[...] = z1
        xh, _ = _ln_norm(z1)
        x1_ref[...] = xh * ln_ref[0:1, :] + ln_ref[1:2, :]

    row = lambda w: pl.BlockSpec((tm, w), lambda i: (i, 0))
    return pl.pallas_call(
        body, name="mix_fwd", grid=(s // tm,),
        in_specs=[row(ATTN_W), row(CONV_W), row(D_MODEL), VMEM_WHOLE,
                  pl.BlockSpec((8, ATTN_W), lambda i: (0, 0)), pl.BlockSpec((8, D_MODEL), lambda i: (0, 0)),
                  pl.BlockSpec((8, D_MODEL), lambda i: (0, 0))],
        out_specs=[row(D_MODEL)] * 4,
        out_shape=[jax.ShapeDtypeStruct((s, D_MODEL), BF16)] + [jax.ShapeDtypeStruct((s, D_MODEL), F32)] * 3,
        compiler_params=_params(("parallel",), VMEM_MID),
    )(o, conv, x, w_out, gvec, mod, lnv)


def _ff_fwd(x1, tgt, w1, w2, mod, lnv, tm):
    s = x1.shape[0]

    def body(x1_ref, t_ref, w1_ref, w2_ref, mod_ref, ln_ref, u2_ref, hid_ref, dff_ref, dxa_ref, st_ref):
        i = pl.program_id(0)
        x1v = x1_ref[...]
        u2 = (x1v * (1.0 + mod_ref[4:5, :]) + mod_ref[3:4, :]).astype(BF16)
        u2_ref[...] = u2
        ff = jnp.zeros((tm, D_MODEL), F32)
        for d in range(N_DEV):
            h = _dot(u2, w1_ref[d])
            hr = jnp.maximum(h, 0.0)
            hb = (hr * hr).astype(BF16)
            hid_ref[:, d * FF_SHARD:(d + 1) * FF_SHARD] = hb
            ff = ff + _dot(hb, w2_ref[d * FF_SHARD:(d + 1) * FF_SHARD, :])
        gate = 1.0 + mod_ref[5:6, :]
        z2 = ALPHA * x1v + gate * ff
        xh, rstd = _ln_norm(z2)
        y = xh * ln_ref[2:3, :] + ln_ref[3:4, :]
        diff = y - t_ref[...]
        dy = diff * (1.0 / D_MODEL)
        dz2 = _ln_bwd(dy * ln_ref[2:3, :], xh, rstd)
        dff_ref[...] = (gate * dz2).astype(BF16)
        dxa_ref[...] = ALPHA * dz2

        @pl.when(i == 0)
        def _():
            st_ref[...] = jnp.zeros_like(st_ref)

        _acc_rows(st_ref, 0, _rowsum(dy * xh))
        _acc_rows(st_ref, 1, _rowsum(dy))
        _acc_rows(st_ref, 2, _rowsum(dz2 * ff))
        sq = _rowsum(jnp.sum(diff * diff, axis=1, keepdims=True))
        _acc_rows(st_ref, 3, jnp.broadcast_to(sq * (0.5 / D_MODEL), (1, D_MODEL)))

    row = lambda w: pl.BlockSpec((tm, w), lambda i: (i, 0))
    vec = pl.BlockSpec((8, D_MODEL), lambda i: (0, 0))
    return pl.pallas_call(
        body, name="ff_fwd", grid=(s // tm,),
        in_specs=[row(D_MODEL), row(D_MODEL), VMEM_WHOLE, VMEM_WHOLE, vec, vec],
        out_specs=[row(D_MODEL), row(D_FF), row(D_MODEL), row(D_MODEL), vec],
        out_shape=[jax.ShapeDtypeStruct((s, D_MODEL), BF16), jax.ShapeDtypeStruct((s, D_FF), BF16),
                   jax.ShapeDtypeStruct((s, D_MODEL), BF16), jax.ShapeDtypeStruct((s, D_MODEL), F32),
                   jax.ShapeDtypeStruct((8, D_MODEL), F32)],
        compiler_params=_params(("arbitrary",), VMEM_BIG),
    )(x1, tgt, w1, w2, mod, lnv)


def _ff_bwd(dff, hid, w1, w2, dxa, x1, z1, mixed, mod, lnv, tm):
    s = x1.shape[0]

    def body(dff_ref, hid_ref, w1_ref, w2_ref, dxa_ref, x1_ref, z1_ref, mixed_ref, mod_ref, ln_ref,
             dh_ref, dxb_ref, dmix_ref, st_ref):
        i = pl.program_id(0)
        dffv = dff_ref[...]
        du2 = jnp.zeros((tm, D_MODEL), F32)
        for d in range(N_DEV):
            cols = slice(d * FF_SHARD, (d + 1) * FF_SHARD)
            dhid = _dot_nt(dffv, w2_ref[cols, :])
            dh = (dhid * (2.0 * jnp.sqrt(hid_ref[:, cols].astype(F32)))).astype(BF16)
            dh_ref[:, cols] = dh
            du2 = du2 + _dot_nt(dh, w1_ref[d])
        x1v = x1_ref[...]
        dx1 = dxa_ref[...] + du2 * (1.0 + mod_ref[4:5, :])
        xh, rstd = _ln_norm(z1_ref[...])
        dz1 = _ln_bwd(dx1 * ln_ref[0:1, :], xh, rstd)
        dxb_ref[...] = ALPHA * dz1
        dmix_ref[...] = ((1.0 + mod_ref[2:3, :]) * dz1).astype(BF16)

        @pl.when(i == 0)
        def _():
            st_ref[...] = jnp.zeros_like(st_ref)

        _acc_rows(st_ref, 0, _rowsum(du2 * x1v))
        _acc_rows(st_ref, 1, _rowsum(du2))
        _acc_rows(st_ref, 2, _rowsum(dx1 * xh))
        _acc_rows(st_ref, 3, _rowsum(dx1))
        _acc_rows(st_ref, 4, _rowsum(dz1 * mixed_ref[...]))

    row = lambda w: pl.BlockSpec((tm, w), lambda i: (i, 0))
    vec = pl.BlockSpec((8, D_MODEL), lambda i: (0, 0))
    return pl.pallas_call(
        body, name="ff_bwd", grid=(s // tm,),
        in_specs=[row(D_MODEL), row(D_FF), VMEM_WHOLE, VMEM_WHOLE, row(D_MODEL), row(D_MODEL), row(D_MODEL),
                  row(D_MODEL), vec, vec],
        out_specs=[row(D_FF), row(D_MODEL), row(D_MODEL), vec],
        out_shape=[jax.ShapeDtypeStruct((s, D_FF), BF16), jax.ShapeDtypeStruct((s, D_MODEL), F32),
                   jax.ShapeDtypeStruct((s, D_MODEL), BF16), jax.ShapeDtypeStruct((8, D_MODEL), F32)],
        compiler_params=_params(("arbitrary",), VMEM_BIG),
    )(dff, hid, w1, w2, dxa, x1, z1, mixed, mod, lnv)


def _mix_bwd(dmix, w_out, o, conv, gvec, tm):
    s = o.shape[0]

    def body(dm_ref, w_ref, o_ref, c_ref, g_ref, do_ref, dc_ref, st_ref):
        i = pl.program_id(0)
        dmv = dm_ref[...]

        @pl.when(i == 0)
        def _():
            st_ref[...] = jnp.zeros_like(st_ref)

        for part, (src, dst) in enumerate(((o_ref, do_ref), (c_ref, dc_ref))):
            dr = _dot_nt(dmv, w_ref[part * ATTN_W:(part + 1) * ATTN_W, :])
            v = src[...]
            rr = lax.rsqrt(jnp.mean(v * v, axis=1, keepdims=True) + LN_EPS)
            vh = v * rr
            _acc_rows(st_ref, part, _rowsum(dr * vh))
            t = dr * g_ref[part:part + 1, :]
            dst[...] = rr * (t - vh * jnp.mean(t * vh, axis=1, keepdims=True))

    row = lambda w: pl.BlockSpec((tm, w), lambda i: (i, 0))
    vec = pl.BlockSpec((8, ATTN_W), lambda i: (0, 0))
    return pl.pallas_call(
        body, name="mix_bwd", grid=(s // tm,),
        in_specs=[row(D_MODEL), VMEM_WHOLE, row(ATTN_W), row(CONV_W), vec],
        out_specs=[row(ATTN_W), row(CONV_W), vec],
        out_shape=[jax.ShapeDtypeStruct((s, ATTN_W), F32), jax.ShapeDtypeStruct((s, CONV_W), F32),
                   jax.ShapeDtypeStruct((8, ATTN_W), F32)],
        compiler_params=_params(("arbitrary",), VMEM_MID),
    )(dmix, w_out, o, conv, gvec)


def _conv_bwd(dconv, ypre, ag, wdw, vecs, gmat, tm):
    s = ag.shape[0]
    hb = tm // HALO
    nt = s // tm
    last_halo = s // HALO - 1

    def body(dc_ref, dch_ref, yp_ref, yph_ref, cur_ref, halo_ref, w_ref, vec_ref, g_ref,
             dag_ref, st_ref, dw_ref, bufu, bufd):
        i = pl.program_id(0)

        def dyc_of(yp, dc):
            yh, rstd, yn = _gn_fwd(yp, vec_ref, g_ref)
            sg = _sigmoid(yn)
            dyn = dc * (sg * (1.0 + yn * (1.0 - sg)))
            dyh = dyn * vec_ref[1:2, :]
            dyc = rstd * (dyh - _dot3(dyh, g_ref) - yh * _dot3(dyh * yh, g_ref))
            return dyc, dyn, yh

        dyc, dyn, yh = dyc_of(yp_ref[...], dc_ref[...])
        dych, _, _ = dyc_of(yph_ref[...], dch_ref[...])
        bufd[0:tm, :] = dyc
        bufd[tm:tm + HALO, :] = jnp.where(i < nt - 1, dych, 0.0)

        av = cur_ref[:, 0:CONV_W]
        sg_g = _sigmoid(cur_ref[:, CONV_W:2 * CONV_W])
        ugh = halo_ref[:, 0:CONV_W] * _sigmoid(halo_ref[:, CONV_W:2 * CONV_W])
        bufu[0:HALO, :] = jnp.where(i > 0, ugh, 0.0)
        bufu[HALO:HALO + tm, :] = av * sg_g

        @pl.when(i == 0)
        def _():
            st_ref[...] = jnp.zeros_like(st_ref)
            dw_ref[...] = jnp.zeros_like(dw_ref)

        _acc_rows(st_ref, 0, _rowsum(dyc))
        _acc_rows(st_ref, 1, _rowsum(dyn * yh))
        _acc_rows(st_ref, 2, _rowsum(dyn))

        dug = jnp.zeros((tm, CONV_W), F32)
        for j in range(CONV_K):
            dug = dug + w_ref[j:j + 1, :] * bufd[pl.ds(CONV_K - 1 - j, tm), :]
            _acc_rows(dw_ref, j, _rowsum(dyc * bufu[pl.ds(HALO - (CONV_K - 1) + j, tm), :]))
        dag_ref[:, 0:CONV_W] = (dug * sg_g).astype(BF16)
        dag_ref[:, CONV_W:2 * CONV_W] = (dug * av * sg_g * (1.0 - sg_g)).astype(BF16)

    row = lambda w: pl.BlockSpec((tm, w), lambda i: (i, 0))
    nxt = lambda w: pl.BlockSpec((HALO, w), lambda i: (jnp.minimum((i + 1) * hb, last_halo), 0))
    return pl.pallas_call(
        body, name="conv_bwd", grid=(nt,),
        in_specs=[row(CONV_W), nxt(CONV_W), row(CONV_W), nxt(CONV_W), row(2 * CONV_W),
                  pl.BlockSpec((HALO, 2 * CONV_W), lambda i: (jnp.maximum(i * hb - 1, 0), 0)),
                  pl.BlockSpec((32, CONV_W), lambda i: (0, 0)),
                  pl.BlockSpec((8, CONV_W), lambda i: (0, 0)),
                  pl.BlockSpec((CONV_W, CONV_W), lambda i: (0, 0))],
        out_specs=[row(2 * CONV_W), pl.BlockSpec((8, CONV_W), lambda i: (0, 0)),
                   pl.BlockSpec((32, CONV_W), lambda i: (0, 0))],
        out_shape=[jax.ShapeDtypeStruct((s, 2 * CONV_W), BF16), jax.ShapeDtypeStruct((8, CONV_W), F32),
                   jax.ShapeDtypeStruct((32, CONV_W), F32)],
        scratch_shapes=[pltpu.VMEM((HALO + tm, CONV_W), F32), pltpu.VMEM((tm + HALO, CONV_W), F32)],
        compiler_params=_params(("arbitrary",), VMEM_MID),
    )(dconv, dconv, ypre, ypre, ag, ag, wdw, vecs, gmat)


def _attn_bwd(qkv, cum_t, o, do, lse, tb):
    s = qkv.shape[0]
    nq = s // tb

    def body(q_ref, k_ref, v_ref, cum_ref, o_ref, do_ref, lse_ref, dq_ref, dkt_ref, dvt_ref, dcum_ref, drow_ref):
        hp = pl.program_id(0)
        qi = pl.program_id(1)

        @pl.when(qi == 0)
        def _():
            dkt_ref[...] = jnp.zeros_like(dkt_ref)
            dvt_ref[...] = jnp.zeros_like(dvt_ref)
            dcum_ref[...] = jnp.zeros_like(dcum_ref)

        qs = q_ref[...] * 0.125
        ov = o_ref[...]
        dob = do_ref[...].astype(BF16)
        lane = lax.broadcasted_iota(jnp.int32, (1, 128), 1)
        subl = lax.broadcasted_iota(jnp.int32, (128, 1), 0)
        causal = (lax.broadcasted_iota(jnp.int32, (tb, tb), 1) <= lax.broadcasted_iota(jnp.int32, (tb, tb), 0))
        hmask = (lane < 64, lane >= 64)
        aux = (64, 0)
        ones_aux = [jnp.where(lane == aux[h], 1.0, 0.0) for h in range(2)]
        qm, dom, delta, lse_h, qat, dot_t = [], [], [], [], [], []
        for h in range(2):
            qm.append(jnp.where(hmask[h], qs, jnp.zeros_like(qs)))
            dom.append(jnp.where(hmask[h], dob, jnp.zeros_like(dob)))
            delta.append(jnp.sum(dom[h].astype(F32) * ov, axis=1, keepdims=True))
            lse_h.append(lse_ref[:, h:h + 1])
            qat.append(jnp.where(hmask[h], qs.astype(F32), ones_aux[h]).T.astype(BF16))
            dot_t.append(dom[h].astype(F32).T.astype(BF16))

        def step(kb, carry, masked):
            k0 = pl.multiple_of(kb * tb, tb)
            k2 = k_ref[pl.ds(k0, tb), :]
            v2 = v_ref[pl.ds(k0, tb), :]
            dq_new, a, b = [], [], []
            for h in range(2):
                cs = cum_ref[pl.ds(2 * hp + h, 1), pl.ds(k0, tb)]
                sc = _dot_nt(qm[h], k2) - cs
                p = jnp.exp(sc - lse_h[h])
                if masked:
                    p = jnp.where(causal, p, 0.0)
                dp = _dot_nt(dom[h], v2)
                ds = p * (dp - delta[h])
                pb = p.astype(BF16)
                dsb = ds.astype(BF16)
                a.append(_dot(qat[h], dsb))
                b.append(_dot(dot_t[h], pb))
                ka = jnp.where(hmask[h], k2, ones_aux[h].astype(k2.dtype))
                dq_new.append(carry[h] + _dot(dsb, ka))
            dkt_ref[:, pl.ds(k0, tb)] = dkt_ref[:, pl.ds(k0, tb)] + jnp.where(subl < 64, a[0], a[1])
            dvt_ref[:, pl.ds(k0, tb)] = dvt_ref[:, pl.ds(k0, tb)] + (b[0] + b[1])
            for h in range(2):
                dcum_ref[pl.ds(h, 1), pl.ds(k0, tb)] = (dcum_ref[pl.ds(h, 1), pl.ds(k0, tb)]
                                                        - a[h][aux[h]:aux[h] + 1, :])
            return tuple(dq_new)

        init = (jnp.zeros((tb, 128), F32),) * 2
        carry = lax.fori_loop(0, qi, lambda kb, cr: step(kb, cr, False), init)
        dq0, dq1 = step(qi, carry, True)
        dq_ref[...] = (jnp.where(hmask[0], dq0, dq1) * 0.125).astype(BF16)
        drow_ref[:, 0:1] = dq0[:, aux[0]:aux[0] + 1]
        drow_ref[:, 1:2] = dq1[:, aux[1]:aux[1] + 1]

    return pl.pallas_call(
        body, name="attn_bwd", grid=(4, nq),
        in_specs=[pl.BlockSpec((tb, 128), lambda h, i: (i, h)),
                  pl.BlockSpec((s, 128), lambda h, i: (0, 4 + h)),
                  pl.BlockSpec((s, 128), lambda h, i: (0, 8 + h)),
                  pl.BlockSpec((8, s), lambda h, i: (0, 0)),
                  pl.BlockSpec((tb, 128), lambda h, i: (i, h)),
                  pl.BlockSpec((tb, 128), lambda h, i: (i, h)),
                  pl.BlockSpec((None, tb, 2), lambda h, i: (h, i, 0))],
        out_specs=[pl.BlockSpec((tb, 128), lambda h, i: (i, h)),
                   pl.BlockSpec((128, s), lambda h, i: (h, 0)),
                   pl.BlockSpec((128, s), lambda h, i: (h, 0)),
                   pl.BlockSpec((None, 8, s), lambda h, i: (h, 0, 0)),
                   pl.BlockSpec((None, tb, 2), lambda h, i: (h, i, 0))],
        out_shape=[jax.ShapeDtypeStruct((s, ATTN_W), BF16), jax.ShapeDtypeStruct((ATTN_W, s), F32),
                   jax.ShapeDtypeStruct((ATTN_W, s), F32), jax.ShapeDtypeStruct((4, 8, s), F32),
                   jax.ShapeDtypeStruct((4, s, 2), F32)],
        compiler_params=_params(("arbitrary", "arbitrary"), VMEM_BIG),
    )(qkv, qkv, qkv, cum_t, o, do, lse)


def _fgate_bwd(dcum_t, drow_t, sneg_t):
    s = dcum_t.shape[1]

    def body(dc_ref, dr_ref, sn_ref, df_ref, db_ref):
        lane = lax.broadcasted_iota(jnp.int32, (8, s), 1)
        acc = dc_ref[...] + dr_ref[...]
        sh = 1
        while sh < s:
            acc = acc + jnp.where(lane < s - sh, pltpu.roll(acc, s - sh, axis=1), 0.0)
            sh *= 2
        df = acc * sn_ref[...]
        df_ref[...] = df
        db_ref[...] = jnp.broadcast_to(jnp.sum(df, axis=1, keepdims=True), (8, 128))

    return pl.pallas_call(
        body, name="fgate_bwd",
        out_shape=[jax.ShapeDtypeStruct((8, s), F32), jax.ShapeDtypeStruct((8, 128), F32)],
        in_specs=[VMEM_WHOLE] * 3, out_specs=[VMEM_WHOLE] * 2,
    )(dcum_t, drow_t, sneg_t)


def _in_proj_bwd(dq, dkt, dvt, dag, dft, wqkv, wkvt, wag, wft, x, dxb, mod, tm):
    s = x.shape[0]

    def body(dq_ref, dkt_ref, dvt_ref, dag_ref, dft_ref, wqkv_ref, wkvt_ref, wag_ref, wft_ref, x_ref, dxb_ref,
             mod_ref, dkvt_ref, dx_ref, st_ref):
        i = pl.program_id(0)
        dkb = dkt_ref[...].astype(BF16)
        dvb = dvt_ref[...].astype(BF16)
        dkvt_ref[0:ATTN_W, :] = dkb
        dkvt_ref[ATTN_W:2 * ATTN_W, :] = dvb
        du = _dot_nt(dq_ref[...], wqkv_ref[:, 0:ATTN_W])
        du = du + _dot_tn(dkb, wkvt_ref[0:ATTN_W, :])
        du = du + _dot_tn(dvb, wkvt_ref[ATTN_W:2 * ATTN_W, :])
        du = du + _dot_nt(dag_ref[...], wag_ref[...])
        du = du + _dot_tn(dft_ref[...].astype(BF16), wft_ref[...])
        xv = x_ref[...]
        dx_ref[...] = dxb_ref[...] + du * (1.0 + mod_ref[1:2, :])

        @pl.when(i == 0)
        def _():
            st_ref[...] = jnp.zeros_like(st_ref)

        _acc_rows(st_ref, 0, _rowsum(du * xv))
        _acc_rows(st_ref, 1, _rowsum(du))

    row = lambda w: pl.BlockSpec((tm, w), lambda i: (i, 0))
    col = lambda h: pl.BlockSpec((h, tm), lambda i: (0, i))
    vec = pl.BlockSpec((8, D_MODEL), lambda i: (0, 0))
    return pl.pallas_call(
        body, name="in_proj_bwd", grid=(s // tm,),
        in_specs=[row(ATTN_W), col(ATTN_W), col(ATTN_W), row(2 * CONV_W), col(8),
                  VMEM_WHOLE, VMEM_WHOLE, VMEM_WHOLE, VMEM_WHOLE, row(D_MODEL), row(D_MODEL), vec],
        out_specs=[col(2 * ATTN_W), row(D_MODEL), vec],
        out_shape=[jax.ShapeDtypeStruct((2 * ATTN_W, s), BF16), jax.ShapeDtypeStruct((s, D_MODEL), F32),
                   jax.ShapeDtypeStruct((8, D_MODEL), F32)],
        compiler_params=_params(("arbitrary",), VMEM_MID),
    )(dq, dkt, dvt, dag, dft, wqkv, wkvt, wag, wft, x, dxb, mod)


def _wgrad_nn(a, b, name, tm, tn, tk):
    m, s = a.shape
    n = b.shape[1]
    nk = s // tk

    def body(a_ref, b_ref, o_ref, acc_ref):
        k = pl.program_id(2)

        @pl.when(k == 0)
        def _():
            acc_ref[...] = jnp.zeros_like(acc_ref)

        acc_ref[...] += _dot(a_ref[...], b_ref[...])

        @pl.when(k == nk - 1)
        def _():
            o_ref[...] = acc_ref[...].astype(o_ref.dtype)

    return pl.pallas_call(
        body, name=name, grid=(m // tm, n // tn, nk),
        in_specs=[pl.BlockSpec((tm, tk), lambda i, j, k: (i, k)), pl.BlockSpec((tk, tn), lambda i, j, k: (k, j))],
        out_specs=pl.BlockSpec((tm, tn), lambda i, j, k: (i, j)),
        out_shape=jax.ShapeDtypeStruct((m, n), BF16),
        scratch_shapes=[pltpu.VMEM((tm, tn), F32)],
        compiler_params=_params(("parallel", "parallel", "arbitrary"), VMEM_MID),
    )(a, b)


def _wgrad_tn(a, b, name, tm, tn, tk, out_shape, out_spec):
    s, m = a.shape
    n = b.shape[1]
    nk = s // tk

    def body(a_ref, b_ref, o_ref, acc_ref):
        k = pl.program_id(2)

        @pl.when(k == 0)
        def _():
            acc_ref[...] = jnp.zeros_like(acc_ref)

        acc_ref[...] += _dot_tn(a_ref[...], b_ref[...])

        @pl.when(k == nk - 1)
        def _():
            o_ref[...] = acc_ref[...].astype(o_ref.dtype)

    return pl.pallas_call(
        body, name=name, grid=(m // tm, n // tn, nk),
        in_specs=[pl.BlockSpec((tk, tm), lambda i, j, k: (k, i)), pl.BlockSpec((tk, tn), lambda i, j, k: (k, j))],
        out_specs=out_spec, out_shape=out_shape,
        scratch_shapes=[pltpu.VMEM((tm, tn), F32)],
        compiler_params=_params(("parallel", "parallel", "arbitrary"), VMEM_MID),
    )(a, b)


def _wgrad_f(dft, u, tk):
    s = u.shape[0]
    nk = s // tk

    def body(d_ref, u_ref, o_ref, acc_ref):
        k = pl.program_id(0)

        @pl.when(k == 0)
        def _():
            acc_ref[...] = jnp.zeros_like(acc_ref)

        acc_ref[...] += _dot(d_ref[...].astype(BF16), u_ref[...])

        @pl.when(k == nk - 1)
        def _():
            o_ref[...] = acc_ref[...].astype(BF16)

    return pl.pallas_call(
        body, name="wgrad_f", grid=(nk,),
        in_specs=[pl.BlockSpec((8, tk), lambda k: (0, k)), pl.BlockSpec((tk, D_MODEL), lambda k: (k, 0))],
        out_specs=pl.BlockSpec((8, D_MODEL), lambda k: (0, 0)),
        out_shape=jax.ShapeDtypeStruct((8, D_MODEL), BF16),
        scratch_shapes=[pltpu.VMEM((8, D_MODEL), F32)],
        compiler_params=_params(("arbitrary",)),
    )(dft, u)


def _adamw_big(recv, w, m, v, name, tr):
    r, cdim = w.shape

    def body(g_ref, w_ref, m_ref, v_ref, go_ref, d_ref, mo_ref, vo_ref):
        g = g_ref[0].astype(F32)
        for k in range(1, N_DEV):
            g = g + g_ref[k].astype(F32)
        delta, m2, v2 = _adamw(w_ref[...], g, m_ref[...], v_ref[...])
        go_ref[...] = g
        d_ref[...] = delta
        mo_ref[...] = m2
        vo_ref[...] = v2

    blk = pl.BlockSpec((tr, cdim), lambda i: (i, 0))
    return pl.pallas_call(
        body, name=name, grid=(r // tr,),
        in_specs=[pl.BlockSpec((N_DEV, tr, cdim), lambda i: (0, i, 0)), blk, blk, blk],
        out_specs=[blk] * 4, out_shape=[jax.ShapeDtypeStruct((r, cdim), F32)] * 4,
        compiler_params=_params(("parallel",), VMEM_MID),
    )(recv, w, m, v)


def _adamw_ada(c_all, dada, w, m, v, tr):
    r, cdim = w.shape

    def body(c_ref, d_ref, w_ref, m_ref, v_ref, go_ref, dl_ref, mo_ref, vo_ref):
        cv = c_ref[...]
        sc = (cv * _sigmoid(cv)).astype(BF16)
        g = _dot_tn(sc, d_ref[...].astype(BF16))
        delta, m2, v2 = _adamw(w_ref[...], g, m_ref[...], v_ref[...])
        go_ref[...] = g
        dl_ref[...] = delta
        mo_ref[...] = m2
        vo_ref[...] = v2

    blk = pl.BlockSpec((tr, cdim), lambda i: (i, 0))
    return pl.pallas_call(
        body, name="adamw_ada", grid=(r // tr,),
        in_specs=[pl.BlockSpec((N_DEV, tr), lambda i: (0, i)), pl.BlockSpec((N_DEV, cdim), lambda i: (0, 0)),
                  blk, blk, blk],
        out_specs=[blk] * 4, out_shape=[jax.ShapeDtypeStruct((r, cdim), F32)] * 4,
        compiler_params=_params(("parallel",), VMEM_MID),
    )(c_all, dada, w, m, v)


def _adamw_small(parts, w, m, v):
    n = w.shape[1]

    def body(g_ref, w_ref, m_ref, v_ref, go_ref, d_ref, mo_ref, vo_ref):
        g = g_ref[0]
        for k in range(1, N_DEV):
            g = g + g_ref[k]
        delta, m2, v2 = _adamw(w_ref[...], g, m_ref[...], v_ref[...])
        go_ref[...] = g
        d_ref[...] = delta
        mo_ref[...] = m2
        vo_ref[...] = v2

    return pl.pallas_call(
        body, name="adamw_small", out_shape=[jax.ShapeDtypeStruct((1, n), F32)] * 4,
        in_specs=[VMEM_WHOLE] * 4, out_specs=[VMEM_WHOLE] * 4,
    )(parts, w, m, v)


def _pad_cols(a, n):
    return jnp.pad(a, ((0, 0), (0, n - a.shape[1])))


def _pad_rows(a, n):
    return jnp.pad(a, ((0, n - a.shape[0]), (0, 0)))


def kernel(x, c, w_ada, b_ada, w_in, b_forget, w_dw, b_dw, gn_g, gn_b, g_attn_out, g_conv_out, w_out, ln1_g, ln1_b, w_ff1, w_ff2, ln2_g, ln2_b, loss_target, m_w_ada, m_b_ada, m_w_in, m_b_forget, m_w_dw, m_b_dw, m_gn_g, m_gn_b, m_g_attn_out, m_g_conv_out, m_w_out, m_ln1_g, m_ln1_b, m_w_ff1, m_w_ff2, m_ln2_g, m_ln2_b, v_w_ada, v_b_ada, v_w_in, v_b_forget, v_w_dw, v_b_dw, v_gn_g, v_gn_b, v_g_attn_out, v_g_conv_out, v_w_out, v_ln1_g, v_ln1_b, v_w_ff1, v_w_ff2, v_ln2_g, v_ln2_b):
    s = x.shape[1]
    tm = min(ROW_TILE, s)
    tb = min(ATT_TILE, s)
    me = 4 * lax.axis_index("x") + 2 * lax.axis_index("y") + lax.axis_index("c")
    xs = x[0]
    tgt = loss_target[0]

    dw_n = CONV_K * DW_SHARD
    pack1 = jnp.concatenate([c, w_dw[0].reshape(1, dw_n), jnp.zeros((1, 2048 - dw_n), F32)], axis=1)
    g1 = _gather_small(pack1, "gather_c")[:, 0, :]
    c_all = g1[:, :D_MODEL]
    wdw_full = g1[:, D_MODEL:D_MODEL + dw_n].reshape(N_DEV, CONV_K, DW_SHARD)
    wdw_full = _pad_rows(wdw_full.transpose(1, 0, 2).reshape(CONV_K, CONV_W), 32)

    b_shard = lax.dynamic_slice(b_ada, (0, me * ADA_SHARD), (1, ADA_SHARD))
    ada_all = _gather_small(_ada_part(c_all, w_ada[0], b_shard), "gather_ada")
    ada = lax.dynamic_index_in_dim(ada_all, me, axis=1, keepdims=False).reshape(6, D_MODEL)
    mod = _pad_rows(ada, 8)

    gw_in, gw_out, gw_ff1, gw_ff2 = _gather_weights([
        _pad_cols(w_in[0], IN_SHARD_PAD).astype(BF16), w_out[0].astype(BF16),
        w_ff1[0].astype(BF16), w_ff2[0].astype(BF16)])
    w_in_full = gw_in[:, :, :IN_SHARD].transpose(1, 0, 2).reshape(D_MODEL, N_IN)
    wqkv = w_in_full[:, :3 * ATTN_W]
    wft = w_in_full[:, 3 * ATTN_W:3 * ATTN_W + N_HEADS].T
    wag = w_in_full[:, 3 * ATTN_W + N_HEADS:]
    w_out_full = gw_out.reshape(D_MODEL, D_MODEL)
    w2_full = gw_ff2.reshape(D_FF, D_MODEL)

    qkv, ag, ft, u = _in_proj(xs, mod, wqkv, wag, wft, tm)
    cum_t, sneg_t = _fgate(ft, b_forget.reshape(N_HEADS, 1))
    o, lse = _attn_fwd(qkv, cum_t, tb)
    cvec = _pad_rows(jnp.concatenate([b_dw, gn_g, gn_b], axis=0), 8)
    grp = jnp.arange(CONV_W) // 64
    gmat = jnp.where(grp[:, None] == grp[None, :], 1.0 / 64.0, 0.0).astype(BF16)
    ypre, conv = _conv_fwd(ag, wdw_full, cvec, gmat, tm)
    gvec = _pad_rows(jnp.concatenate([g_attn_out, g_conv_out], axis=0), 8)
    lnv = _pad_rows(jnp.concatenate([ln1_g, ln1_b, ln2_g, ln2_b], axis=0), 8)
    cat, mixed, z1, x1 = _mix_fwd(o, conv, xs, w_out_full, gvec, mod, lnv, tm)
    tf = min(FF_TILE, s)
    u2, hid, dff, dxa, st_ff = _ff_fwd(x1, tgt, gw_ff1, w2_full, mod, lnv, tf)

    dh, dxb, dmix, st_fb = _ff_bwd(dff, hid, gw_ff1, w2_full, dxa, x1, z1, mixed, mod, lnv, tf)
    do, dconv, st_mix = _mix_bwd(dmix, w_out_full, o, conv, gvec, tm)
    dag, st_conv, dwdw = _conv_bwd(dconv, ypre, ag, wdw_full, cvec, gmat, tm)
    dq, dkt, dvt, dcum, drow = _attn_bwd(qkv, cum_t, o, do, lse, tb)
    dft, dbf = _fgate_bwd(dcum[:, :2, :].reshape(N_HEADS, s), drow.transpose(0, 2, 1).reshape(N_HEADS, s), sneg_t)
    wkvt = wqkv[:, ATTN_W:3 * ATTN_W].T
    dkvt, grad_x, st_in = _in_proj_bwd(dq, dkt, dvt, dag, dft, wqkv, wkvt, wag, wft, xs, dxb, mod, tm)

    tk = min(512, s)
    g_ff2 = _wgrad_tn(hid, dff, "wgrad_ff2", 512, 1024, tk,
                      jax.ShapeDtypeStruct((D_FF, D_MODEL), BF16),
                      pl.BlockSpec((512, 1024), lambda i, j, k: (i, j))).reshape(N_DEV, FF_SHARD, D_MODEL)
    g_ff1 = _wgrad_tn(u2, dh, "wgrad_ff1", 1024, 512, tk,
                      jax.ShapeDtypeStruct((N_DEV, D_MODEL, FF_SHARD), BF16),
                      pl.BlockSpec((None, 1024, 512), lambda i, j, k: (j, i, 0)))
    g_out = _wgrad_tn(cat, dmix, "wgrad_out", 512, 1024, tk,
                      jax.ShapeDtypeStruct((D_MODEL, D_MODEL), BF16),
                      pl.BlockSpec((512, 1024), lambda i, j, k: (i, j))).reshape(N_DEV, OUT_SHARD, D_MODEL)
    g_q = _wgrad_tn(u, dq, "wgrad_q", 1024, 512, tk,
                    jax.ShapeDtypeStruct((D_MODEL, ATTN_W), BF16),
                    pl.BlockSpec((1024, 512), lambda i, j, k: (i, j)))
    g_kvt = _wgrad_nn(dkvt, u, "wgrad_kv", 512, 1024, tk)
    g_qkv = jnp.concatenate([g_q, g_kvt.T], axis=1)
    g_ag = _wgrad_tn(u, dag, "wgrad_ag", 1024, 512, tk,
                     jax.ShapeDtypeStruct((D_MODEL, 2 * CONV_W), BF16),
                     pl.BlockSpec((1024, 512), lambda i, j, k: (i, j)))
    g_ft = _wgrad_f(dft, u, tk)
    g_in = jnp.concatenate([g_qkv, g_ft.T, g_ag], axis=1)
    g_in = jnp.pad(g_in.reshape(D_MODEL, N_DEV, IN_SHARD).transpose(1, 0, 2),
                   ((0, 0), (0, 0), (0, IN_SHARD_PAD - IN_SHARD)))

    r_in, r_out, r_ff1, r_ff2 = _scatter_grads([g_in, g_out, g_ff1, g_ff2])

    dada = jnp.concatenate([st_in[1:2], st_in[0:1], st_fb[4:5], st_fb[1:2], st_fb[0:1], st_ff[2:3]], axis=1)
    loss_part = st_ff[3:4, 0:1]
    pack2 = jnp.concatenate([
        dada,
        _pad_cols(dbf[:, 0].reshape(1, N_HEADS), 128),
        dwdw[:CONV_K].reshape(1, CONV_K * CONV_W),
        st_conv[0:1], st_conv[1:2], st_conv[2:3], st_mix[0:1], st_mix[1:2],
        st_fb[2:3], st_fb[3:4], st_ff[0:1], st_ff[1:2],
        _pad_cols(loss_part, 128)], axis=1)
    g2 = _gather_small(pack2, "gather_small_grads")[:, 0, :]
    o_bf = 6 * D_MODEL
    o_dw = o_bf + 128
    o_v = o_dw + CONV_K * CONV_W
    o_ln = o_v + 5 * CONV_W
    o_loss = o_ln + 4 * D_MODEL
    dw_parts = lax.dynamic_slice_in_dim(
        g2[:, o_dw:o_v].reshape(N_DEV, CONV_K, N_DEV, DW_SHARD), me, 1, axis=2).reshape(N_DEV, dw_n)
    parts = jnp.concatenate([
        g2[:, :o_dw], _pad_cols(dw_parts, 2048), g2[:, o_v:o_loss + 128]], axis=1)[:, None, :]

    def pack_w(b_ada_, b_forget_, w_dw_, smalls):
        return jnp.concatenate([b_ada_, _pad_cols(b_forget_, 128), _pad_cols(w_dw_[0].reshape(1, dw_n), 2048)]
                               + smalls + [jnp.zeros((1, 128), F32)], axis=1)

    pw = pack_w(b_ada, b_forget, w_dw, [b_dw, gn_g, gn_b, g_attn_out, g_conv_out, ln1_g, ln1_b, ln2_g, ln2_b])
    pm = pack_w(m_b_ada, m_b_forget, m_w_dw, [m_b_dw, m_gn_g, m_gn_b, m_g_attn_out, m_g_conv_out,
                                              m_ln1_g, m_ln1_b, m_ln2_g, m_ln2_b])
    pv = pack_w(v_b_ada, v_b_forget, v_w_dw, [v_b_dw, v_gn_g, v_gn_b, v_g_attn_out, v_g_conv_out,
                                              v_ln1_g, v_ln1_b, v_ln2_g, v_ln2_b])
    small = _adamw_small(parts, pw, pm, pv)

    p_bf = 6 * D_MODEL
    p_dw = p_bf + 128
    p_v = p_dw + 2048
    p_ln = p_v + 5 * CONV_W
    p_loss = p_ln + 4 * D_MODEL

    def unpack(t):
        outs = {"b_ada": t[:, :p_bf], "b_forget": t[:, p_bf:p_bf + N_HEADS],
                "w_dw": t[:, p_dw:p_dw + dw_n].reshape(1, CONV_K, 1, DW_SHARD)}
        for k, nm in enumerate(["b_dw", "gn_g", "gn_b", "g_attn_out", "g_conv_out"]):
            outs[nm] = t[:, p_v + k * CONV_W:p_v + (k + 1) * CONV_W]
        for k, nm in enumerate(["ln1_g", "ln1_b", "ln2_g", "ln2_b"]):
            outs[nm] = t[:, p_ln + k * D_MODEL:p_ln + (k + 1) * D_MODEL]
        return outs

    sm = [unpack(t) for t in small]
    loss = small[0][0, p_loss]

    dada_all = g2[:, :6 * D_MODEL]
    dada_shard = lax.dynamic_slice_in_dim(dada_all, me * ADA_SHARD, ADA_SHARD, axis=1)
    big = {
        "w_ada": _adamw_ada(c_all, dada_shard, w_ada[0], m_w_ada[0], v_w_ada[0], 256),
        "w_in": [t[:, :IN_SHARD] for t in _adamw_big(
            r_in, _pad_cols(w_in[0], IN_SHARD_PAD), _pad_cols(m_w_in[0], IN_SHARD_PAD),
            _pad_cols(v_w_in[0], IN_SHARD_PAD), "adamw_in", 256)],
        "w_out": _adamw_big(r_out, w_out[0], m_w_out[0], v_w_out[0], "adamw_out", 128),
        "w_ff1": _adamw_big(r_ff1, w_ff1[0], m_w_ff1[0], v_w_ff1[0], "adamw_ff1", 256),
        "w_ff2": _adamw_big(r_ff2, w_ff2[0], m_w_ff2[0], v_w_ff2[0], "adamw_ff2", 256),
    }

    names = ["w_ada", "b_ada", "w_in", "b_forget", "w_dw", "b_dw", "gn_g", "gn_b", "g_attn_out", "g_conv_out",
             "w_out", "ln1_g", "ln1_b", "w_ff1", "w_ff2", "ln2_g", "ln2_b"]

    def leaf(kind, nm):
        if nm in big:
            return big[nm][kind][None]
        return sm[kind][nm]

    outs = [loss, grad_x[None]]
    for kind in range(4):
        outs += [leaf(kind, nm) for nm in names]
    return tuple(outs)
```

```python
import jax
import jax.numpy as jnp
from jax import lax
from jax.experimental import pallas as pl
from jax.experimental.pallas import tpu as pltpu

F32 = jnp.float32
BF16 = jnp.bfloat16

D_MODEL = 1024
ATTN_W = 512
CONV_W = 512
N_HEADS = 8
CONV_K = 31
D_FF = 4096
N_DEV = 8
N_IN = 3 * ATTN_W + N_HEADS + 2 * CONV_W
IN_SHARD = N_IN // N_DEV
IN_SHARD_PAD = 384
ADA_SHARD = 6 * D_MODEL // N_DEV
FF_SHARD = D_FF // N_DEV
OUT_SHARD = D_MODEL // N_DEV
DW_SHARD = CONV_W // N_DEV
HALO = 32
LN_EPS = 1e-5
ALPHA = 2.0 ** 0.25
ADAM_LR, ADAM_B1, ADAM_B2, ADAM_EPS, ADAM_WD, ADAM_STEP = 0.001, 0.9, 0.999, 1e-08, 0.01, 10

ROW_TILE = 512
FF_TILE = 256
ATT_TILE = 512
VMEM_BIG = 56 * 1024 * 1024
VMEM_MID = 40 * 1024 * 1024

MESH = pl.DeviceIdType.MESH
VMEM_WHOLE = pl.BlockSpec(memory_space=pltpu.VMEM)
ANY_SPACE = pl.BlockSpec(memory_space=pl.ANY)


def _dot(a, b):
    return jnp.dot(a, b, preferred_element_type=F32)


def _dot_nt(a, b):
    return lax.dot_general(a, b, (((1,), (1,)), ((), ())), preferred_element_type=F32)


def _dot_tn(a, b):
    return lax.dot_general(a, b, (((0,), (0,)), ((), ())), preferred_element_type=F32)


def _dot3(xf, g_ref):
    g = g_ref[...]
    hi = xf.astype(BF16)
    r1 = xf - hi.astype(F32)
    mid = r1.astype(BF16)
    lo = (r1 - mid.astype(F32)).astype(BF16)
    return _dot(hi, g) + _dot(mid, g) + _dot(lo, g)


def _sigmoid(x):
    return 1.0 / (1.0 + jnp.exp(-x))


def _rowsum(x):
    return jnp.sum(x, axis=0, keepdims=True)


def _ln_norm(z):
    mu = jnp.mean(z, axis=1, keepdims=True)
    zc = z - mu
    var = jnp.mean(zc * zc, axis=1, keepdims=True)
    rstd = lax.rsqrt(var + LN_EPS)
    return zc * rstd, rstd


def _ln_bwd(dxh, xh, rstd):
    m1 = jnp.mean(dxh, axis=1, keepdims=True)
    m2 = jnp.mean(dxh * xh, axis=1, keepdims=True)
    return rstd * (dxh - m1 - xh * m2)


def _adamw(w, g, m, v):
    m2 = ADAM_B1 * m + (1.0 - ADAM_B1) * g
    v2 = ADAM_B2 * v + (1.0 - ADAM_B2) * (g * g)
    m_hat = m2 / (1.0 - ADAM_B1 ** ADAM_STEP)
    v_hat = v2 / (1.0 - ADAM_B2 ** ADAM_STEP)
    delta = -ADAM_LR * (m_hat / (jnp.sqrt(v_hat) + ADAM_EPS) + ADAM_WD * w)
    return delta, m2, v2


def _params(sem=None, vmem=None):
    kw = {}
    if sem is not None:
        kw["dimension_semantics"] = sem
    if vmem is not None:
        kw["vmem_limit_bytes"] = vmem
    return pltpu.CompilerParams(**kw)


def _coords():
    return lax.axis_index("x"), lax.axis_index("y"), lax.axis_index("c")


def _flip(v, bit):
    return 1 - v if bit else v


def _acc_rows(ref, r, val):
    ref[r:r + 1, :] = ref[r:r + 1, :] + val


def _gather_small(v, name):
    r, n = v.shape

    def body(v_ref, out_ref, send_sems, recv_sems, local_sem):
        x, y, c = _coords()
        me = 4 * x + 2 * y + c
        mine = pltpu.make_async_copy(v_ref, out_ref.at[me], local_sem)
        mine.start()
        sends = []
        for k in range(1, N_DEV):
            tgt = (_flip(x, (k >> 2) & 1), _flip(y, (k >> 1) & 1), _flip(c, k & 1))
            cp = pltpu.make_async_remote_copy(
                src_ref=v_ref, dst_ref=out_ref.at[me], send_sem=send_sems.at[k - 1],
                recv_sem=recv_sems.at[k - 1], device_id=tgt, device_id_type=MESH)
            cp.start()
            sends.append(cp)
        for k in range(1, N_DEV):
            px, py, pc = _flip(x, (k >> 2) & 1), _flip(y, (k >> 1) & 1), _flip(c, k & 1)
            peer = 4 * px + 2 * py + pc
            pltpu.make_async_remote_copy(
                src_ref=v_ref, dst_ref=out_ref.at[peer], send_sem=send_sems.at[k - 1],
                recv_sem=recv_sems.at[k - 1], device_id=(px, py, pc), device_id_type=MESH).wait_recv()
        for cp in sends:
            cp.wait_send()
        mine.wait()

    return pl.pallas_call(
        body, name=name,
        out_shape=jax.ShapeDtypeStruct((N_DEV, r, n), v.dtype),
        in_specs=[VMEM_WHOLE], out_specs=VMEM_WHOLE,
        scratch_shapes=[pltpu.SemaphoreType.DMA((N_DEV - 1,)), pltpu.SemaphoreType.DMA((N_DEV - 1,)),
                        pltpu.SemaphoreType.DMA],
    )(v)


def _gather_weights(shards):
    n = len(shards)

    def body(*refs):
        ins, outs = refs[:n], refs[n:2 * n]
        send_sems, recv_sems, local_sems = refs[2 * n:]
        x, y, c = _coords()
        me, sibling = (x, y, c), (x, y, 1 - c)
        chips = [(1 - x, y), (x, 1 - y), (1 - x, 1 - y)]

        def slot(a, pos):
            return outs[a].at[4 * pos[0] + 2 * pos[1] + pos[2]]

        def copy(a, k, block, to, src=None):
            return pltpu.make_async_remote_copy(
                src_ref=slot(a, block) if src is None else src, dst_ref=slot(a, block),
                send_sem=send_sems.at[7 * a + k], recv_sem=recv_sems.at[7 * a + k],
                device_id=to, device_id_type=MESH)

        started = []
        local = []
        for a in range(n):
            mine = pltpu.make_async_copy(ins[a], slot(a, me), local_sems.at[a])
            mine.start()
            local.append(mine)
        for j, chip in enumerate(chips):
            for a in range(n):
                cp = copy(a, 1 + j, me, (*chip, c), src=ins[a])
                cp.start()
                started.append(cp)
        for a in range(n):
            cp = copy(a, 0, me, sibling, src=ins[a])
            cp.start()
            started.append(cp)
        for j, chip in enumerate(chips):
            for a in range(n):
                copy(a, 1 + j, (*chip, c), me).wait_recv()
                cp = copy(a, 4 + j, (*chip, c), sibling)
                cp.start()
                started.append(cp)
        for a in range(n):
            copy(a, 0, sibling, me).wait_recv()
        for j, chip in enumerate(chips):
            for a in range(n):
                copy(a, 4 + j, (*chip, 1 - c), me).wait_recv()
        for cp in started:
            cp.wait_send()
        for mine in local:
            mine.wait()

    return pl.pallas_call(
        body, name="gather_weights",
        out_shape=[jax.ShapeDtypeStruct((N_DEV,) + s.shape, s.dtype) for s in shards],
        in_specs=[ANY_SPACE] * n, out_specs=[ANY_SPACE] * n,
        scratch_shapes=[pltpu.SemaphoreType.DMA((7 * n,)), pltpu.SemaphoreType.DMA((7 * n,)),
                        pltpu.SemaphoreType.DMA((n,))],
    )(*shards)


def _exchange(ins, outs, send_sems, recv_sems, local_sems, gather):
    n = len(ins)
    x, y, c = _coords()
    me = 4 * x + 2 * y + c
    local = [pltpu.make_async_copy(ins[a] if gather else ins[a].at[me], outs[a].at[me], local_sems.at[a])
             for a in range(n)]
    sends, recvs = [], []
    for k in range(1, N_DEV):
        px, py, pc = _flip(x, (k >> 2) & 1), _flip(y, (k >> 1) & 1), _flip(c, k & 1)
        peer = 4 * px + 2 * py + pc
        for a in range(n):
            src = ins[a] if gather else ins[a].at[peer]
            sems = dict(send_sem=send_sems.at[7 * a + k - 1], recv_sem=recv_sems.at[7 * a + k - 1],
                        device_id=(px, py, pc), device_id_type=MESH)
            sends.append(pltpu.make_async_remote_copy(src_ref=src, dst_ref=outs[a].at[me], **sems))
            recvs.append(pltpu.make_async_remote_copy(src_ref=src, dst_ref=outs[a].at[peer], **sems))

    def start():
        for cp in local + sends:
            cp.start()

    def wait():
        for cp in recvs:
            cp.wait_recv()
        for cp in sends:
            cp.wait_send()
        for cp in local:
            cp.wait()

    return start, wait


def _exchange_scratch(n):
    return [pltpu.SemaphoreType.DMA((7 * n,)), pltpu.SemaphoreType.DMA((7 * n,)), pltpu.SemaphoreType.DMA((n,))]


def _scatter_grads(grads):
    n = len(grads)

    def body(*refs):
        start, wait = _exchange(refs[:n], refs[n:2 * n], *refs[2 * n:], gather=False)
        start()
        wait()

    return pl.pallas_call(
        body, name="scatter_grads",
        out_shape=[jax.ShapeDtypeStruct(g.shape, g.dtype) for g in grads],
        in_specs=[ANY_SPACE] * n, out_specs=[ANY_SPACE] * n,
        scratch_shapes=_exchange_scratch(n),
    )(*grads)


def _ada_part(c_all, w_ada, b_shard):
    def body(c_ref, w_ref, b_ref, o_ref):
        cv = c_ref[...]
        sc = (cv * _sigmoid(cv)).astype(BF16)
        o_ref[...] = _dot(sc, w_ref[...].astype(BF16)) + b_ref[...]

    return pl.pallas_call(
        body, name="ada_part", out_shape=jax.ShapeDtypeStruct((N_DEV, ADA_SHARD), F32),
        in_specs=[VMEM_WHOLE] * 3, out_specs=VMEM_WHOLE,
        compiler_params=_params(vmem=VMEM_MID),
    )(c_all, w_ada, b_shard)


def _in_proj(x, mod, wqkv, wag, wft, tm):
    s = x.shape[0]

    def body(x_ref, mod_ref, wqkv_ref, wag_ref, wft_ref, qkv_ref, ag_ref, ft_ref, u_ref):
        u = x_ref[...] * (1.0 + mod_ref[1:2, :]) + mod_ref[0:1, :]
        ub = u.astype(BF16)
        u_ref[...] = ub
        qkv_ref[...] = _dot(ub, wqkv_ref[...]).astype(BF16)
        ag_ref[...] = _dot(ub, wag_ref[...])
        ft_ref[...] = _dot_nt(wft_ref[...], ub)

    return pl.pallas_call(
        body, name="in_proj", grid=(s // tm,),
        in_specs=[pl.BlockSpec((tm, D_MODEL), lambda i: (i, 0)), pl.BlockSpec((8, D_MODEL), lambda i: (0, 0)),
                  VMEM_WHOLE, VMEM_WHOLE, VMEM_WHOLE],
        out_specs=[pl.BlockSpec((tm, 3 * ATTN_W), lambda i: (i, 0)),
                   pl.BlockSpec((tm, 2 * CONV_W), lambda i: (i, 0)),
                   pl.BlockSpec((8, tm), lambda i: (0, i)),
                   pl.BlockSpec((tm, D_MODEL), lambda i: (i, 0))],
        out_shape=[jax.ShapeDtypeStruct((s, 3 * ATTN_W), BF16), jax.ShapeDtypeStruct((s, 2 * CONV_W), F32),
                   jax.ShapeDtypeStruct((8, s), F32), jax.ShapeDtypeStruct((s, D_MODEL), BF16)],
        compiler_params=_params(("parallel",), VMEM_MID),
    )(x, mod, wqkv, wag, wft)


def _fgate(ft, bcol):
    s = ft.shape[1]

    def body(f_ref, b_ref, cum_ref, sneg_ref):
        z = f_ref[...] + b_ref[...]
        e = jnp.exp(-jnp.abs(z))
        l1p = jnp.where(e < 1e-2, e * (1.0 - e * (0.5 - e * (1.0 / 3.0))), jnp.log(1.0 + e))
        logf = jnp.minimum(z, 0.0) - l1p
        r = 1.0 / (1.0 + e)
        sneg_ref[...] = jnp.where(z >= 0, e * r, r)
        lane = lax.broadcasted_iota(jnp.int32, (8, s), 1)
        acc = logf
        sh = 1
        while sh < s:
            acc = acc + jnp.where(lane >= sh, pltpu.roll(acc, sh, axis=1), 0.0)
            sh *= 2
        cum_ref[...] = acc

    return pl.pallas_call(
        body, name="fgate", out_shape=[jax.ShapeDtypeStruct((8, s), F32)] * 2,
        in_specs=[VMEM_WHOLE] * 2, out_specs=[VMEM_WHOLE] * 2,
    )(ft, bcol)


def _attn_fwd(qkv, cum_t, tb, shards):
    s = qkv.shape[0]
    nq = s // tb
    n = len(shards)

    def body(q_ref, k_ref, v_ref, cum_ref, *rest):
        o_ref, lse_ref = rest[n:n + 2]
        hp = pl.program_id(0)
        qi = pl.program_id(1)
        start, wait = _exchange(rest[:n], rest[n + 2:2 * n + 2], *rest[2 * n + 2:], gather=True)
        pl.when(jnp.logical_and(hp == 0, qi == 0))(start)
        q2 = q_ref[...]
        lane = lax.broadcasted_iota(jnp.int32, (1, 128), 1)
        causal = (lax.broadcasted_iota(jnp.int32, (tb, tb), 1) <= lax.broadcasted_iota(jnp.int32, (tb, tb), 0))
        hmask = (lane < 64, lane >= 64)
        aux = (64, 0)
        qm = [jnp.where(hmask[h], q2, jnp.zeros_like(q2)) * 0.125 for h in range(2)]

        def step(kb, carry, masked):
            k0 = pl.multiple_of(kb * tb, tb)
            k2 = k_ref[pl.ds(k0, tb), :]
            v2 = v_ref[pl.ds(k0, tb), :]
            out = []
            for h in range(2):
                m, acc = carry[2 * h], carry[2 * h + 1]
                cs = cum_ref[pl.ds(2 * hp + h, 1), pl.ds(k0, tb)]
                sc = _dot_nt(qm[h], k2) - cs
                if masked:
                    sc = jnp.where(causal, sc, -jnp.inf)
                m_new = jnp.maximum(m, jnp.max(sc, axis=1, keepdims=True))
                p = jnp.exp(sc - m_new)
                corr = jnp.exp(m - m_new)
                va = jnp.where(hmask[h], v2, jnp.where(lane == aux[h], 1.0, 0.0).astype(v2.dtype))
                out += [m_new, acc * corr + _dot(p.astype(BF16), va)]
            return tuple(out)

        init = (jnp.full((tb, 1), -jnp.inf, F32), jnp.zeros((tb, 128), F32)) * 2
        carry = lax.fori_loop(0, qi, lambda kb, cr: step(kb, cr, False), init)
        m0, acc0, m1, acc1 = step(qi, carry, True)
        l0 = acc0[:, aux[0]:aux[0] + 1]
        l1 = acc1[:, aux[1]:aux[1] + 1]
        o_ref[...] = jnp.where(hmask[0], acc0 / l0, acc1 / l1)
        lse_ref[:, 0:1] = m0 + jnp.log(l0)
        lse_ref[:, 1:2] = m1 + jnp.log(l1)
        pl.when(jnp.logical_and(hp == 3, qi == nq - 1))(wait)

    return pl.pallas_call(
        body, name="attn_fwd", grid=(4, nq),
        in_specs=[pl.BlockSpec((tb, 128), lambda h, i: (i, h)),
                  pl.BlockSpec((s, 128), lambda h, i: (0, 4 + h)),
                  pl.BlockSpec((s, 128), lambda h, i: (0, 8 + h)),
                  pl.BlockSpec((8, s), lambda h, i: (0, 0))] + [ANY_SPACE] * n,
        out_specs=[pl.BlockSpec((tb, 128), lambda h, i: (i, h)),
                   pl.BlockSpec((None, tb, 2), lambda h, i: (h, i, 0))] + [ANY_SPACE] * n,
        out_shape=[jax.ShapeDtypeStruct((s, ATTN_W), F32), jax.ShapeDtypeStruct((4, s, 2), F32)]
        + [jax.ShapeDtypeStruct((N_DEV,) + w.shape, w.dtype) for w in shards],
        scratch_shapes=_exchange_scratch(n),
        compiler_params=_params(("arbitrary", "arbitrary"), VMEM_MID),
    )(qkv, qkv, qkv, cum_t, *shards)


def _gn_fwd(y, vec_ref, g_ref):
    mean = _dot3(y, g_ref)
    yc = y - mean
    var = _dot3(yc * yc, g_ref)
    rstd = lax.rsqrt(var + LN_EPS)
    yh = yc * rstd
    yn = yh * vec_ref[1:2, :] + vec_ref[2:3, :]
    return yh, rstd, yn


def _conv_fwd(ag, wdw, vecs, gmat, tm):
    s = ag.shape[0]
    hb = tm // HALO

    def body(cur_ref, halo_ref, w_ref, vec_ref, g_ref, ypre_ref, conv_ref, buf):
        i = pl.program_id(0)
        ug = cur_ref[:, 0:CONV_W] * _sigmoid(cur_ref[:, CONV_W:2 * CONV_W])
        ugh = halo_ref[:, 0:CONV_W] * _sigmoid(halo_ref[:, CONV_W:2 * CONV_W])
        buf[0:HALO, :] = jnp.where(i > 0, ugh, 0.0)
        buf[HALO:HALO + tm, :] = ug
        y = jnp.broadcast_to(vec_ref[0:1, :], (tm, CONV_W))
        for j in range(CONV_K):
            y = y + w_ref[j:j + 1, :] * buf[pl.ds(HALO - (CONV_K - 1) + j, tm), :]
        ypre_ref[...] = y
        _, _, yn = _gn_fwd(y, vec_ref, g_ref)
        conv_ref[...] = yn * _sigmoid(yn)

    return pl.pallas_call(
        body, name="conv_fwd", grid=(s // tm,),
        in_specs=[pl.BlockSpec((tm, 2 * CONV_W), lambda i: (i, 0)),
                  pl.BlockSpec((HALO, 2 * CONV_W), lambda i: (jnp.maximum(i * hb - 1, 0), 0)),
                  pl.BlockSpec((32, CONV_W), lambda i: (0, 0)),
                  pl.BlockSpec((8, CONV_W), lambda i: (0, 0)),
                  pl.BlockSpec((CONV_W, CONV_W), lambda i: (0, 0))],
        out_specs=[pl.BlockSpec((tm, CONV_W), lambda i: (i, 0))] * 2,
        out_shape=[jax.ShapeDtypeStruct((s, CONV_W), F32)] * 2,
        scratch_shapes=[pltpu.VMEM((HALO + tm, CONV_W), F32)],
        compiler_params=_params(("parallel",), VMEM_MID),
    )(ag, ag, wdw, vecs, gmat)


def _mix_fwd(o, conv, x, w_out, gvec, mod, lnv, tm):
    s = x.shape[0]

    def body(o_ref, c_ref, x_ref, w_ref, g_ref, mod_ref, ln_ref, cat_ref, mixed_ref, z1_ref, x1_ref):
        ov = o_ref[...]
        cv = c_ref[...]
        ra = ov * lax.rsqrt(jnp.mean(ov * ov, axis=1, keepdims=True) + LN_EPS) * g_ref[0:1, :]
        rc = cv * lax.rsqrt(jnp.mean(cv * cv, axis=1, keepdims=True) + LN_EPS) * g_ref[1:2, :]
        rab = ra.astype(BF16)
        rcb = rc.astype(BF16)
        cat_ref[:, 0:ATTN_W] = rab
        cat_ref[:, ATTN_W:D_MODEL] = rcb
        mixed = _dot(rab, w_ref[0:ATTN_W, :]) + _dot(rcb, w_ref[ATTN_W:D_MODEL, :])
        mixed_ref[...] = mixed
        z1 = ALPHA * x_ref[...] + (1.0 + mod_ref[2:3, :]) * mixed
        z1_ref[...] = z1
        xh, _ = _ln_norm(z1)
        x1_ref[...] = xh * ln_ref[0:1, :] + ln_ref[1:2, :]

    row = lambda w: pl.BlockSpec((tm, w), lambda i: (i, 0))
    return pl.pallas_call(
        body, name="mix_fwd", grid=(s // tm,),
        in_specs=[row(ATTN_W), row(CONV_W), row(D_MODEL), VMEM_WHOLE,
                  pl.BlockSpec((8, ATTN_W), lambda i: (0, 0)), pl.BlockSpec((8, D_MODEL), lambda i: (0, 0)),
                  pl.BlockSpec((8, D_MODEL), lambda i: (0, 0))],
        out_specs=[row(D_MODEL)] * 4,
        out_shape=[jax.ShapeDtypeStruct((s, D_MODEL), BF16)] + [jax.ShapeDtypeStruct((s, D_MODEL), F32)] * 3,
        compiler_params=_params(("parallel",), VMEM_MID),
    )(o, conv, x, w_out, gvec, mod, lnv)


def _ff_fwd(x1, tgt, w1, w2, mod, lnv, tm):
    s = x1.shape[0]

    def body(x1_ref, t_ref, w1_ref, w2_ref, mod_ref, ln_ref, u2_ref, hid_ref, dff_ref, dxa_ref, st_ref):
        i = pl.program_id(0)
        x1v = x1_ref[...]
        u2 = (x1v * (1.0 + mod_ref[4:5, :]) + mod_ref[3:4, :]).astype(BF16)
        u2_ref[...] = u2
        ff = jnp.zeros((tm, D_MODEL), F32)
        for d in range(N_DEV):
            h = _dot(u2, w1_ref[d])
            hr = jnp.maximum(h, 0.0)
            hb = (hr * hr).astype(BF16)
            hid_ref[:, d * FF_SHARD:(d + 1) * FF_SHARD] = hb
            ff = ff + _dot(hb, w2_ref[d * FF_SHARD:(d + 1) * FF_SHARD, :])
        gate = 1.0 + mod_ref[5:6, :]
        z2 = ALPHA * x1v + gate * ff
        xh, rstd = _ln_norm(z2)
        y = xh * ln_ref[2:3, :] + ln_ref[3:4, :]
        diff = y - t_ref[...]
        dy = diff * (1.0 / D_MODEL)
        dz2 = _ln_bwd(dy * ln_ref[2:3, :], xh, rstd)
        dff_ref[...] = (gate * dz2).astype(BF16)
        dxa_ref[...] = ALPHA * dz2

        @pl.when(i == 0)
        def _():
            st_ref[...] = jnp.zeros_like(st_ref)

        _acc_rows(st_ref, 0, _rowsum(dy * xh))
        _acc_rows(st_ref, 1, _rowsum(dy))
        _acc_rows(st_ref, 2, _rowsum(dz2 * ff))
        sq = _rowsum(jnp.sum(diff * diff, axis=1, keepdims=True))
        _acc_rows(st_ref, 3, jnp.broadcast_to(sq * (0.5 / D_MODEL), (1, D_MODEL)))

    row = lambda w: pl.BlockSpec((tm, w), lambda i: (i, 0))
    vec = pl.BlockSpec((8, D_MODEL), lambda i: (0, 0))
    return pl.pallas_call(
        body, name="ff_fwd", grid=(s // tm,),
        in_specs=[row(D_MODEL), row(D_MODEL), VMEM_WHOLE, VMEM_WHOLE, vec, vec],
        out_specs=[row(D_MODEL), row(D_FF), row(D_MODEL), row(D_MODEL), vec],
        out_shape=[jax.ShapeDtypeStruct((s, D_MODEL), BF16), jax.ShapeDtypeStruct((s, D_FF), BF16),
                   jax.ShapeDtypeStruct((s, D_MODEL), BF16), jax.ShapeDtypeStruct((s, D_MODEL), F32),
                   jax.ShapeDtypeStruct((8, D_MODEL), F32)],
        compiler_params=_params(("arbitrary",), VMEM_BIG),
    )(x1, tgt, w1, w2, mod, lnv)


def _ff_bwd(dff, hid, w1, w2, dxa, x1, z1, mixed, mod, lnv, tm):
    s = x1.shape[0]

    def body(dff_ref, hid_ref, w1_ref, w2_ref, dxa_ref, x1_ref, z1_ref, mixed_ref, mod_ref, ln_ref,
             dh_ref, dxb_ref, dmix_ref, st_ref):
        i = pl.program_id(0)
        dffv = dff_ref[...]
        du2 = jnp.zeros((tm, D_MODEL), F32)
        for d in range(N_DEV):
            cols = slice(d * FF_SHARD, (d + 1) * FF_SHARD)
            dhid = _dot_nt(dffv, w2_ref[cols, :])
            dh = (dhid * (2.0 * jnp.sqrt(hid_ref[:, cols].astype(F32)))).astype(BF16)
            dh_ref[:, cols] = dh
            du2 = du2 + _dot_nt(dh, w1_ref[d])
        x1v = x1_ref[...]
        dx1 = dxa_ref[...] + du2 * (1.0 + mod_ref[4:5, :])
        xh, rstd = _ln_norm(z1_ref[...])
        dz1 = _ln_bwd(dx1 * ln_ref[0:1, :], xh, rstd)
        dxb_ref[...] = ALPHA * dz1
        dmix_ref[...] = ((1.0 + mod_ref[2:3, :]) * dz1).astype(BF16)

        @pl.when(i == 0)
        def _():
            st_ref[...] = jnp.zeros_like(st_ref)

        _acc_rows(st_ref, 0, _rowsum(du2 * x1v))
        _acc_rows(st_ref, 1, _rowsum(du2))
        _acc_rows(st_ref, 2, _rowsum(dx1 * xh))
        _acc_rows(st_ref, 3, _rowsum(dx1))
        _acc_rows(st_ref, 4, _rowsum(dz1 * mixed_ref[...]))

    row = lambda w: pl.BlockSpec((tm, w), lambda i: (i, 0))
    vec = pl.BlockSpec((8, D_MODEL), lambda i: (0, 0))
    return pl.pallas_call(
        body, name="ff_bwd", grid=(s // tm,),
        in_specs=[row(D_MODEL), row(D_FF), VMEM_WHOLE, VMEM_WHOLE, row(D_MODEL), row(D_MODEL), row(D_MODEL),
                  row(D_MODEL), vec, vec],
        out_specs=[row(D_FF), row(D_MODEL), row(D_MODEL), vec],
        out_shape=[jax.ShapeDtypeStruct((s, D_FF), BF16), jax.ShapeDtypeStruct((s, D_MODEL), F32),
                   jax.ShapeDtypeStruct((s, D_MODEL), BF16), jax.ShapeDtypeStruct((8, D_MODEL), F32)],
        compiler_params=_params(("arbitrary",), VMEM_BIG),
    )(dff, hid, w1, w2, dxa, x1, z1, mixed, mod, lnv)


def _mix_bwd(dmix, w_out, o, conv, gvec, tm):
    s = o.shape[0]

    def body(dm_ref, w_ref, o_ref, c_ref, g_ref, do_ref, dc_ref, st_ref):
        i = pl.program_id(0)
        dmv = dm_ref[...]

        @pl.when(i == 0)
        def _():
            st_ref[...] = jnp.zeros_like(st_ref)

        for part, (src, dst) in enumerate(((o_ref, do_ref), (c_ref, dc_ref))):
            dr = _dot_nt(dmv, w_ref[part * ATTN_W:(part + 1) * ATTN_W, :])
            v = src[...]
            rr = lax.rsqrt(jnp.mean(v * v, axis=1, keepdims=True) + LN_EPS)
            vh = v * rr
            _acc_rows(st_ref, part, _rowsum(dr * vh))
            t = dr * g_ref[part:part + 1, :]
            dst[...] = rr * (t - vh * jnp.mean(t * vh, axis=1, keepdims=True))

    row = lambda w: pl.BlockSpec((tm, w), lambda i: (i, 0))
    vec = pl.BlockSpec((8, ATTN_W), lambda i: (0, 0))
    return pl.pallas_call(
        body, name="mix_bwd", grid=(s // tm,),
        in_specs=[row(D_MODEL), VMEM_WHOLE, row(ATTN_W), row(CONV_W), vec],
        out_specs=[row(ATTN_W), row(CONV_W), vec],
        out_shape=[jax.ShapeDtypeStruct((s, ATTN_W), F32), jax.ShapeDtypeStruct((s, CONV_W), F32),
                   jax.ShapeDtypeStruct((8, ATTN_W), F32)],
        compiler_params=_params(("arbitrary",), VMEM_MID),
    )(dmix, w_out, o, conv, gvec)


def _conv_bwd(dconv, ypre, ag, wdw, vecs, gmat, tm):
    s = ag.shape[0]
    hb = tm // HALO
    nt = s // tm
    last_halo = s // HALO - 1

    def body(dc_ref, dch_ref, yp_ref, yph_ref, cur_ref, halo_ref, w_ref, vec_ref, g_ref,
             dag_ref, st_ref, dw_ref, bufu, bufd):
        i = pl.program_id(0)

        def dyc_of(yp, dc):
            yh, rstd, yn = _gn_fwd(yp, vec_ref, g_ref)
            sg = _sigmoid(yn)
            dyn = dc * (sg * (1.0 + yn * (1.0 - sg)))
            dyh = dyn * vec_ref[1:2, :]
            dyc = rstd * (dyh - _dot3(dyh, g_ref) - yh * _dot3(dyh * yh, g_ref))
            return dyc, dyn, yh

        dyc, dyn, yh = dyc_of(yp_ref[...], dc_ref[...])
        dych, _, _ = dyc_of(yph_ref[...], dch_ref[...])
        bufd[0:tm, :] = dyc
        bufd[tm:tm + HALO, :] = jnp.where(i < nt - 1, dych, 0.0)

        av = cur_ref[:, 0:CONV_W]
        sg_g = _sigmoid(cur_ref[:, CONV_W:2 * CONV_W])
        ugh = halo_ref[:, 0:CONV_W] * _sigmoid(halo_ref[:, CONV_W:2 * CONV_W])
        bufu[0:HALO, :] = jnp.where(i > 0, ugh, 0.0)
        bufu[HALO:HALO + tm, :] = av * sg_g

        @pl.when(i == 0)
        def _():
            st_ref[...] = jnp.zeros_like(st_ref)
            dw_ref[...] = jnp.zeros_like(dw_ref)

        _acc_rows(st_ref, 0, _rowsum(dyc))
        _acc_rows(st_ref, 1, _rowsum(dyn * yh))
        _acc_rows(st_ref, 2, _rowsum(dyn))

        dug = jnp.zeros((tm, CONV_W), F32)
        for j in range(CONV_K):
            dug = dug + w_ref[j:j + 1, :] * bufd[pl.ds(CONV_K - 1 - j, tm), :]
            _acc_rows(dw_ref, j, _rowsum(dyc * bufu[pl.ds(HALO - (CONV_K - 1) + j, tm), :]))
        dag_ref[:, 0:CONV_W] = (dug * sg_g).astype(BF16)
        dag_ref[:, CONV_W:2 * CONV_W] = (dug * av * sg_g * (1.0 - sg_g)).astype(BF16)

    row = lambda w: pl.BlockSpec((tm, w), lambda i: (i, 0))
    nxt = lambda w: pl.BlockSpec((HALO, w), lambda i: (jnp.minimum((i + 1) * hb, last_halo), 0))
    return pl.pallas_call(
        body, name="conv_bwd", grid=(nt,),
        in_specs=[row(CONV_W), nxt(CONV_W), row(CONV_W), nxt(CONV_W), row(2 * CONV_W),
                  pl.BlockSpec((HALO, 2 * CONV_W), lambda i: (jnp.maximum(i * hb - 1, 0), 0)),
                  pl.BlockSpec((32, CONV_W), lambda i: (0, 0)),
                  pl.BlockSpec((8, CONV_W), lambda i: (0, 0)),
                  pl.BlockSpec((CONV_W, CONV_W), lambda i: (0, 0))],
        out_specs=[row(2 * CONV_W), pl.BlockSpec((8, CONV_W), lambda i: (0, 0)),
                   pl.BlockSpec((32, CONV_W), lambda i: (0, 0))],
        out_shape=[jax.ShapeDtypeStruct((s, 2 * CONV_W), BF16), jax.ShapeDtypeStruct((8, CONV_W), F32),
                   jax.ShapeDtypeStruct((32, CONV_W), F32)],
        scratch_shapes=[pltpu.VMEM((HALO + tm, CONV_W), F32), pltpu.VMEM((tm + HALO, CONV_W), F32)],
        compiler_params=_params(("arbitrary",), VMEM_MID),
    )(dconv, dconv, ypre, ypre, ag, ag, wdw, vecs, gmat)


def _attn_bwd(qkv, cum_t, o, do, lse, tb, grads):
    s = qkv.shape[0]
    nq = s // tb
    n = len(grads)

    def body(q_ref, k_ref, v_ref, cum_ref, o_ref, do_ref, lse_ref, *rest):
        dq_ref, dkt_ref, dvt_ref, dcum_ref, drow_ref = rest[n:n + 5]
        hp = pl.program_id(0)
        qi = pl.program_id(1)
        start, wait = _exchange(rest[:n], rest[n + 5:2 * n + 5], *rest[2 * n + 5:], gather=False)
        pl.when(jnp.logical_and(hp == 0, qi == 0))(start)

        @pl.when(qi == 0)
        def _():
            dkt_ref[...] = jnp.zeros_like(dkt_ref)
            dvt_ref[...] = jnp.zeros_like(dvt_ref)
            dcum_ref[...] = jnp.zeros_like(dcum_ref)

        qs = q_ref[...] * 0.125
        ov = o_ref[...]
        dob = do_ref[...].astype(BF16)
        lane = lax.broadcasted_iota(jnp.int32, (1, 128), 1)
        subl = lax.broadcasted_iota(jnp.int32, (128, 1), 0)
        causal = (lax.broadcasted_iota(jnp.int32, (tb, tb), 1) <= lax.broadcasted_iota(jnp.int32, (tb, tb), 0))
        hmask = (lane < 64, lane >= 64)
        aux = (64, 0)
        ones_aux = [jnp.where(lane == aux[h], 1.0, 0.0) for h in range(2)]
        qm, dom, delta, lse_h, qat, dot_t = [], [], [], [], [], []
        for h in range(2):
            qm.append(jnp.where(hmask[h], qs, jnp.zeros_like(qs)))
            dom.append(jnp.where(hmask[h], dob, jnp.zeros_like(dob)))
            delta.append(jnp.sum(dom[h].astype(F32) * ov, axis=1, keepdims=True))
            lse_h.append(lse_ref[:, h:h + 1])
            qat.append(jnp.where(hmask[h], qs.astype(F32), ones_aux[h]).T.astype(BF16))
            dot_t.append(dom[h].astype(F32).T.astype(BF16))

        def step(kb, carry, masked):
            k0 = pl.multiple_of(kb * tb, tb)
            k2 = k_ref[pl.ds(k0, tb), :]
            v2 = v_ref[pl.ds(k0, tb), :]
            dq_new, a, b = [], [], []
            for h in range(2):
                cs = cum_ref[pl.ds(2 * hp + h, 1), pl.ds(k0, tb)]
                sc = _dot_nt(qm[h], k2) - cs
                p = jnp.exp(sc - lse_h[h])
                if masked:
                    p = jnp.where(causal, p, 0.0)
                dp = _dot_nt(dom[h], v2)
                ds = p * (dp - delta[h])
                pb = p.astype(BF16)
                dsb = ds.astype(BF16)
                a.append(_dot(qat[h], dsb))
                b.append(_dot(dot_t[h], pb))
                ka = jnp.where(hmask[h], k2, ones_aux[h].astype(k2.dtype))
                dq_new.append(carry[h] + _dot(dsb, ka))
            dkt_ref[:, pl.ds(k0, tb)] = dkt_ref[:, pl.ds(k0, tb)] + jnp.where(subl < 64, a[0], a[1])
            dvt_ref[:, pl.ds(k0, tb)] = dvt_ref[:, pl.ds(k0, tb)] + (b[0] + b[1])
            for h in range(2):
                dcum_ref[pl.ds(h, 1), pl.ds(k0, tb)] = (dcum_ref[pl.ds(h, 1), pl.ds(k0, tb)]
                                                        - a[h][aux[h]:aux[h] + 1, :])
            return tuple(dq_new)

        init = (jnp.zeros((tb, 128), F32),) * 2
        carry = lax.fori_loop(0, qi, lambda kb, cr: step(kb, cr, False), init)
        dq0, dq1 = step(qi, carry, True)
        dq_ref[...] = (jnp.where(hmask[0], dq0, dq1) * 0.125).astype(BF16)
        drow_ref[:, 0:1] = dq0[:, aux[0]:aux[0] + 1]
        drow_ref[:, 1:2] = dq1[:, aux[1]:aux[1] + 1]
        pl.when(jnp.logical_and(hp == 3, qi == nq - 1))(wait)

    return pl.pallas_call(
        body, name="attn_bwd", grid=(4, nq),
        in_specs=[pl.BlockSpec((tb, 128), lambda h, i: (i, h)),
                  pl.BlockSpec((s, 128), lambda h, i: (0, 4 + h)),
                  pl.BlockSpec((s, 128), lambda h, i: (0, 8 + h)),
                  pl.BlockSpec((8, s), lambda h, i: (0, 0)),
                  pl.BlockSpec((tb, 128), lambda h, i: (i, h)),
                  pl.BlockSpec((tb, 128), lambda h, i: (i, h)),
                  pl.BlockSpec((None, tb, 2), lambda h, i: (h, i, 0))] + [ANY_SPACE] * n,
        out_specs=[pl.BlockSpec((tb, 128), lambda h, i: (i, h)),
                   pl.BlockSpec((128, s), lambda h, i: (h, 0)),
                   pl.BlockSpec((128, s), lambda h, i: (h, 0)),
                   pl.BlockSpec((None, 8, s), lambda h, i: (h, 0, 0)),
                   pl.BlockSpec((None, tb, 2), lambda h, i: (h, i, 0))] + [ANY_SPACE] * n,
        out_shape=[jax.ShapeDtypeStruct((s, ATTN_W), BF16), jax.ShapeDtypeStruct((ATTN_W, s), F32),
                   jax.ShapeDtypeStruct((ATTN_W, s), F32), jax.ShapeDtypeStruct((4, 8, s), F32),
                   jax.ShapeDtypeStruct((4, s, 2), F32)]
        + [jax.ShapeDtypeStruct(g.shape, g.dtype) for g in grads],
        scratch_shapes=_exchange_scratch(n),
        compiler_params=_params(("arbitrary", "arbitrary"), VMEM_BIG),
    )(qkv, qkv, qkv, cum_t, o, do, lse, *grads)


def _fgate_bwd(dcum_t, drow_t, sneg_t):
    s = dcum_t.shape[1]

    def body(dc_ref, dr_ref, sn_ref, df_ref, db_ref):
        lane = lax.broadcasted_iota(jnp.int32, (8, s), 1)
        acc = dc_ref[...] + dr_ref[...]
        sh = 1
        while sh < s:
            acc = acc + jnp.where(lane < s - sh, pltpu.roll(acc, s - sh, axis=1), 0.0)
            sh *= 2
        df = acc * sn_ref[...]
        df_ref[...] = df
        db_ref[...] = jnp.broadcast_to(jnp.sum(df, axis=1, keepdims=True), (8, 128))

    return pl.pallas_call(
        body, name="fgate_bwd",
        out_shape=[jax.ShapeDtypeStruct((8, s), F32), jax.ShapeDtypeStruct((8, 128), F32)],
        in_specs=[VMEM_WHOLE] * 3, out_specs=[VMEM_WHOLE] * 2,
    )(dcum_t, drow_t, sneg_t)


def _in_proj_bwd(dq, dkt, dvt, dag, dft, wqkv, wkvt, wag, wft, x, dxb, mod, tm):
    s = x.shape[0]

    def body(dq_ref, dkt_ref, dvt_ref, dag_ref, dft_ref, wqkv_ref, wkvt_ref, wag_ref, wft_ref, x_ref, dxb_ref,
             mod_ref, dkvt_ref, dx_ref, st_ref):
        i = pl.program_id(0)
        dkb = dkt_ref[...].astype(BF16)
        dvb = dvt_ref[...].astype(BF16)
        dkvt_ref[0:ATTN_W, :] = dkb
        dkvt_ref[ATTN_W:2 * ATTN_W, :] = dvb
        du = _dot_nt(dq_ref[...], wqkv_ref[:, 0:ATTN_W])
        du = du + _dot_tn(dkb, wkvt_ref[0:ATTN_W, :])
        du = du + _dot_tn(dvb, wkvt_ref[ATTN_W:2 * ATTN_W, :])
        du = du + _dot_nt(dag_ref[...], wag_ref[...])
        du = du + _dot_tn(dft_ref[...].astype(BF16), wft_ref[...])
        xv = x_ref[...]
        dx_ref[...] = dxb_ref[...] + du * (1.0 + mod_ref[1:2, :])

        @pl.when(i == 0)
        def _():
            st_ref[...] = jnp.zeros_like(st_ref)

        _acc_rows(st_ref, 0, _rowsum(du * xv))
        _acc_rows(st_ref, 1, _rowsum(du))

    row = lambda w: pl.BlockSpec((tm, w), lambda i: (i, 0))
    col = lambda h: pl.BlockSpec((h, tm), lambda i: (0, i))
    vec = pl.BlockSpec((8, D_MODEL), lambda i: (0, 0))
    return pl.pallas_call(
        body, name="in_proj_bwd", grid=(s // tm,),
        in_specs=[row(ATTN_W), col(ATTN_W), col(ATTN_W), row(2 * CONV_W), col(8),
                  VMEM_WHOLE, VMEM_WHOLE, VMEM_WHOLE, VMEM_WHOLE, row(D_MODEL), row(D_MODEL), vec],
        out_specs=[col(2 * ATTN_W), row(D_MODEL), vec],
        out_shape=[jax.ShapeDtypeStruct((2 * ATTN_W, s), BF16), jax.ShapeDtypeStruct((s, D_MODEL), F32),
                   jax.ShapeDtypeStruct((8, D_MODEL), F32)],
        compiler_params=_params(("arbitrary",), VMEM_MID),
    )(dq, dkt, dvt, dag, dft, wqkv, wkvt, wag, wft, x, dxb, mod)


def _wgrad_nn(a, b, name, tm, tn, tk):
    m, s = a.shape
    n = b.shape[1]
    nk = s // tk

    def body(a_ref, b_ref, o_ref, acc_ref):
        k = pl.program_id(2)

        @pl.when(k == 0)
        def _():
            acc_ref[...] = jnp.zeros_like(acc_ref)

        acc_ref[...] += _dot(a_ref[...], b_ref[...])

        @pl.when(k == nk - 1)
        def _():
            o_ref[...] = acc_ref[...].astype(o_ref.dtype)

    return pl.pallas_call(
        body, name=name, grid=(m // tm, n // tn, nk),
        in_specs=[pl.BlockSpec((tm, tk), lambda i, j, k: (i, k)), pl.BlockSpec((tk, tn), lambda i, j, k: (k, j))],
        out_specs=pl.BlockSpec((tm, tn), lambda i, j, k: (i, j)),
        out_shape=jax.ShapeDtypeStruct((m, n), BF16),
        scratch_shapes=[pltpu.VMEM((tm, tn), F32)],
        compiler_params=_params(("parallel", "parallel", "arbitrary"), VMEM_MID),
    )(a, b)


def _wgrad_tn(a, b, name, tm, tn, tk, out_shape, out_spec):
    s, m = a.shape
    n = b.shape[1]
    nk = s // tk

    def body(a_ref, b_ref, o_ref, acc_ref):
        k = pl.program_id(2)

        @pl.when(k == 0)
        def _():
            acc_ref[...] = jnp.zeros_like(acc_ref)

        acc_ref[...] += _dot_tn(a_ref[...], b_ref[...])

        @pl.when(k == nk - 1)
        def _():
            o_ref[...] = acc_ref[...].astype(o_ref.dtype)

    return pl.pallas_call(
        body, name=name, grid=(m // tm, n // tn, nk),
        in_specs=[pl.BlockSpec((tk, tm), lambda i, j, k: (k, i)), pl.BlockSpec((tk, tn), lambda i, j, k: (k, j))],
        out_specs=out_spec, out_shape=out_shape,
        scratch_shapes=[pltpu.VMEM((tm, tn), F32)],
        compiler_params=_params(("parallel", "parallel", "arbitrary"), VMEM_MID),
    )(a, b)


def _wgrad_f(dft, u, tk):
    s = u.shape[0]
    nk = s // tk

    def body(d_ref, u_ref, o_ref, acc_ref):
        k = pl.program_id(0)

        @pl.when(k == 0)
        def _():
            acc_ref[...] = jnp.zeros_like(acc_ref)

        acc_ref[...] += _dot(d_ref[...].astype(BF16), u_ref[...])

        @pl.when(k == nk - 1)
        def _():
            o_ref[...] = acc_ref[...].astype(BF16)

    return pl.pallas_call(
        body, name="wgrad_f", grid=(nk,),
        in_specs=[pl.BlockSpec((8, tk), lambda k: (0, k)), pl.BlockSpec((tk, D_MODEL), lambda k: (k, 0))],
        out_specs=pl.BlockSpec((8, D_MODEL), lambda k: (0, 0)),
        out_shape=jax.ShapeDtypeStruct((8, D_MODEL), BF16),
        scratch_shapes=[pltpu.VMEM((8, D_MODEL), F32)],
        compiler_params=_params(("arbitrary",)),
    )(dft, u)


def _adamw_big(recv, w, m, v, name, tr):
    r, cdim = w.shape

    def body(g_ref, w_ref, m_ref, v_ref, go_ref, d_ref, mo_ref, vo_ref):
        g = g_ref[0].astype(F32)
        for k in range(1, N_DEV):
            g = g + g_ref[k].astype(F32)
        delta, m2, v2 = _adamw(w_ref[...], g, m_ref[...], v_ref[...])
        go_ref[...] = g
        d_ref[...] = delta
        mo_ref[...] = m2
        vo_ref[...] = v2

    blk = pl.BlockSpec((tr, cdim), lambda i: (i, 0))
    return pl.pallas_call(
        body, name=name, grid=(r // tr,),
        in_specs=[pl.BlockSpec((N_DEV, tr, cdim), lambda i: (0, i, 0)), blk, blk, blk],
        out_specs=[blk] * 4, out_shape=[jax.ShapeDtypeStruct((r, cdim), F32)] * 4,
        compiler_params=_params(("parallel",), VMEM_MID),
    )(recv, w, m, v)


def _adamw_ada(c_all, dada, w, m, v, tr):
    r, cdim = w.shape

    def body(c_ref, d_ref, w_ref, m_ref, v_ref, go_ref, dl_ref, mo_ref, vo_ref):
        cv = c_ref[...]
        sc = (cv * _sigmoid(cv)).astype(BF16)
        g = _dot_tn(sc, d_ref[...].astype(BF16))
        delta, m2, v2 = _adamw(w_ref[...], g, m_ref[...], v_ref[...])
        go_ref[...] = g
        dl_ref[...] = delta
        mo_ref[...] = m2
        vo_ref[...] = v2

    blk = pl.BlockSpec((tr, cdim), lambda i: (i, 0))
    return pl.pallas_call(
        body, name="adamw_ada", grid=(r // tr,),
        in_specs=[pl.BlockSpec((N_DEV, tr), lambda i: (0, i)), pl.BlockSpec((N_DEV, cdim), lambda i: (0, 0)),
                  blk, blk, blk],
        out_specs=[blk] * 4, out_shape=[jax.ShapeDtypeStruct((r, cdim), F32)] * 4,
        compiler_params=_params(("parallel",), VMEM_MID),
    )(c_all, dada, w, m, v)


def _adamw_small(parts, w, m, v):
    n = w.shape[1]

    def body(g_ref, w_ref, m_ref, v_ref, go_ref, d_ref, mo_ref, vo_ref):
        g = g_ref[0]
        for k in range(1, N_DEV):
            g = g + g_ref[k]
        delta, m2, v2 = _adamw(w_ref[...], g, m_ref[...], v_ref[...])
        go_ref[...] = g
        d_ref[...] = delta
        mo_ref[...] = m2
        vo_ref[...] = v2

    return pl.pallas_call(
        body, name="adamw_small", out_shape=[jax.ShapeDtypeStruct((1, n), F32)] * 4,
        in_specs=[VMEM_WHOLE] * 4, out_specs=[VMEM_WHOLE] * 4,
    )(parts, w, m, v)


def _pad_cols(a, n):
    return jnp.pad(a, ((0, 0), (0, n - a.shape[1])))


def _pad_rows(a, n):
    return jnp.pad(a, ((0, n - a.shape[0]), (0, 0)))


def kernel(x, c, w_ada, b_ada, w_in, b_forget, w_dw, b_dw, gn_g, gn_b, g_attn_out, g_conv_out, w_out, ln1_g, ln1_b, w_ff1, w_ff2, ln2_g, ln2_b, loss_target, m_w_ada, m_b_ada, m_w_in, m_b_forget, m_w_dw, m_b_dw, m_gn_g, m_gn_b, m_g_attn_out, m_g_conv_out, m_w_out, m_ln1_g, m_ln1_b, m_w_ff1, m_w_ff2, m_ln2_g, m_ln2_b, v_w_ada, v_b_ada, v_w_in, v_b_forget, v_w_dw, v_b_dw, v_gn_g, v_gn_b, v_g_attn_out, v_g_conv_out, v_w_out, v_ln1_g, v_ln1_b, v_w_ff1, v_w_ff2, v_ln2_g, v_ln2_b):
    s = x.shape[1]
    tm = min(ROW_TILE, s)
    tb = min(ATT_TILE, s)
    me = 4 * lax.axis_index("x") + 2 * lax.axis_index("y") + lax.axis_index("c")
    xs = x[0]
    tgt = loss_target[0]

    dw_n = CONV_K * DW_SHARD
    pack1 = jnp.concatenate([c, w_dw[0].reshape(1, dw_n), jnp.zeros((1, 2048 - dw_n), F32)], axis=1)
    g1 = _gather_small(pack1, "gather_c")[:, 0, :]
    c_all = g1[:, :D_MODEL]
    wdw_full = g1[:, D_MODEL:D_MODEL + dw_n].reshape(N_DEV, CONV_K, DW_SHARD)
    wdw_full = _pad_rows(wdw_full.transpose(1, 0, 2).reshape(CONV_K, CONV_W), 32)

    b_shard = lax.dynamic_slice(b_ada, (0, me * ADA_SHARD), (1, ADA_SHARD))
    ada_all = _gather_small(_ada_part(c_all, w_ada[0], b_shard), "gather_ada")
    ada = lax.dynamic_index_in_dim(ada_all, me, axis=1, keepdims=False).reshape(6, D_MODEL)
    mod = _pad_rows(ada, 8)

    gw_in, = _gather_weights([_pad_cols(w_in[0], IN_SHARD_PAD).astype(BF16)])
    w_in_full = gw_in[:, :, :IN_SHARD].transpose(1, 0, 2).reshape(D_MODEL, N_IN)
    wqkv = w_in_full[:, :3 * ATTN_W]
    wft = w_in_full[:, 3 * ATTN_W:3 * ATTN_W + N_HEADS].T
    wag = w_in_full[:, 3 * ATTN_W + N_HEADS:]

    qkv, ag, ft, u = _in_proj(xs, mod, wqkv, wag, wft, tm)
    cum_t, sneg_t = _fgate(ft, b_forget.reshape(N_HEADS, 1))
    o, lse, gw_out, gw_ff1, gw_ff2 = _attn_fwd(
        qkv, cum_t, tb, [w_out[0].astype(BF16), w_ff1[0].astype(BF16), w_ff2[0].astype(BF16)])
    w_out_full = gw_out.reshape(D_MODEL, D_MODEL)
    w2_full = gw_ff2.reshape(D_FF, D_MODEL)
    cvec = _pad_rows(jnp.concatenate([b_dw, gn_g, gn_b], axis=0), 8)
    grp = jnp.arange(CONV_W) // 64
    gmat = jnp.where(grp[:, None] == grp[None, :], 1.0 / 64.0, 0.0).astype(BF16)
    ypre, conv = _conv_fwd(ag, wdw_full, cvec, gmat, tm)
    gvec = _pad_rows(jnp.concatenate([g_attn_out, g_conv_out], axis=0), 8)
    lnv = _pad_rows(jnp.concatenate([ln1_g, ln1_b, ln2_g, ln2_b], axis=0), 8)
    cat, mixed, z1, x1 = _mix_fwd(o, conv, xs, w_out_full, gvec, mod, lnv, tm)
    tf = min(FF_TILE, s)
    u2, hid, dff, dxa, st_ff = _ff_fwd(x1, tgt, gw_ff1, w2_full, mod, lnv, tf)

    dh, dxb, dmix, st_fb = _ff_bwd(dff, hid, gw_ff1, w2_full, dxa, x1, z1, mixed, mod, lnv, tf)
    do, dconv, st_mix = _mix_bwd(dmix, w_out_full, o, conv, gvec, tm)
    dag, st_conv, dwdw = _conv_bwd(dconv, ypre, ag, wdw_full, cvec, gmat, tm)

    tk = min(512, s)
    g_ff2 = _wgrad_tn(hid, dff, "wgrad_ff2", 512, 1024, tk,
                      jax.ShapeDtypeStruct((D_FF, D_MODEL), BF16),
                      pl.BlockSpec((512, 1024), lambda i, j, k: (i, j))).reshape(N_DEV, FF_SHARD, D_MODEL)
    g_ff1 = _wgrad_tn(u2, dh, "wgrad_ff1", 1024, 512, tk,
                      jax.ShapeDtypeStruct((N_DEV, D_MODEL, FF_SHARD), BF16),
                      pl.BlockSpec((None, 1024, 512), lambda i, j, k: (j, i, 0)))
    g_out = _wgrad_tn(cat, dmix, "wgrad_out", 512, 1024, tk,
                      jax.ShapeDtypeStruct((D_MODEL, D_MODEL), BF16),
                      pl.BlockSpec((512, 1024), lambda i, j, k: (i, j))).reshape(N_DEV, OUT_SHARD, D_MODEL)
    dq, dkt, dvt, dcum, drow, r_out, r_ff1, r_ff2 = _attn_bwd(qkv, cum_t, o, do, lse, tb, [g_out, g_ff1, g_ff2])
    dft, dbf = _fgate_bwd(dcum[:, :2, :].reshape(N_HEADS, s), drow.transpose(0, 2, 1).reshape(N_HEADS, s), sneg_t)
    wkvt = wqkv[:, ATTN_W:3 * ATTN_W].T
    dkvt, grad_x, st_in = _in_proj_bwd(dq, dkt, dvt, dag, dft, wqkv, wkvt, wag, wft, xs, dxb, mod, tm)

    g_q = _wgrad_tn(u, dq, "wgrad_q", 1024, 512, tk,
                    jax.ShapeDtypeStruct((D_MODEL, ATTN_W), BF16),
                    pl.BlockSpec((1024, 512), lambda i, j, k: (i, j)))
    g_kvt = _wgrad_nn(dkvt, u, "wgrad_kv", 512, 1024, tk)
    g_qkv = jnp.concatenate([g_q, g_kvt.T], axis=1)
    g_ag = _wgrad_tn(u, dag, "wgrad_ag", 1024, 512, tk,
                     jax.ShapeDtypeStruct((D_MODEL, 2 * CONV_W), BF16),
                     pl.BlockSpec((1024, 512), lambda i, j, k: (i, j)))
    g_ft = _wgrad_f(dft, u, tk)
    g_in = jnp.concatenate([g_qkv, g_ft.T, g_ag], axis=1)
    g_in = jnp.pad(g_in.reshape(D_MODEL, N_DEV, IN_SHARD).transpose(1, 0, 2),
                   ((0, 0), (0, 0), (0, IN_SHARD_PAD - IN_SHARD)))

    r_in, = _scatter_grads([g_in])

    dada = jnp.concatenate([st_in[1:2], st_in[0:1], st_fb[4:5], st_fb[1:2], st_fb[0:1], st_ff[2:3]], axis=1)
    loss_part = st_ff[3:4, 0:1]
    pack2 = jnp.concatenate([
        dada,
        _pad_cols(dbf[:, 0].reshape(1, N_HEADS), 128),
        dwdw[:CONV_K].reshape(1, CONV_K * CONV_W),
        st_conv[0:1], st_conv[1:2], st_conv[2:3], st_mix[0:1], st_mix[1:2],
        st_fb[2:3], st_fb[3:4], st_ff[0:1], st_ff[1:2],
        _pad_cols(loss_part, 128)], axis=1)
    g2 = _gather_small(pack2, "gather_small_grads")[:, 0, :]
    o_bf = 6 * D_MODEL
    o_dw = o_bf + 128
    o_v = o_dw + CONV_K * CONV_W
    o_ln = o_v + 5 * CONV_W
    o_loss = o_ln + 4 * D_MODEL
    dw_parts = lax.dynamic_slice_in_dim(
        g2[:, o_dw:o_v].reshape(N_DEV, CONV_K, N_DEV, DW_SHARD), me, 1, axis=2).reshape(N_DEV, dw_n)
    parts = jnp.concatenate([
        g2[:, :o_dw], _pad_cols(dw_parts, 2048), g2[:, o_v:o_loss + 128]], axis=1)[:, None, :]

    def pack_w(b_ada_, b_forget_, w_dw_, smalls):
        return jnp.concatenate([b_ada_, _pad_cols(b_forget_, 128), _pad_cols(w_dw_[0].reshape(1, dw_n), 2048)]
                               + smalls + [jnp.zeros((1, 128), F32)], axis=1)

    pw = pack_w(b_ada, b_forget, w_dw, [b_dw, gn_g, gn_b, g_attn_out, g_conv_out, ln1_g, ln1_b, ln2_g, ln2_b])
    pm = pack_w(m_b_ada, m_b_forget, m_w_dw, [m_b_dw, m_gn_g, m_gn_b, m_g_attn_out, m_g_conv_out,
                                              m_ln1_g, m_ln1_b, m_ln2_g, m_ln2_b])
    pv = pack_w(v_b_ada, v_b_forget, v_w_dw, [v_b_dw, v_gn_g, v_gn_b, v_g_attn_out, v_g_conv_out,
                                              v_ln1_g, v_ln1_b, v_ln2_g, v_ln2_b])
    small = _adamw_small(parts, pw, pm, pv)

    p_bf = 6 * D_MODEL
    p_dw = p_bf + 128
    p_v = p_dw + 2048
    p_ln = p_v + 5 * CONV_W
    p_loss = p_ln + 4 * D_MODEL

    def unpack(t):
        outs = {"b_ada": t[:, :p_bf], "b_forget": t[:, p_bf:p_bf + N_HEADS],
                "w_dw": t[:, p_dw:p_dw + dw_n].reshape(1, CONV_K, 1, DW_SHARD)}
        for k, nm in enumerate(["b_dw", "gn_g", "gn_b", "g_attn_out", "g_conv_out"]):
            outs[nm] = t[:, p_v + k * CONV_W:p_v + (k + 1) * CONV_W]
        for k, nm in enumerate(["ln1_g", "ln1_b", "ln2_g", "ln2_b"]):
            outs[nm] = t[:, p_ln + k * D_MODEL:p_ln + (k + 1) * D_MODEL]
        return outs

    sm = [unpack(t) for t in small]
    loss = small[0][0, p_loss]

    dada_all = g2[:, :6 * D_MODEL]
    dada_shard = lax.dynamic_slice_in_dim(dada_all, me * ADA_SHARD, ADA_SHARD, axis=1)
    big = {
        "w_ada": _adamw_ada(c_all, dada_shard, w_ada[0], m_w_ada[0], v_w_ada[0], 256),
        "w_in": [t[:, :IN_SHARD] for t in _adamw_big(
            r_in, _pad_cols(w_in[0], IN_SHARD_PAD), _pad_cols(m_w_in[0], IN_SHARD_PAD),
            _pad_cols(v_w_in[0], IN_SHARD_PAD), "adamw_in", 256)],
        "w_out": _adamw_big(r_out, w_out[0], m_w_out[0], v_w_out[0], "adamw_out", 128),
        "w_ff1": _adamw_big(r_ff1, w_ff1[0], m_w_ff1[0], v_w_ff1[0], "adamw_ff1", 256),
        "w_ff2": _adamw_big(r_ff2, w_ff2[0], m_w_ff2[0], v_w_ff2[0], "adamw_ff2", 256),
    }

    names = ["w_ada", "b_ada", "w_in", "b_forget", "w_dw", "b_dw", "gn_g", "gn_b", "g_attn_out", "g_conv_out",
             "w_out", "ln1_g", "ln1_b", "w_ff1", "w_ff2", "ln2_g", "ln2_b"]

    def leaf(kind, nm):
        if nm in big:
            return big[nm][kind][None]
        return sm[kind][nm]

    outs = [loss, grad_x[None]]
    for kind in range(4):
        outs += [leaf(kind, nm) for nm in names]
    return tuple(outs)
```

```python
import jax
import jax.numpy as jnp
from jax import lax
from jax.experimental import pallas as pl
from jax.experimental.pallas import tpu as pltpu

F32 = jnp.float32
BF16 = jnp.bfloat16

D_MODEL = 1024
ATTN_W = 512
CONV_W = 512
N_HEADS = 8
CONV_K = 31
D_FF = 4096
N_DEV = 8
N_IN = 3 * ATTN_W + N_HEADS + 2 * CONV_W
IN_SHARD = N_IN // N_DEV
IN_SHARD_PAD = 384
ADA_SHARD = 6 * D_MODEL // N_DEV
FF_SHARD = D_FF // N_DEV
OUT_SHARD = D_MODEL // N_DEV
DW_SHARD = CONV_W // N_DEV
HALO = 32
LN_EPS = 1e-5
ALPHA = 2.0 ** 0.25
ADAM_LR, ADAM_B1, ADAM_B2, ADAM_EPS, ADAM_WD, ADAM_STEP = 0.001, 0.9, 0.999, 1e-08, 0.01, 10

ROW_TILE = 512
FF_TILE = 256
ATT_TILE = 512
ATT_Q_BLOCKS = 2
VMEM_BIG = 56 * 1024 * 1024
VMEM_MID = 40 * 1024 * 1024

MESH = pl.DeviceIdType.MESH
VMEM_WHOLE = pl.BlockSpec(memory_space=pltpu.VMEM)
ANY_SPACE = pl.BlockSpec(memory_space=pl.ANY)


def _dot(a, b):
    return jnp.dot(a, b, preferred_element_type=F32)


def _dot_nt(a, b):
    return lax.dot_general(a, b, (((1,), (1,)), ((), ())), preferred_element_type=F32)


def _dot_tn(a, b):
    return lax.dot_general(a, b, (((0,), (0,)), ((), ())), preferred_element_type=F32)


def _dot3(xf, g_ref):
    g = g_ref[...]
    hi = xf.astype(BF16)
    r1 = xf - hi.astype(F32)
    mid = r1.astype(BF16)
    lo = (r1 - mid.astype(F32)).astype(BF16)
    return _dot(hi, g) + _dot(mid, g) + _dot(lo, g)


def _sigmoid(x):
    return 1.0 / (1.0 + jnp.exp(-x))


def _rowsum(x):
    return jnp.sum(x, axis=0, keepdims=True)


def _ln_norm(z):
    mu = jnp.mean(z, axis=1, keepdims=True)
    zc = z - mu
    var = jnp.mean(zc * zc, axis=1, keepdims=True)
    rstd = lax.rsqrt(var + LN_EPS)
    return zc * rstd, rstd


def _ln_bwd(dxh, xh, rstd):
    m1 = jnp.mean(dxh, axis=1, keepdims=True)
    m2 = jnp.mean(dxh * xh, axis=1, keepdims=True)
    return rstd * (dxh - m1 - xh * m2)


def _adamw(w, g, m, v):
    m2 = ADAM_B1 * m + (1.0 - ADAM_B1) * g
    v2 = ADAM_B2 * v + (1.0 - ADAM_B2) * (g * g)
    m_hat = m2 / (1.0 - ADAM_B1 ** ADAM_STEP)
    v_hat = v2 / (1.0 - ADAM_B2 ** ADAM_STEP)
    delta = -ADAM_LR * (m_hat / (jnp.sqrt(v_hat) + ADAM_EPS) + ADAM_WD * w)
    return delta, m2, v2


def _params(sem=None, vmem=None):
    kw = {}
    if sem is not None:
        kw["dimension_semantics"] = sem
    if vmem is not None:
        kw["vmem_limit_bytes"] = vmem
    return pltpu.CompilerParams(**kw)


def _coords():
    return lax.axis_index("x"), lax.axis_index("y"), lax.axis_index("c")


def _flip(v, bit):
    return 1 - v if bit else v


def _acc_rows(ref, r, val):
    ref[r:r + 1, :] = ref[r:r + 1, :] + val


def _gather_small(v, name):
    r, n = v.shape

    def body(v_ref, out_ref, send_sems, recv_sems, local_sem):
        x, y, c = _coords()
        me = 4 * x + 2 * y + c
        mine = pltpu.make_async_copy(v_ref, out_ref.at[me], local_sem)
        mine.start()
        sends = []
        for k in range(1, N_DEV):
            tgt = (_flip(x, (k >> 2) & 1), _flip(y, (k >> 1) & 1), _flip(c, k & 1))
            cp = pltpu.make_async_remote_copy(
                src_ref=v_ref, dst_ref=out_ref.at[me], send_sem=send_sems.at[k - 1],
                recv_sem=recv_sems.at[k - 1], device_id=tgt, device_id_type=MESH)
            cp.start()
            sends.append(cp)
        for k in range(1, N_DEV):
            px, py, pc = _flip(x, (k >> 2) & 1), _flip(y, (k >> 1) & 1), _flip(c, k & 1)
            peer = 4 * px + 2 * py + pc
            pltpu.make_async_remote_copy(
                src_ref=v_ref, dst_ref=out_ref.at[peer], send_sem=send_sems.at[k - 1],
                recv_sem=recv_sems.at[k - 1], device_id=(px, py, pc), device_id_type=MESH).wait_recv()
        for cp in sends:
            cp.wait_send()
        mine.wait()

    return pl.pallas_call(
        body, name=name,
        out_shape=jax.ShapeDtypeStruct((N_DEV, r, n), v.dtype),
        in_specs=[VMEM_WHOLE], out_specs=VMEM_WHOLE,
        scratch_shapes=[pltpu.SemaphoreType.DMA((N_DEV - 1,)), pltpu.SemaphoreType.DMA((N_DEV - 1,)),
                        pltpu.SemaphoreType.DMA],
    )(v)


def _gather_weights(shards):
    n = len(shards)

    def body(*refs):
        ins, outs = refs[:n], refs[n:2 * n]
        send_sems, recv_sems, local_sems = refs[2 * n:]
        x, y, c = _coords()
        me, sibling = (x, y, c), (x, y, 1 - c)
        chips = [(1 - x, y), (x, 1 - y), (1 - x, 1 - y)]

        def slot(a, pos):
            return outs[a].at[4 * pos[0] + 2 * pos[1] + pos[2]]

        def copy(a, k, block, to, src=None):
            return pltpu.make_async_remote_copy(
                src_ref=slot(a, block) if src is None else src, dst_ref=slot(a, block),
                send_sem=send_sems.at[7 * a + k], recv_sem=recv_sems.at[7 * a + k],
                device_id=to, device_id_type=MESH)

        started = []
        local = []
        for a in range(n):
            mine = pltpu.make_async_copy(ins[a], slot(a, me), local_sems.at[a])
            mine.start()
            local.append(mine)
        for j, chip in enumerate(chips):
            for a in range(n):
                cp = copy(a, 1 + j, me, (*chip, c), src=ins[a])
                cp.start()
                started.append(cp)
        for a in range(n):
            cp = copy(a, 0, me, sibling, src=ins[a])
            cp.start()
            started.append(cp)
        for j, chip in enumerate(chips):
            for a in range(n):
                copy(a, 1 + j, (*chip, c), me).wait_recv()
                cp = copy(a, 4 + j, (*chip, c), sibling)
                cp.start()
                started.append(cp)
        for a in range(n):
            copy(a, 0, sibling, me).wait_recv()
        for j, chip in enumerate(chips):
            for a in range(n):
                copy(a, 4 + j, (*chip, 1 - c), me).wait_recv()
        for cp in started:
            cp.wait_send()
        for mine in local:
            mine.wait()

    return pl.pallas_call(
        body, name="gather_weights",
        out_shape=[jax.ShapeDtypeStruct((N_DEV,) + s.shape, s.dtype) for s in shards],
        in_specs=[ANY_SPACE] * n, out_specs=[ANY_SPACE] * n,
        scratch_shapes=[pltpu.SemaphoreType.DMA((7 * n,)), pltpu.SemaphoreType.DMA((7 * n,)),
                        pltpu.SemaphoreType.DMA((n,))],
    )(*shards)


def _exchange(ins, outs, send_sems, recv_sems, local_sems, gather):
    n = len(ins)
    x, y, c = _coords()
    me = 4 * x + 2 * y + c
    local = [pltpu.make_async_copy(ins[a] if gather else ins[a].at[me], outs[a].at[me], local_sems.at[a])
             for a in range(n)]
    sends, recvs = [], []
    for k in range(1, N_DEV):
        px, py, pc = _flip(x, (k >> 2) & 1), _flip(y, (k >> 1) & 1), _flip(c, k & 1)
        peer = 4 * px + 2 * py + pc
        for a in range(n):
            src = ins[a] if gather else ins[a].at[peer]
            sems = dict(send_sem=send_sems.at[7 * a + k - 1], recv_sem=recv_sems.at[7 * a + k - 1],
                        device_id=(px, py, pc), device_id_type=MESH)
            sends.append(pltpu.make_async_remote_copy(src_ref=src, dst_ref=outs[a].at[me], **sems))
            recvs.append(pltpu.make_async_remote_copy(src_ref=src, dst_ref=outs[a].at[peer], **sems))

    def start():
        for cp in local + sends:
            cp.start()

    def wait():
        for cp in recvs:
            cp.wait_recv()
        for cp in sends:
            cp.wait_send()
        for cp in local:
            cp.wait()

    return start, wait


def _exchange_scratch(n):
    return [pltpu.SemaphoreType.DMA((7 * n,)), pltpu.SemaphoreType.DMA((7 * n,)), pltpu.SemaphoreType.DMA((n,))]


def _scatter_grads(grads):
    n = len(grads)

    def body(*refs):
        start, wait = _exchange(refs[:n], refs[n:2 * n], *refs[2 * n:], gather=False)
        start()
        wait()

    return pl.pallas_call(
        body, name="scatter_grads",
        out_shape=[jax.ShapeDtypeStruct(g.shape, g.dtype) for g in grads],
        in_specs=[ANY_SPACE] * n, out_specs=[ANY_SPACE] * n,
        scratch_shapes=_exchange_scratch(n),
    )(*grads)


def _ada_part(c_all, w_ada, b_shard):
    def body(c_ref, w_ref, b_ref, o_ref):
        cv = c_ref[...]
        sc = (cv * _sigmoid(cv)).astype(BF16)
        o_ref[...] = _dot(sc, w_ref[...].astype(BF16)) + b_ref[...]

    return pl.pallas_call(
        body, name="ada_part", out_shape=jax.ShapeDtypeStruct((N_DEV, ADA_SHARD), F32),
        in_specs=[VMEM_WHOLE] * 3, out_specs=VMEM_WHOLE,
        compiler_params=_params(vmem=VMEM_MID),
    )(c_all, w_ada, b_shard)


def _in_proj(x, mod, wqkv, wag, wft, tm):
    s = x.shape[0]

    def body(x_ref, mod_ref, wqkv_ref, wag_ref, wft_ref, qkv_ref, ag_ref, ft_ref, u_ref):
        u = x_ref[...] * (1.0 + mod_ref[1:2, :]) + mod_ref[0:1, :]
        ub = u.astype(BF16)
        u_ref[...] = ub
        qkv_ref[...] = _dot(ub, wqkv_ref[...]).astype(BF16)
        ag_ref[...] = _dot(ub, wag_ref[...])
        ft_ref[...] = _dot_nt(wft_ref[...], ub)

    return pl.pallas_call(
        body, name="in_proj", grid=(s // tm,),
        in_specs=[pl.BlockSpec((tm, D_MODEL), lambda i: (i, 0)), pl.BlockSpec((8, D_MODEL), lambda i: (0, 0)),
                  VMEM_WHOLE, VMEM_WHOLE, VMEM_WHOLE],
        out_specs=[pl.BlockSpec((tm, 3 * ATTN_W), lambda i: (i, 0)),
                   pl.BlockSpec((tm, 2 * CONV_W), lambda i: (i, 0)),
                   pl.BlockSpec((8, tm), lambda i: (0, i)),
                   pl.BlockSpec((tm, D_MODEL), lambda i: (i, 0))],
        out_shape=[jax.ShapeDtypeStruct((s, 3 * ATTN_W), BF16), jax.ShapeDtypeStruct((s, 2 * CONV_W), F32),
                   jax.ShapeDtypeStruct((8, s), F32), jax.ShapeDtypeStruct((s, D_MODEL), BF16)],
        compiler_params=_params(("parallel",), VMEM_MID),
    )(x, mod, wqkv, wag, wft)


def _fgate(ft, bcol):
    s = ft.shape[1]

    def body(f_ref, b_ref, cum_ref, sneg_ref):
        z = f_ref[...] + b_ref[...]
        e = jnp.exp(-jnp.abs(z))
        l1p = jnp.where(e < 1e-2, e * (1.0 - e * (0.5 - e * (1.0 / 3.0))), jnp.log(1.0 + e))
        logf = jnp.minimum(z, 0.0) - l1p
        r = 1.0 / (1.0 + e)
        sneg_ref[...] = jnp.where(z >= 0, e * r, r)
        lane = lax.broadcasted_iota(jnp.int32, (8, s), 1)
        acc = logf
        sh = 1
        while sh < s:
            acc = acc + jnp.where(lane >= sh, pltpu.roll(acc, sh, axis=1), 0.0)
            sh *= 2
        cum_ref[...] = acc

    return pl.pallas_call(
        body, name="fgate", out_shape=[jax.ShapeDtypeStruct((8, s), F32)] * 2,
        in_specs=[VMEM_WHOLE] * 2, out_specs=[VMEM_WHOLE] * 2,
    )(ft, bcol)


def _attn_fwd(qkv, cum_t, tq, tk, shards):
    s = qkv.shape[0]
    nq = s // tq
    r = tq // tk
    n = len(shards)

    def body(q_ref, k_ref, v_ref, cum_ref, *rest):
        o_ref, lse_ref = rest[n:n + 2]
        hp = pl.program_id(0)
        qi = pl.program_id(1)
        start, wait = _exchange(rest[:n], rest[n + 2:2 * n + 2], *rest[2 * n + 2:], gather=True)
        pl.when(jnp.logical_and(hp == 0, qi == 0))(start)
        q2 = q_ref[...]
        lane = lax.broadcasted_iota(jnp.int32, (1, 128), 1)
        col_minus_row = (lax.broadcasted_iota(jnp.int32, (tq, tk), 1) - lax.broadcasted_iota(jnp.int32, (tq, tk), 0))
        hmask = (lane < 64, lane >= 64)
        aux = (64, 0)
        qm = [jnp.where(hmask[h], q2, jnp.zeros_like(q2)) * 0.125 for h in range(2)]

        def step(kb, carry, masked):
            k0 = pl.multiple_of(kb * tk, tk)
            k2 = k_ref[pl.ds(k0, tk), :]
            v2 = v_ref[pl.ds(k0, tk), :]
            out = []
            for h in range(2):
                m, acc = carry[2 * h], carry[2 * h + 1]
                sc = _dot_nt(qm[h], k2) - cum_ref[pl.ds(2 * hp + h, 1), pl.ds(k0, tk)]
                if masked:
                    sc = jnp.where(col_minus_row <= qi * tq - k0, sc, -jnp.inf)
                m_new = jnp.maximum(m, jnp.max(sc, axis=1, keepdims=True))
                p = jnp.exp(sc - m_new)
                corr = jnp.exp(m - m_new)
                va = jnp.where(hmask[h], v2, jnp.where(lane == aux[h], 1.0, 0.0).astype(v2.dtype))
                out += [m_new, acc * corr + _dot(p.astype(BF16), va)]
            return tuple(out)

        init = (jnp.full((tq, 1), -jnp.inf, F32), jnp.zeros((tq, 128), F32)) * 2
        carry = lax.fori_loop(0, qi * r, lambda kb, cr: step(kb, cr, False), init)
        m0, acc0, m1, acc1 = lax.fori_loop(qi * r, qi * r + r, lambda kb, cr: step(kb, cr, True), carry)
        l0 = acc0[:, aux[0]:aux[0] + 1]
        l1 = acc1[:, aux[1]:aux[1] + 1]
        o_ref[...] = jnp.where(hmask[0], acc0 / l0, acc1 / l1)
        lse_ref[:, 0:1] = m0 + jnp.log(l0)
        lse_ref[:, 1:2] = m1 + jnp.log(l1)
        pl.when(jnp.logical_and(hp == 3, qi == nq - 1))(wait)

    return pl.pallas_call(
        body, name="attn_fwd", grid=(4, nq),
        in_specs=[pl.BlockSpec((tq, 128), lambda h, i: (i, h)),
                  pl.BlockSpec((s, 128), lambda h, i: (0, 4 + h)),
                  pl.BlockSpec((s, 128), lambda h, i: (0, 8 + h)),
                  pl.BlockSpec((8, s), lambda h, i: (0, 0))] + [ANY_SPACE] * n,
        out_specs=[pl.BlockSpec((tq, 128), lambda h, i: (i, h)),
                   pl.BlockSpec((None, tq, 2), lambda h, i: (h, i, 0))] + [ANY_SPACE] * n,
        out_shape=[jax.ShapeDtypeStruct((s, ATTN_W), F32), jax.ShapeDtypeStruct((4, s, 2), F32)]
        + [jax.ShapeDtypeStruct((N_DEV,) + w.shape, w.dtype) for w in shards],
        scratch_shapes=_exchange_scratch(n),
        compiler_params=_params(("arbitrary", "arbitrary"), VMEM_BIG),
    )(qkv, qkv, qkv, cum_t, *shards)


def _gn_fwd(y, vec_ref, g_ref):
    mean = _dot3(y, g_ref)
    yc = y - mean
    var = _dot3(yc * yc, g_ref)
    rstd = lax.rsqrt(var + LN_EPS)
    yh = yc * rstd
    yn = yh * vec_ref[1:2, :] + vec_ref[2:3, :]
    return yh, rstd, yn


SHIFT_ROWS = HALO - 8


def _shifted_copies(buf, sh, tm):
    for b in range(1, 8):
        sh[b - 1] = buf[pl.ds(b, tm + SHIFT_ROWS), :]


def _window(buf, sh, off, tm):
    a, b = divmod(off, 8)
    if b == 0:
        return buf[pl.ds(8 * a, tm), :]
    return sh[b - 1, pl.ds(8 * a, tm), :]


def _conv_fwd(ag, wdw, vecs, gmat, tm):
    s = ag.shape[0]
    hb = tm // HALO

    def body(cur_ref, halo_ref, w_ref, vec_ref, g_ref, ypre_ref, conv_ref, buf, sh):
        i = pl.program_id(0)
        ug = cur_ref[:, 0:CONV_W] * _sigmoid(cur_ref[:, CONV_W:2 * CONV_W])
        ugh = halo_ref[:, 0:CONV_W] * _sigmoid(halo_ref[:, CONV_W:2 * CONV_W])
        buf[0:HALO, :] = jnp.where(i > 0, ugh, 0.0)
        buf[HALO:HALO + tm, :] = ug
        _shifted_copies(buf, sh, tm)
        y = jnp.broadcast_to(vec_ref[0:1, :], (tm, CONV_W))
        for j in range(CONV_K):
            y = y + w_ref[j:j + 1, :] * _window(buf, sh, HALO - (CONV_K - 1) + j, tm)
        ypre_ref[...] = y
        _, _, yn = _gn_fwd(y, vec_ref, g_ref)
        conv_ref[...] = yn * _sigmoid(yn)

    return pl.pallas_call(
        body, name="conv_fwd", grid=(s // tm,),
        in_specs=[pl.BlockSpec((tm, 2 * CONV_W), lambda i: (i, 0)),
                  pl.BlockSpec((HALO, 2 * CONV_W), lambda i: (jnp.maximum(i * hb - 1, 0), 0)),
                  pl.BlockSpec((32, CONV_W), lambda i: (0, 0)),
                  pl.BlockSpec((8, CONV_W), lambda i: (0, 0)),
                  pl.BlockSpec((CONV_W, CONV_W), lambda i: (0, 0))],
        out_specs=[pl.BlockSpec((tm, CONV_W), lambda i: (i, 0))] * 2,
        out_shape=[jax.ShapeDtypeStruct((s, CONV_W), F32)] * 2,
        scratch_shapes=[pltpu.VMEM((HALO + tm, CONV_W), F32), pltpu.VMEM((7, tm + SHIFT_ROWS, CONV_W), F32)],
        compiler_params=_params(("parallel",), VMEM_MID),
    )(ag, ag, wdw, vecs, gmat)


def _mix_fwd(o, conv, x, w_out, gvec, mod, lnv, tm):
    s = x.shape[0]

    def body(o_ref, c_ref, x_ref, w_ref, g_ref, mod_ref, ln_ref, cat_ref, mixed_ref, z1_ref, x1_ref):
        ov = o_ref[...]
        cv = c_ref[...]
        ra = ov * lax.rsqrt(jnp.mean(ov * ov, axis=1, keepdims=True) + LN_EPS) * g_ref[0:1, :]
        rc = cv * lax.rsqrt(jnp.mean(cv * cv, axis=1, keepdims=True) + LN_EPS) * g_ref[1:2, :]
        rab = ra.astype(BF16)
        rcb = rc.astype(BF16)
        cat_ref[:, 0:ATTN_W] = rab
        cat_ref[:, ATTN_W:D_MODEL] = rcb
        mixed = _dot(rab, w_ref[0:ATTN_W, :]) + _dot(rcb, w_ref[ATTN_W:D_MODEL, :])
        mixed_ref[...] = mixed
        z1 = ALPHA * x_ref[...] + (1.0 + mod_ref[2:3, :]) * mixed
        z1_ref[...] = z1
        xh, _ = _ln_norm(z1)
        x1_ref[...] = xh * ln_ref[0:1, :] + ln_ref[1:2, :]

    row = lambda w: pl.BlockSpec((tm, w), lambda i: (i, 0))
    return pl.pallas_call(
        body, name="mix_fwd", grid=(s // tm,),
        in_specs=[row(ATTN_W), row(CONV_W), row(D_MODEL), VMEM_WHOLE,
                  pl.BlockSpec((8, ATTN_W), lambda i: (0, 0)), pl.BlockSpec((8, D_MODEL), lambda i: (0, 0)),
                  pl.BlockSpec((8, D_MODEL), lambda i: (0, 0))],
        out_specs=[row(D_MODEL)] * 4,
        out_shape=[jax.ShapeDtypeStruct((s, D_MODEL), BF16)] + [jax.ShapeDtypeStruct((s, D_MODEL), F32)] * 3,
        compiler_params=_params(("parallel",), VMEM_MID),
    )(o, conv, x, w_out, gvec, mod, lnv)


def _ff_fwd(x1, tgt, w1, w2, mod, lnv, tm):
    s = x1.shape[0]

    def body(x1_ref, t_ref, w1_ref, w2_ref, mod_ref, ln_ref, u2_ref, hid_ref, dff_ref, dxa_ref, st_ref):
        i = pl.program_id(0)
        x1v = x1_ref[...]
        u2 = (x1v * (1.0 + mod_ref[4:5, :]) + mod_ref[3:4, :]).astype(BF16)
        u2_ref[...] = u2
        ff = jnp.zeros((tm, D_MODEL), F32)
        for d in range(N_DEV):
            h = _dot(u2, w1_ref[d])
            hr = jnp.maximum(h, 0.0)
            hb = (hr * hr).astype(BF16)
            hid_ref[:, d * FF_SHARD:(d + 1) * FF_SHARD] = hb
            ff = ff + _dot(hb, w2_ref[d * FF_SHARD:(d + 1) * FF_SHARD, :])
        gate = 1.0 + mod_ref[5:6, :]
        z2 = ALPHA * x1v + gate * ff
        xh, rstd = _ln_norm(z2)
        y = xh * ln_ref[2:3, :] + ln_ref[3:4, :]
        diff = y - t_ref[...]
        dy = diff * (1.0 / D_MODEL)
        dz2 = _ln_bwd(dy * ln_ref[2:3, :], xh, rstd)
        dff_ref[...] = (gate * dz2).astype(BF16)
        dxa_ref[...] = ALPHA * dz2

        @pl.when(i == 0)
        def _():
            st_ref[...] = jnp.zeros_like(st_ref)

        _acc_rows(st_ref, 0, _rowsum(dy * xh))
        _acc_rows(st_ref, 1, _rowsum(dy))
        _acc_rows(st_ref, 2, _rowsum(dz2 * ff))
        sq = _rowsum(jnp.sum(diff * diff, axis=1, keepdims=True))
        _acc_rows(st_ref, 3, jnp.broadcast_to(sq * (0.5 / D_MODEL), (1, D_MODEL)))

    row = lambda w: pl.BlockSpec((tm, w), lambda i: (i, 0))
    vec = pl.BlockSpec((8, D_MODEL), lambda i: (0, 0))
    return pl.pallas_call(
        body, name="ff_fwd", grid=(s // tm,),
        in_specs=[row(D_MODEL), row(D_MODEL), VMEM_WHOLE, VMEM_WHOLE, vec, vec],
        out_specs=[row(D_MODEL), row(D_FF), row(D_MODEL), row(D_MODEL), vec],
        out_shape=[jax.ShapeDtypeStruct((s, D_MODEL), BF16), jax.ShapeDtypeStruct((s, D_FF), BF16),
                   jax.ShapeDtypeStruct((s, D_MODEL), BF16), jax.ShapeDtypeStruct((s, D_MODEL), F32),
                   jax.ShapeDtypeStruct((8, D_MODEL), F32)],
        compiler_params=_params(("arbitrary",), VMEM_BIG),
    )(x1, tgt, w1, w2, mod, lnv)


def _ff_bwd(dff, hid, w1, w2, dxa, x1, z1, mixed, mod, lnv, tm):
    s = x1.shape[0]

    def body(dff_ref, hid_ref, w1_ref, w2_ref, dxa_ref, x1_ref, z1_ref, mixed_ref, mod_ref, ln_ref,
             dh_ref, dxb_ref, dmix_ref, st_ref):
        i = pl.program_id(0)
        dffv = dff_ref[...]
        du2 = jnp.zeros((tm, D_MODEL), F32)
        for d in range(N_DEV):
            cols = slice(d * FF_SHARD, (d + 1) * FF_SHARD)
            dhid = _dot_nt(dffv, w2_ref[cols, :])
            dh = (dhid * (2.0 * jnp.sqrt(hid_ref[:, cols].astype(F32)))).astype(BF16)
            dh_ref[:, cols] = dh
            du2 = du2 + _dot_nt(dh, w1_ref[d])
        x1v = x1_ref[...]
        dx1 = dxa_ref[...] + du2 * (1.0 + mod_ref[4:5, :])
        xh, rstd = _ln_norm(z1_ref[...])
        dz1 = _ln_bwd(dx1 * ln_ref[0:1, :], xh, rstd)
        dxb_ref[...] = ALPHA * dz1
        dmix_ref[...] = ((1.0 + mod_ref[2:3, :]) * dz1).astype(BF16)

        @pl.when(i == 0)
        def _():
            st_ref[...] = jnp.zeros_like(st_ref)

        _acc_rows(st_ref, 0, _rowsum(du2 * x1v))
        _acc_rows(st_ref, 1, _rowsum(du2))
        _acc_rows(st_ref, 2, _rowsum(dx1 * xh))
        _acc_rows(st_ref, 3, _rowsum(dx1))
        _acc_rows(st_ref, 4, _rowsum(dz1 * mixed_ref[...]))

    row = lambda w: pl.BlockSpec((tm, w), lambda i: (i, 0))
    vec = pl.BlockSpec((8, D_MODEL), lambda i: (0, 0))
    return pl.pallas_call(
        body, name="ff_bwd", grid=(s // tm,),
        in_specs=[row(D_MODEL), row(D_FF), VMEM_WHOLE, VMEM_WHOLE, row(D_MODEL), row(D_MODEL), row(D_MODEL),
                  row(D_MODEL), vec, vec],
        out_specs=[row(D_FF), row(D_MODEL), row(D_MODEL), vec],
        out_shape=[jax.ShapeDtypeStruct((s, D_FF), BF16), jax.ShapeDtypeStruct((s, D_MODEL), F32),
                   jax.ShapeDtypeStruct((s, D_MODEL), BF16), jax.ShapeDtypeStruct((8, D_MODEL), F32)],
        compiler_params=_params(("arbitrary",), VMEM_BIG),
    )(dff, hid, w1, w2, dxa, x1, z1, mixed, mod, lnv)


def _mix_bwd(dmix, w_out, o, conv, gvec, tm):
    s = o.shape[0]

    def body(dm_ref, w_ref, o_ref, c_ref, g_ref, do_ref, dc_ref, st_ref):
        i = pl.program_id(0)
        dmv = dm_ref[...]

        @pl.when(i == 0)
        def _():
            st_ref[...] = jnp.zeros_like(st_ref)

        for part, (src, dst) in enumerate(((o_ref, do_ref), (c_ref, dc_ref))):
            dr = _dot_nt(dmv, w_ref[part * ATTN_W:(part + 1) * ATTN_W, :])
            v = src[...]
            rr = lax.rsqrt(jnp.mean(v * v, axis=1, keepdims=True) + LN_EPS)
            vh = v * rr
            _acc_rows(st_ref, part, _rowsum(dr * vh))
            t = dr * g_ref[part:part + 1, :]
            dst[...] = rr * (t - vh * jnp.mean(t * vh, axis=1, keepdims=True))

    row = lambda w: pl.BlockSpec((tm, w), lambda i: (i, 0))
    vec = pl.BlockSpec((8, ATTN_W), lambda i: (0, 0))
    return pl.pallas_call(
        body, name="mix_bwd", grid=(s // tm,),
        in_specs=[row(D_MODEL), VMEM_WHOLE, row(ATTN_W), row(CONV_W), vec],
        out_specs=[row(ATTN_W), row(CONV_W), vec],
        out_shape=[jax.ShapeDtypeStruct((s, ATTN_W), F32), jax.ShapeDtypeStruct((s, CONV_W), F32),
                   jax.ShapeDtypeStruct((8, ATTN_W), F32)],
        compiler_params=_params(("arbitrary",), VMEM_MID),
    )(dmix, w_out, o, conv, gvec)


def _conv_bwd(dconv, ypre, ag, wdw, vecs, gmat, tm):
    s = ag.shape[0]
    hb = tm // HALO
    nt = s // tm
    last_halo = s // HALO - 1

    def body(dc_ref, dch_ref, yp_ref, yph_ref, cur_ref, halo_ref, w_ref, vec_ref, g_ref,
             dag_ref, st_ref, dw_ref, bufu, bufd, shu, shd):
        i = pl.program_id(0)

        def dyc_of(yp, dc):
            yh, rstd, yn = _gn_fwd(yp, vec_ref, g_ref)
            sg = _sigmoid(yn)
            dyn = dc * (sg * (1.0 + yn * (1.0 - sg)))
            dyh = dyn * vec_ref[1:2, :]
            dyc = rstd * (dyh - _dot3(dyh, g_ref) - yh * _dot3(dyh * yh, g_ref))
            return dyc, dyn, yh

        dyc, dyn, yh = dyc_of(yp_ref[...], dc_ref[...])
        dych, _, _ = dyc_of(yph_ref[...], dch_ref[...])
        bufd[0:tm, :] = dyc
        bufd[tm:tm + HALO, :] = jnp.where(i < nt - 1, dych, 0.0)

        av = cur_ref[:, 0:CONV_W]
        sg_g = _sigmoid(cur_ref[:, CONV_W:2 * CONV_W])
        ugh = halo_ref[:, 0:CONV_W] * _sigmoid(halo_ref[:, CONV_W:2 * CONV_W])
        bufu[0:HALO, :] = jnp.where(i > 0, ugh, 0.0)
        bufu[HALO:HALO + tm, :] = av * sg_g

        @pl.when(i == 0)
        def _():
            st_ref[...] = jnp.zeros_like(st_ref)
            dw_ref[...] = jnp.zeros_like(dw_ref)

        _acc_rows(st_ref, 0, _rowsum(dyc))
        _acc_rows(st_ref, 1, _rowsum(dyn * yh))
        _acc_rows(st_ref, 2, _rowsum(dyn))

        _shifted_copies(bufd, shd, tm)
        _shifted_copies(bufu, shu, tm)
        dug = jnp.zeros((tm, CONV_W), F32)
        for j in range(CONV_K):
            dug = dug + w_ref[j:j + 1, :] * _window(bufd, shd, CONV_K - 1 - j, tm)
            _acc_rows(dw_ref, j, _rowsum(bufd[0:tm, :] * _window(bufu, shu, HALO - (CONV_K - 1) + j, tm)))
        dag_ref[:, 0:CONV_W] = (dug * sg_g).astype(BF16)
        dag_ref[:, CONV_W:2 * CONV_W] = (dug * av * sg_g * (1.0 - sg_g)).astype(BF16)

    row = lambda w: pl.BlockSpec((tm, w), lambda i: (i, 0))
    nxt = lambda w: pl.BlockSpec((HALO, w), lambda i: (jnp.minimum((i + 1) * hb, last_halo), 0))
    return pl.pallas_call(
        body, name="conv_bwd", grid=(nt,),
        in_specs=[row(CONV_W), nxt(CONV_W), row(CONV_W), nxt(CONV_W), row(2 * CONV_W),
                  pl.BlockSpec((HALO, 2 * CONV_W), lambda i: (jnp.maximum(i * hb - 1, 0), 0)),
                  pl.BlockSpec((32, CONV_W), lambda i: (0, 0)),
                  pl.BlockSpec((8, CONV_W), lambda i: (0, 0)),
                  pl.BlockSpec((CONV_W, CONV_W), lambda i: (0, 0))],
        out_specs=[row(2 * CONV_W), pl.BlockSpec((8, CONV_W), lambda i: (0, 0)),
                   pl.BlockSpec((32, CONV_W), lambda i: (0, 0))],
        out_shape=[jax.ShapeDtypeStruct((s, 2 * CONV_W), BF16), jax.ShapeDtypeStruct((8, CONV_W), F32),
                   jax.ShapeDtypeStruct((32, CONV_W), F32)],
        scratch_shapes=[pltpu.VMEM((HALO + tm, CONV_W), F32), pltpu.VMEM((tm + HALO, CONV_W), F32),
                        pltpu.VMEM((7, tm + SHIFT_ROWS, CONV_W), F32), pltpu.VMEM((7, tm + SHIFT_ROWS, CONV_W), F32)],
        compiler_params=_params(("arbitrary",), VMEM_BIG),
    )(dconv, dconv, ypre, ypre, ag, ag, wdw, vecs, gmat)


def _attn_bwd(qkv, cum_t, o, do, lse, tq, tk, grads):
    s = qkv.shape[0]
    nq = s // tq
    r = tq // tk
    n = len(grads)

    def body(q_ref, k_ref, v_ref, cum_ref, o_ref, do_ref, lse_ref, *rest):
        dq_ref, dkt_ref, dvt_ref, dcum_ref, drow_ref = rest[n:n + 5]
        hp = pl.program_id(0)
        qi = pl.program_id(1)
        start, wait = _exchange(rest[:n], rest[n + 5:2 * n + 5], *rest[2 * n + 5:], gather=False)
        pl.when(jnp.logical_and(hp == 0, qi == 0))(start)

        @pl.when(qi == 0)
        def _():
            dkt_ref[...] = jnp.zeros_like(dkt_ref)
            dvt_ref[...] = jnp.zeros_like(dvt_ref)
            dcum_ref[...] = jnp.zeros_like(dcum_ref)

        qs = q_ref[...] * 0.125
        ov = o_ref[...]
        dob = do_ref[...].astype(BF16)
        lane = lax.broadcasted_iota(jnp.int32, (1, 128), 1)
        subl = lax.broadcasted_iota(jnp.int32, (128, 1), 0)
        col_minus_row = (lax.broadcasted_iota(jnp.int32, (tq, tk), 1) - lax.broadcasted_iota(jnp.int32, (tq, tk), 0))
        hmask = (lane < 64, lane >= 64)
        aux = (64, 0)
        ones_aux = [jnp.where(lane == aux[h], 1.0, 0.0) for h in range(2)]
        qm, dom, delta, lse_h, qat, dot_t = [], [], [], [], [], []
        for h in range(2):
            qm.append(jnp.where(hmask[h], qs, jnp.zeros_like(qs)))
            dom.append(jnp.where(hmask[h], dob, jnp.zeros_like(dob)))
            delta.append(jnp.sum(dom[h].astype(F32) * ov, axis=1, keepdims=True))
            lse_h.append(lse_ref[:, h:h + 1])
            qat.append(jnp.where(hmask[h], qs.astype(F32), ones_aux[h]).T.astype(BF16))
            dot_t.append(dom[h].astype(F32).T.astype(BF16))

        def step(kb, carry, masked):
            k0 = pl.multiple_of(kb * tk, tk)
            k2 = k_ref[pl.ds(k0, tk), :]
            v2 = v_ref[pl.ds(k0, tk), :]
            dq_new, a, b = [], [], []
            for h in range(2):
                cs = cum_ref[pl.ds(2 * hp + h, 1), pl.ds(k0, tk)]
                sc = _dot_nt(qm[h], k2) - cs
                p = jnp.exp(sc - lse_h[h])
                if masked:
                    p = jnp.where(col_minus_row <= qi * tq - k0, p, 0.0)
                dp = _dot_nt(dom[h], v2)
                ds = p * (dp - delta[h])
                pb = p.astype(BF16)
                dsb = ds.astype(BF16)
                a.append(_dot(qat[h], dsb))
                b.append(_dot(dot_t[h], pb))
                ka = jnp.where(hmask[h], k2, ones_aux[h].astype(k2.dtype))
                dq_new.append(carry[h] + _dot(dsb, ka))
            dkt_ref[:, pl.ds(k0, tk)] = dkt_ref[:, pl.ds(k0, tk)] + jnp.where(subl < 64, a[0], a[1])
            dvt_ref[:, pl.ds(k0, tk)] = dvt_ref[:, pl.ds(k0, tk)] + (b[0] + b[1])
            for h in range(2):
                dcum_ref[pl.ds(h, 1), pl.ds(k0, tk)] = (dcum_ref[pl.ds(h, 1), pl.ds(k0, tk)]
                                                        - a[h][aux[h]:aux[h] + 1, :])
            return tuple(dq_new)

        init = (jnp.zeros((tq, 128), F32),) * 2
        carry = lax.fori_loop(0, qi * r, lambda kb, cr: step(kb, cr, False), init)
        dq0, dq1 = lax.fori_loop(qi * r, qi * r + r, lambda kb, cr: step(kb, cr, True), carry)
        dq_ref[...] = (jnp.where(hmask[0], dq0, dq1) * 0.125).astype(BF16)
        drow_ref[:, 0:1] = dq0[:, aux[0]:aux[0] + 1]
        drow_ref[:, 1:2] = dq1[:, aux[1]:aux[1] + 1]
        pl.when(jnp.logical_and(hp == 3, qi == nq - 1))(wait)

    return pl.pallas_call(
        body, name="attn_bwd", grid=(4, nq),
        in_specs=[pl.BlockSpec((tq, 128), lambda h, i: (i, h)),
                  pl.BlockSpec((s, 128), lambda h, i: (0, 4 + h)),
                  pl.BlockSpec((s, 128), lambda h, i: (0, 8 + h)),
                  pl.BlockSpec((8, s), lambda h, i: (0, 0)),
                  pl.BlockSpec((tq, 128), lambda h, i: (i, h)),
                  pl.BlockSpec((tq, 128), lambda h, i: (i, h)),
                  pl.BlockSpec((None, tq, 2), lambda h, i: (h, i, 0))] + [ANY_SPACE] * n,
        out_specs=[pl.BlockSpec((tq, 128), lambda h, i: (i, h)),
                   pl.BlockSpec((128, s), lambda h, i: (h, 0)),
                   pl.BlockSpec((128, s), lambda h, i: (h, 0)),
                   pl.BlockSpec((None, 8, s), lambda h, i: (h, 0, 0)),
                   pl.BlockSpec((None, tq, 2), lambda h, i: (h, i, 0))] + [ANY_SPACE] * n,
        out_shape=[jax.ShapeDtypeStruct((s, ATTN_W), BF16), jax.ShapeDtypeStruct((ATTN_W, s), F32),
                   jax.ShapeDtypeStruct((ATTN_W, s), F32), jax.ShapeDtypeStruct((4, 8, s), F32),
                   jax.ShapeDtypeStruct((4, s, 2), F32)]
        + [jax.ShapeDtypeStruct(g.shape, g.dtype) for g in grads],
        scratch_shapes=_exchange_scratch(n),
        compiler_params=_params(("arbitrary", "arbitrary"), VMEM_BIG),
    )(qkv, qkv, qkv, cum_t, o, do, lse, *grads)


def _fgate_bwd(dcum_t, drow_t, sneg_t):
    s = dcum_t.shape[1]

    def body(dc_ref, dr_ref, sn_ref, df_ref, db_ref):
        lane = lax.broadcasted_iota(jnp.int32, (8, s), 1)
        acc = dc_ref[...] + dr_ref[...]
        sh = 1
        while sh < s:
            acc = acc + jnp.where(lane < s - sh, pltpu.roll(acc, s - sh, axis=1), 0.0)
            sh *= 2
        df = acc * sn_ref[...]
        df_ref[...] = df
        db_ref[...] = jnp.broadcast_to(jnp.sum(df, axis=1, keepdims=True), (8, 128))

    return pl.pallas_call(
        body, name="fgate_bwd",
        out_shape=[jax.ShapeDtypeStruct((8, s), F32), jax.ShapeDtypeStruct((8, 128), F32)],
        in_specs=[VMEM_WHOLE] * 3, out_specs=[VMEM_WHOLE] * 2,
    )(dcum_t, drow_t, sneg_t)


def _in_proj_bwd(dq, dkt, dvt, dag, dft, wqkv, wkvt, wag, wft, x, dxb, mod, tm):
    s = x.shape[0]

    def body(dq_ref, dkt_ref, dvt_ref, dag_ref, dft_ref, wqkv_ref, wkvt_ref, wag_ref, wft_ref, x_ref, dxb_ref,
             mod_ref, dkvt_ref, dx_ref, st_ref):
        i = pl.program_id(0)
        dkb = dkt_ref[...].astype(BF16)
        dvb = dvt_ref[...].astype(BF16)
        dkvt_ref[0:ATTN_W, :] = dkb
        dkvt_ref[ATTN_W:2 * ATTN_W, :] = dvb
        du = _dot_nt(dq_ref[...], wqkv_ref[:, 0:ATTN_W])
        du = du + _dot_tn(dkb, wkvt_ref[0:ATTN_W, :])
        du = du + _dot_tn(dvb, wkvt_ref[ATTN_W:2 * ATTN_W, :])
        du = du + _dot_nt(dag_ref[...], wag_ref[...])
        du = du + _dot_tn(dft_ref[...].astype(BF16), wft_ref[...])
        xv = x_ref[...]
        dx_ref[...] = dxb_ref[...] + du * (1.0 + mod_ref[1:2, :])

        @pl.when(i == 0)
        def _():
            st_ref[...] = jnp.zeros_like(st_ref)

        _acc_rows(st_ref, 0, _rowsum(du * xv))
        _acc_rows(st_ref, 1, _rowsum(du))

    row = lambda w: pl.BlockSpec((tm, w), lambda i: (i, 0))
    col = lambda h: pl.BlockSpec((h, tm), lambda i: (0, i))
    vec = pl.BlockSpec((8, D_MODEL), lambda i: (0, 0))
    return pl.pallas_call(
        body, name="in_proj_bwd", grid=(s // tm,),
        in_specs=[row(ATTN_W), col(ATTN_W), col(ATTN_W), row(2 * CONV_W), col(8),
                  VMEM_WHOLE, VMEM_WHOLE, VMEM_WHOLE, VMEM_WHOLE, row(D_MODEL), row(D_MODEL), vec],
        out_specs=[col(2 * ATTN_W), row(D_MODEL), vec],
        out_shape=[jax.ShapeDtypeStruct((2 * ATTN_W, s), BF16), jax.ShapeDtypeStruct((s, D_MODEL), F32),
                   jax.ShapeDtypeStruct((8, D_MODEL), F32)],
        compiler_params=_params(("arbitrary",), VMEM_MID),
    )(dq, dkt, dvt, dag, dft, wqkv, wkvt, wag, wft, x, dxb, mod)


def _wgrad_nn(a, b, name, tm, tn, tk):
    m, s = a.shape
    n = b.shape[1]
    nk = s // tk

    def body(a_ref, b_ref, o_ref, acc_ref):
        k = pl.program_id(2)

        @pl.when(k == 0)
        def _():
            acc_ref[...] = jnp.zeros_like(acc_ref)

        acc_ref[...] += _dot(a_ref[...], b_ref[...])

        @pl.when(k == nk - 1)
        def _():
            o_ref[...] = acc_ref[...].astype(o_ref.dtype)

    return pl.pallas_call(
        body, name=name, grid=(m // tm, n // tn, nk),
        in_specs=[pl.BlockSpec((tm, tk), lambda i, j, k: (i, k)), pl.BlockSpec((tk, tn), lambda i, j, k: (k, j))],
        out_specs=pl.BlockSpec((tm, tn), lambda i, j, k: (i, j)),
        out_shape=jax.ShapeDtypeStruct((m, n), BF16),
        scratch_shapes=[pltpu.VMEM((tm, tn), F32)],
        compiler_params=_params(("parallel", "parallel", "arbitrary"), VMEM_MID),
    )(a, b)


def _wgrad_tn(a, b, name, tm, tn, tk, out_shape, out_spec):
    s, m = a.shape
    n = b.shape[1]
    nk = s // tk

    def body(a_ref, b_ref, o_ref, acc_ref):
        k = pl.program_id(2)

        @pl.when(k == 0)
        def _():
            acc_ref[...] = jnp.zeros_like(acc_ref)

        acc_ref[...] += _dot_tn(a_ref[...], b_ref[...])

        @pl.when(k == nk - 1)
        def _():
            o_ref[...] = acc_ref[...].astype(o_ref.dtype)

    return pl.pallas_call(
        body, name=name, grid=(m // tm, n // tn, nk),
        in_specs=[pl.BlockSpec((tk, tm), lambda i, j, k: (k, i)), pl.BlockSpec((tk, tn), lambda i, j, k: (k, j))],
        out_specs=out_spec, out_shape=out_shape,
        scratch_shapes=[pltpu.VMEM((tm, tn), F32)],
        compiler_params=_params(("parallel", "parallel", "arbitrary"), VMEM_MID),
    )(a, b)


def _wgrad_f(dft, u, tk):
    s = u.shape[0]
    nk = s // tk

    def body(d_ref, u_ref, o_ref, acc_ref):
        k = pl.program_id(0)

        @pl.when(k == 0)
        def _():
            acc_ref[...] = jnp.zeros_like(acc_ref)

        acc_ref[...] += _dot(d_ref[...].astype(BF16), u_ref[...])

        @pl.when(k == nk - 1)
        def _():
            o_ref[...] = acc_ref[...].astype(BF16)

    return pl.pallas_call(
        body, name="wgrad_f", grid=(nk,),
        in_specs=[pl.BlockSpec((8, tk), lambda k: (0, k)), pl.BlockSpec((tk, D_MODEL), lambda k: (k, 0))],
        out_specs=pl.BlockSpec((8, D_MODEL), lambda k: (0, 0)),
        out_shape=jax.ShapeDtypeStruct((8, D_MODEL), BF16),
        scratch_shapes=[pltpu.VMEM((8, D_MODEL), F32)],
        compiler_params=_params(("arbitrary",)),
    )(dft, u)


def _adamw_big(recv, w, m, v, name, tr):
    r, cdim = w.shape

    def body(g_ref, w_ref, m_ref, v_ref, go_ref, d_ref, mo_ref, vo_ref):
        g = g_ref[0].astype(F32)
        for k in range(1, N_DEV):
            g = g + g_ref[k].astype(F32)
        delta, m2, v2 = _adamw(w_ref[...], g, m_ref[...], v_ref[...])
        go_ref[...] = g
        d_ref[...] = delta
        mo_ref[...] = m2
        vo_ref[...] = v2

    blk = pl.BlockSpec((tr, cdim), lambda i: (i, 0))
    return pl.pallas_call(
        body, name=name, grid=(r // tr,),
        in_specs=[pl.BlockSpec((N_DEV, tr, cdim), lambda i: (0, i, 0)), blk, blk, blk],
        out_specs=[blk] * 4, out_shape=[jax.ShapeDtypeStruct((r, cdim), F32)] * 4,
        compiler_params=_params(("parallel",), VMEM_MID),
    )(recv, w, m, v)


def _adamw_ada(c_all, dada, w, m, v, tr):
    r, cdim = w.shape

    def body(c_ref, d_ref, w_ref, m_ref, v_ref, go_ref, dl_ref, mo_ref, vo_ref):
        cv = c_ref[...]
        sc = (cv * _sigmoid(cv)).astype(BF16)
        g = _dot_tn(sc, d_ref[...].astype(BF16))
        delta, m2, v2 = _adamw(w_ref[...], g, m_ref[...], v_ref[...])
        go_ref[...] = g
        dl_ref[...] = delta
        mo_ref[...] = m2
        vo_ref[...] = v2

    blk = pl.BlockSpec((tr, cdim), lambda i: (i, 0))
    return pl.pallas_call(
        body, name="adamw_ada", grid=(r // tr,),
        in_specs=[pl.BlockSpec((N_DEV, tr), lambda i: (0, i)), pl.BlockSpec((N_DEV, cdim), lambda i: (0, 0)),
                  blk, blk, blk],
        out_specs=[blk] * 4, out_shape=[jax.ShapeDtypeStruct((r, cdim), F32)] * 4,
        compiler_params=_params(("parallel",), VMEM_MID),
    )(c_all, dada, w, m, v)


def _adamw_small(parts, w, m, v):
    n = w.shape[1]

    def body(g_ref, w_ref, m_ref, v_ref, go_ref, d_ref, mo_ref, vo_ref):
        g = g_ref[0]
        for k in range(1, N_DEV):
            g = g + g_ref[k]
        delta, m2, v2 = _adamw(w_ref[...], g, m_ref[...], v_ref[...])
        go_ref[...] = g
        d_ref[...] = delta
        mo_ref[...] = m2
        vo_ref[...] = v2

    return pl.pallas_call(
        body, name="adamw_small", out_shape=[jax.ShapeDtypeStruct((1, n), F32)] * 4,
        in_specs=[VMEM_WHOLE] * 4, out_specs=[VMEM_WHOLE] * 4,
    )(parts, w, m, v)


def _pad_cols(a, n):
    return jnp.pad(a, ((0, 0), (0, n - a.shape[1])))


def _pad_rows(a, n):
    return jnp.pad(a, ((0, n - a.shape[0]), (0, 0)))


def kernel(x, c, w_ada, b_ada, w_in, b_forget, w_dw, b_dw, gn_g, gn_b, g_attn_out, g_conv_out, w_out, ln1_g, ln1_b, w_ff1, w_ff2, ln2_g, ln2_b, loss_target, m_w_ada, m_b_ada, m_w_in, m_b_forget, m_w_dw, m_b_dw, m_gn_g, m_gn_b, m_g_attn_out, m_g_conv_out, m_w_out, m_ln1_g, m_ln1_b, m_w_ff1, m_w_ff2, m_ln2_g, m_ln2_b, v_w_ada, v_b_ada, v_w_in, v_b_forget, v_w_dw, v_b_dw, v_gn_g, v_gn_b, v_g_attn_out, v_g_conv_out, v_w_out, v_ln1_g, v_ln1_b, v_w_ff1, v_w_ff2, v_ln2_g, v_ln2_b):
    s = x.shape[1]
    tm = min(ROW_TILE, s)
    tk_att = min(ATT_TILE, s)
    tq_att = min(ATT_Q_BLOCKS * tk_att, s)
    me = 4 * lax.axis_index("x") + 2 * lax.axis_index("y") + lax.axis_index("c")
    xs = x[0]
    tgt = loss_target[0]

    dw_n = CONV_K * DW_SHARD
    pack1 = jnp.concatenate([c, w_dw[0].reshape(1, dw_n), jnp.zeros((1, 2048 - dw_n), F32)], axis=1)
    g1 = _gather_small(pack1, "gather_c")[:, 0, :]
    c_all = g1[:, :D_MODEL]
    wdw_full = g1[:, D_MODEL:D_MODEL + dw_n].reshape(N_DEV, CONV_K, DW_SHARD)
    wdw_full = _pad_rows(wdw_full.transpose(1, 0, 2).reshape(CONV_K, CONV_W), 32)

    b_shard = lax.dynamic_slice(b_ada, (0, me * ADA_SHARD), (1, ADA_SHARD))
    ada_all = _gather_small(_ada_part(c_all, w_ada[0], b_shard), "gather_ada")
    ada = lax.dynamic_index_in_dim(ada_all, me, axis=1, keepdims=False).reshape(6, D_MODEL)
    mod = _pad_rows(ada, 8)

    gw_in, = _gather_weights([_pad_cols(w_in[0], IN_SHARD_PAD).astype(BF16)])
    w_in_full = gw_in[:, :, :IN_SHARD].transpose(1, 0, 2).reshape(D_MODEL, N_IN)
    wqkv = w_in_full[:, :3 * ATTN_W]
    wft = w_in_full[:, 3 * ATTN_W:3 * ATTN_W + N_HEADS].T
    wag = w_in_full[:, 3 * ATTN_W + N_HEADS:]

    qkv, ag, ft, u = _in_proj(xs, mod, wqkv, wag, wft, tm)
    cum_t, sneg_t = _fgate(ft, b_forget.reshape(N_HEADS, 1))
    o, lse, gw_out, gw_ff1, gw_ff2 = _attn_fwd(
        qkv, cum_t, tq_att, tk_att, [w_out[0].astype(BF16), w_ff1[0].astype(BF16), w_ff2[0].astype(BF16)])
    w_out_full = gw_out.reshape(D_MODEL, D_MODEL)
    w2_full = gw_ff2.reshape(D_FF, D_MODEL)
    cvec = _pad_rows(jnp.concatenate([b_dw, gn_g, gn_b], axis=0), 8)
    grp = jnp.arange(CONV_W) // 64
    gmat = jnp.where(grp[:, None] == grp[None, :], 1.0 / 64.0, 0.0).astype(BF16)
    ypre, conv = _conv_fwd(ag, wdw_full, cvec, gmat, tm)
    gvec = _pad_rows(jnp.concatenate([g_attn_out, g_conv_out], axis=0), 8)
    lnv = _pad_rows(jnp.concatenate([ln1_g, ln1_b, ln2_g, ln2_b], axis=0), 8)
    cat, mixed, z1, x1 = _mix_fwd(o, conv, xs, w_out_full, gvec, mod, lnv, tm)
    tf = min(FF_TILE, s)
    u2, hid, dff, dxa, st_ff = _ff_fwd(x1, tgt, gw_ff1, w2_full, mod, lnv, tf)

    dh, dxb, dmix, st_fb = _ff_bwd(dff, hid, gw_ff1, w2_full, dxa, x1, z1, mixed, mod, lnv, tf)
    do, dconv, st_mix = _mix_bwd(dmix, w_out_full, o, conv, gvec, tm)
    dag, st_conv, dwdw = _conv_bwd(dconv, ypre, ag, wdw_full, cvec, gmat, tm)

    tk = min(512, s)
    g_ff2 = _wgrad_tn(hid, dff, "wgrad_ff2", 512, 1024, tk,
                      jax.ShapeDtypeStruct((D_FF, D_MODEL), BF16),
                      pl.BlockSpec((512, 1024), lambda i, j, k: (i, j))).reshape(N_DEV, FF_SHARD, D_MODEL)
    g_ff1 = _wgrad_tn(u2, dh, "wgrad_ff1", 1024, 512, tk,
                      jax.ShapeDtypeStruct((N_DEV, D_MODEL, FF_SHARD), BF16),
                      pl.BlockSpec((None, 1024, 512), lambda i, j, k: (j, i, 0)))
    g_out = _wgrad_tn(cat, dmix, "wgrad_out", 512, 1024, tk,
                      jax.ShapeDtypeStruct((D_MODEL, D_MODEL), BF16),
                      pl.BlockSpec((512, 1024), lambda i, j, k: (i, j))).reshape(N_DEV, OUT_SHARD, D_MODEL)
    dq, dkt, dvt, dcum, drow, r_out, r_ff1, r_ff2 = _attn_bwd(
        qkv, cum_t, o, do, lse, tq_att, tk_att, [g_out, g_ff1, g_ff2])
    dft, dbf = _fgate_bwd(dcum[:, :2, :].reshape(N_HEADS, s), drow.transpose(0, 2, 1).reshape(N_HEADS, s), sneg_t)
    wkvt = wqkv[:, ATTN_W:3 * ATTN_W].T
    dkvt, grad_x, st_in = _in_proj_bwd(dq, dkt, dvt, dag, dft, wqkv, wkvt, wag, wft, xs, dxb, mod, tm)

    g_q = _wgrad_tn(u, dq, "wgrad_q", 1024, 512, tk,
                    jax.ShapeDtypeStruct((D_MODEL, ATTN_W), BF16),
                    pl.BlockSpec((1024, 512), lambda i, j, k: (i, j)))
    g_kvt = _wgrad_nn(dkvt, u, "wgrad_kv", 512, 1024, tk)
    g_qkv = jnp.concatenate([g_q, g_kvt.T], axis=1)
    g_ag = _wgrad_tn(u, dag, "wgrad_ag", 1024, 512, tk,
                     jax.ShapeDtypeStruct((D_MODEL, 2 * CONV_W), BF16),
                     pl.BlockSpec((1024, 512), lambda i, j, k: (i, j)))
    g_ft = _wgrad_f(dft, u, tk)
    g_in = jnp.concatenate([g_qkv, g_ft.T, g_ag], axis=1)
    g_in = jnp.pad(g_in.reshape(D_MODEL, N_DEV, IN_SHARD).transpose(1, 0, 2),
                   ((0, 0), (0, 0), (0, IN_SHARD_PAD - IN_SHARD)))

    r_in, = _scatter_grads([g_in])

    dada = jnp.concatenate([st_in[1:2], st_in[0:1], st_fb[4:5], st_fb[1:2], st_fb[0:1], st_ff[2:3]], axis=1)
    loss_part = st_ff[3:4, 0:1]
    pack2 = jnp.concatenate([
        dada,
        _pad_cols(dbf[:, 0].reshape(1, N_HEADS), 128),
        dwdw[:CONV_K].reshape(1, CONV_K * CONV_W),
        st_conv[0:1], st_conv[1:2], st_conv[2:3], st_mix[0:1], st_mix[1:2],
        st_fb[2:3], st_fb[3:4], st_ff[0:1], st_ff[1:2],
        _pad_cols(loss_part, 128)], axis=1)
    g2 = _gather_small(pack2, "gather_small_grads")[:, 0, :]
    o_bf = 6 * D_MODEL
    o_dw = o_bf + 128
    o_v = o_dw + CONV_K * CONV_W
    o_ln = o_v + 5 * CONV_W
    o_loss = o_ln + 4 * D_MODEL
    dw_parts = lax.dynamic_slice_in_dim(
        g2[:, o_dw:o_v].reshape(N_DEV, CONV_K, N_DEV, DW_SHARD), me, 1, axis=2).reshape(N_DEV, dw_n)
    parts = jnp.concatenate([
        g2[:, :o_dw], _pad_cols(dw_parts, 2048), g2[:, o_v:o_loss + 128]], axis=1)[:, None, :]

    def pack_w(b_ada_, b_forget_, w_dw_, smalls):
        return jnp.concatenate([b_ada_, _pad_cols(b_forget_, 128), _pad_cols(w_dw_[0].reshape(1, dw_n), 2048)]
                               + smalls + [jnp.zeros((1, 128), F32)], axis=1)

    pw = pack_w(b_ada, b_forget, w_dw, [b_dw, gn_g, gn_b, g_attn_out, g_conv_out, ln1_g, ln1_b, ln2_g, ln2_b])
    pm = pack_w(m_b_ada, m_b_forget, m_w_dw, [m_b_dw, m_gn_g, m_gn_b, m_g_attn_out, m_g_conv_out,
                                              m_ln1_g, m_ln1_b, m_ln2_g, m_ln2_b])
    pv = pack_w(v_b_ada, v_b_forget, v_w_dw, [v_b_dw, v_gn_g, v_gn_b, v_g_attn_out, v_g_conv_out,
                                              v_ln1_g, v_ln1_b, v_ln2_g, v_ln2_b])
    small = _adamw_small(parts, pw, pm, pv)

    p_bf = 6 * D_MODEL
    p_dw = p_bf + 128
    p_v = p_dw + 2048
    p_ln = p_v + 5 * CONV_W
    p_loss = p_ln + 4 * D_MODEL

    def unpack(t):
        outs = {"b_ada": t[:, :p_bf], "b_forget": t[:, p_bf:p_bf + N_HEADS],
                "w_dw": t[:, p_dw:p_dw + dw_n].reshape(1, CONV_K, 1, DW_SHARD)}
        for k, nm in enumerate(["b_dw", "gn_g", "gn_b", "g_attn_out", "g_conv_out"]):
            outs[nm] = t[:, p_v + k * CONV_W:p_v + (k + 1) * CONV_W]
        for k, nm in enumerate(["ln1_g", "ln1_b", "ln2_g", "ln2_b"]):
            outs[nm] = t[:, p_ln + k * D_MODEL:p_ln + (k + 1) * D_MODEL]
        return outs

    sm = [unpack(t) for t in small]
    loss = small[0][0, p_loss]

    dada_all = g2[:, :6 * D_MODEL]
    dada_shard = lax.dynamic_slice_in_dim(dada_all, me * ADA_SHARD, ADA_SHARD, axis=1)
    big = {
        "w_ada": _adamw_ada(c_all, dada_shard, w_ada[0], m_w_ada[0], v_w_ada[0], 256),
        "w_in": [t[:, :IN_SHARD] for t in _adamw_big(
            r_in, _pad_cols(w_in[0], IN_SHARD_PAD), _pad_cols(m_w_in[0], IN_SHARD_PAD),
            _pad_cols(v_w_in[0], IN_SHARD_PAD), "adamw_in", 256)],
        "w_out": _adamw_big(r_out, w_out[0], m_w_out[0], v_w_out[0], "adamw_out", 128),
        "w_ff1": _adamw_big(r_ff1, w_ff1[0], m_w_ff1[0], v_w_ff1[0], "adamw_ff1", 256),
        "w_ff2": _adamw_big(r_ff2, w_ff2[0], m_w_ff2[0], v_w_ff2[0], "adamw_ff2", 256),
    }

    names = ["w_ada", "b_ada", "w_in", "b_forget", "w_dw", "b_dw", "gn_g", "gn_b", "g_attn_out", "g_conv_out",
             "w_out", "ln1_g", "ln1_b", "w_ff1", "w_ff2", "ln2_g", "ln2_b"]

    def leaf(kind, nm):
        if nm in big:
            return big[nm][kind][None]
        return sm[kind][nm]

    outs = [loss, grad_x[None]]
    for kind in range(4):
        outs += [leaf(kind, nm) for nm in names]
    return tuple(outs)
```

```python
import jax
import jax.numpy as jnp
from jax import lax
from jax.experimental import pallas as pl
from jax.experimental.pallas import tpu as pltpu

F32 = jnp.float32
BF16 = jnp.bfloat16

D_MODEL = 1024
ATTN_W = 512
CONV_W = 512
N_HEADS = 8
CONV_K = 31
D_FF = 4096
N_DEV = 8
N_IN = 3 * ATTN_W + N_HEADS + 2 * CONV_W
IN_SHARD = N_IN // N_DEV
IN_SHARD_PAD = 384
ADA_SHARD = 6 * D_MODEL // N_DEV
FF_SHARD = D_FF // N_DEV
OUT_SHARD = D_MODEL // N_DEV
DW_SHARD = CONV_W // N_DEV
HALO = 32
LN_EPS = 1e-5
ALPHA = 2.0 ** 0.25
ADAM_LR, ADAM_B1, ADAM_B2, ADAM_EPS, ADAM_WD, ADAM_STEP = 0.001, 0.9, 0.999, 1e-08, 0.01, 10

ROW_TILE = 512
FF_TILE = 256
WGRAD_K_TILE = 2048
ATT_TILE = 512
ATT_Q_BLOCKS = 2
VMEM_BIG = 56 * 1024 * 1024
VMEM_MID = 40 * 1024 * 1024

MESH = pl.DeviceIdType.MESH
VMEM_WHOLE = pl.BlockSpec(memory_space=pltpu.VMEM)
ANY_SPACE = pl.BlockSpec(memory_space=pl.ANY)


def _dot(a, b):
    return jnp.dot(a, b, preferred_element_type=F32)


def _dot_nt(a, b):
    return lax.dot_general(a, b, (((1,), (1,)), ((), ())), preferred_element_type=F32)


def _dot_tn(a, b):
    return lax.dot_general(a, b, (((0,), (0,)), ((), ())), preferred_element_type=F32)


def _dot3(xf, g_ref):
    g = g_ref[...]
    hi = xf.astype(BF16)
    r1 = xf - hi.astype(F32)
    mid = r1.astype(BF16)
    lo = (r1 - mid.astype(F32)).astype(BF16)
    return _dot(hi, g) + _dot(mid, g) + _dot(lo, g)


def _sigmoid(x):
    return 1.0 / (1.0 + jnp.exp(-x))


def _rowsum(x):
    return jnp.sum(x, axis=0, keepdims=True)


def _ln_norm(z):
    mu = jnp.mean(z, axis=1, keepdims=True)
    zc = z - mu
    var = jnp.mean(zc * zc, axis=1, keepdims=True)
    rstd = lax.rsqrt(var + LN_EPS)
    return zc * rstd, rstd


def _ln_bwd(dxh, xh, rstd):
    m1 = jnp.mean(dxh, axis=1, keepdims=True)
    m2 = jnp.mean(dxh * xh, axis=1, keepdims=True)
    return rstd * (dxh - m1 - xh * m2)


def _adamw(w, g, m, v):
    m2 = ADAM_B1 * m + (1.0 - ADAM_B1) * g
    v2 = ADAM_B2 * v + (1.0 - ADAM_B2) * (g * g)
    m_hat = m2 / (1.0 - ADAM_B1 ** ADAM_STEP)
    v_hat = v2 / (1.0 - ADAM_B2 ** ADAM_STEP)
    delta = -ADAM_LR * (m_hat / (jnp.sqrt(v_hat) + ADAM_EPS) + ADAM_WD * w)
    return delta, m2, v2


def _params(sem=None, vmem=None):
    kw = {}
    if sem is not None:
        kw["dimension_semantics"] = sem
    if vmem is not None:
        kw["vmem_limit_bytes"] = vmem
    return pltpu.CompilerParams(**kw)


def _coords():
    return lax.axis_index("x"), lax.axis_index("y"), lax.axis_index("c")


def _flip(v, bit):
    return 1 - v if bit else v


def _acc_rows(ref, r, val):
    ref[r:r + 1, :] = ref[r:r + 1, :] + val


def _gather_small(v, name):
    r, n = v.shape

    def body(v_ref, out_ref, send_sems, recv_sems, local_sem):
        x, y, c = _coords()
        me = 4 * x + 2 * y + c
        mine = pltpu.make_async_copy(v_ref, out_ref.at[me], local_sem)
        mine.start()
        sends = []
        for k in range(1, N_DEV):
            tgt = (_flip(x, (k >> 2) & 1), _flip(y, (k >> 1) & 1), _flip(c, k & 1))
            cp = pltpu.make_async_remote_copy(
                src_ref=v_ref, dst_ref=out_ref.at[me], send_sem=send_sems.at[k - 1],
                recv_sem=recv_sems.at[k - 1], device_id=tgt, device_id_type=MESH)
            cp.start()
            sends.append(cp)
        for k in range(1, N_DEV):
            px, py, pc = _flip(x, (k >> 2) & 1), _flip(y, (k >> 1) & 1), _flip(c, k & 1)
            peer = 4 * px + 2 * py + pc
            pltpu.make_async_remote_copy(
                src_ref=v_ref, dst_ref=out_ref.at[peer], send_sem=send_sems.at[k - 1],
                recv_sem=recv_sems.at[k - 1], device_id=(px, py, pc), device_id_type=MESH).wait_recv()
        for cp in sends:
            cp.wait_send()
        mine.wait()

    return pl.pallas_call(
        body, name=name,
        out_shape=jax.ShapeDtypeStruct((N_DEV, r, n), v.dtype),
        in_specs=[VMEM_WHOLE], out_specs=VMEM_WHOLE,
        scratch_shapes=[pltpu.SemaphoreType.DMA((N_DEV - 1,)), pltpu.SemaphoreType.DMA((N_DEV - 1,)),
                        pltpu.SemaphoreType.DMA],
    )(v)


def _gather_weights(shards):
    n = len(shards)

    def body(*refs):
        ins, outs = refs[:n], refs[n:2 * n]
        send_sems, recv_sems, local_sems = refs[2 * n:]
        x, y, c = _coords()
        me, sibling = (x, y, c), (x, y, 1 - c)
        chips = [(1 - x, y), (x, 1 - y), (1 - x, 1 - y)]

        def slot(a, pos):
            return outs[a].at[4 * pos[0] + 2 * pos[1] + pos[2]]

        def copy(a, k, block, to, src=None):
            return pltpu.make_async_remote_copy(
                src_ref=slot(a, block) if src is None else src, dst_ref=slot(a, block),
                send_sem=send_sems.at[7 * a + k], recv_sem=recv_sems.at[7 * a + k],
                device_id=to, device_id_type=MESH)

        started = []
        local = []
        for a in range(n):
            mine = pltpu.make_async_copy(ins[a], slot(a, me), local_sems.at[a])
            mine.start()
            local.append(mine)
        for j, chip in enumerate(chips):
            for a in range(n):
                cp = copy(a, 1 + j, me, (*chip, c), src=ins[a])
                cp.start()
                started.append(cp)
        for a in range(n):
            cp = copy(a, 0, me, sibling, src=ins[a])
            cp.start()
            started.append(cp)
        for j, chip in enumerate(chips):
            for a in range(n):
                copy(a, 1 + j, (*chip, c), me).wait_recv()
                cp = copy(a, 4 + j, (*chip, c), sibling)
                cp.start()
                started.append(cp)
        for a in range(n):
            copy(a, 0, sibling, me).wait_recv()
        for j, chip in enumerate(chips):
            for a in range(n):
                copy(a, 4 + j, (*chip, 1 - c), me).wait_recv()
        for cp in started:
            cp.wait_send()
        for mine in local:
            mine.wait()

    return pl.pallas_call(
        body, name="gather_weights",
        out_shape=[jax.ShapeDtypeStruct((N_DEV,) + s.shape, s.dtype) for s in shards],
        in_specs=[ANY_SPACE] * n, out_specs=[ANY_SPACE] * n,
        scratch_shapes=[pltpu.SemaphoreType.DMA((7 * n,)), pltpu.SemaphoreType.DMA((7 * n,)),
                        pltpu.SemaphoreType.DMA((n,))],
    )(*shards)


def _exchange(ins, outs, send_sems, recv_sems, local_sems, gather):
    n = len(ins)
    x, y, c = _coords()
    me = 4 * x + 2 * y + c
    local = [pltpu.make_async_copy(ins[a] if gather else ins[a].at[me], outs[a].at[me], local_sems.at[a])
             for a in range(n)]
    sends, recvs = [], []
    for k in range(1, N_DEV):
        px, py, pc = _flip(x, (k >> 2) & 1), _flip(y, (k >> 1) & 1), _flip(c, k & 1)
        peer = 4 * px + 2 * py + pc
        for a in range(n):
            src = ins[a] if gather else ins[a].at[peer]
            sems = dict(send_sem=send_sems.at[7 * a + k - 1], recv_sem=recv_sems.at[7 * a + k - 1],
                        device_id=(px, py, pc), device_id_type=MESH)
            sends.append(pltpu.make_async_remote_copy(src_ref=src, dst_ref=outs[a].at[me], **sems))
            recvs.append(pltpu.make_async_remote_copy(src_ref=src, dst_ref=outs[a].at[peer], **sems))

    def start():
        for cp in local + sends:
            cp.start()

    def wait():
        for cp in recvs:
            cp.wait_recv()
        for cp in sends:
            cp.wait_send()
        for cp in local:
            cp.wait()

    return start, wait


def _exchange_scratch(n):
    return [pltpu.SemaphoreType.DMA((7 * n,)), pltpu.SemaphoreType.DMA((7 * n,)), pltpu.SemaphoreType.DMA((n,))]


def _scatter_grads(grads):
    n = len(grads)

    def body(*refs):
        start, wait = _exchange(refs[:n], refs[n:2 * n], *refs[2 * n:], gather=False)
        start()
        wait()

    return pl.pallas_call(
        body, name="scatter_grads",
        out_shape=[jax.ShapeDtypeStruct(g.shape, g.dtype) for g in grads],
        in_specs=[ANY_SPACE] * n, out_specs=[ANY_SPACE] * n,
        scratch_shapes=_exchange_scratch(n),
    )(*grads)


def _ada_part(c_all, w_ada, b_shard):
    def body(c_ref, w_ref, b_ref, o_ref):
        cv = c_ref[...]
        sc = (cv * _sigmoid(cv)).astype(BF16)
        o_ref[...] = _dot(sc, w_ref[...].astype(BF16)) + b_ref[...]

    return pl.pallas_call(
        body, name="ada_part", out_shape=jax.ShapeDtypeStruct((N_DEV, ADA_SHARD), F32),
        in_specs=[VMEM_WHOLE] * 3, out_specs=VMEM_WHOLE,
        compiler_params=_params(vmem=VMEM_MID),
    )(c_all, w_ada, b_shard)


def _in_proj(x, mod, wqkv, wag, wft, tm):
    s = x.shape[0]

    def body(x_ref, mod_ref, wqkv_ref, wag_ref, wft_ref, qkv_ref, ag_ref, ft_ref, u_ref):
        u = x_ref[...] * (1.0 + mod_ref[1:2, :]) + mod_ref[0:1, :]
        ub = u.astype(BF16)
        u_ref[...] = ub
        qkv_ref[...] = _dot(ub, wqkv_ref[...]).astype(BF16)
        ag_ref[...] = _dot(ub, wag_ref[...])
        ft_ref[...] = _dot_nt(wft_ref[...], ub)

    return pl.pallas_call(
        body, name="in_proj", grid=(s // tm,),
        in_specs=[pl.BlockSpec((tm, D_MODEL), lambda i: (i, 0)), pl.BlockSpec((8, D_MODEL), lambda i: (0, 0)),
                  VMEM_WHOLE, VMEM_WHOLE, VMEM_WHOLE],
        out_specs=[pl.BlockSpec((tm, 3 * ATTN_W), lambda i: (i, 0)),
                   pl.BlockSpec((tm, 2 * CONV_W), lambda i: (i, 0)),
                   pl.BlockSpec((8, tm), lambda i: (0, i)),
                   pl.BlockSpec((tm, D_MODEL), lambda i: (i, 0))],
        out_shape=[jax.ShapeDtypeStruct((s, 3 * ATTN_W), BF16), jax.ShapeDtypeStruct((s, 2 * CONV_W), F32),
                   jax.ShapeDtypeStruct((8, s), F32), jax.ShapeDtypeStruct((s, D_MODEL), BF16)],
        compiler_params=_params(("parallel",), VMEM_MID),
    )(x, mod, wqkv, wag, wft)


def _fgate(ft, bcol):
    s = ft.shape[1]

    def body(f_ref, b_ref, cum_ref, sneg_ref):
        z = f_ref[...] + b_ref[...]
        e = jnp.exp(-jnp.abs(z))
        l1p = jnp.where(e < 1e-2, e * (1.0 - e * (0.5 - e * (1.0 / 3.0))), jnp.log(1.0 + e))
        logf = jnp.minimum(z, 0.0) - l1p
        r = 1.0 / (1.0 + e)
        sneg_ref[...] = jnp.where(z >= 0, e * r, r)
        lane = lax.broadcasted_iota(jnp.int32, (8, s), 1)
        acc = logf
        sh = 1
        while sh < s:
            acc = acc + jnp.where(lane >= sh, pltpu.roll(acc, sh, axis=1), 0.0)
            sh *= 2
        cum_ref[...] = acc

    return pl.pallas_call(
        body, name="fgate", out_shape=[jax.ShapeDtypeStruct((8, s), F32)] * 2,
        in_specs=[VMEM_WHOLE] * 2, out_specs=[VMEM_WHOLE] * 2,
    )(ft, bcol)


def _attn_fwd(qkv, cum_t, tq, tk, shards):
    s = qkv.shape[0]
    nq = s // tq
    r = tq // tk
    n = len(shards)

    def body(q_ref, k_ref, v_ref, cum_ref, *rest):
        o_ref, lse_ref = rest[n:n + 2]
        hp = pl.program_id(0)
        qi = pl.program_id(1)
        start, wait = _exchange(rest[:n], rest[n + 2:2 * n + 2], *rest[2 * n + 2:], gather=True)
        pl.when(jnp.logical_and(hp == 0, qi == 0))(start)
        q2 = q_ref[...]
        lane = lax.broadcasted_iota(jnp.int32, (1, 128), 1)
        hmask = (lane < 64, lane >= 64)
        aux = (64, 0)
        qm = [jnp.where(hmask[h], q2, jnp.zeros_like(q2)) * 0.125 for h in range(2)]

        def step(kb, carry, r0=None):
            lo = 0 if r0 is None else r0
            k0 = pl.multiple_of(kb * tk, tk)
            k2 = k_ref[pl.ds(k0, tk), :]
            v2 = v_ref[pl.ds(k0, tk), :]
            out = []
            for h in range(2):
                m_all, acc_all = carry[2 * h], carry[2 * h + 1]
                m, acc = m_all[lo:], acc_all[lo:]
                sc = _dot_nt(qm[h][lo:], k2) - cum_ref[pl.ds(2 * hp + h, 1), pl.ds(k0, tk)]
                if r0 is not None:
                    visible = (lax.broadcasted_iota(jnp.int32, (tq - lo, tk), 1)
                               <= lax.broadcasted_iota(jnp.int32, (tq - lo, tk), 0))
                    sc = jnp.where(visible, sc, -jnp.inf)
                m_new = jnp.maximum(m, jnp.max(sc, axis=1, keepdims=True))
                p = jnp.exp(sc - m_new)
                corr = jnp.exp(m - m_new)
                va = jnp.where(hmask[h], v2, jnp.where(lane == aux[h], 1.0, 0.0).astype(v2.dtype))
                acc_new = acc * corr + _dot(p.astype(BF16), va)
                if lo:
                    m_new = jnp.concatenate([m_all[:lo], m_new], axis=0)
                    acc_new = jnp.concatenate([acc_all[:lo], acc_new], axis=0)
                out += [m_new, acc_new]
            return tuple(out)

        init = (jnp.full((tq, 1), -jnp.inf, F32), jnp.zeros((tq, 128), F32)) * 2
        carry = lax.fori_loop(0, qi * r, step, init)
        for d in range(r):
            carry = step(qi * r + d, carry, r0=d * tk)
        m0, acc0, m1, acc1 = carry
        l0 = acc0[:, aux[0]:aux[0] + 1]
        l1 = acc1[:, aux[1]:aux[1] + 1]
        o_ref[...] = jnp.where(hmask[0], acc0 / l0, acc1 / l1)
        lse_ref[:, 0:1] = m0 + jnp.log(l0)
        lse_ref[:, 1:2] = m1 + jnp.log(l1)
        pl.when(jnp.logical_and(hp == 3, qi == nq - 1))(wait)

    return pl.pallas_call(
        body, name="attn_fwd", grid=(4, nq),
        in_specs=[pl.BlockSpec((tq, 128), lambda h, i: (i, h)),
                  pl.BlockSpec((s, 128), lambda h, i: (0, 4 + h)),
                  pl.BlockSpec((s, 128), lambda h, i: (0, 8 + h)),
                  pl.BlockSpec((8, s), lambda h, i: (0, 0))] + [ANY_SPACE] * n,
        out_specs=[pl.BlockSpec((tq, 128), lambda h, i: (i, h)),
                   pl.BlockSpec((None, tq, 2), lambda h, i: (h, i, 0))] + [ANY_SPACE] * n,
        out_shape=[jax.ShapeDtypeStruct((s, ATTN_W), F32), jax.ShapeDtypeStruct((4, s, 2), F32)]
        + [jax.ShapeDtypeStruct((N_DEV,) + w.shape, w.dtype) for w in shards],
        scratch_shapes=_exchange_scratch(n),
        compiler_params=_params(("arbitrary", "arbitrary"), VMEM_BIG),
    )(qkv, qkv, qkv, cum_t, *shards)


def _gn_fwd(y, vec_ref, g_ref):
    mean = _dot3(y, g_ref)
    yc = y - mean
    var = _dot3(yc * yc, g_ref)
    rstd = lax.rsqrt(var + LN_EPS)
    yh = yc * rstd
    yn = yh * vec_ref[1:2, :] + vec_ref[2:3, :]
    return yh, rstd, yn


SHIFT_ROWS = HALO - 8


def _shifted_copies(buf, sh, tm):
    for b in range(1, 8):
        sh[b - 1] = buf[pl.ds(b, tm + SHIFT_ROWS), :]


def _window(buf, sh, off, tm):
    a, b = divmod(off, 8)
    if b == 0:
        return buf[pl.ds(8 * a, tm), :]
    return sh[b - 1, pl.ds(8 * a, tm), :]


def _conv_fwd(ag, wdw, vecs, gmat, tm):
    s = ag.shape[0]
    hb = tm // HALO

    def body(cur_ref, halo_ref, w_ref, vec_ref, g_ref, ypre_ref, conv_ref, buf, sh):
        i = pl.program_id(0)
        ug = cur_ref[:, 0:CONV_W] * _sigmoid(cur_ref[:, CONV_W:2 * CONV_W])
        ugh = halo_ref[:, 0:CONV_W] * _sigmoid(halo_ref[:, CONV_W:2 * CONV_W])
        buf[0:HALO, :] = jnp.where(i > 0, ugh, 0.0)
        buf[HALO:HALO + tm, :] = ug
        _shifted_copies(buf, sh, tm)
        y = jnp.broadcast_to(vec_ref[0:1, :], (tm, CONV_W))
        for j in range(CONV_K):
            y = y + w_ref[j:j + 1, :] * _window(buf, sh, HALO - (CONV_K - 1) + j, tm)
        ypre_ref[...] = y
        _, _, yn = _gn_fwd(y, vec_ref, g_ref)
        conv_ref[...] = yn * _sigmoid(yn)

    return pl.pallas_call(
        body, name="conv_fwd", grid=(s // tm,),
        in_specs=[pl.BlockSpec((tm, 2 * CONV_W), lambda i: (i, 0)),
                  pl.BlockSpec((HALO, 2 * CONV_W), lambda i: (jnp.maximum(i * hb - 1, 0), 0)),
                  pl.BlockSpec((32, CONV_W), lambda i: (0, 0)),
                  pl.BlockSpec((8, CONV_W), lambda i: (0, 0)),
                  pl.BlockSpec((CONV_W, CONV_W), lambda i: (0, 0))],
        out_specs=[pl.BlockSpec((tm, CONV_W), lambda i: (i, 0))] * 2,
        out_shape=[jax.ShapeDtypeStruct((s, CONV_W), F32)] * 2,
        scratch_shapes=[pltpu.VMEM((HALO + tm, CONV_W), F32), pltpu.VMEM((7, tm + SHIFT_ROWS, CONV_W), F32)],
        compiler_params=_params(("parallel",), VMEM_MID),
    )(ag, ag, wdw, vecs, gmat)


def _mix_fwd(o, conv, x, w_out, gvec, mod, lnv, tm):
    s = x.shape[0]

    def body(o_ref, c_ref, x_ref, w_ref, g_ref, mod_ref, ln_ref, cat_ref, mixed_ref, z1_ref, x1_ref):
        ov = o_ref[...]
        cv = c_ref[...]
        ra = ov * lax.rsqrt(jnp.mean(ov * ov, axis=1, keepdims=True) + LN_EPS) * g_ref[0:1, :]
        rc = cv * lax.rsqrt(jnp.mean(cv * cv, axis=1, keepdims=True) + LN_EPS) * g_ref[1:2, :]
        rab = ra.astype(BF16)
        rcb = rc.astype(BF16)
        cat_ref[:, 0:ATTN_W] = rab
        cat_ref[:, ATTN_W:D_MODEL] = rcb
        mixed = _dot(rab, w_ref[0:ATTN_W, :]) + _dot(rcb, w_ref[ATTN_W:D_MODEL, :])
        mixed_ref[...] = mixed
        z1 = ALPHA * x_ref[...] + (1.0 + mod_ref[2:3, :]) * mixed
        z1_ref[...] = z1
        xh, _ = _ln_norm(z1)
        x1_ref[...] = xh * ln_ref[0:1, :] + ln_ref[1:2, :]

    row = lambda w: pl.BlockSpec((tm, w), lambda i: (i, 0))
    return pl.pallas_call(
        body, name="mix_fwd", grid=(s // tm,),
        in_specs=[row(ATTN_W), row(CONV_W), row(D_MODEL), VMEM_WHOLE,
                  pl.BlockSpec((8, ATTN_W), lambda i: (0, 0)), pl.BlockSpec((8, D_MODEL), lambda i: (0, 0)),
                  pl.BlockSpec((8, D_MODEL), lambda i: (0, 0))],
        out_specs=[row(D_MODEL)] * 4,
        out_shape=[jax.ShapeDtypeStruct((s, D_MODEL), BF16)] + [jax.ShapeDtypeStruct((s, D_MODEL), F32)] * 3,
        compiler_params=_params(("parallel",), VMEM_MID),
    )(o, conv, x, w_out, gvec, mod, lnv)


def _ff_fwd(x1, tgt, w1, w2, mod, lnv, tm):
    s = x1.shape[0]

    def body(x1_ref, t_ref, w1_ref, w2_ref, mod_ref, ln_ref, u2_ref, hid_ref, dff_ref, dxa_ref, st_ref):
        i = pl.program_id(0)
        x1v = x1_ref[...]
        u2 = (x1v * (1.0 + mod_ref[4:5, :]) + mod_ref[3:4, :]).astype(BF16)
        u2_ref[...] = u2
        ff = jnp.zeros((tm, D_MODEL), F32)
        for d in range(N_DEV):
            h = _dot(u2, w1_ref[d])
            hr = jnp.maximum(h, 0.0)
            hb = (hr * hr).astype(BF16)
            hid_ref[:, d * FF_SHARD:(d + 1) * FF_SHARD] = hb
            ff = ff + _dot(hb, w2_ref[d * FF_SHARD:(d + 1) * FF_SHARD, :])
        gate = 1.0 + mod_ref[5:6, :]
        z2 = ALPHA * x1v + gate * ff
        xh, rstd = _ln_norm(z2)
        y = xh * ln_ref[2:3, :] + ln_ref[3:4, :]
        diff = y - t_ref[...]
        dy = diff * (1.0 / D_MODEL)
        dz2 = _ln_bwd(dy * ln_ref[2:3, :], xh, rstd)
        dff_ref[...] = (gate * dz2).astype(BF16)
        dxa_ref[...] = ALPHA * dz2

        @pl.when(i == 0)
        def _():
            st_ref[...] = jnp.zeros_like(st_ref)

        _acc_rows(st_ref, 0, _rowsum(dy * xh))
        _acc_rows(st_ref, 1, _rowsum(dy))
        _acc_rows(st_ref, 2, _rowsum(dz2 * ff))
        sq = _rowsum(jnp.sum(diff * diff, axis=1, keepdims=True))
        _acc_rows(st_ref, 3, jnp.broadcast_to(sq * (0.5 / D_MODEL), (1, D_MODEL)))

    row = lambda w: pl.BlockSpec((tm, w), lambda i: (i, 0))
    vec = pl.BlockSpec((8, D_MODEL), lambda i: (0, 0))
    return pl.pallas_call(
        body, name="ff_fwd", grid=(s // tm,),
        in_specs=[row(D_MODEL), row(D_MODEL), VMEM_WHOLE, VMEM_WHOLE, vec, vec],
        out_specs=[row(D_MODEL), row(D_FF), row(D_MODEL), row(D_MODEL), vec],
        out_shape=[jax.ShapeDtypeStruct((s, D_MODEL), BF16), jax.ShapeDtypeStruct((s, D_FF), BF16),
                   jax.ShapeDtypeStruct((s, D_MODEL), BF16), jax.ShapeDtypeStruct((s, D_MODEL), F32),
                   jax.ShapeDtypeStruct((8, D_MODEL), F32)],
        compiler_params=_params(("arbitrary",), VMEM_BIG),
    )(x1, tgt, w1, w2, mod, lnv)


def _ff_bwd(dff, hid, w1, w2, dxa, x1, z1, mixed, mod, lnv, tm):
    s = x1.shape[0]

    def body(dff_ref, hid_ref, w1_ref, w2_ref, dxa_ref, x1_ref, z1_ref, mixed_ref, mod_ref, ln_ref,
             dh_ref, dxb_ref, dmix_ref, st_ref):
        i = pl.program_id(0)
        dffv = dff_ref[...]
        du2 = jnp.zeros((tm, D_MODEL), F32)
        for d in range(N_DEV):
            cols = slice(d * FF_SHARD, (d + 1) * FF_SHARD)
            dhid = _dot_nt(dffv, w2_ref[cols, :])
            dh = (dhid * (2.0 * jnp.sqrt(hid_ref[:, cols].astype(F32)))).astype(BF16)
            dh_ref[:, cols] = dh
            du2 = du2 + _dot_nt(dh, w1_ref[d])
        x1v = x1_ref[...]
        dx1 = dxa_ref[...] + du2 * (1.0 + mod_ref[4:5, :])
        xh, rstd = _ln_norm(z1_ref[...])
        dz1 = _ln_bwd(dx1 * ln_ref[0:1, :], xh, rstd)
        dxb_ref[...] = ALPHA * dz1
        dmix_ref[...] = ((1.0 + mod_ref[2:3, :]) * dz1).astype(BF16)

        @pl.when(i == 0)
        def _():
            st_ref[...] = jnp.zeros_like(st_ref)

        _acc_rows(st_ref, 0, _rowsum(du2 * x1v))
        _acc_rows(st_ref, 1, _rowsum(du2))
        _acc_rows(st_ref, 2, _rowsum(dx1 * xh))
        _acc_rows(st_ref, 3, _rowsum(dx1))
        _acc_rows(st_ref, 4, _rowsum(dz1 * mixed_ref[...]))

    row = lambda w: pl.BlockSpec((tm, w), lambda i: (i, 0))
    vec = pl.BlockSpec((8, D_MODEL), lambda i: (0, 0))
    return pl.pallas_call(
        body, name="ff_bwd", grid=(s // tm,),
        in_specs=[row(D_MODEL), row(D_FF), VMEM_WHOLE, VMEM_WHOLE, row(D_MODEL), row(D_MODEL), row(D_MODEL),
                  row(D_MODEL), vec, vec],
        out_specs=[row(D_FF), row(D_MODEL), row(D_MODEL), vec],
        out_shape=[jax.ShapeDtypeStruct((s, D_FF), BF16), jax.ShapeDtypeStruct((s, D_MODEL), F32),
                   jax.ShapeDtypeStruct((s, D_MODEL), BF16), jax.ShapeDtypeStruct((8, D_MODEL), F32)],
        compiler_params=_params(("arbitrary",), VMEM_BIG),
    )(dff, hid, w1, w2, dxa, x1, z1, mixed, mod, lnv)


def _mix_bwd(dmix, w_out, o, conv, gvec, tm):
    s = o.shape[0]

    def body(dm_ref, w_ref, o_ref, c_ref, g_ref, do_ref, dc_ref, st_ref):
        i = pl.program_id(0)
        dmv = dm_ref[...]

        @pl.when(i == 0)
        def _():
            st_ref[...] = jnp.zeros_like(st_ref)

        for part, (src, dst) in enumerate(((o_ref, do_ref), (c_ref, dc_ref))):
            dr = _dot_nt(dmv, w_ref[part * ATTN_W:(part + 1) * ATTN_W, :])
            v = src[...]
            rr = lax.rsqrt(jnp.mean(v * v, axis=1, keepdims=True) + LN_EPS)
            vh = v * rr
            _acc_rows(st_ref, part, _rowsum(dr * vh))
            t = dr * g_ref[part:part + 1, :]
            dst[...] = rr * (t - vh * jnp.mean(t * vh, axis=1, keepdims=True))

    row = lambda w: pl.BlockSpec((tm, w), lambda i: (i, 0))
    vec = pl.BlockSpec((8, ATTN_W), lambda i: (0, 0))
    return pl.pallas_call(
        body, name="mix_bwd", grid=(s // tm,),
        in_specs=[row(D_MODEL), VMEM_WHOLE, row(ATTN_W), row(CONV_W), vec],
        out_specs=[row(ATTN_W), row(CONV_W), vec],
        out_shape=[jax.ShapeDtypeStruct((s, ATTN_W), F32), jax.ShapeDtypeStruct((s, CONV_W), F32),
                   jax.ShapeDtypeStruct((8, ATTN_W), F32)],
        compiler_params=_params(("arbitrary",), VMEM_MID),
    )(dmix, w_out, o, conv, gvec)


def _conv_bwd(dconv, ypre, ag, wdw, vecs, gmat, tm):
    s = ag.shape[0]
    hb = tm // HALO
    nt = s // tm
    last_halo = s // HALO - 1

    def body(dc_ref, dch_ref, yp_ref, yph_ref, cur_ref, halo_ref, w_ref, vec_ref, g_ref,
             dag_ref, st_ref, dw_ref, bufu, bufd, shu, shd):
        i = pl.program_id(0)

        def dyc_of(yp, dc):
            yh, rstd, yn = _gn_fwd(yp, vec_ref, g_ref)
            sg = _sigmoid(yn)
            dyn = dc * (sg * (1.0 + yn * (1.0 - sg)))
            dyh = dyn * vec_ref[1:2, :]
            dyc = rstd * (dyh - _dot3(dyh, g_ref) - yh * _dot3(dyh * yh, g_ref))
            return dyc, dyn, yh

        dyc, dyn, yh = dyc_of(yp_ref[...], dc_ref[...])
        dych, _, _ = dyc_of(yph_ref[...], dch_ref[...])
        bufd[0:tm, :] = dyc
        bufd[tm:tm + HALO, :] = jnp.where(i < nt - 1, dych, 0.0)

        av = cur_ref[:, 0:CONV_W]
        sg_g = _sigmoid(cur_ref[:, CONV_W:2 * CONV_W])
        ugh = halo_ref[:, 0:CONV_W] * _sigmoid(halo_ref[:, CONV_W:2 * CONV_W])
        bufu[0:HALO, :] = jnp.where(i > 0, ugh, 0.0)
        bufu[HALO:HALO + tm, :] = av * sg_g

        @pl.when(i == 0)
        def _():
            st_ref[...] = jnp.zeros_like(st_ref)
            dw_ref[...] = jnp.zeros_like(dw_ref)

        _acc_rows(st_ref, 0, _rowsum(dyc))
        _acc_rows(st_ref, 1, _rowsum(dyn * yh))
        _acc_rows(st_ref, 2, _rowsum(dyn))

        _shifted_copies(bufd, shd, tm)
        _shifted_copies(bufu, shu, tm)
        dug = jnp.zeros((tm, CONV_W), F32)
        for j in range(CONV_K):
            dug = dug + w_ref[j:j + 1, :] * _window(bufd, shd, CONV_K - 1 - j, tm)
            _acc_rows(dw_ref, j, _rowsum(bufd[0:tm, :] * _window(bufu, shu, HALO - (CONV_K - 1) + j, tm)))
        dag_ref[:, 0:CONV_W] = (dug * sg_g).astype(BF16)
        dag_ref[:, CONV_W:2 * CONV_W] = (dug * av * sg_g * (1.0 - sg_g)).astype(BF16)

    row = lambda w: pl.BlockSpec((tm, w), lambda i: (i, 0))
    nxt = lambda w: pl.BlockSpec((HALO, w), lambda i: (jnp.minimum((i + 1) * hb, last_halo), 0))
    return pl.pallas_call(
        body, name="conv_bwd", grid=(nt,),
        in_specs=[row(CONV_W), nxt(CONV_W), row(CONV_W), nxt(CONV_W), row(2 * CONV_W),
                  pl.BlockSpec((HALO, 2 * CONV_W), lambda i: (jnp.maximum(i * hb - 1, 0), 0)),
                  pl.BlockSpec((32, CONV_W), lambda i: (0, 0)),
                  pl.BlockSpec((8, CONV_W), lambda i: (0, 0)),
                  pl.BlockSpec((CONV_W, CONV_W), lambda i: (0, 0))],
        out_specs=[row(2 * CONV_W), pl.BlockSpec((8, CONV_W), lambda i: (0, 0)),
                   pl.BlockSpec((32, CONV_W), lambda i: (0, 0))],
        out_shape=[jax.ShapeDtypeStruct((s, 2 * CONV_W), BF16), jax.ShapeDtypeStruct((8, CONV_W), F32),
                   jax.ShapeDtypeStruct((32, CONV_W), F32)],
        scratch_shapes=[pltpu.VMEM((HALO + tm, CONV_W), F32), pltpu.VMEM((tm + HALO, CONV_W), F32),
                        pltpu.VMEM((7, tm + SHIFT_ROWS, CONV_W), F32), pltpu.VMEM((7, tm + SHIFT_ROWS, CONV_W), F32)],
        compiler_params=_params(("arbitrary",), VMEM_BIG),
    )(dconv, dconv, ypre, ypre, ag, ag, wdw, vecs, gmat)


def _attn_bwd(qkv, cum_t, o, do, lse, tq, tk, grads):
    s = qkv.shape[0]
    nq = s // tq
    r = tq // tk
    n = len(grads)

    def body(q_ref, k_ref, v_ref, cum_ref, o_ref, do_ref, lse_ref, *rest):
        dq_ref, dkt_ref, dvt_ref, dcum_ref, drow_ref = rest[n:n + 5]
        hp = pl.program_id(0)
        qi = pl.program_id(1)
        start, wait = _exchange(rest[:n], rest[n + 5:2 * n + 5], *rest[2 * n + 5:], gather=False)
        pl.when(jnp.logical_and(hp == 0, qi == 0))(start)

        @pl.when(qi == 0)
        def _():
            dkt_ref[...] = jnp.zeros_like(dkt_ref)
            dvt_ref[...] = jnp.zeros_like(dvt_ref)
            dcum_ref[...] = jnp.zeros_like(dcum_ref)

        qs = q_ref[...] * 0.125
        ov = o_ref[...]
        dob = do_ref[...].astype(BF16)
        lane = lax.broadcasted_iota(jnp.int32, (1, 128), 1)
        subl = lax.broadcasted_iota(jnp.int32, (128, 1), 0)
        hmask = (lane < 64, lane >= 64)
        aux = (64, 0)
        ones_aux = [jnp.where(lane == aux[h], 1.0, 0.0) for h in range(2)]
        qm, dom, delta, lse_h, qat, dot_t = [], [], [], [], [], []
        for h in range(2):
            qm.append(jnp.where(hmask[h], qs, jnp.zeros_like(qs)))
            dom.append(jnp.where(hmask[h], dob, jnp.zeros_like(dob)))
            delta.append(jnp.sum(dom[h].astype(F32) * ov, axis=1, keepdims=True))
            lse_h.append(lse_ref[:, h:h + 1])
            qat.append(jnp.where(hmask[h], qs.astype(F32), ones_aux[h]).T.astype(BF16))
            dot_t.append(dom[h].astype(F32).T.astype(BF16))

        def step(kb, carry, r0=None):
            lo = 0 if r0 is None else r0
            k0 = pl.multiple_of(kb * tk, tk)
            k2 = k_ref[pl.ds(k0, tk), :]
            v2 = v_ref[pl.ds(k0, tk), :]
            dq_new, a, b = [], [], []
            for h in range(2):
                cs = cum_ref[pl.ds(2 * hp + h, 1), pl.ds(k0, tk)]
                sc = _dot_nt(qm[h][lo:], k2) - cs
                p = jnp.exp(sc - lse_h[h][lo:])
                if r0 is not None:
                    visible = (lax.broadcasted_iota(jnp.int32, (tq - lo, tk), 1)
                               <= lax.broadcasted_iota(jnp.int32, (tq - lo, tk), 0))
                    p = jnp.where(visible, p, 0.0)
                dp = _dot_nt(dom[h][lo:], v2)
                ds = p * (dp - delta[h][lo:])
                pb = p.astype(BF16)
                dsb = ds.astype(BF16)
                a.append(_dot(qat[h][:, lo:], dsb))
                b.append(_dot(dot_t[h][:, lo:], pb))
                ka = jnp.where(hmask[h], k2, ones_aux[h].astype(k2.dtype))
                dq_h = carry[h][lo:] + _dot(dsb, ka)
                dq_new.append(jnp.concatenate([carry[h][:lo], dq_h], axis=0) if lo else dq_h)
            dkt_ref[:, pl.ds(k0, tk)] = dkt_ref[:, pl.ds(k0, tk)] + jnp.where(subl < 64, a[0], a[1])
            dvt_ref[:, pl.ds(k0, tk)] = dvt_ref[:, pl.ds(k0, tk)] + (b[0] + b[1])
            for h in range(2):
                dcum_ref[pl.ds(h, 1), pl.ds(k0, tk)] = (dcum_ref[pl.ds(h, 1), pl.ds(k0, tk)]
                                                        - a[h][aux[h]:aux[h] + 1, :])
            return tuple(dq_new)

        init = (jnp.zeros((tq, 128), F32),) * 2
        carry = lax.fori_loop(0, qi * r, step, init)
        for d in range(r):
            carry = step(qi * r + d, carry, r0=d * tk)
        dq0, dq1 = carry
        dq_ref[...] = (jnp.where(hmask[0], dq0, dq1) * 0.125).astype(BF16)
        drow_ref[:, 0:1] = dq0[:, aux[0]:aux[0] + 1]
        drow_ref[:, 1:2] = dq1[:, aux[1]:aux[1] + 1]
        pl.when(jnp.logical_and(hp == 3, qi == nq - 1))(wait)

    return pl.pallas_call(
        body, name="attn_bwd", grid=(4, nq),
        in_specs=[pl.BlockSpec((tq, 128), lambda h, i: (i, h)),
                  pl.BlockSpec((s, 128), lambda h, i: (0, 4 + h)),
                  pl.BlockSpec((s, 128), lambda h, i: (0, 8 + h)),
                  pl.BlockSpec((8, s), lambda h, i: (0, 0)),
                  pl.BlockSpec((tq, 128), lambda h, i: (i, h)),
                  pl.BlockSpec((tq, 128), lambda h, i: (i, h)),
                  pl.BlockSpec((None, tq, 2), lambda h, i: (h, i, 0))] + [ANY_SPACE] * n,
        out_specs=[pl.BlockSpec((tq, 128), lambda h, i: (i, h)),
                   pl.BlockSpec((128, s), lambda h, i: (h, 0)),
                   pl.BlockSpec((128, s), lambda h, i: (h, 0)),
                   pl.BlockSpec((None, 8, s), lambda h, i: (h, 0, 0)),
                   pl.BlockSpec((None, tq, 2), lambda h, i: (h, i, 0))] + [ANY_SPACE] * n,
        out_shape=[jax.ShapeDtypeStruct((s, ATTN_W), BF16), jax.ShapeDtypeStruct((ATTN_W, s), F32),
                   jax.ShapeDtypeStruct((ATTN_W, s), F32), jax.ShapeDtypeStruct((4, 8, s), F32),
                   jax.ShapeDtypeStruct((4, s, 2), F32)]
        + [jax.ShapeDtypeStruct(g.shape, g.dtype) for g in grads],
        scratch_shapes=_exchange_scratch(n),
        compiler_params=_params(("arbitrary", "arbitrary"), VMEM_BIG),
    )(qkv, qkv, qkv, cum_t, o, do, lse, *grads)


def _fgate_bwd(dcum_t, drow_t, sneg_t):
    s = dcum_t.shape[1]

    def body(dc_ref, dr_ref, sn_ref, df_ref, db_ref):
        lane = lax.broadcasted_iota(jnp.int32, (8, s), 1)
        acc = dc_ref[...] + dr_ref[...]
        sh = 1
        while sh < s:
            acc = acc + jnp.where(lane < s - sh, pltpu.roll(acc, s - sh, axis=1), 0.0)
            sh *= 2
        df = acc * sn_ref[...]
        df_ref[...] = df
        db_ref[...] = jnp.broadcast_to(jnp.sum(df, axis=1, keepdims=True), (8, 128))

    return pl.pallas_call(
        body, name="fgate_bwd",
        out_shape=[jax.ShapeDtypeStruct((8, s), F32), jax.ShapeDtypeStruct((8, 128), F32)],
        in_specs=[VMEM_WHOLE] * 3, out_specs=[VMEM_WHOLE] * 2,
    )(dcum_t, drow_t, sneg_t)


def _in_proj_bwd(dq, dkt, dvt, dag, dft, wqkv, wkvt, wag, wft, x, dxb, mod, tm):
    s = x.shape[0]

    def body(dq_ref, dkt_ref, dvt_ref, dag_ref, dft_ref, wqkv_ref, wkvt_ref, wag_ref, wft_ref, x_ref, dxb_ref,
             mod_ref, dkvt_ref, dx_ref, st_ref):
        i = pl.program_id(0)
        dkb = dkt_ref[...].astype(BF16)
        dvb = dvt_ref[...].astype(BF16)
        dkvt_ref[0:ATTN_W, :] = dkb
        dkvt_ref[ATTN_W:2 * ATTN_W, :] = dvb
        du = _dot_nt(dq_ref[...], wqkv_ref[:, 0:ATTN_W])
        du = du + _dot_tn(dkb, wkvt_ref[0:ATTN_W, :])
        du = du + _dot_tn(dvb, wkvt_ref[ATTN_W:2 * ATTN_W, :])
        du = du + _dot_nt(dag_ref[...], wag_ref[...])
        du = du + _dot_tn(dft_ref[...].astype(BF16), wft_ref[...])
        xv = x_ref[...]
        dx_ref[...] = dxb_ref[...] + du * (1.0 + mod_ref[1:2, :])

        @pl.when(i == 0)
        def _():
            st_ref[...] = jnp.zeros_like(st_ref)

        _acc_rows(st_ref, 0, _rowsum(du * xv))
        _acc_rows(st_ref, 1, _rowsum(du))

    row = lambda w: pl.BlockSpec((tm, w), lambda i: (i, 0))
    col = lambda h: pl.BlockSpec((h, tm), lambda i: (0, i))
    vec = pl.BlockSpec((8, D_MODEL), lambda i: (0, 0))
    return pl.pallas_call(
        body, name="in_proj_bwd", grid=(s // tm,),
        in_specs=[row(ATTN_W), col(ATTN_W), col(ATTN_W), row(2 * CONV_W), col(8),
                  VMEM_WHOLE, VMEM_WHOLE, VMEM_WHOLE, VMEM_WHOLE, row(D_MODEL), row(D_MODEL), vec],
        out_specs=[col(2 * ATTN_W), row(D_MODEL), vec],
        out_shape=[jax.ShapeDtypeStruct((2 * ATTN_W, s), BF16), jax.ShapeDtypeStruct((s, D_MODEL), F32),
                   jax.ShapeDtypeStruct((8, D_MODEL), F32)],
        compiler_params=_params(("arbitrary",), VMEM_MID),
    )(dq, dkt, dvt, dag, dft, wqkv, wkvt, wag, wft, x, dxb, mod)


def _wgrad_nn(a, b, name, tm, tn, tk):
    m, s = a.shape
    n = b.shape[1]
    nk = s // tk

    def body(a_ref, b_ref, o_ref, acc_ref):
        k = pl.program_id(2)

        @pl.when(k == 0)
        def _():
            acc_ref[...] = jnp.zeros_like(acc_ref)

        acc_ref[...] += _dot(a_ref[...], b_ref[...])

        @pl.when(k == nk - 1)
        def _():
            o_ref[...] = acc_ref[...].astype(o_ref.dtype)

    return pl.pallas_call(
        body, name=name, grid=(m // tm, n // tn, nk),
        in_specs=[pl.BlockSpec((tm, tk), lambda i, j, k: (i, k)), pl.BlockSpec((tk, tn), lambda i, j, k: (k, j))],
        out_specs=pl.BlockSpec((tm, tn), lambda i, j, k: (i, j)),
        out_shape=jax.ShapeDtypeStruct((m, n), BF16),
        scratch_shapes=[pltpu.VMEM((tm, tn), F32)],
        compiler_params=_params(("parallel", "parallel", "arbitrary"), VMEM_MID),
    )(a, b)


def _wgrad_tn(a, b, name, tm, tn, tk, out_shape, out_spec):
    s, m = a.shape
    n = b.shape[1]
    nk = s // tk

    def body(a_ref, b_ref, o_ref, acc_ref):
        k = pl.program_id(2)

        @pl.when(k == 0)
        def _():
            acc_ref[...] = jnp.zeros_like(acc_ref)

        acc_ref[...] += _dot_tn(a_ref[...], b_ref[...])

        @pl.when(k == nk - 1)
        def _():
            o_ref[...] = acc_ref[...].astype(o_ref.dtype)

    return pl.pallas_call(
        body, name=name, grid=(m // tm, n // tn, nk),
        in_specs=[pl.BlockSpec((tk, tm), lambda i, j, k: (k, i)), pl.BlockSpec((tk, tn), lambda i, j, k: (k, j))],
        out_specs=out_spec, out_shape=out_shape,
        scratch_shapes=[pltpu.VMEM((tm, tn), F32)],
        compiler_params=_params(("parallel", "parallel", "arbitrary"), VMEM_MID),
    )(a, b)


def _wgrad_f(dft, u, tk):
    s = u.shape[0]
    nk = s // tk

    def body(d_ref, u_ref, o_ref, acc_ref):
        k = pl.program_id(0)

        @pl.when(k == 0)
        def _():
            acc_ref[...] = jnp.zeros_like(acc_ref)

        acc_ref[...] += _dot(d_ref[...].astype(BF16), u_ref[...])

        @pl.when(k == nk - 1)
        def _():
            o_ref[...] = acc_ref[...].astype(BF16)

    return pl.pallas_call(
        body, name="wgrad_f", grid=(nk,),
        in_specs=[pl.BlockSpec((8, tk), lambda k: (0, k)), pl.BlockSpec((tk, D_MODEL), lambda k: (k, 0))],
        out_specs=pl.BlockSpec((8, D_MODEL), lambda k: (0, 0)),
        out_shape=jax.ShapeDtypeStruct((8, D_MODEL), BF16),
        scratch_shapes=[pltpu.VMEM((8, D_MODEL), F32)],
        compiler_params=_params(("arbitrary",)),
    )(dft, u)


def _adamw_big(recv, w, m, v, name, tr):
    r, cdim = w.shape

    def body(g_ref, w_ref, m_ref, v_ref, go_ref, d_ref, mo_ref, vo_ref):
        g = g_ref[0].astype(F32)
        for k in range(1, N_DEV):
            g = g + g_ref[k].astype(F32)
        delta, m2, v2 = _adamw(w_ref[...], g, m_ref[...], v_ref[...])
        go_ref[...] = g
        d_ref[...] = delta
        mo_ref[...] = m2
        vo_ref[...] = v2

    blk = pl.BlockSpec((tr, cdim), lambda i: (i, 0))
    return pl.pallas_call(
        body, name=name, grid=(r // tr,),
        in_specs=[pl.BlockSpec((N_DEV, tr, cdim), lambda i: (0, i, 0)), blk, blk, blk],
        out_specs=[blk] * 4, out_shape=[jax.ShapeDtypeStruct((r, cdim), F32)] * 4,
        compiler_params=_params(("parallel",), VMEM_MID),
    )(recv, w, m, v)


def _adamw_ada(c_all, dada, w, m, v, tr):
    r, cdim = w.shape

    def body(c_ref, d_ref, w_ref, m_ref, v_ref, go_ref, dl_ref, mo_ref, vo_ref):
        cv = c_ref[...]
        sc = (cv * _sigmoid(cv)).astype(BF16)
        g = _dot_tn(sc, d_ref[...].astype(BF16))
        delta, m2, v2 = _adamw(w_ref[...], g, m_ref[...], v_ref[...])
        go_ref[...] = g
        dl_ref[...] = delta
        mo_ref[...] = m2
        vo_ref[...] = v2

    blk = pl.BlockSpec((tr, cdim), lambda i: (i, 0))
    return pl.pallas_call(
        body, name="adamw_ada", grid=(r // tr,),
        in_specs=[pl.BlockSpec((N_DEV, tr), lambda i: (0, i)), pl.BlockSpec((N_DEV, cdim), lambda i: (0, 0)),
                  blk, blk, blk],
        out_specs=[blk] * 4, out_shape=[jax.ShapeDtypeStruct((r, cdim), F32)] * 4,
        compiler_params=_params(("parallel",), VMEM_MID),
    )(c_all, dada, w, m, v)


def _adamw_small(parts, w, m, v):
    n = w.shape[1]

    def body(g_ref, w_ref, m_ref, v_ref, go_ref, d_ref, mo_ref, vo_ref):
        g = g_ref[0]
        for k in range(1, N_DEV):
            g = g + g_ref[k]
        delta, m2, v2 = _adamw(w_ref[...], g, m_ref[...], v_ref[...])
        go_ref[...] = g
        d_ref[...] = delta
        mo_ref[...] = m2
        vo_ref[...] = v2

    return pl.pallas_call(
        body, name="adamw_small", out_shape=[jax.ShapeDtypeStruct((1, n), F32)] * 4,
        in_specs=[VMEM_WHOLE] * 4, out_specs=[VMEM_WHOLE] * 4,
    )(parts, w, m, v)


def _pad_cols(a, n):
    return jnp.pad(a, ((0, 0), (0, n - a.shape[1])))


def _pad_rows(a, n):
    return jnp.pad(a, ((0, n - a.shape[0]), (0, 0)))


def kernel(x, c, w_ada, b_ada, w_in, b_forget, w_dw, b_dw, gn_g, gn_b, g_attn_out, g_conv_out, w_out, ln1_g, ln1_b, w_ff1, w_ff2, ln2_g, ln2_b, loss_target, m_w_ada, m_b_ada, m_w_in, m_b_forget, m_w_dw, m_b_dw, m_gn_g, m_gn_b, m_g_attn_out, m_g_conv_out, m_w_out, m_ln1_g, m_ln1_b, m_w_ff1, m_w_ff2, m_ln2_g, m_ln2_b, v_w_ada, v_b_ada, v_w_in, v_b_forget, v_w_dw, v_b_dw, v_gn_g, v_gn_b, v_g_attn_out, v_g_conv_out, v_w_out, v_ln1_g, v_ln1_b, v_w_ff1, v_w_ff2, v_ln2_g, v_ln2_b):
    s = x.shape[1]
    tm = min(ROW_TILE, s)
    tk_att = min(ATT_TILE, s)
    tq_att = min(ATT_Q_BLOCKS * tk_att, s)
    me = 4 * lax.axis_index("x") + 2 * lax.axis_index("y") + lax.axis_index("c")
    xs = x[0]
    tgt = loss_target[0]

    dw_n = CONV_K * DW_SHARD
    pack1 = jnp.concatenate([c, w_dw[0].reshape(1, dw_n), jnp.zeros((1, 2048 - dw_n), F32)], axis=1)
    g1 = _gather_small(pack1, "gather_c")[:, 0, :]
    c_all = g1[:, :D_MODEL]
    wdw_full = g1[:, D_MODEL:D_MODEL + dw_n].reshape(N_DEV, CONV_K, DW_SHARD)
    wdw_full = _pad_rows(wdw_full.transpose(1, 0, 2).reshape(CONV_K, CONV_W), 32)

    b_shard = lax.dynamic_slice(b_ada, (0, me * ADA_SHARD), (1, ADA_SHARD))
    ada_all = _gather_small(_ada_part(c_all, w_ada[0], b_shard), "gather_ada")
    ada = lax.dynamic_index_in_dim(ada_all, me, axis=1, keepdims=False).reshape(6, D_MODEL)
    mod = _pad_rows(ada, 8)

    gw_in, = _gather_weights([_pad_cols(w_in[0], IN_SHARD_PAD).astype(BF16)])
    w_in_full = gw_in[:, :, :IN_SHARD].transpose(1, 0, 2).reshape(D_MODEL, N_IN)
    wqkv = w_in_full[:, :3 * ATTN_W]
    wft = w_in_full[:, 3 * ATTN_W:3 * ATTN_W + N_HEADS].T
    wag = w_in_full[:, 3 * ATTN_W + N_HEADS:]

    qkv, ag, ft, u = _in_proj(xs, mod, wqkv, wag, wft, tm)
    cum_t, sneg_t = _fgate(ft, b_forget.reshape(N_HEADS, 1))
    o, lse, gw_out, gw_ff1, gw_ff2 = _attn_fwd(
        qkv, cum_t, tq_att, tk_att, [w_out[0].astype(BF16), w_ff1[0].astype(BF16), w_ff2[0].astype(BF16)])
    w_out_full = gw_out.reshape(D_MODEL, D_MODEL)
    w2_full = gw_ff2.reshape(D_FF, D_MODEL)
    cvec = _pad_rows(jnp.concatenate([b_dw, gn_g, gn_b], axis=0), 8)
    grp = jnp.arange(CONV_W) // 64
    gmat = jnp.where(grp[:, None] == grp[None, :], 1.0 / 64.0, 0.0).astype(BF16)
    ypre, conv = _conv_fwd(ag, wdw_full, cvec, gmat, tm)
    gvec = _pad_rows(jnp.concatenate([g_attn_out, g_conv_out], axis=0), 8)
    lnv = _pad_rows(jnp.concatenate([ln1_g, ln1_b, ln2_g, ln2_b], axis=0), 8)
    cat, mixed, z1, x1 = _mix_fwd(o, conv, xs, w_out_full, gvec, mod, lnv, tm)
    tf = min(FF_TILE, s)
    u2, hid, dff, dxa, st_ff = _ff_fwd(x1, tgt, gw_ff1, w2_full, mod, lnv, tf)

    dh, dxb, dmix, st_fb = _ff_bwd(dff, hid, gw_ff1, w2_full, dxa, x1, z1, mixed, mod, lnv, tf)
    do, dconv, st_mix = _mix_bwd(dmix, w_out_full, o, conv, gvec, tm)
    dag, st_conv, dwdw = _conv_bwd(dconv, ypre, ag, wdw_full, cvec, gmat, tm)

    tk = min(WGRAD_K_TILE, s)
    g_ff2 = _wgrad_tn(hid, dff, "wgrad_ff2", 512, 1024, tk,
                      jax.ShapeDtypeStruct((D_FF, D_MODEL), BF16),
                      pl.BlockSpec((512, 1024), lambda i, j, k: (i, j))).reshape(N_DEV, FF_SHARD, D_MODEL)
    g_ff1 = _wgrad_tn(u2, dh, "wgrad_ff1", 1024, 512, tk,
                      jax.ShapeDtypeStruct((N_DEV, D_MODEL, FF_SHARD), BF16),
                      pl.BlockSpec((None, 1024, 512), lambda i, j, k: (j, i, 0)))
    g_out = _wgrad_tn(cat, dmix, "wgrad_out", 512, 1024, tk,
                      jax.ShapeDtypeStruct((D_MODEL, D_MODEL), BF16),
                      pl.BlockSpec((512, 1024), lambda i, j, k: (i, j))).reshape(N_DEV, OUT_SHARD, D_MODEL)
    dq, dkt, dvt, dcum, drow, r_out, r_ff1, r_ff2 = _attn_bwd(
        qkv, cum_t, o, do, lse, tq_att, tk_att, [g_out, g_ff1, g_ff2])
    dft, dbf = _fgate_bwd(dcum[:, :2, :].reshape(N_HEADS, s), drow.transpose(0, 2, 1).reshape(N_HEADS, s), sneg_t)
    wkvt = wqkv[:, ATTN_W:3 * ATTN_W].T
    dkvt, grad_x, st_in = _in_proj_bwd(dq, dkt, dvt, dag, dft, wqkv, wkvt, wag, wft, xs, dxb, mod, tm)

    g_q = _wgrad_tn(u, dq, "wgrad_q", 1024, 512, tk,
                    jax.ShapeDtypeStruct((D_MODEL, ATTN_W), BF16),
                    pl.BlockSpec((1024, 512), lambda i, j, k: (i, j)))
    g_kvt = _wgrad_nn(dkvt, u, "wgrad_kv", 512, 1024, tk)
    g_qkv = jnp.concatenate([g_q, g_kvt.T], axis=1)
    g_ag = _wgrad_tn(u, dag, "wgrad_ag", 1024, 512, tk,
                     jax.ShapeDtypeStruct((D_MODEL, 2 * CONV_W), BF16),
                     pl.BlockSpec((1024, 512), lambda i, j, k: (i, j)))
    g_ft = _wgrad_f(dft, u, tk)
    g_in = jnp.concatenate([g_qkv, g_ft.T, g_ag], axis=1)
    g_in = jnp.pad(g_in.reshape(D_MODEL, N_DEV, IN_SHARD).transpose(1, 0, 2),
                   ((0, 0), (0, 0), (0, IN_SHARD_PAD - IN_SHARD)))

    r_in, = _scatter_grads([g_in])

    dada = jnp.concatenate([st_in[1:2], st_in[0:1], st_fb[4:5], st_fb[1:2], st_fb[0:1], st_ff[2:3]], axis=1)
    loss_part = st_ff[3:4, 0:1]
    pack2 = jnp.concatenate([
        dada,
        _pad_cols(dbf[:, 0].reshape(1, N_HEADS), 128),
        dwdw[:CONV_K].reshape(1, CONV_K * CONV_W),
        st_conv[0:1], st_conv[1:2], st_conv[2:3], st_mix[0:1], st_mix[1:2],
        st_fb[2:3], st_fb[3:4], st_ff[0:1], st_ff[1:2],
        _pad_cols(loss_part, 128)], axis=1)
    g2 = _gather_small(pack2, "gather_small_grads")[:, 0, :]
    o_bf = 6 * D_MODEL
    o_dw = o_bf + 128
    o_v = o_dw + CONV_K * CONV_W
    o_ln = o_v + 5 * CONV_W
    o_loss = o_ln + 4 * D_MODEL
    dw_parts = lax.dynamic_slice_in_dim(
        g2[:, o_dw:o_v].reshape(N_DEV, CONV_K, N_DEV, DW_SHARD), me, 1, axis=2).reshape(N_DEV, dw_n)
    parts = jnp.concatenate([
        g2[:, :o_dw], _pad_cols(dw_parts, 2048), g2[:, o_v:o_loss + 128]], axis=1)[:, None, :]

    def pack_w(b_ada_, b_forget_, w_dw_, smalls):
        return jnp.concatenate([b_ada_, _pad_cols(b_forget_, 128), _pad_cols(w_dw_[0].reshape(1, dw_n), 2048)]
                               + smalls + [jnp.zeros((1, 128), F32)], axis=1)

    pw = pack_w(b_ada, b_forget, w_dw, [b_dw, gn_g, gn_b, g_attn_out, g_conv_out, ln1_g, ln1_b, ln2_g, ln2_b])
    pm = pack_w(m_b_ada, m_b_forget, m_w_dw, [m_b_dw, m_gn_g, m_gn_b, m_g_attn_out, m_g_conv_out,
                                              m_ln1_g, m_ln1_b, m_ln2_g, m_ln2_b])
    pv = pack_w(v_b_ada, v_b_forget, v_w_dw, [v_b_dw, v_gn_g, v_gn_b, v_g_attn_out, v_g_conv_out,
                                              v_ln1_g, v_ln1_b, v_ln2_g, v_ln2_b])
    small = _adamw_small(parts, pw, pm, pv)

    p_bf = 6 * D_MODEL
    p_dw = p_bf + 128
    p_v = p_dw + 2048
    p_ln = p_v + 5 * CONV_W
    p_loss = p_ln + 4 * D_MODEL

    def unpack(t):
        outs = {"b_ada": t[:, :p_bf], "b_forget": t[:, p_bf:p_bf + N_HEADS],
                "w_dw": t[:, p_dw:p_dw + dw_n].reshape(1, CONV_K, 1, DW_SHARD)}
        for k, nm in enumerate(["b_dw", "gn_g", "gn_b", "g_attn_out", "g_conv_out"]):
            outs[nm] = t[:, p_v + k * CONV_W:p_v + (k + 1) * CONV_W]
        for k, nm in enumerate(["ln1_g", "ln1_b", "ln2_g", "ln2_b"]):
            outs[nm] = t[:, p_ln + k * D_MODEL:p_ln + (k + 1) * D_MODEL]
        return outs

    sm = [unpack(t) for t in small]
    loss = small[0][0, p_loss]

    dada_all = g2[:, :6 * D_MODEL]
    dada_shard = lax.dynamic_slice_in_dim(dada_all, me * ADA_SHARD, ADA_SHARD, axis=1)
    big = {
        "w_ada": _adamw_ada(c_all, dada_shard, w_ada[0], m_w_ada[0], v_w_ada[0], 256),
        "w_in": [t[:, :IN_SHARD] for t in _adamw_big(
            r_in, _pad_cols(w_in[0], IN_SHARD_PAD), _pad_cols(m_w_in[0], IN_SHARD_PAD),
            _pad_cols(v_w_in[0], IN_SHARD_PAD), "adamw_in", 256)],
        "w_out": _adamw_big(r_out, w_out[0], m_w_out[0], v_w_out[0], "adamw_out", 128),
        "w_ff1": _adamw_big(r_ff1, w_ff1[0], m_w_ff1[0], v_w_ff1[0], "adamw_ff1", 256),
        "w_ff2": _adamw_big(r_ff2, w_ff2[0], m_w_ff2[0], v_w_ff2[0], "adamw_ff2", 256),
    }

    names = ["w_ada", "b_ada", "w_in", "b_forget", "w_dw", "b_dw", "gn_g", "gn_b", "g_attn_out", "g_conv_out",
             "w_out", "ln1_g", "ln1_b", "w_ff1", "w_ff2", "ln2_g", "ln2_b"]

    def leaf(kind, nm):
        if nm in big:
            return big[nm][kind][None]
        return sm[kind][nm]

    outs = [loss, grad_x[None]]
    for kind in range(4):
        outs += [leaf(kind, nm) for nm in names]
    return tuple(outs)
```

```python
import jax
import jax.numpy as jnp
from jax import lax
from jax.experimental import pallas as pl
from jax.experimental.pallas import tpu as pltpu

F32 = jnp.float32
BF16 = jnp.bfloat16

D_MODEL = 1024
ATTN_W = 512
CONV_W = 512
N_HEADS = 8
CONV_K = 31
D_FF = 4096
N_DEV = 8
N_IN = 3 * ATTN_W + N_HEADS + 2 * CONV_W
IN_SHARD = N_IN // N_DEV
IN_SHARD_PAD = 384
ADA_SHARD = 6 * D_MODEL // N_DEV
FF_SHARD = D_FF // N_DEV
OUT_SHARD = D_MODEL // N_DEV
DW_SHARD = CONV_W // N_DEV
HALO = 32
LN_EPS = 1e-5
ALPHA = 2.0 ** 0.25
ADAM_LR, ADAM_B1, ADAM_B2, ADAM_EPS, ADAM_WD, ADAM_STEP = 0.001, 0.9, 0.999, 1e-08, 0.01, 10

ROW_TILE = 512
FF_TILE = 256
WGRAD_K_TILE = 2048
ATT_TILE = 512
ATT_Q_BLOCKS = 2
VMEM_BIG = 56 * 1024 * 1024
VMEM_MID = 40 * 1024 * 1024

MESH = pl.DeviceIdType.MESH
VMEM_WHOLE = pl.BlockSpec(memory_space=pltpu.VMEM)
ANY_SPACE = pl.BlockSpec(memory_space=pl.ANY)


def _dot(a, b):
    return jnp.dot(a, b, preferred_element_type=F32)


def _dot_nt(a, b):
    return lax.dot_general(a, b, (((1,), (1,)), ((), ())), preferred_element_type=F32)


def _dot_tn(a, b):
    return lax.dot_general(a, b, (((0,), (0,)), ((), ())), preferred_element_type=F32)


def _dot3(xf, g_ref):
    g = g_ref[...]
    hi = xf.astype(BF16)
    lo = (xf - hi.astype(F32)).astype(BF16)
    return _dot(hi, g) + _dot(lo, g)


def _sigmoid(x):
    return 1.0 / (1.0 + jnp.exp(-x))


def _rowsum(x):
    return jnp.sum(x, axis=0, keepdims=True)


def _ln_norm(z):
    mu = jnp.mean(z, axis=1, keepdims=True)
    zc = z - mu
    var = jnp.mean(zc * zc, axis=1, keepdims=True)
    rstd = lax.rsqrt(var + LN_EPS)
    return zc * rstd, rstd


def _ln_bwd(dxh, xh, rstd):
    m1 = jnp.mean(dxh, axis=1, keepdims=True)
    m2 = jnp.mean(dxh * xh, axis=1, keepdims=True)
    return rstd * (dxh - m1 - xh * m2)


def _adamw(w, g, m, v):
    m2 = ADAM_B1 * m + (1.0 - ADAM_B1) * g
    v2 = ADAM_B2 * v + (1.0 - ADAM_B2) * (g * g)
    m_hat = m2 / (1.0 - ADAM_B1 ** ADAM_STEP)
    v_hat = v2 / (1.0 - ADAM_B2 ** ADAM_STEP)
    delta = -ADAM_LR * (m_hat / (jnp.sqrt(v_hat) + ADAM_EPS) + ADAM_WD * w)
    return delta, m2, v2


def _params(sem=None, vmem=None):
    kw = {}
    if sem is not None:
        kw["dimension_semantics"] = sem
    if vmem is not None:
        kw["vmem_limit_bytes"] = vmem
    return pltpu.CompilerParams(**kw)


def _coords():
    return lax.axis_index("x"), lax.axis_index("y"), lax.axis_index("c")


def _flip(v, bit):
    return 1 - v if bit else v


def _acc_rows(ref, r, val):
    ref[r:r + 1, :] = ref[r:r + 1, :] + val


def _gather_small(v, name):
    r, n = v.shape

    def body(v_ref, out_ref, send_sems, recv_sems, local_sem):
        x, y, c = _coords()
        me = 4 * x + 2 * y + c
        mine = pltpu.make_async_copy(v_ref, out_ref.at[me], local_sem)
        mine.start()
        sends = []
        for k in range(1, N_DEV):
            tgt = (_flip(x, (k >> 2) & 1), _flip(y, (k >> 1) & 1), _flip(c, k & 1))
            cp = pltpu.make_async_remote_copy(
                src_ref=v_ref, dst_ref=out_ref.at[me], send_sem=send_sems.at[k - 1],
                recv_sem=recv_sems.at[k - 1], device_id=tgt, device_id_type=MESH)
            cp.start()
            sends.append(cp)
        for k in range(1, N_DEV):
            px, py, pc = _flip(x, (k >> 2) & 1), _flip(y, (k >> 1) & 1), _flip(c, k & 1)
            peer = 4 * px + 2 * py + pc
            pltpu.make_async_remote_copy(
                src_ref=v_ref, dst_ref=out_ref.at[peer], send_sem=send_sems.at[k - 1],
                recv_sem=recv_sems.at[k - 1], device_id=(px, py, pc), device_id_type=MESH).wait_recv()
        for cp in sends:
            cp.wait_send()
        mine.wait()

    return pl.pallas_call(
        body, name=name,
        out_shape=jax.ShapeDtypeStruct((N_DEV, r, n), v.dtype),
        in_specs=[VMEM_WHOLE], out_specs=VMEM_WHOLE,
        scratch_shapes=[pltpu.SemaphoreType.DMA((N_DEV - 1,)), pltpu.SemaphoreType.DMA((N_DEV - 1,)),
                        pltpu.SemaphoreType.DMA],
    )(v)


def _gather_weights(shards):
    n = len(shards)

    def body(*refs):
        ins, outs = refs[:n], refs[n:2 * n]
        send_sems, recv_sems, local_sems = refs[2 * n:]
        x, y, c = _coords()
        me, sibling = (x, y, c), (x, y, 1 - c)
        chips = [(1 - x, y), (x, 1 - y), (1 - x, 1 - y)]

        def slot(a, pos):
            return outs[a].at[4 * pos[0] + 2 * pos[1] + pos[2]]

        def copy(a, k, block, to, src=None):
            return pltpu.make_async_remote_copy(
                src_ref=slot(a, block) if src is None else src, dst_ref=slot(a, block),
                send_sem=send_sems.at[7 * a + k], recv_sem=recv_sems.at[7 * a + k],
                device_id=to, device_id_type=MESH)

        started = []
        local = []
        for a in range(n):
            mine = pltpu.make_async_copy(ins[a], slot(a, me), local_sems.at[a])
            mine.start()
            local.append(mine)
        for j, chip in enumerate(chips):
            for a in range(n):
                cp = copy(a, 1 + j, me, (*chip, c), src=ins[a])
                cp.start()
                started.append(cp)
        for a in range(n):
            cp = copy(a, 0, me, sibling, src=ins[a])
            cp.start()
            started.append(cp)
        for j, chip in enumerate(chips):
            for a in range(n):
                copy(a, 1 + j, (*chip, c), me).wait_recv()
                cp = copy(a, 4 + j, (*chip, c), sibling)
                cp.start()
                started.append(cp)
        for a in range(n):
            copy(a, 0, sibling, me).wait_recv()
        for j, chip in enumerate(chips):
            for a in range(n):
                copy(a, 4 + j, (*chip, 1 - c), me).wait_recv()
        for cp in started:
            cp.wait_send()
        for mine in local:
            mine.wait()

    return pl.pallas_call(
        body, name="gather_weights",
        out_shape=[jax.ShapeDtypeStruct((N_DEV,) + s.shape, s.dtype) for s in shards],
        in_specs=[ANY_SPACE] * n, out_specs=[ANY_SPACE] * n,
        scratch_shapes=[pltpu.SemaphoreType.DMA((7 * n,)), pltpu.SemaphoreType.DMA((7 * n,)),
                        pltpu.SemaphoreType.DMA((n,))],
    )(*shards)


def _exchange(ins, outs, send_sems, recv_sems, local_sems, gather):
    n = len(ins)
    x, y, c = _coords()
    me = 4 * x + 2 * y + c
    local = [pltpu.make_async_copy(ins[a] if gather else ins[a].at[me], outs[a].at[me], local_sems.at[a])
             for a in range(n)]
    sends, recvs = [], []
    for k in range(1, N_DEV):
        px, py, pc = _flip(x, (k >> 2) & 1), _flip(y, (k >> 1) & 1), _flip(c, k & 1)
        peer = 4 * px + 2 * py + pc
        for a in range(n):
            src = ins[a] if gather else ins[a].at[peer]
            sems = dict(send_sem=send_sems.at[7 * a + k - 1], recv_sem=recv_sems.at[7 * a + k - 1],
                        device_id=(px, py, pc), device_id_type=MESH)
            sends.append(pltpu.make_async_remote_copy(src_ref=src, dst_ref=outs[a].at[me], **sems))
            recvs.append(pltpu.make_async_remote_copy(src_ref=src, dst_ref=outs[a].at[peer], **sems))

    def start():
        for cp in local + sends:
            cp.start()

    def wait():
        for cp in recvs:
            cp.wait_recv()
        for cp in sends:
            cp.wait_send()
        for cp in local:
            cp.wait()

    return start, wait


def _exchange_scratch(n):
    return [pltpu.SemaphoreType.DMA((7 * n,)), pltpu.SemaphoreType.DMA((7 * n,)), pltpu.SemaphoreType.DMA((n,))]


def _scatter_grads(grads):
    n = len(grads)

    def body(*refs):
        start, wait = _exchange(refs[:n], refs[n:2 * n], *refs[2 * n:], gather=False)
        start()
        wait()

    return pl.pallas_call(
        body, name="scatter_grads",
        out_shape=[jax.ShapeDtypeStruct(g.shape, g.dtype) for g in grads],
        in_specs=[ANY_SPACE] * n, out_specs=[ANY_SPACE] * n,
        scratch_shapes=_exchange_scratch(n),
    )(*grads)


def _ada_part(c_all, w_ada, b_shard):
    def body(c_ref, w_ref, b_ref, o_ref):
        cv = c_ref[...]
        sc = (cv * _sigmoid(cv)).astype(BF16)
        o_ref[...] = _dot(sc, w_ref[...].astype(BF16)) + b_ref[...]

    return pl.pallas_call(
        body, name="ada_part", out_shape=jax.ShapeDtypeStruct((N_DEV, ADA_SHARD), F32),
        in_specs=[VMEM_WHOLE] * 3, out_specs=VMEM_WHOLE,
        compiler_params=_params(vmem=VMEM_MID),
    )(c_all, w_ada, b_shard)


def _in_proj(x, mod, wqkv, wag, wft, tm):
    s = x.shape[0]

    def body(x_ref, mod_ref, wqkv_ref, wag_ref, wft_ref, qkv_ref, ag_ref, ft_ref, u_ref):
        u = x_ref[...] * (1.0 + mod_ref[1:2, :]) + mod_ref[0:1, :]
        ub = u.astype(BF16)
        u_ref[...] = ub
        qkv_ref[...] = _dot(ub, wqkv_ref[...]).astype(BF16)
        ag_ref[...] = _dot(ub, wag_ref[...])
        ft_ref[...] = _dot_nt(wft_ref[...], ub)

    return pl.pallas_call(
        body, name="in_proj", grid=(s // tm,),
        in_specs=[pl.BlockSpec((tm, D_MODEL), lambda i: (i, 0)), pl.BlockSpec((8, D_MODEL), lambda i: (0, 0)),
                  VMEM_WHOLE, VMEM_WHOLE, VMEM_WHOLE],
        out_specs=[pl.BlockSpec((tm, 3 * ATTN_W), lambda i: (i, 0)),
                   pl.BlockSpec((tm, 2 * CONV_W), lambda i: (i, 0)),
                   pl.BlockSpec((8, tm), lambda i: (0, i)),
                   pl.BlockSpec((tm, D_MODEL), lambda i: (i, 0))],
        out_shape=[jax.ShapeDtypeStruct((s, 3 * ATTN_W), BF16), jax.ShapeDtypeStruct((s, 2 * CONV_W), F32),
                   jax.ShapeDtypeStruct((8, s), F32), jax.ShapeDtypeStruct((s, D_MODEL), BF16)],
        compiler_params=_params(("parallel",), VMEM_MID),
    )(x, mod, wqkv, wag, wft)


def _fgate(ft, bcol):
    s = ft.shape[1]

    def body(f_ref, b_ref, cum_ref, sneg_ref):
        z = f_ref[...] + b_ref[...]
        e = jnp.exp(-jnp.abs(z))
        l1p = jnp.where(e < 1e-2, e * (1.0 - e * (0.5 - e * (1.0 / 3.0))), jnp.log(1.0 + e))
        logf = jnp.minimum(z, 0.0) - l1p
        r = 1.0 / (1.0 + e)
        sneg_ref[...] = jnp.where(z >= 0, e * r, r)
        lane = lax.broadcasted_iota(jnp.int32, (8, s), 1)
        acc = logf
        sh = 1
        while sh < s:
            acc = acc + jnp.where(lane >= sh, pltpu.roll(acc, sh, axis=1), 0.0)
            sh *= 2
        cum_ref[...] = acc

    return pl.pallas_call(
        body, name="fgate", out_shape=[jax.ShapeDtypeStruct((8, s), F32)] * 2,
        in_specs=[VMEM_WHOLE] * 2, out_specs=[VMEM_WHOLE] * 2,
    )(ft, bcol)


def _attn_fwd(qkv, cum_t, tq, tk, shards):
    s = qkv.shape[0]
    nq = s // tq
    r = tq // tk
    n = len(shards)

    def body(q_ref, k_ref, v_ref, cum_ref, *rest):
        o_ref, lse_ref = rest[n:n + 2]
        hp = pl.program_id(0)
        qi = pl.program_id(1)
        start, wait = _exchange(rest[:n], rest[n + 2:2 * n + 2], *rest[2 * n + 2:], gather=True)
        pl.when(jnp.logical_and(hp == 0, qi == 0))(start)
        q2 = q_ref[...]
        lane = lax.broadcasted_iota(jnp.int32, (1, 128), 1)
        hmask = (lane < 64, lane >= 64)
        aux = (64, 0)
        qm = [jnp.where(hmask[h], q2, jnp.zeros_like(q2)) * 0.125 for h in range(2)]

        def step(kb, carry, r0=None):
            lo = 0 if r0 is None else r0
            k0 = pl.multiple_of(kb * tk, tk)
            k2 = k_ref[pl.ds(k0, tk), :]
            v2 = v_ref[pl.ds(k0, tk), :]
            out = []
            for h in range(2):
                m_all, acc_all = carry[2 * h], carry[2 * h + 1]
                m, acc = m_all[lo:], acc_all[lo:]
                sc = _dot_nt(qm[h][lo:], k2) - cum_ref[pl.ds(2 * hp + h, 1), pl.ds(k0, tk)]
                if r0 is not None:
                    visible = (lax.broadcasted_iota(jnp.int32, (tq - lo, tk), 1)
                               <= lax.broadcasted_iota(jnp.int32, (tq - lo, tk), 0))
                    sc = jnp.where(visible, sc, -jnp.inf)
                m_new = jnp.maximum(m, jnp.max(sc, axis=1, keepdims=True))
                p = jnp.exp(sc - m_new)
                corr = jnp.exp(m - m_new)
                va = jnp.where(hmask[h], v2, jnp.where(lane == aux[h], 1.0, 0.0).astype(v2.dtype))
                acc_new = acc * corr + _dot(p.astype(BF16), va)
                if lo:
                    m_new = jnp.concatenate([m_all[:lo], m_new], axis=0)
                    acc_new = jnp.concatenate([acc_all[:lo], acc_new], axis=0)
                out += [m_new, acc_new]
            return tuple(out)

        init = (jnp.full((tq, 1), -jnp.inf, F32), jnp.zeros((tq, 128), F32)) * 2
        carry = lax.fori_loop(0, qi * r, step, init)
        for d in range(r):
            carry = step(qi * r + d, carry, r0=d * tk)
        m0, acc0, m1, acc1 = carry
        l0 = acc0[:, aux[0]:aux[0] + 1]
        l1 = acc1[:, aux[1]:aux[1] + 1]
        o_ref[...] = jnp.where(hmask[0], acc0 / l0, acc1 / l1)
        lse_ref[:, 0:1] = m0 + jnp.log(l0)
        lse_ref[:, 1:2] = m1 + jnp.log(l1)
        pl.when(jnp.logical_and(hp == 3, qi == nq - 1))(wait)

    return pl.pallas_call(
        body, name="attn_fwd", grid=(4, nq),
        in_specs=[pl.BlockSpec((tq, 128), lambda h, i: (i, h)),
                  pl.BlockSpec((s, 128), lambda h, i: (0, 4 + h)),
                  pl.BlockSpec((s, 128), lambda h, i: (0, 8 + h)),
                  pl.BlockSpec((8, s), lambda h, i: (0, 0))] + [ANY_SPACE] * n,
        out_specs=[pl.BlockSpec((tq, 128), lambda h, i: (i, h)),
                   pl.BlockSpec((None, tq, 2), lambda h, i: (h, i, 0))] + [ANY_SPACE] * n,
        out_shape=[jax.ShapeDtypeStruct((s, ATTN_W), F32), jax.ShapeDtypeStruct((4, s, 2), F32)]
        + [jax.ShapeDtypeStruct((N_DEV,) + w.shape, w.dtype) for w in shards],
        scratch_shapes=_exchange_scratch(n),
        compiler_params=_params(("arbitrary", "arbitrary"), VMEM_BIG),
    )(qkv, qkv, qkv, cum_t, *shards)


def _gn_fwd(y, vec_ref, g_ref):
    mean = _dot3(y, g_ref)
    yc = y - mean
    var = _dot3(yc * yc, g_ref)
    rstd = lax.rsqrt(var + LN_EPS)
    yh = yc * rstd
    yn = yh * vec_ref[1:2, :] + vec_ref[2:3, :]
    return yh, rstd, yn


SHIFT_ROWS = HALO - 8
CONV_CHUNK = 32


def _shifted_copies(buf, sh, tm):
    for b in range(1, 8):
        sh[b - 1] = buf[pl.ds(b, tm + SHIFT_ROWS), :]


def _window(buf, sh, off, rows, r0=0):
    a, b = divmod(off, 8)
    if b == 0:
        return buf[pl.ds(8 * a + r0, rows), :]
    return sh[b - 1, pl.ds(8 * a + r0, rows), :]


def _conv_fwd(ag, wdw, vecs, gmat, tm):
    s = ag.shape[0]
    hb = tm // HALO

    def body(cur_ref, halo_ref, w_ref, vec_ref, g_ref, ypre_ref, conv_ref, buf, sh):
        i = pl.program_id(0)
        ug = cur_ref[:, 0:CONV_W] * _sigmoid(cur_ref[:, CONV_W:2 * CONV_W])
        ugh = halo_ref[:, 0:CONV_W] * _sigmoid(halo_ref[:, CONV_W:2 * CONV_W])
        buf[0:HALO, :] = jnp.where(i > 0, ugh, 0.0)
        buf[HALO:HALO + tm, :] = ug
        _shifted_copies(buf, sh, tm)
        y = jnp.broadcast_to(vec_ref[0:1, :], (tm, CONV_W))
        for j in range(CONV_K):
            y = y + w_ref[j:j + 1, :] * _window(buf, sh, HALO - (CONV_K - 1) + j, tm)
        ypre_ref[...] = y
        _, _, yn = _gn_fwd(y, vec_ref, g_ref)
        conv_ref[...] = yn * _sigmoid(yn)

    return pl.pallas_call(
        body, name="conv_fwd", grid=(s // tm,),
        in_specs=[pl.BlockSpec((tm, 2 * CONV_W), lambda i: (i, 0)),
                  pl.BlockSpec((HALO, 2 * CONV_W), lambda i: (jnp.maximum(i * hb - 1, 0), 0)),
                  pl.BlockSpec((32, CONV_W), lambda i: (0, 0)),
                  pl.BlockSpec((8, CONV_W), lambda i: (0, 0)),
                  pl.BlockSpec((CONV_W, CONV_W), lambda i: (0, 0))],
        out_specs=[pl.BlockSpec((tm, CONV_W), lambda i: (i, 0))] * 2,
        out_shape=[jax.ShapeDtypeStruct((s, CONV_W), F32)] * 2,
        scratch_shapes=[pltpu.VMEM((HALO + tm, CONV_W), F32), pltpu.VMEM((7, tm + SHIFT_ROWS, CONV_W), F32)],
        compiler_params=_params(("parallel",), VMEM_MID),
    )(ag, ag, wdw, vecs, gmat)


def _mix_fwd(o, conv, x, w_out, gvec, mod, lnv, tm):
    s = x.shape[0]

    def body(o_ref, c_ref, x_ref, w_ref, g_ref, mod_ref, ln_ref, cat_ref, mixed_ref, z1_ref, x1_ref):
        ov = o_ref[...]
        cv = c_ref[...]
        ra = ov * lax.rsqrt(jnp.mean(ov * ov, axis=1, keepdims=True) + LN_EPS) * g_ref[0:1, :]
        rc = cv * lax.rsqrt(jnp.mean(cv * cv, axis=1, keepdims=True) + LN_EPS) * g_ref[1:2, :]
        rab = ra.astype(BF16)
        rcb = rc.astype(BF16)
        cat_ref[:, 0:ATTN_W] = rab
        cat_ref[:, ATTN_W:D_MODEL] = rcb
        mixed = _dot(rab, w_ref[0:ATTN_W, :]) + _dot(rcb, w_ref[ATTN_W:D_MODEL, :])
        mixed_ref[...] = mixed
        z1 = ALPHA * x_ref[...] + (1.0 + mod_ref[2:3, :]) * mixed
        z1_ref[...] = z1
        xh, _ = _ln_norm(z1)
        x1_ref[...] = xh * ln_ref[0:1, :] + ln_ref[1:2, :]

    row = lambda w: pl.BlockSpec((tm, w), lambda i: (i, 0))
    return pl.pallas_call(
        body, name="mix_fwd", grid=(s // tm,),
        in_specs=[row(ATTN_W), row(CONV_W), row(D_MODEL), VMEM_WHOLE,
                  pl.BlockSpec((8, ATTN_W), lambda i: (0, 0)), pl.BlockSpec((8, D_MODEL), lambda i: (0, 0)),
                  pl.BlockSpec((8, D_MODEL), lambda i: (0, 0))],
        out_specs=[row(D_MODEL)] * 4,
        out_shape=[jax.ShapeDtypeStruct((s, D_MODEL), BF16)] + [jax.ShapeDtypeStruct((s, D_MODEL), F32)] * 3,
        compiler_params=_params(("parallel",), VMEM_MID),
    )(o, conv, x, w_out, gvec, mod, lnv)


def _ff_fwd(x1, tgt, w1, w2, mod, lnv, tm):
    s = x1.shape[0]

    def body(x1_ref, t_ref, w1_ref, w2_ref, mod_ref, ln_ref, u2_ref, hid_ref, dff_ref, dxa_ref, st_ref):
        i = pl.program_id(0)
        x1v = x1_ref[...]
        u2 = (x1v * (1.0 + mod_ref[4:5, :]) + mod_ref[3:4, :]).astype(BF16)
        u2_ref[...] = u2
        ff = jnp.zeros((tm, D_MODEL), F32)
        for d in range(N_DEV):
            h = _dot(u2, w1_ref[d])
            hr = jnp.maximum(h, 0.0)
            hb = (hr * hr).astype(BF16)
            hid_ref[:, d * FF_SHARD:(d + 1) * FF_SHARD] = hb
            ff = ff + _dot(hb, w2_ref[d * FF_SHARD:(d + 1) * FF_SHARD, :])
        gate = 1.0 + mod_ref[5:6, :]
        z2 = ALPHA * x1v + gate * ff
        xh, rstd = _ln_norm(z2)
        y = xh * ln_ref[2:3, :] + ln_ref[3:4, :]
        diff = y - t_ref[...]
        dy = diff * (1.0 / D_MODEL)
        dz2 = _ln_bwd(dy * ln_ref[2:3, :], xh, rstd)
        dff_ref[...] = (gate * dz2).astype(BF16)
        dxa_ref[...] = ALPHA * dz2

        @pl.when(i == 0)
        def _():
            st_ref[...] = jnp.zeros_like(st_ref)

        _acc_rows(st_ref, 0, _rowsum(dy * xh))
        _acc_rows(st_ref, 1, _rowsum(dy))
        _acc_rows(st_ref, 2, _rowsum(dz2 * ff))
        sq = _rowsum(jnp.sum(diff * diff, axis=1, keepdims=True))
        _acc_rows(st_ref, 3, jnp.broadcast_to(sq * (0.5 / D_MODEL), (1, D_MODEL)))

    row = lambda w: pl.BlockSpec((tm, w), lambda i: (i, 0))
    vec = pl.BlockSpec((8, D_MODEL), lambda i: (0, 0))
    return pl.pallas_call(
        body, name="ff_fwd", grid=(s // tm,),
        in_specs=[row(D_MODEL), row(D_MODEL), VMEM_WHOLE, VMEM_WHOLE, vec, vec],
        out_specs=[row(D_MODEL), row(D_FF), row(D_MODEL), row(D_MODEL), vec],
        out_shape=[jax.ShapeDtypeStruct((s, D_MODEL), BF16), jax.ShapeDtypeStruct((s, D_FF), BF16),
                   jax.ShapeDtypeStruct((s, D_MODEL), BF16), jax.ShapeDtypeStruct((s, D_MODEL), F32),
                   jax.ShapeDtypeStruct((8, D_MODEL), F32)],
        compiler_params=_params(("arbitrary",), VMEM_BIG),
    )(x1, tgt, w1, w2, mod, lnv)


def _ff_bwd(dff, hid, w1, w2, dxa, x1, z1, mixed, mod, lnv, tm):
    s = x1.shape[0]

    def body(dff_ref, hid_ref, w1_ref, w2_ref, dxa_ref, x1_ref, z1_ref, mixed_ref, mod_ref, ln_ref,
             dh_ref, dxb_ref, dmix_ref, st_ref):
        i = pl.program_id(0)
        dffv = dff_ref[...]
        du2 = jnp.zeros((tm, D_MODEL), F32)
        for d in range(N_DEV):
            cols = slice(d * FF_SHARD, (d + 1) * FF_SHARD)
            dhid = _dot_nt(dffv, w2_ref[cols, :])
            dh = (dhid * (2.0 * jnp.sqrt(hid_ref[:, cols].astype(F32)))).astype(BF16)
            dh_ref[:, cols] = dh
            du2 = du2 + _dot_nt(dh, w1_ref[d])
        x1v = x1_ref[...]
        dx1 = dxa_ref[...] + du2 * (1.0 + mod_ref[4:5, :])
        xh, rstd = _ln_norm(z1_ref[...])
        dz1 = _ln_bwd(dx1 * ln_ref[0:1, :], xh, rstd)
        dxb_ref[...] = ALPHA * dz1
        dmix_ref[...] = ((1.0 + mod_ref[2:3, :]) * dz1).astype(BF16)

        @pl.when(i == 0)
        def _():
            st_ref[...] = jnp.zeros_like(st_ref)

        _acc_rows(st_ref, 0, _rowsum(du2 * x1v))
        _acc_rows(st_ref, 1, _rowsum(du2))
        _acc_rows(st_ref, 2, _rowsum(dx1 * xh))
        _acc_rows(st_ref, 3, _rowsum(dx1))
        _acc_rows(st_ref, 4, _rowsum(dz1 * mixed_ref[...]))

    row = lambda w: pl.BlockSpec((tm, w), lambda i: (i, 0))
    vec = pl.BlockSpec((8, D_MODEL), lambda i: (0, 0))
    return pl.pallas_call(
        body, name="ff_bwd", grid=(s // tm,),
        in_specs=[row(D_MODEL), row(D_FF), VMEM_WHOLE, VMEM_WHOLE, row(D_MODEL), row(D_MODEL), row(D_MODEL),
                  row(D_MODEL), vec, vec],
        out_specs=[row(D_FF), row(D_MODEL), row(D_MODEL), vec],
        out_shape=[jax.ShapeDtypeStruct((s, D_FF), BF16), jax.ShapeDtypeStruct((s, D_MODEL), F32),
                   jax.ShapeDtypeStruct((s, D_MODEL), BF16), jax.ShapeDtypeStruct((8, D_MODEL), F32)],
        compiler_params=_params(("arbitrary",), VMEM_BIG),
    )(dff, hid, w1, w2, dxa, x1, z1, mixed, mod, lnv)


def _mix_bwd(dmix, w_out, o, conv, gvec, tm):
    s = o.shape[0]

    def body(dm_ref, w_ref, o_ref, c_ref, g_ref, do_ref, dc_ref, st_ref):
        i = pl.program_id(0)
        dmv = dm_ref[...]

        @pl.when(i == 0)
        def _():
            st_ref[...] = jnp.zeros_like(st_ref)

        for part, (src, dst) in enumerate(((o_ref, do_ref), (c_ref, dc_ref))):
            dr = _dot_nt(dmv, w_ref[part * ATTN_W:(part + 1) * ATTN_W, :])
            v = src[...]
            rr = lax.rsqrt(jnp.mean(v * v, axis=1, keepdims=True) + LN_EPS)
            vh = v * rr
            _acc_rows(st_ref, part, _rowsum(dr * vh))
            t = dr * g_ref[part:part + 1, :]
            dst[...] = rr * (t - vh * jnp.mean(t * vh, axis=1, keepdims=True))

    row = lambda w: pl.BlockSpec((tm, w), lambda i: (i, 0))
    vec = pl.BlockSpec((8, ATTN_W), lambda i: (0, 0))
    return pl.pallas_call(
        body, name="mix_bwd", grid=(s // tm,),
        in_specs=[row(D_MODEL), VMEM_WHOLE, row(ATTN_W), row(CONV_W), vec],
        out_specs=[row(ATTN_W), row(CONV_W), vec],
        out_shape=[jax.ShapeDtypeStruct((s, ATTN_W), F32), jax.ShapeDtypeStruct((s, CONV_W), F32),
                   jax.ShapeDtypeStruct((8, ATTN_W), F32)],
        compiler_params=_params(("arbitrary",), VMEM_MID),
    )(dmix, w_out, o, conv, gvec)


def _conv_bwd(dconv, ypre, ag, wdw, vecs, gmat, tm):
    s = ag.shape[0]
    hb = tm // HALO
    nt = s // tm
    last_halo = s // HALO - 1

    def body(dc_ref, dch_ref, yp_ref, yph_ref, cur_ref, halo_ref, w_ref, vec_ref, g_ref,
             dag_ref, st_ref, dw_ref, bufu, bufd, shu, shd, sgbuf, dwacc):
        i = pl.program_id(0)

        def dyc_of(yp, dc):
            yh, rstd, yn = _gn_fwd(yp, vec_ref, g_ref)
            sg = _sigmoid(yn)
            dyn = dc * (sg * (1.0 + yn * (1.0 - sg)))
            dyh = dyn * vec_ref[1:2, :]
            dyc = rstd * (dyh - _dot3(dyh, g_ref) - yh * _dot3(dyh * yh, g_ref))
            return dyc, dyn, yh

        dyc, dyn, yh = dyc_of(yp_ref[...], dc_ref[...])
        dych, _, _ = dyc_of(yph_ref[...], dch_ref[...])
        bufd[0:tm, :] = dyc
        bufd[tm:tm + HALO, :] = jnp.where(i < nt - 1, dych, 0.0)

        av = cur_ref[:, 0:CONV_W]
        sg_g = _sigmoid(cur_ref[:, CONV_W:2 * CONV_W])
        ugh = halo_ref[:, 0:CONV_W] * _sigmoid(halo_ref[:, CONV_W:2 * CONV_W])
        bufu[0:HALO, :] = jnp.where(i > 0, ugh, 0.0)
        bufu[HALO:HALO + tm, :] = av * sg_g

        @pl.when(i == 0)
        def _():
            st_ref[...] = jnp.zeros_like(st_ref)
            dw_ref[...] = jnp.zeros_like(dw_ref)

        _acc_rows(st_ref, 0, _rowsum(dyc))
        _acc_rows(st_ref, 1, _rowsum(dyn * yh))
        _acc_rows(st_ref, 2, _rowsum(dyn))

        _shifted_copies(bufd, shd, tm)
        _shifted_copies(bufu, shu, tm)
        sgbuf[...] = sg_g
        dwacc[...] = jnp.zeros_like(dwacc)

        def chunk(c, carry):
            r0 = pl.multiple_of(c * CONV_CHUNK, CONV_CHUNK)
            dyc_c = bufd[pl.ds(r0, CONV_CHUNK), :]
            dug = jnp.zeros((CONV_CHUNK, CONV_W), F32)
            for j in range(CONV_K):
                dug = dug + w_ref[j:j + 1, :] * _window(bufd, shd, CONV_K - 1 - j, CONV_CHUNK, r0)
                prod = dyc_c * _window(bufu, shu, HALO - (CONV_K - 1) + j, CONV_CHUNK, r0)
                part = prod[0:8]
                for g8 in range(1, CONV_CHUNK // 8):
                    part = part + prod[8 * g8:8 * g8 + 8]
                dwacc[j] = dwacc[j] + part
            a_c = cur_ref[pl.ds(r0, CONV_CHUNK), 0:CONV_W]
            sg_c = sgbuf[pl.ds(r0, CONV_CHUNK), :]
            dag_ref[pl.ds(r0, CONV_CHUNK), 0:CONV_W] = (dug * sg_c).astype(BF16)
            dag_ref[pl.ds(r0, CONV_CHUNK), CONV_W:2 * CONV_W] = (dug * a_c * sg_c * (1.0 - sg_c)).astype(BF16)
            return carry

        lax.fori_loop(0, tm // CONV_CHUNK, chunk, 0)
        for j in range(CONV_K):
            _acc_rows(dw_ref, j, _rowsum(dwacc[j]))

    row = lambda w: pl.BlockSpec((tm, w), lambda i: (i, 0))
    nxt = lambda w: pl.BlockSpec((HALO, w), lambda i: (jnp.minimum((i + 1) * hb, last_halo), 0))
    return pl.pallas_call(
        body, name="conv_bwd", grid=(nt,),
        in_specs=[row(CONV_W), nxt(CONV_W), row(CONV_W), nxt(CONV_W), row(2 * CONV_W),
                  pl.BlockSpec((HALO, 2 * CONV_W), lambda i: (jnp.maximum(i * hb - 1, 0), 0)),
                  pl.BlockSpec((32, CONV_W), lambda i: (0, 0)),
                  pl.BlockSpec((8, CONV_W), lambda i: (0, 0)),
                  pl.BlockSpec((CONV_W, CONV_W), lambda i: (0, 0))],
        out_specs=[row(2 * CONV_W), pl.BlockSpec((8, CONV_W), lambda i: (0, 0)),
                   pl.BlockSpec((32, CONV_W), lambda i: (0, 0))],
        out_shape=[jax.ShapeDtypeStruct((s, 2 * CONV_W), BF16), jax.ShapeDtypeStruct((8, CONV_W), F32),
                   jax.ShapeDtypeStruct((32, CONV_W), F32)],
        scratch_shapes=[pltpu.VMEM((HALO + tm, CONV_W), F32), pltpu.VMEM((tm + HALO, CONV_W), F32),
                        pltpu.VMEM((7, tm + SHIFT_ROWS, CONV_W), F32), pltpu.VMEM((7, tm + SHIFT_ROWS, CONV_W), F32),
                        pltpu.VMEM((tm, CONV_W), F32), pltpu.VMEM((32, 8, CONV_W), F32)],
        compiler_params=_params(("arbitrary",), VMEM_BIG),
    )(dconv, dconv, ypre, ypre, ag, ag, wdw, vecs, gmat)


def _attn_bwd(qkv, cum_t, o, do, lse, tq, tk, grads):
    s = qkv.shape[0]
    nq = s // tq
    r = tq // tk
    n = len(grads)

    def body(q_ref, k_ref, v_ref, cum_ref, o_ref, do_ref, lse_ref, *rest):
        dq_ref, dkt_ref, dvt_ref, dcum_ref, drow_ref = rest[n:n + 5]
        hp = pl.program_id(0)
        qi = pl.program_id(1)
        start, wait = _exchange(rest[:n], rest[n + 5:2 * n + 5], *rest[2 * n + 5:], gather=False)
        pl.when(jnp.logical_and(hp == 0, qi == 0))(start)

        @pl.when(qi == 0)
        def _():
            dkt_ref[...] = jnp.zeros_like(dkt_ref)
            dvt_ref[...] = jnp.zeros_like(dvt_ref)
            dcum_ref[...] = jnp.zeros_like(dcum_ref)

        qs = q_ref[...] * 0.125
        ov = o_ref[...]
        dob = do_ref[...].astype(BF16)
        lane = lax.broadcasted_iota(jnp.int32, (1, 128), 1)
        subl = lax.broadcasted_iota(jnp.int32, (128, 1), 0)
        hmask = (lane < 64, lane >= 64)
        aux = (64, 0)
        ones_aux = [jnp.where(lane == aux[h], 1.0, 0.0) for h in range(2)]
        qm, dom, delta, lse_h, qat, dot_t = [], [], [], [], [], []
        for h in range(2):
            qm.append(jnp.where(hmask[h], qs, jnp.zeros_like(qs)))
            dom.append(jnp.where(hmask[h], dob, jnp.zeros_like(dob)))
            delta.append(jnp.sum(dom[h].astype(F32) * ov, axis=1, keepdims=True))
            lse_h.append(lse_ref[:, h:h + 1])
            qat.append(jnp.where(hmask[h], qs.astype(F32), ones_aux[h]).T.astype(BF16))
            dot_t.append(dom[h].astype(F32).T.astype(BF16))

        def step(kb, carry, r0=None):
            lo = 0 if r0 is None else r0
            k0 = pl.multiple_of(kb * tk, tk)
            k2 = k_ref[pl.ds(k0, tk), :]
            v2 = v_ref[pl.ds(k0, tk), :]
            dq_new, a, b = [], [], []
            for h in range(2):
                cs = cum_ref[pl.ds(2 * hp + h, 1), pl.ds(k0, tk)]
                sc = _dot_nt(qm[h][lo:], k2) - cs
                p = jnp.exp(sc - lse_h[h][lo:])
                if r0 is not None:
                    visible = (lax.broadcasted_iota(jnp.int32, (tq - lo, tk), 1)
                               <= lax.broadcasted_iota(jnp.int32, (tq - lo, tk), 0))
                    p = jnp.where(visible, p, 0.0)
                dp = _dot_nt(dom[h][lo:], v2)
                ds = p * (dp - delta[h][lo:])
                pb = p.astype(BF16)
                dsb = ds.astype(BF16)
                a.append(_dot(qat[h][:, lo:], dsb))
                b.append(_dot(dot_t[h][:, lo:], pb))
                ka = jnp.where(hmask[h], k2, ones_aux[h].astype(k2.dtype))
                dq_h = carry[h][lo:] + _dot(dsb, ka)
                dq_new.append(jnp.concatenate([carry[h][:lo], dq_h], axis=0) if lo else dq_h)
            dkt_ref[:, pl.ds(k0, tk)] = dkt_ref[:, pl.ds(k0, tk)] + jnp.where(subl < 64, a[0], a[1])
            dvt_ref[:, pl.ds(k0, tk)] = dvt_ref[:, pl.ds(k0, tk)] + (b[0] + b[1])
            for h in range(2):
                dcum_ref[pl.ds(h, 1), pl.ds(k0, tk)] = (dcum_ref[pl.ds(h, 1), pl.ds(k0, tk)]
                                                        - a[h][aux[h]:aux[h] + 1, :])
            return tuple(dq_new)

        init = (jnp.zeros((tq, 128), F32),) * 2
        carry = lax.fori_loop(0, qi * r, step, init)
        for d in range(r):
            carry = step(qi * r + d, carry, r0=d * tk)
        dq0, dq1 = carry
        dq_ref[...] = (jnp.where(hmask[0], dq0, dq1) * 0.125).astype(BF16)
        drow_ref[:, 0:1] = dq0[:, aux[0]:aux[0] + 1]
        drow_ref[:, 1:2] = dq1[:, aux[1]:aux[1] + 1]
        pl.when(jnp.logical_and(hp == 3, qi == nq - 1))(wait)

    return pl.pallas_call(
        body, name="attn_bwd", grid=(4, nq),
        in_specs=[pl.BlockSpec((tq, 128), lambda h, i: (i, h)),
                  pl.BlockSpec((s, 128), lambda h, i: (0, 4 + h)),
                  pl.BlockSpec((s, 128), lambda h, i: (0, 8 + h)),
                  pl.BlockSpec((8, s), lambda h, i: (0, 0)),
                  pl.BlockSpec((tq, 128), lambda h, i: (i, h)),
                  pl.BlockSpec((tq, 128), lambda h, i: (i, h)),
                  pl.BlockSpec((None, tq, 2), lambda h, i: (h, i, 0))] + [ANY_SPACE] * n,
        out_specs=[pl.BlockSpec((tq, 128), lambda h, i: (i, h)),
                   pl.BlockSpec((128, s), lambda h, i: (h, 0)),
                   pl.BlockSpec((128, s), lambda h, i: (h, 0)),
                   pl.BlockSpec((None, 8, s), lambda h, i: (h, 0, 0)),
                   pl.BlockSpec((None, tq, 2), lambda h, i: (h, i, 0))] + [ANY_SPACE] * n,
        out_shape=[jax.ShapeDtypeStruct((s, ATTN_W), BF16), jax.ShapeDtypeStruct((ATTN_W, s), F32),
                   jax.ShapeDtypeStruct((ATTN_W, s), F32), jax.ShapeDtypeStruct((4, 8, s), F32),
                   jax.ShapeDtypeStruct((4, s, 2), F32)]
        + [jax.ShapeDtypeStruct(g.shape, g.dtype) for g in grads],
        scratch_shapes=_exchange_scratch(n),
        compiler_params=_params(("arbitrary", "arbitrary"), VMEM_BIG),
    )(qkv, qkv, qkv, cum_t, o, do, lse, *grads)


def _fgate_bwd(dcum_t, drow_t, sneg_t):
    s = dcum_t.shape[1]

    def body(dc_ref, dr_ref, sn_ref, df_ref, db_ref):
        lane = lax.broadcasted_iota(jnp.int32, (8, s), 1)
        acc = dc_ref[...] + dr_ref[...]
        sh = 1
        while sh < s:
            acc = acc + jnp.where(lane < s - sh, pltpu.roll(acc, s - sh, axis=1), 0.0)
            sh *= 2
        df = acc * sn_ref[...]
        df_ref[...] = df
        db_ref[...] = jnp.broadcast_to(jnp.sum(df, axis=1, keepdims=True), (8, 128))

    return pl.pallas_call(
        body, name="fgate_bwd",
        out_shape=[jax.ShapeDtypeStruct((8, s), F32), jax.ShapeDtypeStruct((8, 128), F32)],
        in_specs=[VMEM_WHOLE] * 3, out_specs=[VMEM_WHOLE] * 2,
    )(dcum_t, drow_t, sneg_t)


def _in_proj_bwd(dq, dkt, dvt, dag, dft, wqkv, wkvt, wag, wft, x, dxb, mod, tm):
    s = x.shape[0]

    def body(dq_ref, dkt_ref, dvt_ref, dag_ref, dft_ref, wqkv_ref, wkvt_ref, wag_ref, wft_ref, x_ref, dxb_ref,
             mod_ref, dkvt_ref, dx_ref, st_ref):
        i = pl.program_id(0)
        dkb = dkt_ref[...].astype(BF16)
        dvb = dvt_ref[...].astype(BF16)
        dkvt_ref[0:ATTN_W, :] = dkb
        dkvt_ref[ATTN_W:2 * ATTN_W, :] = dvb
        du = _dot_nt(dq_ref[...], wqkv_ref[:, 0:ATTN_W])
        du = du + _dot_tn(dkb, wkvt_ref[0:ATTN_W, :])
        du = du + _dot_tn(dvb, wkvt_ref[ATTN_W:2 * ATTN_W, :])
        du = du + _dot_nt(dag_ref[...], wag_ref[...])
        du = du + _dot_tn(dft_ref[...].astype(BF16), wft_ref[...])
        xv = x_ref[...]
        dx_ref[...] = dxb_ref[...] + du * (1.0 + mod_ref[1:2, :])

        @pl.when(i == 0)
        def _():
            st_ref[...] = jnp.zeros_like(st_ref)

        _acc_rows(st_ref, 0, _rowsum(du * xv))
        _acc_rows(st_ref, 1, _rowsum(du))

    row = lambda w: pl.BlockSpec((tm, w), lambda i: (i, 0))
    col = lambda h: pl.BlockSpec((h, tm), lambda i: (0, i))
    vec = pl.BlockSpec((8, D_MODEL), lambda i: (0, 0))
    return pl.pallas_call(
        body, name="in_proj_bwd", grid=(s // tm,),
        in_specs=[row(ATTN_W), col(ATTN_W), col(ATTN_W), row(2 * CONV_W), col(8),
                  VMEM_WHOLE, VMEM_WHOLE, VMEM_WHOLE, VMEM_WHOLE, row(D_MODEL), row(D_MODEL), vec],
        out_specs=[col(2 * ATTN_W), row(D_MODEL), vec],
        out_shape=[jax.ShapeDtypeStruct((2 * ATTN_W, s), BF16), jax.ShapeDtypeStruct((s, D_MODEL), F32),
                   jax.ShapeDtypeStruct((8, D_MODEL), F32)],
        compiler_params=_params(("arbitrary",), VMEM_MID),
    )(dq, dkt, dvt, dag, dft, wqkv, wkvt, wag, wft, x, dxb, mod)


def _wgrad_nn(a, b, name, tm, tn, tk):
    m, s = a.shape
    n = b.shape[1]
    nk = s // tk

    def body(a_ref, b_ref, o_ref, acc_ref):
        k = pl.program_id(2)

        @pl.when(k == 0)
        def _():
            acc_ref[...] = jnp.zeros_like(acc_ref)

        acc_ref[...] += _dot(a_ref[...], b_ref[...])

        @pl.when(k == nk - 1)
        def _():
            o_ref[...] = acc_ref[...].astype(o_ref.dtype)

    return pl.pallas_call(
        body, name=name, grid=(m // tm, n // tn, nk),
        in_specs=[pl.BlockSpec((tm, tk), lambda i, j, k: (i, k)), pl.BlockSpec((tk, tn), lambda i, j, k: (k, j))],
        out_specs=pl.BlockSpec((tm, tn), lambda i, j, k: (i, j)),
        out_shape=jax.ShapeDtypeStruct((m, n), BF16),
        scratch_shapes=[pltpu.VMEM((tm, tn), F32)],
        compiler_params=_params(("parallel", "parallel", "arbitrary"), VMEM_MID),
    )(a, b)


def _wgrad_tn(a, b, name, tm, tn, tk, out_shape, out_spec):
    s, m = a.shape
    n = b.shape[1]
    nk = s // tk

    def body(a_ref, b_ref, o_ref, acc_ref):
        k = pl.program_id(2)

        @pl.when(k == 0)
        def _():
            acc_ref[...] = jnp.zeros_like(acc_ref)

        acc_ref[...] += _dot_tn(a_ref[...], b_ref[...])

        @pl.when(k == nk - 1)
        def _():
            o_ref[...] = acc_ref[...].astype(o_ref.dtype)

    return pl.pallas_call(
        body, name=name, grid=(m // tm, n // tn, nk),
        in_specs=[pl.BlockSpec((tk, tm), lambda i, j, k: (k, i)), pl.BlockSpec((tk, tn), lambda i, j, k: (k, j))],
        out_specs=out_spec, out_shape=out_shape,
        scratch_shapes=[pltpu.VMEM((tm, tn), F32)],
        compiler_params=_params(("parallel", "parallel", "arbitrary"), VMEM_MID),
    )(a, b)


def _wgrad_f(dft, u, tk):
    s = u.shape[0]
    nk = s // tk

    def body(d_ref, u_ref, o_ref, acc_ref):
        k = pl.program_id(0)

        @pl.when(k == 0)
        def _():
            acc_ref[...] = jnp.zeros_like(acc_ref)

        acc_ref[...] += _dot(d_ref[...].astype(BF16), u_ref[...])

        @pl.when(k == nk - 1)
        def _():
            o_ref[...] = acc_ref[...].astype(BF16)

    return pl.pallas_call(
        body, name="wgrad_f", grid=(nk,),
        in_specs=[pl.BlockSpec((8, tk), lambda k: (0, k)), pl.BlockSpec((tk, D_MODEL), lambda k: (k, 0))],
        out_specs=pl.BlockSpec((8, D_MODEL), lambda k: (0, 0)),
        out_shape=jax.ShapeDtypeStruct((8, D_MODEL), BF16),
        scratch_shapes=[pltpu.VMEM((8, D_MODEL), F32)],
        compiler_params=_params(("arbitrary",)),
    )(dft, u)


def _adamw_big(recv, w, m, v, name, tr):
    r, cdim = w.shape

    def body(g_ref, w_ref, m_ref, v_ref, go_ref, d_ref, mo_ref, vo_ref):
        g = g_ref[0].astype(F32)
        for k in range(1, N_DEV):
            g = g + g_ref[k].astype(F32)
        delta, m2, v2 = _adamw(w_ref[...], g, m_ref[...], v_ref[...])
        go_ref[...] = g
        d_ref[...] = delta
        mo_ref[...] = m2
        vo_ref[...] = v2

    blk = pl.BlockSpec((tr, cdim), lambda i: (i, 0))
    return pl.pallas_call(
        body, name=name, grid=(r // tr,),
        in_specs=[pl.BlockSpec((N_DEV, tr, cdim), lambda i: (0, i, 0)), blk, blk, blk],
        out_specs=[blk] * 4, out_shape=[jax.ShapeDtypeStruct((r, cdim), F32)] * 4,
        compiler_params=_params(("parallel",), VMEM_MID),
    )(recv, w, m, v)


def _adamw_ada(c_all, dada, w, m, v, tr):
    r, cdim = w.shape

    def body(c_ref, d_ref, w_ref, m_ref, v_ref, go_ref, dl_ref, mo_ref, vo_ref):
        cv = c_ref[...]
        sc = (cv * _sigmoid(cv)).astype(BF16)
        g = _dot_tn(sc, d_ref[...].astype(BF16))
        delta, m2, v2 = _adamw(w_ref[...], g, m_ref[...], v_ref[...])
        go_ref[...] = g
        dl_ref[...] = delta
        mo_ref[...] = m2
        vo_ref[...] = v2

    blk = pl.BlockSpec((tr, cdim), lambda i: (i, 0))
    return pl.pallas_call(
        body, name="adamw_ada", grid=(r // tr,),
        in_specs=[pl.BlockSpec((N_DEV, tr), lambda i: (0, i)), pl.BlockSpec((N_DEV, cdim), lambda i: (0, 0)),
                  blk, blk, blk],
        out_specs=[blk] * 4, out_shape=[jax.ShapeDtypeStruct((r, cdim), F32)] * 4,
        compiler_params=_params(("parallel",), VMEM_MID),
    )(c_all, dada, w, m, v)


def _adamw_small(parts, w, m, v):
    n = w.shape[1]

    def body(g_ref, w_ref, m_ref, v_ref, go_ref, d_ref, mo_ref, vo_ref):
        g = g_ref[0]
        for k in range(1, N_DEV):
            g = g + g_ref[k]
        delta, m2, v2 = _adamw(w_ref[...], g, m_ref[...], v_ref[...])
        go_ref[...] = g
        d_ref[...] = delta
        mo_ref[...] = m2
        vo_ref[...] = v2

    return pl.pallas_call(
        body, name="adamw_small", out_shape=[jax.ShapeDtypeStruct((1, n), F32)] * 4,
        in_specs=[VMEM_WHOLE] * 4, out_specs=[VMEM_WHOLE] * 4,
    )(parts, w, m, v)


def _pad_cols(a, n):
    return jnp.pad(a, ((0, 0), (0, n - a.shape[1])))


def _pad_rows(a, n):
    return jnp.pad(a, ((0, n - a.shape[0]), (0, 0)))


def kernel(x, c, w_ada, b_ada, w_in, b_forget, w_dw, b_dw, gn_g, gn_b, g_attn_out, g_conv_out, w_out, ln1_g, ln1_b, w_ff1, w_ff2, ln2_g, ln2_b, loss_target, m_w_ada, m_b_ada, m_w_in, m_b_forget, m_w_dw, m_b_dw, m_gn_g, m_gn_b, m_g_attn_out, m_g_conv_out, m_w_out, m_ln1_g, m_ln1_b, m_w_ff1, m_w_ff2, m_ln2_g, m_ln2_b, v_w_ada, v_b_ada, v_w_in, v_b_forget, v_w_dw, v_b_dw, v_gn_g, v_gn_b, v_g_attn_out, v_g_conv_out, v_w_out, v_ln1_g, v_ln1_b, v_w_ff1, v_w_ff2, v_ln2_g, v_ln2_b):
    s = x.shape[1]
    tm = min(ROW_TILE, s)
    tk_att = min(ATT_TILE, s)
    tq_att = min(ATT_Q_BLOCKS * tk_att, s)
    me = 4 * lax.axis_index("x") + 2 * lax.axis_index("y") + lax.axis_index("c")
    xs = x[0]
    tgt = loss_target[0]

    dw_n = CONV_K * DW_SHARD
    pack1 = jnp.concatenate([c, w_dw[0].reshape(1, dw_n), jnp.zeros((1, 2048 - dw_n), F32)], axis=1)
    g1 = _gather_small(pack1, "gather_c")[:, 0, :]
    c_all = g1[:, :D_MODEL]
    wdw_full = g1[:, D_MODEL:D_MODEL + dw_n].reshape(N_DEV, CONV_K, DW_SHARD)
    wdw_full = _pad_rows(wdw_full.transpose(1, 0, 2).reshape(CONV_K, CONV_W), 32)

    b_shard = lax.dynamic_slice(b_ada, (0, me * ADA_SHARD), (1, ADA_SHARD))
    ada_all = _gather_small(_ada_part(c_all, w_ada[0], b_shard), "gather_ada")
    ada = lax.dynamic_index_in_dim(ada_all, me, axis=1, keepdims=False).reshape(6, D_MODEL)
    mod = _pad_rows(ada, 8)

    gw_in, = _gather_weights([_pad_cols(w_in[0], IN_SHARD_PAD).astype(BF16)])
    w_in_full = gw_in[:, :, :IN_SHARD].transpose(1, 0, 2).reshape(D_MODEL, N_IN)
    wqkv = w_in_full[:, :3 * ATTN_W]
    wft = w_in_full[:, 3 * ATTN_W:3 * ATTN_W + N_HEADS].T
    wag = w_in_full[:, 3 * ATTN_W + N_HEADS:]

    qkv, ag, ft, u = _in_proj(xs, mod, wqkv, wag, wft, tm)
    cum_t, sneg_t = _fgate(ft, b_forget.reshape(N_HEADS, 1))
    o, lse, gw_out, gw_ff1, gw_ff2 = _attn_fwd(
        qkv, cum_t, tq_att, tk_att, [w_out[0].astype(BF16), w_ff1[0].astype(BF16), w_ff2[0].astype(BF16)])
    w_out_full = gw_out.reshape(D_MODEL, D_MODEL)
    w2_full = gw_ff2.reshape(D_FF, D_MODEL)
    cvec = _pad_rows(jnp.concatenate([b_dw, gn_g, gn_b], axis=0), 8)
    grp = jnp.arange(CONV_W) // 64
    gmat = jnp.where(grp[:, None] == grp[None, :], 1.0 / 64.0, 0.0).astype(BF16)
    ypre, conv = _conv_fwd(ag, wdw_full, cvec, gmat, tm)
    gvec = _pad_rows(jnp.concatenate([g_attn_out, g_conv_out], axis=0), 8)
    lnv = _pad_rows(jnp.concatenate([ln1_g, ln1_b, ln2_g, ln2_b], axis=0), 8)
    cat, mixed, z1, x1 = _mix_fwd(o, conv, xs, w_out_full, gvec, mod, lnv, tm)
    tf = min(FF_TILE, s)
    u2, hid, dff, dxa, st_ff = _ff_fwd(x1, tgt, gw_ff1, w2_full, mod, lnv, tf)

    dh, dxb, dmix, st_fb = _ff_bwd(dff, hid, gw_ff1, w2_full, dxa, x1, z1, mixed, mod, lnv, tf)
    do, dconv, st_mix = _mix_bwd(dmix, w_out_full, o, conv, gvec, tm)
    dag, st_conv, dwdw = _conv_bwd(dconv, ypre, ag, wdw_full, cvec, gmat, tm)

    tk = min(WGRAD_K_TILE, s)
    g_ff2 = _wgrad_tn(hid, dff, "wgrad_ff2", 512, 1024, tk,
                      jax.ShapeDtypeStruct((D_FF, D_MODEL), BF16),
                      pl.BlockSpec((512, 1024), lambda i, j, k: (i, j))).reshape(N_DEV, FF_SHARD, D_MODEL)
    g_ff1 = _wgrad_tn(u2, dh, "wgrad_ff1", 1024, 512, tk,
                      jax.ShapeDtypeStruct((N_DEV, D_MODEL, FF_SHARD), BF16),
                      pl.BlockSpec((None, 1024, 512), lambda i, j, k: (j, i, 0)))
    g_out = _wgrad_tn(cat, dmix, "wgrad_out", 512, 1024, tk,
                      jax.ShapeDtypeStruct((D_MODEL, D_MODEL), BF16),
                      pl.BlockSpec((512, 1024), lambda i, j, k: (i, j))).reshape(N_DEV, OUT_SHARD, D_MODEL)
    dq, dkt, dvt, dcum, drow, r_out, r_ff1, r_ff2 = _attn_bwd(
        qkv, cum_t, o, do, lse, tq_att, tk_att, [g_out, g_ff1, g_ff2])
    dft, dbf = _fgate_bwd(dcum[:, :2, :].reshape(N_HEADS, s), drow.transpose(0, 2, 1).reshape(N_HEADS, s), sneg_t)
    wkvt = wqkv[:, ATTN_W:3 * ATTN_W].T
    dkvt, grad_x, st_in = _in_proj_bwd(dq, dkt, dvt, dag, dft, wqkv, wkvt, wag, wft, xs, dxb, mod, tm)

    g_q = _wgrad_tn(u, dq, "wgrad_q", 1024, 512, tk,
                    jax.ShapeDtypeStruct((D_MODEL, ATTN_W), BF16),
                    pl.BlockSpec((1024, 512), lambda i, j, k: (i, j)))
    g_kvt = _wgrad_nn(dkvt, u, "wgrad_kv", 512, 1024, tk)
    g_qkv = jnp.concatenate([g_q, g_kvt.T], axis=1)
    g_ag = _wgrad_tn(u, dag, "wgrad_ag", 1024, 512, tk,
                     jax.ShapeDtypeStruct((D_MODEL, 2 * CONV_W), BF16),
                     pl.BlockSpec((1024, 512), lambda i, j, k: (i, j)))
    g_ft = _wgrad_f(dft, u, tk)
    g_in = jnp.concatenate([g_qkv, g_ft.T, g_ag], axis=1)
    g_in = jnp.pad(g_in.reshape(D_MODEL, N_DEV, IN_SHARD).transpose(1, 0, 2),
                   ((0, 0), (0, 0), (0, IN_SHARD_PAD - IN_SHARD)))

    r_in, = _scatter_grads([g_in])

    dada = jnp.concatenate([st_in[1:2], st_in[0:1], st_fb[4:5], st_fb[1:2], st_fb[0:1], st_ff[2:3]], axis=1)
    loss_part = st_ff[3:4, 0:1]
    pack2 = jnp.concatenate([
        dada,
        _pad_cols(dbf[:, 0].reshape(1, N_HEADS), 128),
        dwdw[:CONV_K].reshape(1, CONV_K * CONV_W),
        st_conv[0:1], st_conv[1:2], st_conv[2:3], st_mix[0:1], st_mix[1:2],
        st_fb[2:3], st_fb[3:4], st_ff[0:1], st_ff[1:2],
        _pad_cols(loss_part, 128)], axis=1)
    g2 = _gather_small(pack2, "gather_small_grads")[:, 0, :]
    o_bf = 6 * D_MODEL
    o_dw = o_bf + 128
    o_v = o_dw + CONV_K * CONV_W
    o_ln = o_v + 5 * CONV_W
    o_loss = o_ln + 4 * D_MODEL
    dw_parts = lax.dynamic_slice_in_dim(
        g2[:, o_dw:o_v].reshape(N_DEV, CONV_K, N_DEV, DW_SHARD), me, 1, axis=2).reshape(N_DEV, dw_n)
    parts = jnp.concatenate([
        g2[:, :o_dw], _pad_cols(dw_parts, 2048), g2[:, o_v:o_loss + 128]], axis=1)[:, None, :]

    def pack_w(b_ada_, b_forget_, w_dw_, smalls):
        return jnp.concatenate([b_ada_, _pad_cols(b_forget_, 128), _pad_cols(w_dw_[0].reshape(1, dw_n), 2048)]
                               + smalls + [jnp.zeros((1, 128), F32)], axis=1)

    pw = pack_w(b_ada, b_forget, w_dw, [b_dw, gn_g, gn_b, g_attn_out, g_conv_out, ln1_g, ln1_b, ln2_g, ln2_b])
    pm = pack_w(m_b_ada, m_b_forget, m_w_dw, [m_b_dw, m_gn_g, m_gn_b, m_g_attn_out, m_g_conv_out,
                                              m_ln1_g, m_ln1_b, m_ln2_g, m_ln2_b])
    pv = pack_w(v_b_ada, v_b_forget, v_w_dw, [v_b_dw, v_gn_g, v_gn_b, v_g_attn_out, v_g_conv_out,
                                              v_ln1_g, v_ln1_b, v_ln2_g, v_ln2_b])
    small = _adamw_small(parts, pw, pm, pv)

    p_bf = 6 * D_MODEL
    p_dw = p_bf + 128
    p_v = p_dw + 2048
    p_ln = p_v + 5 * CONV_W
    p_loss = p_ln + 4 * D_MODEL

    def unpack(t):
        outs = {"b_ada": t[:, :p_bf], "b_forget": t[:, p_bf:p_bf + N_HEADS],
                "w_dw": t[:, p_dw:p_dw + dw_n].reshape(1, CONV_K, 1, DW_SHARD)}
        for k, nm in enumerate(["b_dw", "gn_g", "gn_b", "g_attn_out", "g_conv_out"]):
            outs[nm] = t[:, p_v + k * CONV_W:p_v + (k + 1) * CONV_W]
        for k, nm in enumerate(["ln1_g", "ln1_b", "ln2_g", "ln2_b"]):
            outs[nm] = t[:, p_ln + k * D_MODEL:p_ln + (k + 1) * D_MODEL]
        return outs

    sm = [unpack(t) for t in small]
    loss = small[0][0, p_loss]

    dada_all = g2[:, :6 * D_MODEL]
    dada_shard = lax.dynamic_slice_in_dim(dada_all, me * ADA_SHARD, ADA_SHARD, axis=1)
    big = {
        "w_ada": _adamw_ada(c_all, dada_shard, w_ada[0], m_w_ada[0], v_w_ada[0], 256),
        "w_in": [t[:, :IN_SHARD] for t in _adamw_big(
            r_in, _pad_cols(w_in[0], IN_SHARD_PAD), _pad_cols(m_w_in[0], IN_SHARD_PAD),
            _pad_cols(v_w_in[0], IN_SHARD_PAD), "adamw_in", 256)],
        "w_out": _adamw_big(r_out, w_out[0], m_w_out[0], v_w_out[0], "adamw_out", 128),
        "w_ff1": _adamw_big(r_ff1, w_ff1[0], m_w_ff1[0], v_w_ff1[0], "adamw_ff1", 256),
        "w_ff2": _adamw_big(r_ff2, w_ff2[0], m_w_ff2[0], v_w_ff2[0], "adamw_ff2", 256),
    }

    names = ["w_ada", "b_ada", "w_in", "b_forget", "w_dw", "b_dw", "gn_g", "gn_b", "g_attn_out", "g_conv_out",
             "w_out", "ln1_g", "ln1_b", "w_ff1", "w_ff2", "ln2_g", "ln2_b"]

    def leaf(kind, nm):
        if nm in big:
            return big[nm][kind][None]
        return sm[kind][nm]

    outs = [loss, grad_x[None]]
    for kind in range(4):
        outs += [leaf(kind, nm) for nm in names]
    return tuple(outs)
```

```python
import jax
import jax.numpy as jnp
from jax import lax
from jax.experimental import pallas as pl
from jax.experimental.pallas import tpu as pltpu

F32 = jnp.float32
BF16 = jnp.bfloat16

D_MODEL = 1024
ATTN_W = 512
CONV_W = 512
N_HEADS = 8
CONV_K = 31
D_FF = 4096
N_DEV = 8
N_IN = 3 * ATTN_W + N_HEADS + 2 * CONV_W
IN_SHARD = N_IN // N_DEV
IN_SHARD_PAD = 384
ADA_SHARD = 6 * D_MODEL // N_DEV
FF_SHARD = D_FF // N_DEV
OUT_SHARD = D_MODEL // N_DEV
DW_SHARD = CONV_W // N_DEV
HALO = 32
LN_EPS = 1e-5
ALPHA = 2.0 ** 0.25
ADAM_LR, ADAM_B1, ADAM_B2, ADAM_EPS, ADAM_WD, ADAM_STEP = 0.001, 0.9, 0.999, 1e-08, 0.01, 10

ROW_TILE = 512
FF_TILE = 256
WGRAD_K_TILE = 2048
ATT_TILE = 512
ATT_Q_BLOCKS = 2
VMEM_BIG = 56 * 1024 * 1024
VMEM_MID = 40 * 1024 * 1024

MESH = pl.DeviceIdType.MESH
VMEM_WHOLE = pl.BlockSpec(memory_space=pltpu.VMEM)
ANY_SPACE = pl.BlockSpec(memory_space=pl.ANY)


def _dot(a, b):
    return jnp.dot(a, b, preferred_element_type=F32)


def _dot_nt(a, b):
    return lax.dot_general(a, b, (((1,), (1,)), ((), ())), preferred_element_type=F32)


def _dot_tn(a, b):
    return lax.dot_general(a, b, (((0,), (0,)), ((), ())), preferred_element_type=F32)


def _dot3(xf, g_ref):
    g = g_ref[...]
    hi = xf.astype(BF16)
    lo = (xf - hi.astype(F32)).astype(BF16)
    return _dot(hi, g) + _dot(lo, g)


def _sigmoid(x):
    return 1.0 / (1.0 + jnp.exp(-x))


def _rowsum(x):
    return jnp.sum(x, axis=0, keepdims=True)


def _ln_norm(z):
    mu = jnp.mean(z, axis=1, keepdims=True)
    zc = z - mu
    var = jnp.mean(zc * zc, axis=1, keepdims=True)
    rstd = lax.rsqrt(var + LN_EPS)
    return zc * rstd, rstd


def _ln_bwd(dxh, xh, rstd):
    m1 = jnp.mean(dxh, axis=1, keepdims=True)
    m2 = jnp.mean(dxh * xh, axis=1, keepdims=True)
    return rstd * (dxh - m1 - xh * m2)


def _adamw(w, g, m, v):
    m2 = ADAM_B1 * m + (1.0 - ADAM_B1) * g
    v2 = ADAM_B2 * v + (1.0 - ADAM_B2) * (g * g)
    m_hat = m2 / (1.0 - ADAM_B1 ** ADAM_STEP)
    v_hat = v2 / (1.0 - ADAM_B2 ** ADAM_STEP)
    delta = -ADAM_LR * (m_hat / (jnp.sqrt(v_hat) + ADAM_EPS) + ADAM_WD * w)
    return delta, m2, v2


def _params(sem=None, vmem=None):
    kw = {}
    if sem is not None:
        kw["dimension_semantics"] = sem
    if vmem is not None:
        kw["vmem_limit_bytes"] = vmem
    return pltpu.CompilerParams(**kw)


def _coords():
    return lax.axis_index("x"), lax.axis_index("y"), lax.axis_index("c")


def _flip(v, bit):
    return 1 - v if bit else v


def _acc_rows(ref, r, val):
    ref[r:r + 1, :] = ref[r:r + 1, :] + val


def _gather_small(v, name):
    r, n = v.shape

    def body(v_ref, out_ref, send_sems, recv_sems, local_sem):
        x, y, c = _coords()
        me = 4 * x + 2 * y + c
        mine = pltpu.make_async_copy(v_ref, out_ref.at[me], local_sem)
        mine.start()
        sends = []
        for k in range(1, N_DEV):
            tgt = (_flip(x, (k >> 2) & 1), _flip(y, (k >> 1) & 1), _flip(c, k & 1))
            cp = pltpu.make_async_remote_copy(
                src_ref=v_ref, dst_ref=out_ref.at[me], send_sem=send_sems.at[k - 1],
                recv_sem=recv_sems.at[k - 1], device_id=tgt, device_id_type=MESH)
            cp.start()
            sends.append(cp)
        for k in range(1, N_DEV):
            px, py, pc = _flip(x, (k >> 2) & 1), _flip(y, (k >> 1) & 1), _flip(c, k & 1)
            peer = 4 * px + 2 * py + pc
            pltpu.make_async_remote_copy(
                src_ref=v_ref, dst_ref=out_ref.at[peer], send_sem=send_sems.at[k - 1],
                recv_sem=recv_sems.at[k - 1], device_id=(px, py, pc), device_id_type=MESH).wait_recv()
        for cp in sends:
            cp.wait_send()
        mine.wait()

    return pl.pallas_call(
        body, name=name,
        out_shape=jax.ShapeDtypeStruct((N_DEV, r, n), v.dtype),
        in_specs=[VMEM_WHOLE], out_specs=VMEM_WHOLE,
        scratch_shapes=[pltpu.SemaphoreType.DMA((N_DEV - 1,)), pltpu.SemaphoreType.DMA((N_DEV - 1,)),
                        pltpu.SemaphoreType.DMA],
    )(v)


def _gather_weights(shards):
    n = len(shards)

    def body(*refs):
        ins, outs = refs[:n], refs[n:2 * n]
        send_sems, recv_sems, local_sems = refs[2 * n:]
        x, y, c = _coords()
        me, sibling = (x, y, c), (x, y, 1 - c)
        chips = [(1 - x, y), (x, 1 - y), (1 - x, 1 - y)]

        def slot(a, pos):
            return outs[a].at[4 * pos[0] + 2 * pos[1] + pos[2]]

        def copy(a, k, block, to, src=None):
            return pltpu.make_async_remote_copy(
                src_ref=slot(a, block) if src is None else src, dst_ref=slot(a, block),
                send_sem=send_sems.at[7 * a + k], recv_sem=recv_sems.at[7 * a + k],
                device_id=to, device_id_type=MESH)

        started = []
        local = []
        for a in range(n):
            mine = pltpu.make_async_copy(ins[a], slot(a, me), local_sems.at[a])
            mine.start()
            local.append(mine)
        for j, chip in enumerate(chips):
            for a in range(n):
                cp = copy(a, 1 + j, me, (*chip, c), src=ins[a])
                cp.start()
                started.append(cp)
        for a in range(n):
            cp = copy(a, 0, me, sibling, src=ins[a])
            cp.start()
            started.append(cp)
        for j, chip in enumerate(chips):
            for a in range(n):
                copy(a, 1 + j, (*chip, c), me).wait_recv()
                cp = copy(a, 4 + j, (*chip, c), sibling)
                cp.start()
                started.append(cp)
        for a in range(n):
            copy(a, 0, sibling, me).wait_recv()
        for j, chip in enumerate(chips):
            for a in range(n):
                copy(a, 4 + j, (*chip, 1 - c), me).wait_recv()
        for cp in started:
            cp.wait_send()
        for mine in local:
            mine.wait()

    return pl.pallas_call(
        body, name="gather_weights",
        out_shape=[jax.ShapeDtypeStruct((N_DEV,) + s.shape, s.dtype) for s in shards],
        in_specs=[ANY_SPACE] * n, out_specs=[ANY_SPACE] * n,
        scratch_shapes=[pltpu.SemaphoreType.DMA((7 * n,)), pltpu.SemaphoreType.DMA((7 * n,)),
                        pltpu.SemaphoreType.DMA((n,))],
    )(*shards)


def _exchange(ins, outs, send_sems, recv_sems, local_sems, gather):
    n = len(ins)
    x, y, c = _coords()
    me = 4 * x + 2 * y + c
    local = [pltpu.make_async_copy(ins[a] if gather else ins[a].at[me], outs[a].at[me], local_sems.at[a])
             for a in range(n)]
    sends, recvs = [], []
    for k in range(1, N_DEV):
        px, py, pc = _flip(x, (k >> 2) & 1), _flip(y, (k >> 1) & 1), _flip(c, k & 1)
        peer = 4 * px + 2 * py + pc
        for a in range(n):
            src = ins[a] if gather else ins[a].at[peer]
            sems = dict(send_sem=send_sems.at[7 * a + k - 1], recv_sem=recv_sems.at[7 * a + k - 1],
                        device_id=(px, py, pc), device_id_type=MESH)
            sends.append(pltpu.make_async_remote_copy(src_ref=src, dst_ref=outs[a].at[me], **sems))
            recvs.append(pltpu.make_async_remote_copy(src_ref=src, dst_ref=outs[a].at[peer], **sems))

    def start():
        for cp in local + sends:
            cp.start()

    def wait():
        for cp in recvs:
            cp.wait_recv()
        for cp in sends:
            cp.wait_send()
        for cp in local:
            cp.wait()

    return start, wait


def _exchange_scratch(n):
    return [pltpu.SemaphoreType.DMA((7 * n,)), pltpu.SemaphoreType.DMA((7 * n,)), pltpu.SemaphoreType.DMA((n,))]


def _scatter_grads(grads):
    n = len(grads)

    def body(*refs):
        start, wait = _exchange(refs[:n], refs[n:2 * n], *refs[2 * n:], gather=False)
        start()
        wait()

    return pl.pallas_call(
        body, name="scatter_grads",
        out_shape=[jax.ShapeDtypeStruct(g.shape, g.dtype) for g in grads],
        in_specs=[ANY_SPACE] * n, out_specs=[ANY_SPACE] * n,
        scratch_shapes=_exchange_scratch(n),
    )(*grads)


def _ada_part(c_all, w_ada, b_shard):
    def body(c_ref, w_ref, b_ref, o_ref):
        cv = c_ref[...]
        sc = (cv * _sigmoid(cv)).astype(BF16)
        o_ref[...] = _dot(sc, w_ref[...].astype(BF16)) + b_ref[...]

    return pl.pallas_call(
        body, name="ada_part", out_shape=jax.ShapeDtypeStruct((N_DEV, ADA_SHARD), F32),
        in_specs=[VMEM_WHOLE] * 3, out_specs=VMEM_WHOLE,
        compiler_params=_params(vmem=VMEM_MID),
    )(c_all, w_ada, b_shard)


def _in_proj(x, mod, wqkv, wag, wft, tm):
    s = x.shape[0]

    def body(x_ref, mod_ref, wqkv_ref, wag_ref, wft_ref, qkv_ref, ag_ref, ft_ref, u_ref):
        u = x_ref[...] * (1.0 + mod_ref[1:2, :]) + mod_ref[0:1, :]
        ub = u.astype(BF16)
        u_ref[...] = ub
        qkv_ref[...] = _dot(ub, wqkv_ref[...]).astype(BF16)
        ag_ref[...] = _dot(ub, wag_ref[...])
        ft_ref[...] = _dot_nt(wft_ref[...], ub)

    return pl.pallas_call(
        body, name="in_proj", grid=(s // tm,),
        in_specs=[pl.BlockSpec((tm, D_MODEL), lambda i: (i, 0)), pl.BlockSpec((8, D_MODEL), lambda i: (0, 0)),
                  VMEM_WHOLE, VMEM_WHOLE, VMEM_WHOLE],
        out_specs=[pl.BlockSpec((tm, 3 * ATTN_W), lambda i: (i, 0)),
                   pl.BlockSpec((tm, 2 * CONV_W), lambda i: (i, 0)),
                   pl.BlockSpec((8, tm), lambda i: (0, i)),
                   pl.BlockSpec((tm, D_MODEL), lambda i: (i, 0))],
        out_shape=[jax.ShapeDtypeStruct((s, 3 * ATTN_W), BF16), jax.ShapeDtypeStruct((s, 2 * CONV_W), F32),
                   jax.ShapeDtypeStruct((8, s), F32), jax.ShapeDtypeStruct((s, D_MODEL), BF16)],
        compiler_params=_params(("parallel",), VMEM_MID),
    )(x, mod, wqkv, wag, wft)


def _fgate(ft, bcol):
    s = ft.shape[1]

    def body(f_ref, b_ref, cum_ref, sneg_ref):
        z = f_ref[...] + b_ref[...]
        e = jnp.exp(-jnp.abs(z))
        l1p = jnp.where(e < 1e-2, e * (1.0 - e * (0.5 - e * (1.0 / 3.0))), jnp.log(1.0 + e))
        logf = jnp.minimum(z, 0.0) - l1p
        r = 1.0 / (1.0 + e)
        sneg_ref[...] = jnp.where(z >= 0, e * r, r)
        lane = lax.broadcasted_iota(jnp.int32, (8, s), 1)
        acc = logf
        sh = 1
        while sh < s:
            acc = acc + jnp.where(lane >= sh, pltpu.roll(acc, sh, axis=1), 0.0)
            sh *= 2
        cum_ref[...] = acc

    return pl.pallas_call(
        body, name="fgate", out_shape=[jax.ShapeDtypeStruct((8, s), F32)] * 2,
        in_specs=[VMEM_WHOLE] * 2, out_specs=[VMEM_WHOLE] * 2,
    )(ft, bcol)


def _attn_fwd(qkv, cum_t, tq, tk, shards):
    s = qkv.shape[0]
    nq = s // tq
    r = tq // tk
    n = len(shards)

    def body(q_ref, k_ref, v_ref, cum_ref, *rest):
        o_ref, lse_ref = rest[n:n + 2]
        hp = pl.program_id(0)
        qi = pl.program_id(1)
        start, wait = _exchange(rest[:n], rest[n + 2:2 * n + 2], *rest[2 * n + 2:], gather=True)
        pl.when(jnp.logical_and(hp == 0, qi == 0))(start)
        q2 = q_ref[...]
        lane = lax.broadcasted_iota(jnp.int32, (1, 128), 1)
        hmask = (lane < 64, lane >= 64)
        aux = (64, 0)
        qm = [jnp.where(hmask[h], q2, jnp.zeros_like(q2)) * 0.125 for h in range(2)]

        def step(kb, carry, r0=None):
            lo = 0 if r0 is None else r0
            kw = tq if r0 is None else tk
            k0 = pl.multiple_of(kb * kw, kw)
            k2 = k_ref[pl.ds(k0, kw), :]
            v2 = v_ref[pl.ds(k0, kw), :]
            out = []
            for h in range(2):
                m_all, acc_all = carry[2 * h], carry[2 * h + 1]
                m, acc = m_all[lo:], acc_all[lo:]
                sc = _dot_nt(qm[h][lo:], k2) - cum_ref[pl.ds(2 * hp + h, 1), pl.ds(k0, kw)]
                if r0 is not None:
                    visible = (lax.broadcasted_iota(jnp.int32, (tq - lo, tk), 1)
                               <= lax.broadcasted_iota(jnp.int32, (tq - lo, tk), 0))
                    sc = jnp.where(visible, sc, -jnp.inf)
                m_new = jnp.maximum(m, jnp.max(sc, axis=1, keepdims=True))
                p = jnp.exp(sc - m_new)
                corr = jnp.exp(m - m_new)
                va = jnp.where(hmask[h], v2, jnp.where(lane == aux[h], 1.0, 0.0).astype(v2.dtype))
                acc_new = acc * corr + _dot(p.astype(BF16), va)
                if lo:
                    m_new = jnp.concatenate([m_all[:lo], m_new], axis=0)
                    acc_new = jnp.concatenate([acc_all[:lo], acc_new], axis=0)
                out += [m_new, acc_new]
            return tuple(out)

        init = (jnp.full((tq, 1), -jnp.inf, F32), jnp.zeros((tq, 128), F32)) * 2
        carry = lax.fori_loop(0, qi, step, init)
        for d in range(r):
            carry = step(qi * r + d, carry, r0=d * tk)
        m0, acc0, m1, acc1 = carry
        l0 = acc0[:, aux[0]:aux[0] + 1]
        l1 = acc1[:, aux[1]:aux[1] + 1]
        o_ref[...] = jnp.where(hmask[0], acc0 / l0, acc1 / l1)
        lse_ref[:, 0:1] = m0 + jnp.log(l0)
        lse_ref[:, 1:2] = m1 + jnp.log(l1)
        pl.when(jnp.logical_and(hp == 3, qi == nq - 1))(wait)

    return pl.pallas_call(
        body, name="attn_fwd", grid=(4, nq),
        in_specs=[pl.BlockSpec((tq, 128), lambda h, i: (i, h)),
                  pl.BlockSpec((s, 128), lambda h, i: (0, 4 + h)),
                  pl.BlockSpec((s, 128), lambda h, i: (0, 8 + h)),
                  pl.BlockSpec((8, s), lambda h, i: (0, 0))] + [ANY_SPACE] * n,
        out_specs=[pl.BlockSpec((tq, 128), lambda h, i: (i, h)),
                   pl.BlockSpec((None, tq, 2), lambda h, i: (h, i, 0))] + [ANY_SPACE] * n,
        out_shape=[jax.ShapeDtypeStruct((s, ATTN_W), F32), jax.ShapeDtypeStruct((4, s, 2), F32)]
        + [jax.ShapeDtypeStruct((N_DEV,) + w.shape, w.dtype) for w in shards],
        scratch_shapes=_exchange_scratch(n),
        compiler_params=_params(("arbitrary", "arbitrary"), VMEM_BIG),
    )(qkv, qkv, qkv, cum_t, *shards)


def _gn_fwd(y, vec_ref, g_ref):
    mean = _dot3(y, g_ref)
    yc = y - mean
    var = _dot3(yc * yc, g_ref)
    rstd = lax.rsqrt(var + LN_EPS)
    yh = yc * rstd
    yn = yh * vec_ref[1:2, :] + vec_ref[2:3, :]
    return yh, rstd, yn


SHIFT_ROWS = HALO - 8
CONV_CHUNK = 32


def _shifted_copies(buf, sh, tm):
    for b in range(1, 8):
        sh[b - 1] = buf[pl.ds(b, tm + SHIFT_ROWS), :]


def _window(buf, sh, off, rows, r0=0):
    a, b = divmod(off, 8)
    if b == 0:
        return buf[pl.ds(8 * a + r0, rows), :]
    return sh[b - 1, pl.ds(8 * a + r0, rows), :]


def _conv_fwd(ag, wdw, vecs, gmat, tm):
    s = ag.shape[0]
    hb = tm // HALO

    def body(cur_ref, halo_ref, w_ref, vec_ref, g_ref, ypre_ref, conv_ref, buf, sh):
        i = pl.program_id(0)
        ug = cur_ref[:, 0:CONV_W] * _sigmoid(cur_ref[:, CONV_W:2 * CONV_W])
        ugh = halo_ref[:, 0:CONV_W] * _sigmoid(halo_ref[:, CONV_W:2 * CONV_W])
        buf[0:HALO, :] = jnp.where(i > 0, ugh, 0.0)
        buf[HALO:HALO + tm, :] = ug
        _shifted_copies(buf, sh, tm)
        y = jnp.broadcast_to(vec_ref[0:1, :], (tm, CONV_W))
        for j in range(CONV_K):
            y = y + w_ref[j:j + 1, :] * _window(buf, sh, HALO - (CONV_K - 1) + j, tm)
        ypre_ref[...] = y
        _, _, yn = _gn_fwd(y, vec_ref, g_ref)
        conv_ref[...] = yn * _sigmoid(yn)

    return pl.pallas_call(
        body, name="conv_fwd", grid=(s // tm,),
        in_specs=[pl.BlockSpec((tm, 2 * CONV_W), lambda i: (i, 0)),
                  pl.BlockSpec((HALO, 2 * CONV_W), lambda i: (jnp.maximum(i * hb - 1, 0), 0)),
                  pl.BlockSpec((32, CONV_W), lambda i: (0, 0)),
                  pl.BlockSpec((8, CONV_W), lambda i: (0, 0)),
                  pl.BlockSpec((CONV_W, CONV_W), lambda i: (0, 0))],
        out_specs=[pl.BlockSpec((tm, CONV_W), lambda i: (i, 0))] * 2,
        out_shape=[jax.ShapeDtypeStruct((s, CONV_W), F32)] * 2,
        scratch_shapes=[pltpu.VMEM((HALO + tm, CONV_W), F32), pltpu.VMEM((7, tm + SHIFT_ROWS, CONV_W), F32)],
        compiler_params=_params(("parallel",), VMEM_MID),
    )(ag, ag, wdw, vecs, gmat)


def _mix_fwd(o, conv, x, w_out, gvec, mod, lnv, tm):
    s = x.shape[0]

    def body(o_ref, c_ref, x_ref, w_ref, g_ref, mod_ref, ln_ref, cat_ref, mixed_ref, z1_ref, x1_ref):
        ov = o_ref[...]
        cv = c_ref[...]
        ra = ov * lax.rsqrt(jnp.mean(ov * ov, axis=1, keepdims=True) + LN_EPS) * g_ref[0:1, :]
        rc = cv * lax.rsqrt(jnp.mean(cv * cv, axis=1, keepdims=True) + LN_EPS) * g_ref[1:2, :]
        rab = ra.astype(BF16)
        rcb = rc.astype(BF16)
        cat_ref[:, 0:ATTN_W] = rab
        cat_ref[:, ATTN_W:D_MODEL] = rcb
        mixed = _dot(rab, w_ref[0:ATTN_W, :]) + _dot(rcb, w_ref[ATTN_W:D_MODEL, :])
        mixed_ref[...] = mixed
        z1 = ALPHA * x_ref[...] + (1.0 + mod_ref[2:3, :]) * mixed
        z1_ref[...] = z1
        xh, _ = _ln_norm(z1)
        x1_ref[...] = xh * ln_ref[0:1, :] + ln_ref[1:2, :]

    row = lambda w: pl.BlockSpec((tm, w), lambda i: (i, 0))
    return pl.pallas_call(
        body, name="mix_fwd", grid=(s // tm,),
        in_specs=[row(ATTN_W), row(CONV_W), row(D_MODEL), VMEM_WHOLE,
                  pl.BlockSpec((8, ATTN_W), lambda i: (0, 0)), pl.BlockSpec((8, D_MODEL), lambda i: (0, 0)),
                  pl.BlockSpec((8, D_MODEL), lambda i: (0, 0))],
        out_specs=[row(D_MODEL)] * 4,
        out_shape=[jax.ShapeDtypeStruct((s, D_MODEL), BF16)] + [jax.ShapeDtypeStruct((s, D_MODEL), F32)] * 3,
        compiler_params=_params(("parallel",), VMEM_MID),
    )(o, conv, x, w_out, gvec, mod, lnv)


def _ff_fwd(x1, tgt, w1, w2, mod, lnv, tm):
    s = x1.shape[0]

    def body(x1_ref, t_ref, w1_ref, w2_ref, mod_ref, ln_ref, u2_ref, hid_ref, dff_ref, dxa_ref, st_ref):
        i = pl.program_id(0)
        x1v = x1_ref[...]
        u2 = (x1v * (1.0 + mod_ref[4:5, :]) + mod_ref[3:4, :]).astype(BF16)
        u2_ref[...] = u2
        ff = jnp.zeros((tm, D_MODEL), F32)
        for d in range(N_DEV):
            h = _dot(u2, w1_ref[d])
            hr = jnp.maximum(h, 0.0)
            hb = (hr * hr).astype(BF16)
            hid_ref[:, d * FF_SHARD:(d + 1) * FF_SHARD] = hb
            ff = ff + _dot(hb, w2_ref[d * FF_SHARD:(d + 1) * FF_SHARD, :])
        gate = 1.0 + mod_ref[5:6, :]
        z2 = ALPHA * x1v + gate * ff
        xh, rstd = _ln_norm(z2)
        y = xh * ln_ref[2:3, :] + ln_ref[3:4, :]
        diff = y - t_ref[...]
        dy = diff * (1.0 / D_MODEL)
        dz2 = _ln_bwd(dy * ln_ref[2:3, :], xh, rstd)
        dff_ref[...] = (gate * dz2).astype(BF16)
        dxa_ref[...] = ALPHA * dz2

        @pl.when(i == 0)
        def _():
            st_ref[...] = jnp.zeros_like(st_ref)

        _acc_rows(st_ref, 0, _rowsum(dy * xh))
        _acc_rows(st_ref, 1, _rowsum(dy))
        _acc_rows(st_ref, 2, _rowsum(dz2 * ff))
        sq = _rowsum(jnp.sum(diff * diff, axis=1, keepdims=True))
        _acc_rows(st_ref, 3, jnp.broadcast_to(sq * (0.5 / D_MODEL), (1, D_MODEL)))

    row = lambda w: pl.BlockSpec((tm, w), lambda i: (i, 0))
    vec = pl.BlockSpec((8, D_MODEL), lambda i: (0, 0))
    return pl.pallas_call(
        body, name="ff_fwd", grid=(s // tm,),
        in_specs=[row(D_MODEL), row(D_MODEL), VMEM_WHOLE, VMEM_WHOLE, vec, vec],
        out_specs=[row(D_MODEL), row(D_FF), row(D_MODEL), row(D_MODEL), vec],
        out_shape=[jax.ShapeDtypeStruct((s, D_MODEL), BF16), jax.ShapeDtypeStruct((s, D_FF), BF16),
                   jax.ShapeDtypeStruct((s, D_MODEL), BF16), jax.ShapeDtypeStruct((s, D_MODEL), F32),
                   jax.ShapeDtypeStruct((8, D_MODEL), F32)],
        compiler_params=_params(("arbitrary",), VMEM_BIG),
    )(x1, tgt, w1, w2, mod, lnv)


def _ff_bwd(dff, hid, w1, w2, dxa, x1, z1, mixed, mod, lnv, tm):
    s = x1.shape[0]

    def body(dff_ref, hid_ref, w1_ref, w2_ref, dxa_ref, x1_ref, z1_ref, mixed_ref, mod_ref, ln_ref,
             dh_ref, dxb_ref, dmix_ref, st_ref):
        i = pl.program_id(0)
        dffv = dff_ref[...]
        du2 = jnp.zeros((tm, D_MODEL), F32)
        for d in range(N_DEV):
            cols = slice(d * FF_SHARD, (d + 1) * FF_SHARD)
            dhid = _dot_nt(dffv, w2_ref[cols, :])
            dh = (dhid * (2.0 * jnp.sqrt(hid_ref[:, cols].astype(F32)))).astype(BF16)
            dh_ref[:, cols] = dh
            du2 = du2 + _dot_nt(dh, w1_ref[d])
        x1v = x1_ref[...]
        dx1 = dxa_ref[...] + du2 * (1.0 + mod_ref[4:5, :])
        xh, rstd = _ln_norm(z1_ref[...])
        dz1 = _ln_bwd(dx1 * ln_ref[0:1, :], xh, rstd)
        dxb_ref[...] = ALPHA * dz1
        dmix_ref[...] = ((1.0 + mod_ref[2:3, :]) * dz1).astype(BF16)

        @pl.when(i == 0)
        def _():
            st_ref[...] = jnp.zeros_like(st_ref)

        _acc_rows(st_ref, 0, _rowsum(du2 * x1v))
        _acc_rows(st_ref, 1, _rowsum(du2))
        _acc_rows(st_ref, 2, _rowsum(dx1 * xh))
        _acc_rows(st_ref, 3, _rowsum(dx1))
        _acc_rows(st_ref, 4, _rowsum(dz1 * mixed_ref[...]))

    row = lambda w: pl.BlockSpec((tm, w), lambda i: (i, 0))
    vec = pl.BlockSpec((8, D_MODEL), lambda i: (0, 0))
    return pl.pallas_call(
        body, name="ff_bwd", grid=(s // tm,),
        in_specs=[row(D_MODEL), row(D_FF), VMEM_WHOLE, VMEM_WHOLE, row(D_MODEL), row(D_MODEL), row(D_MODEL),
                  row(D_MODEL), vec, vec],
        out_specs=[row(D_FF), row(D_MODEL), row(D_MODEL), vec],
        out_shape=[jax.ShapeDtypeStruct((s, D_FF), BF16), jax.ShapeDtypeStruct((s, D_MODEL), F32),
                   jax.ShapeDtypeStruct((s, D_MODEL), BF16), jax.ShapeDtypeStruct((8, D_MODEL), F32)],
        compiler_params=_params(("arbitrary",), VMEM_BIG),
    )(dff, hid, w1, w2, dxa, x1, z1, mixed, mod, lnv)


def _mix_bwd(dmix, w_out, o, conv, gvec, tm):
    s = o.shape[0]

    def body(dm_ref, w_ref, o_ref, c_ref, g_ref, do_ref, dc_ref, st_ref):
        i = pl.program_id(0)
        dmv = dm_ref[...]

        @pl.when(i == 0)
        def _():
            st_ref[...] = jnp.zeros_like(st_ref)

        for part, (src, dst) in enumerate(((o_ref, do_ref), (c_ref, dc_ref))):
            dr = _dot_nt(dmv, w_ref[part * ATTN_W:(part + 1) * ATTN_W, :])
            v = src[...]
            rr = lax.rsqrt(jnp.mean(v * v, axis=1, keepdims=True) + LN_EPS)
            vh = v * rr
            _acc_rows(st_ref, part, _rowsum(dr * vh))
            t = dr * g_ref[part:part + 1, :]
            dst[...] = rr * (t - vh * jnp.mean(t * vh, axis=1, keepdims=True))

    row = lambda w: pl.BlockSpec((tm, w), lambda i: (i, 0))
    vec = pl.BlockSpec((8, ATTN_W), lambda i: (0, 0))
    return pl.pallas_call(
        body, name="mix_bwd", grid=(s // tm,),
        in_specs=[row(D_MODEL), VMEM_WHOLE, row(ATTN_W), row(CONV_W), vec],
        out_specs=[row(ATTN_W), row(CONV_W), vec],
        out_shape=[jax.ShapeDtypeStruct((s, ATTN_W), F32), jax.ShapeDtypeStruct((s, CONV_W), F32),
                   jax.ShapeDtypeStruct((8, ATTN_W), F32)],
        compiler_params=_params(("arbitrary",), VMEM_MID),
    )(dmix, w_out, o, conv, gvec)


def _conv_bwd(dconv, ypre, ag, wdw, vecs, gmat, tm):
    s = ag.shape[0]
    hb = tm // HALO
    nt = s // tm
    last_halo = s // HALO - 1

    def body(dc_ref, dch_ref, yp_ref, yph_ref, cur_ref, halo_ref, w_ref, vec_ref, g_ref,
             dag_ref, st_ref, dw_ref, bufu, bufd, shu, shd, sgbuf, dwacc):
        i = pl.program_id(0)

        def dyc_of(yp, dc):
            yh, rstd, yn = _gn_fwd(yp, vec_ref, g_ref)
            sg = _sigmoid(yn)
            dyn = dc * (sg * (1.0 + yn * (1.0 - sg)))
            dyh = dyn * vec_ref[1:2, :]
            dyc = rstd * (dyh - _dot3(dyh, g_ref) - yh * _dot3(dyh * yh, g_ref))
            return dyc, dyn, yh

        dyc, dyn, yh = dyc_of(yp_ref[...], dc_ref[...])
        dych, _, _ = dyc_of(yph_ref[...], dch_ref[...])
        bufd[0:tm, :] = dyc
        bufd[tm:tm + HALO, :] = jnp.where(i < nt - 1, dych, 0.0)

        av = cur_ref[:, 0:CONV_W]
        sg_g = _sigmoid(cur_ref[:, CONV_W:2 * CONV_W])
        ugh = halo_ref[:, 0:CONV_W] * _sigmoid(halo_ref[:, CONV_W:2 * CONV_W])
        bufu[0:HALO, :] = jnp.where(i > 0, ugh, 0.0)
        bufu[HALO:HALO + tm, :] = av * sg_g

        @pl.when(i == 0)
        def _():
            st_ref[...] = jnp.zeros_like(st_ref)
            dw_ref[...] = jnp.zeros_like(dw_ref)

        _acc_rows(st_ref, 0, _rowsum(dyc))
        _acc_rows(st_ref, 1, _rowsum(dyn * yh))
        _acc_rows(st_ref, 2, _rowsum(dyn))

        _shifted_copies(bufd, shd, tm)
        _shifted_copies(bufu, shu, tm)
        sgbuf[...] = sg_g
        dwacc[...] = jnp.zeros_like(dwacc)

        def chunk(c, carry):
            r0 = pl.multiple_of(c * CONV_CHUNK, CONV_CHUNK)
            dyc_c = bufd[pl.ds(r0, CONV_CHUNK), :]
            dug = jnp.zeros((CONV_CHUNK, CONV_W), F32)
            for j in range(CONV_K):
                dug = dug + w_ref[j:j + 1, :] * _window(bufd, shd, CONV_K - 1 - j, CONV_CHUNK, r0)
                prod = dyc_c * _window(bufu, shu, HALO - (CONV_K - 1) + j, CONV_CHUNK, r0)
                part = prod[0:8]
                for g8 in range(1, CONV_CHUNK // 8):
                    part = part + prod[8 * g8:8 * g8 + 8]
                dwacc[j] = dwacc[j] + part
            a_c = cur_ref[pl.ds(r0, CONV_CHUNK), 0:CONV_W]
            sg_c = sgbuf[pl.ds(r0, CONV_CHUNK), :]
            dag_ref[pl.ds(r0, CONV_CHUNK), 0:CONV_W] = (dug * sg_c).astype(BF16)
            dag_ref[pl.ds(r0, CONV_CHUNK), CONV_W:2 * CONV_W] = (dug * a_c * sg_c * (1.0 - sg_c)).astype(BF16)
            return carry

        lax.fori_loop(0, tm // CONV_CHUNK, chunk, 0)
        for j in range(CONV_K):
            _acc_rows(dw_ref, j, _rowsum(dwacc[j]))

    row = lambda w: pl.BlockSpec((tm, w), lambda i: (i, 0))
    nxt = lambda w: pl.BlockSpec((HALO, w), lambda i: (jnp.minimum((i + 1) * hb, last_halo), 0))
    return pl.pallas_call(
        body, name="conv_bwd", grid=(nt,),
        in_specs=[row(CONV_W), nxt(CONV_W), row(CONV_W), nxt(CONV_W), row(2 * CONV_W),
                  pl.BlockSpec((HALO, 2 * CONV_W), lambda i: (jnp.maximum(i * hb - 1, 0), 0)),
                  pl.BlockSpec((32, CONV_W), lambda i: (0, 0)),
                  pl.BlockSpec((8, CONV_W), lambda i: (0, 0)),
                  pl.BlockSpec((CONV_W, CONV_W), lambda i: (0, 0))],
        out_specs=[row(2 * CONV_W), pl.BlockSpec((8, CONV_W), lambda i: (0, 0)),
                   pl.BlockSpec((32, CONV_W), lambda i: (0, 0))],
        out_shape=[jax.ShapeDtypeStruct((s, 2 * CONV_W), BF16), jax.ShapeDtypeStruct((8, CONV_W), F32),
                   jax.ShapeDtypeStruct((32, CONV_W), F32)],
        scratch_shapes=[pltpu.VMEM((HALO + tm, CONV_W), F32), pltpu.VMEM((tm + HALO, CONV_W), F32),
                        pltpu.VMEM((7, tm + SHIFT_ROWS, CONV_W), F32), pltpu.VMEM((7, tm + SHIFT_ROWS, CONV_W), F32),
                        pltpu.VMEM((tm, CONV_W), F32), pltpu.VMEM((32, 8, CONV_W), F32)],
        compiler_params=_params(("arbitrary",), VMEM_BIG),
    )(dconv, dconv, ypre, ypre, ag, ag, wdw, vecs, gmat)


def _attn_bwd(qkv, cum_t, o, do, lse, tq, tk, grads):
    s = qkv.shape[0]
    nq = s // tq
    r = tq // tk
    n = len(grads)

    def body(q_ref, k_ref, v_ref, cum_ref, o_ref, do_ref, lse_ref, *rest):
        dq_ref, dkt_ref, dvt_ref, dcum_ref, drow_ref = rest[n:n + 5]
        hp = pl.program_id(0)
        qi = pl.program_id(1)
        start, wait = _exchange(rest[:n], rest[n + 5:2 * n + 5], *rest[2 * n + 5:], gather=False)
        pl.when(jnp.logical_and(hp == 0, qi == 0))(start)

        @pl.when(qi == 0)
        def _():
            dkt_ref[...] = jnp.zeros_like(dkt_ref)
            dvt_ref[...] = jnp.zeros_like(dvt_ref)
            dcum_ref[...] = jnp.zeros_like(dcum_ref)

        qs = q_ref[...] * 0.125
        ov = o_ref[...]
        dob = do_ref[...].astype(BF16)
        lane = lax.broadcasted_iota(jnp.int32, (1, 128), 1)
        subl = lax.broadcasted_iota(jnp.int32, (128, 1), 0)
        hmask = (lane < 64, lane >= 64)
        aux = (64, 0)
        ones_aux = [jnp.where(lane == aux[h], 1.0, 0.0) for h in range(2)]
        qm, dom, delta, lse_h, qat, dot_t = [], [], [], [], [], []
        for h in range(2):
            qm.append(jnp.where(hmask[h], qs, jnp.zeros_like(qs)))
            dom.append(jnp.where(hmask[h], dob, jnp.zeros_like(dob)))
            delta.append(jnp.sum(dom[h].astype(F32) * ov, axis=1, keepdims=True))
            lse_h.append(lse_ref[:, h:h + 1])
            qat.append(jnp.where(hmask[h], qs.astype(F32), ones_aux[h]).T.astype(BF16))
            dot_t.append(dom[h].astype(F32).T.astype(BF16))

        def step(kb, carry, r0=None):
            lo = 0 if r0 is None else r0
            k0 = pl.multiple_of(kb * tk, tk)
            k2 = k_ref[pl.ds(k0, tk), :]
            v2 = v_ref[pl.ds(k0, tk), :]
            dq_new, a, b = [], [], []
            for h in range(2):
                cs = cum_ref[pl.ds(2 * hp + h, 1), pl.ds(k0, tk)]
                sc = _dot_nt(qm[h][lo:], k2) - cs
                p = jnp.exp(sc - lse_h[h][lo:])
                if r0 is not None:
                    visible = (lax.broadcasted_iota(jnp.int32, (tq - lo, tk), 1)
                               <= lax.broadcasted_iota(jnp.int32, (tq - lo, tk), 0))
                    p = jnp.where(visible, p, 0.0)
                dp = _dot_nt(dom[h][lo:], v2)
                ds = p * (dp - delta[h][lo:])
                pb = p.astype(BF16)
                dsb = ds.astype(BF16)
                a.append(_dot(qat[h][:, lo:], dsb))
                b.append(_dot(dot_t[h][:, lo:], pb))
                ka = jnp.where(hmask[h], k2, ones_aux[h].astype(k2.dtype))
                dq_h = carry[h][lo:] + _dot(dsb, ka)
                dq_new.append(jnp.concatenate([carry[h][:lo], dq_h], axis=0) if lo else dq_h)
            dkt_ref[:, pl.ds(k0, tk)] = dkt_ref[:, pl.ds(k0, tk)] + jnp.where(subl < 64, a[0], a[1])
            dvt_ref[:, pl.ds(k0, tk)] = dvt_ref[:, pl.ds(k0, tk)] + (b[0] + b[1])
            for h in range(2):
                dcum_ref[pl.ds(h, 1), pl.ds(k0, tk)] = (dcum_ref[pl.ds(h, 1), pl.ds(k0, tk)]
                                                        - a[h][aux[h]:aux[h] + 1, :])
            return tuple(dq_new)

        init = (jnp.zeros((tq, 128), F32),) * 2
        carry = lax.fori_loop(0, qi * r, step, init)
        for d in range(r):
            carry = step(qi * r + d, carry, r0=d * tk)
        dq0, dq1 = carry
        dq_ref[...] = (jnp.where(hmask[0], dq0, dq1) * 0.125).astype(BF16)
        drow_ref[:, 0:1] = dq0[:, aux[0]:aux[0] + 1]
        drow_ref[:, 1:2] = dq1[:, aux[1]:aux[1] + 1]
        pl.when(jnp.logical_and(hp == 3, qi == nq - 1))(wait)

    return pl.pallas_call(
        body, name="attn_bwd", grid=(4, nq),
        in_specs=[pl.BlockSpec((tq, 128), lambda h, i: (i, h)),
                  pl.BlockSpec((s, 128), lambda h, i: (0, 4 + h)),
                  pl.BlockSpec((s, 128), lambda h, i: (0, 8 + h)),
                  pl.BlockSpec((8, s), lambda h, i: (0, 0)),
                  pl.BlockSpec((tq, 128), lambda h, i: (i, h)),
                  pl.BlockSpec((tq, 128), lambda h, i: (i, h)),
                  pl.BlockSpec((None, tq, 2), lambda h, i: (h, i, 0))] + [ANY_SPACE] * n,
        out_specs=[pl.BlockSpec((tq, 128), lambda h, i: (i, h)),
                   pl.BlockSpec((128, s), lambda h, i: (h, 0)),
                   pl.BlockSpec((128, s), lambda h, i: (h, 0)),
                   pl.BlockSpec((None, 8, s), lambda h, i: (h, 0, 0)),
                   pl.BlockSpec((None, tq, 2), lambda h, i: (h, i, 0))] + [ANY_SPACE] * n,
        out_shape=[jax.ShapeDtypeStruct((s, ATTN_W), BF16), jax.ShapeDtypeStruct((ATTN_W, s), F32),
                   jax.ShapeDtypeStruct((ATTN_W, s), F32), jax.ShapeDtypeStruct((4, 8, s), F32),
                   jax.ShapeDtypeStruct((4, s, 2), F32)]
        + [jax.ShapeDtypeStruct(g.shape, g.dtype) for g in grads],
        scratch_shapes=_exchange_scratch(n),
        compiler_params=_params(("arbitrary", "arbitrary"), VMEM_BIG),
    )(qkv, qkv, qkv, cum_t, o, do, lse, *grads)


def _fgate_bwd(dcum_t, drow_t, sneg_t):
    s = dcum_t.shape[1]

    def body(dc_ref, dr_ref, sn_ref, df_ref, db_ref):
        lane = lax.broadcasted_iota(jnp.int32, (8, s), 1)
        acc = dc_ref[...] + dr_ref[...]
        sh = 1
        while sh < s:
            acc = acc + jnp.where(lane < s - sh, pltpu.roll(acc, s - sh, axis=1), 0.0)
            sh *= 2
        df = acc * sn_ref[...]
        df_ref[...] = df
        db_ref[...] = jnp.broadcast_to(jnp.sum(df, axis=1, keepdims=True), (8, 128))

    return pl.pallas_call(
        body, name="fgate_bwd",
        out_shape=[jax.ShapeDtypeStruct((8, s), F32), jax.ShapeDtypeStruct((8, 128), F32)],
        in_specs=[VMEM_WHOLE] * 3, out_specs=[VMEM_WHOLE] * 2,
    )(dcum_t, drow_t, sneg_t)


def _in_proj_bwd(dq, dkt, dvt, dag, dft, wqkv, wkvt, wag, wft, x, dxb, mod, tm):
    s = x.shape[0]

    def body(dq_ref, dkt_ref, dvt_ref, dag_ref, dft_ref, wqkv_ref, wkvt_ref, wag_ref, wft_ref, x_ref, dxb_ref,
             mod_ref, dkvt_ref, dx_ref, st_ref):
        i = pl.program_id(0)
        dkb = dkt_ref[...].astype(BF16)
        dvb = dvt_ref[...].astype(BF16)
        dkvt_ref[0:ATTN_W, :] = dkb
        dkvt_ref[ATTN_W:2 * ATTN_W, :] = dvb
        du = _dot_nt(dq_ref[...], wqkv_ref[:, 0:ATTN_W])
        du = du + _dot_tn(dkb, wkvt_ref[0:ATTN_W, :])
        du = du + _dot_tn(dvb, wkvt_ref[ATTN_W:2 * ATTN_W, :])
        du = du + _dot_nt(dag_ref[...], wag_ref[...])
        du = du + _dot_tn(dft_ref[...].astype(BF16), wft_ref[...])
        xv = x_ref[...]
        dx_ref[...] = dxb_ref[...] + du * (1.0 + mod_ref[1:2, :])

        @pl.when(i == 0)
        def _():
            st_ref[...] = jnp.zeros_like(st_ref)

        _acc_rows(st_ref, 0, _rowsum(du * xv))
        _acc_rows(st_ref, 1, _rowsum(du))

    row = lambda w: pl.BlockSpec((tm, w), lambda i: (i, 0))
    col = lambda h: pl.BlockSpec((h, tm), lambda i: (0, i))
    vec = pl.BlockSpec((8, D_MODEL), lambda i: (0, 0))
    return pl.pallas_call(
        body, name="in_proj_bwd", grid=(s // tm,),
        in_specs=[row(ATTN_W), col(ATTN_W), col(ATTN_W), row(2 * CONV_W), col(8),
                  VMEM_WHOLE, VMEM_WHOLE, VMEM_WHOLE, VMEM_WHOLE, row(D_MODEL), row(D_MODEL), vec],
        out_specs=[col(2 * ATTN_W), row(D_MODEL), vec],
        out_shape=[jax.ShapeDtypeStruct((2 * ATTN_W, s), BF16), jax.ShapeDtypeStruct((s, D_MODEL), F32),
                   jax.ShapeDtypeStruct((8, D_MODEL), F32)],
        compiler_params=_params(("arbitrary",), VMEM_MID),
    )(dq, dkt, dvt, dag, dft, wqkv, wkvt, wag, wft, x, dxb, mod)


def _wgrad_nn(a, b, name, tm, tn, tk):
    m, s = a.shape
    n = b.shape[1]
    nk = s // tk

    def body(a_ref, b_ref, o_ref, acc_ref):
        k = pl.program_id(2)

        @pl.when(k == 0)
        def _():
            acc_ref[...] = jnp.zeros_like(acc_ref)

        acc_ref[...] += _dot(a_ref[...], b_ref[...])

        @pl.when(k == nk - 1)
        def _():
            o_ref[...] = acc_ref[...].astype(o_ref.dtype)

    return pl.pallas_call(
        body, name=name, grid=(m // tm, n // tn, nk),
        in_specs=[pl.BlockSpec((tm, tk), lambda i, j, k: (i, k)), pl.BlockSpec((tk, tn), lambda i, j, k: (k, j))],
        out_specs=pl.BlockSpec((tm, tn), lambda i, j, k: (i, j)),
        out_shape=jax.ShapeDtypeStruct((m, n), BF16),
        scratch_shapes=[pltpu.VMEM((tm, tn), F32)],
        compiler_params=_params(("parallel", "parallel", "arbitrary"), VMEM_MID),
    )(a, b)


def _wgrad_tn(a, b, name, tm, tn, tk, out_shape, out_spec):
    s, m = a.shape
    n = b.shape[1]
    nk = s // tk

    def body(a_ref, b_ref, o_ref, acc_ref):
        k = pl.program_id(2)

        @pl.when(k == 0)
        def _():
            acc_ref[...] = jnp.zeros_like(acc_ref)

        acc_ref[...] += _dot_tn(a_ref[...], b_ref[...])

        @pl.when(k == nk - 1)
        def _():
            o_ref[...] = acc_ref[...].astype(o_ref.dtype)

    return pl.pallas_call(
        body, name=name, grid=(m // tm, n // tn, nk),
        in_specs=[pl.BlockSpec((tk, tm), lambda i, j, k: (k, i)), pl.BlockSpec((tk, tn), lambda i, j, k: (k, j))],
        out_specs=out_spec, out_shape=out_shape,
        scratch_shapes=[pltpu.VMEM((tm, tn), F32)],
        compiler_params=_params(("parallel", "parallel", "arbitrary"), VMEM_MID),
    )(a, b)


def _wgrad_f(dft, u, tk):
    s = u.shape[0]
    nk = s // tk

    def body(d_ref, u_ref, o_ref, acc_ref):
        k = pl.program_id(0)

        @pl.when(k == 0)
        def _():
            acc_ref[...] = jnp.zeros_like(acc_ref)

        acc_ref[...] += _dot(d_ref[...].astype(BF16), u_ref[...])

        @pl.when(k == nk - 1)
        def _():
            o_ref[...] = acc_ref[...].astype(BF16)

    return pl.pallas_call(
        body, name="wgrad_f", grid=(nk,),
        in_specs=[pl.BlockSpec((8, tk), lambda k: (0, k)), pl.BlockSpec((tk, D_MODEL), lambda k: (k, 0))],
        out_specs=pl.BlockSpec((8, D_MODEL), lambda k: (0, 0)),
        out_shape=jax.ShapeDtypeStruct((8, D_MODEL), BF16),
        scratch_shapes=[pltpu.VMEM((8, D_MODEL), F32)],
        compiler_params=_params(("arbitrary",)),
    )(dft, u)


def _adamw_big(recv, w, m, v, name, tr):
    r, cdim = w.shape

    def body(g_ref, w_ref, m_ref, v_ref, go_ref, d_ref, mo_ref, vo_ref):
        g = g_ref[0].astype(F32)
        for k in range(1, N_DEV):
            g = g + g_ref[k].astype(F32)
        delta, m2, v2 = _adamw(w_ref[...], g, m_ref[...], v_ref[...])
        go_ref[...] = g
        d_ref[...] = delta
        mo_ref[...] = m2
        vo_ref[...] = v2

    blk = pl.BlockSpec((tr, cdim), lambda i: (i, 0))
    return pl.pallas_call(
        body, name=name, grid=(r // tr,),
        in_specs=[pl.BlockSpec((N_DEV, tr, cdim), lambda i: (0, i, 0)), blk, blk, blk],
        out_specs=[blk] * 4, out_shape=[jax.ShapeDtypeStruct((r, cdim), F32)] * 4,
        compiler_params=_params(("parallel",), VMEM_MID),
    )(recv, w, m, v)


def _adamw_ada(c_all, dada, w, m, v, tr):
    r, cdim = w.shape

    def body(c_ref, d_ref, w_ref, m_ref, v_ref, go_ref, dl_ref, mo_ref, vo_ref):
        cv = c_ref[...]
        sc = (cv * _sigmoid(cv)).astype(BF16)
        g = _dot_tn(sc, d_ref[...].astype(BF16))
        delta, m2, v2 = _adamw(w_ref[...], g, m_ref[...], v_ref[...])
        go_ref[...] = g
        dl_ref[...] = delta
        mo_ref[...] = m2
        vo_ref[...] = v2

    blk = pl.BlockSpec((tr, cdim), lambda i: (i, 0))
    return pl.pallas_call(
        body, name="adamw_ada", grid=(r // tr,),
        in_specs=[pl.BlockSpec((N_DEV, tr), lambda i: (0, i)), pl.BlockSpec((N_DEV, cdim), lambda i: (0, 0)),
                  blk, blk, blk],
        out_specs=[blk] * 4, out_shape=[jax.ShapeDtypeStruct((r, cdim), F32)] * 4,
        compiler_params=_params(("parallel",), VMEM_MID),
    )(c_all, dada, w, m, v)


def _adamw_small(parts, w, m, v):
    n = w.shape[1]

    def body(g_ref, w_ref, m_ref, v_ref, go_ref, d_ref, mo_ref, vo_ref):
        g = g_ref[0]
        for k in range(1, N_DEV):
            g = g + g_ref[k]
        delta, m2, v2 = _adamw(w_ref[...], g, m_ref[...], v_ref[...])
        go_ref[...] = g
        d_ref[...] = delta
        mo_ref[...] = m2
        vo_ref[...] = v2

    return pl.pallas_call(
        body, name="adamw_small", out_shape=[jax.ShapeDtypeStruct((1, n), F32)] * 4,
        in_specs=[VMEM_WHOLE] * 4, out_specs=[VMEM_WHOLE] * 4,
    )(parts, w, m, v)


def _pad_cols(a, n):
    return jnp.pad(a, ((0, 0), (0, n - a.shape[1])))


def _pad_rows(a, n):
    return jnp.pad(a, ((0, n - a.shape[0]), (0, 0)))


def kernel(x, c, w_ada, b_ada, w_in, b_forget, w_dw, b_dw, gn_g, gn_b, g_attn_out, g_conv_out, w_out, ln1_g, ln1_b, w_ff1, w_ff2, ln2_g, ln2_b, loss_target, m_w_ada, m_b_ada, m_w_in, m_b_forget, m_w_dw, m_b_dw, m_gn_g, m_gn_b, m_g_attn_out, m_g_conv_out, m_w_out, m_ln1_g, m_ln1_b, m_w_ff1, m_w_ff2, m_ln2_g, m_ln2_b, v_w_ada, v_b_ada, v_w_in, v_b_forget, v_w_dw, v_b_dw, v_gn_g, v_gn_b, v_g_attn_out, v_g_conv_out, v_w_out, v_ln1_g, v_ln1_b, v_w_ff1, v_w_ff2, v_ln2_g, v_ln2_b):
    s = x.shape[1]
    tm = min(ROW_TILE, s)
    tk_att = min(ATT_TILE, s)
    tq_att = min(ATT_Q_BLOCKS * tk_att, s)
    me = 4 * lax.axis_index("x") + 2 * lax.axis_index("y") + lax.axis_index("c")
    xs = x[0]
    tgt = loss_target[0]

    dw_n = CONV_K * DW_SHARD
    pack1 = jnp.concatenate([c, w_dw[0].reshape(1, dw_n), jnp.zeros((1, 2048 - dw_n), F32)], axis=1)
    g1 = _gather_small(pack1, "gather_c")[:, 0, :]
    c_all = g1[:, :D_MODEL]
    wdw_full = g1[:, D_MODEL:D_MODEL + dw_n].reshape(N_DEV, CONV_K, DW_SHARD)
    wdw_full = _pad_rows(wdw_full.transpose(1, 0, 2).reshape(CONV_K, CONV_W), 32)

    b_shard = lax.dynamic_slice(b_ada, (0, me * ADA_SHARD), (1, ADA_SHARD))
    ada_all = _gather_small(_ada_part(c_all, w_ada[0], b_shard), "gather_ada")
    ada = lax.dynamic_index_in_dim(ada_all, me, axis=1, keepdims=False).reshape(6, D_MODEL)
    mod = _pad_rows(ada, 8)

    gw_in, = _gather_weights([_pad_cols(w_in[0], IN_SHARD_PAD).astype(BF16)])
    w_in_full = gw_in[:, :, :IN_SHARD].transpose(1, 0, 2).reshape(D_MODEL, N_IN)
    wqkv = w_in_full[:, :3 * ATTN_W]
    wft = w_in_full[:, 3 * ATTN_W:3 * ATTN_W + N_HEADS].T
    wag = w_in_full[:, 3 * ATTN_W + N_HEADS:]

    qkv, ag, ft, u = _in_proj(xs, mod, wqkv, wag, wft, tm)
    cum_t, sneg_t = _fgate(ft, b_forget.reshape(N_HEADS, 1))
    o, lse, gw_out, gw_ff1, gw_ff2 = _attn_fwd(
        qkv, cum_t, tq_att, tk_att, [w_out[0].astype(BF16), w_ff1[0].astype(BF16), w_ff2[0].astype(BF16)])
    w_out_full = gw_out.reshape(D_MODEL, D_MODEL)
    w2_full = gw_ff2.reshape(D_FF, D_MODEL)
    cvec = _pad_rows(jnp.concatenate([b_dw, gn_g, gn_b], axis=0), 8)
    grp = jnp.arange(CONV_W) // 64
    gmat = jnp.where(grp[:, None] == grp[None, :], 1.0 / 64.0, 0.0).astype(BF16)
    ypre, conv = _conv_fwd(ag, wdw_full, cvec, gmat, tm)
    gvec = _pad_rows(jnp.concatenate([g_attn_out, g_conv_out], axis=0), 8)
    lnv = _pad_rows(jnp.concatenate([ln1_g, ln1_b, ln2_g, ln2_b], axis=0), 8)
    cat, mixed, z1, x1 = _mix_fwd(o, conv, xs, w_out_full, gvec, mod, lnv, tm)
    tf = min(FF_TILE, s)
    u2, hid, dff, dxa, st_ff = _ff_fwd(x1, tgt, gw_ff1, w2_full, mod, lnv, tf)

    dh, dxb, dmix, st_fb = _ff_bwd(dff, hid, gw_ff1, w2_full, dxa, x1, z1, mixed, mod, lnv, tf)
    do, dconv, st_mix = _mix_bwd(dmix, w_out_full, o, conv, gvec, tm)
    dag, st_conv, dwdw = _conv_bwd(dconv, ypre, ag, wdw_full, cvec, gmat, tm)

    tk = min(WGRAD_K_TILE, s)
    g_ff2 = _wgrad_tn(hid, dff, "wgrad_ff2", 512, 1024, tk,
                      jax.ShapeDtypeStruct((D_FF, D_MODEL), BF16),
                      pl.BlockSpec((512, 1024), lambda i, j, k: (i, j))).reshape(N_DEV, FF_SHARD, D_MODEL)
    g_ff1 = _wgrad_tn(u2, dh, "wgrad_ff1", 1024, 512, tk,
                      jax.ShapeDtypeStruct((N_DEV, D_MODEL, FF_SHARD), BF16),
                      pl.BlockSpec((None, 1024, 512), lambda i, j, k: (j, i, 0)))
    g_out = _wgrad_tn(cat, dmix, "wgrad_out", 512, 1024, tk,
                      jax.ShapeDtypeStruct((D_MODEL, D_MODEL), BF16),
                      pl.BlockSpec((512, 1024), lambda i, j, k: (i, j))).reshape(N_DEV, OUT_SHARD, D_MODEL)
    dq, dkt, dvt, dcum, drow, r_out, r_ff1, r_ff2 = _attn_bwd(
        qkv, cum_t, o, do, lse, tq_att, tk_att, [g_out, g_ff1, g_ff2])
    dft, dbf = _fgate_bwd(dcum[:, :2, :].reshape(N_HEADS, s), drow.transpose(0, 2, 1).reshape(N_HEADS, s), sneg_t)
    wkvt = wqkv[:, ATTN_W:3 * ATTN_W].T
    dkvt, grad_x, st_in = _in_proj_bwd(dq, dkt, dvt, dag, dft, wqkv, wkvt, wag, wft, xs, dxb, mod, tm)

    g_q = _wgrad_tn(u, dq, "wgrad_q", 1024, 512, tk,
                    jax.ShapeDtypeStruct((D_MODEL, ATTN_W), BF16),
                    pl.BlockSpec((1024, 512), lambda i, j, k: (i, j)))
    g_kvt = _wgrad_nn(dkvt, u, "wgrad_kv", 512, 1024, tk)
    g_qkv = jnp.concatenate([g_q, g_kvt.T], axis=1)
    g_ag = _wgrad_tn(u, dag, "wgrad_ag", 1024, 512, tk,
                     jax.ShapeDtypeStruct((D_MODEL, 2 * CONV_W), BF16),
                     pl.BlockSpec((1024, 512), lambda i, j, k: (i, j)))
    g_ft = _wgrad_f(dft, u, tk)
    g_in = jnp.concatenate([g_qkv, g_ft.T, g_ag], axis=1)
    g_in = jnp.pad(g_in.reshape(D_MODEL, N_DEV, IN_SHARD).transpose(1, 0, 2),
                   ((0, 0), (0, 0), (0, IN_SHARD_PAD - IN_SHARD)))

    r_in, = _scatter_grads([g_in])

    dada = jnp.concatenate([st_in[1:2], st_in[0:1], st_fb[4:5], st_fb[1:2], st_fb[0:1], st_ff[2:3]], axis=1)
    loss_part = st_ff[3:4, 0:1]
    pack2 = jnp.concatenate([
        dada,
        _pad_cols(dbf[:, 0].reshape(1, N_HEADS), 128),
        dwdw[:CONV_K].reshape(1, CONV_K * CONV_W),
        st_conv[0:1], st_conv[1:2], st_conv[2:3], st_mix[0:1], st_mix[1:2],
        st_fb[2:3], st_fb[3:4], st_ff[0:1], st_ff[1:2],
        _pad_cols(loss_part, 128)], axis=1)
    g2 = _gather_small(pack2, "gather_small_grads")[:, 0, :]
    o_bf = 6 * D_MODEL
    o_dw = o_bf + 128
    o_v = o_dw + CONV_K * CONV_W
    o_ln = o_v + 5 * CONV_W
    o_loss = o_ln + 4 * D_MODEL
    dw_parts = lax.dynamic_slice_in_dim(
        g2[:, o_dw:o_v].reshape(N_DEV, CONV_K, N_DEV, DW_SHARD), me, 1, axis=2).reshape(N_DEV, dw_n)
    parts = jnp.concatenate([
        g2[:, :o_dw], _pad_cols(dw_parts, 2048), g2[:, o_v:o_loss + 128]], axis=1)[:, None, :]

    def pack_w(b_ada_, b_forget_, w_dw_, smalls):
        return jnp.concatenate([b_ada_, _pad_cols(b_forget_, 128), _pad_cols(w_dw_[0].reshape(1, dw_n), 2048)]
                               + smalls + [jnp.zeros((1, 128), F32)], axis=1)

    pw = pack_w(b_ada, b_forget, w_dw, [b_dw, gn_g, gn_b, g_attn_out, g_conv_out, ln1_g, ln1_b, ln2_g, ln2_b])
    pm = pack_w(m_b_ada, m_b_forget, m_w_dw, [m_b_dw, m_gn_g, m_gn_b, m_g_attn_out, m_g_conv_out,
                                              m_ln1_g, m_ln1_b, m_ln2_g, m_ln2_b])
    pv = pack_w(v_b_ada, v_b_forget, v_w_dw, [v_b_dw, v_gn_g, v_gn_b, v_g_attn_out, v_g_conv_out,
                                              v_ln1_g, v_ln1_b, v_ln2_g, v_ln2_b])
    small = _adamw_small(parts, pw, pm, pv)

    p_bf = 6 * D_MODEL
    p_dw = p_bf + 128
    p_v = p_dw + 2048
    p_ln = p_v + 5 * CONV_W
    p_loss = p_ln + 4 * D_MODEL

    def unpack(t):
        outs = {"b_ada": t[:, :p_bf], "b_forget": t[:, p_bf:p_bf + N_HEADS],
                "w_dw": t[:, p_dw:p_dw + dw_n].reshape(1, CONV_K, 1, DW_SHARD)}
        for k, nm in enumerate(["b_dw", "gn_g", "gn_b", "g_attn_out", "g_conv_out"]):
            outs[nm] = t[:, p_v + k * CONV_W:p_v + (k + 1) * CONV_W]
        for k, nm in enumerate(["ln1_g", "ln1_b", "ln2_g", "ln2_b"]):
            outs[nm] = t[:, p_ln + k * D_MODEL:p_ln + (k + 1) * D_MODEL]
        return outs

    sm = [unpack(t) for t in small]
    loss = small[0][0, p_loss]

    dada_all = g2[:, :6 * D_MODEL]
    dada_shard = lax.dynamic_slice_in_dim(dada_all, me * ADA_SHARD, ADA_SHARD, axis=1)
    big = {
        "w_ada": _adamw_ada(c_all, dada_shard, w_ada[0], m_w_ada[0], v_w_ada[0], 256),
        "w_in": [t[:, :IN_SHARD] for t in _adamw_big(
            r_in, _pad_cols(w_in[0], IN_SHARD_PAD), _pad_cols(m_w_in[0], IN_SHARD_PAD),
            _pad_cols(v_w_in[0], IN_SHARD_PAD), "adamw_in", 256)],
        "w_out": _adamw_big(r_out, w_out[0], m_w_out[0], v_w_out[0], "adamw_out", 128),
        "w_ff1": _adamw_big(r_ff1, w_ff1[0], m_w_ff1[0], v_w_ff1[0], "adamw_ff1", 256),
        "w_ff2": _adamw_big(r_ff2, w_ff2[0], m_w_ff2[0], v_w_ff2[0], "adamw_ff2", 256),
    }

    names = ["w_ada", "b_ada", "w_in", "b_forget", "w_dw", "b_dw", "gn_g", "gn_b", "g_attn_out", "g_conv_out",
             "w_out", "ln1_g", "ln1_b", "w_ff1", "w_ff2", "ln2_g", "ln2_b"]

    def leaf(kind, nm):
        if nm in big:
            return big[nm][kind][None]
        return sm[kind][nm]

    outs = [loss, grad_x[None]]
    for kind in range(4):
        outs += [leaf(kind, nm) for nm in names]
    return tuple(outs)
```

```python
import jax
import jax.numpy as jnp
from jax import lax
from jax.experimental import pallas as pl
from jax.experimental.pallas import tpu as pltpu

F32 = jnp.float32
BF16 = jnp.bfloat16

D_MODEL = 1024
ATTN_W = 512
CONV_W = 512
N_HEADS = 8
CONV_K = 31
D_FF = 4096
N_DEV = 8
N_IN = 3 * ATTN_W + N_HEADS + 2 * CONV_W
IN_SHARD = N_IN // N_DEV
IN_SHARD_PAD = 384
ADA_SHARD = 6 * D_MODEL // N_DEV
FF_SHARD = D_FF // N_DEV
OUT_SHARD = D_MODEL // N_DEV
DW_SHARD = CONV_W // N_DEV
HALO = 32
LN_EPS = 1e-5
ALPHA = 2.0 ** 0.25
ADAM_LR, ADAM_B1, ADAM_B2, ADAM_EPS, ADAM_WD, ADAM_STEP = 0.001, 0.9, 0.999, 1e-08, 0.01, 10

ROW_TILE = 512
FF_TILE = 256
WGRAD_K_TILE = 2048
ATT_TILE = 512
ATT_Q_BLOCKS = 2
VMEM_BIG = 56 * 1024 * 1024
VMEM_MID = 40 * 1024 * 1024

MESH = pl.DeviceIdType.MESH
VMEM_WHOLE = pl.BlockSpec(memory_space=pltpu.VMEM)
ANY_SPACE = pl.BlockSpec(memory_space=pl.ANY)


def _dot(a, b):
    return jnp.dot(a, b, preferred_element_type=F32)


def _dot_nt(a, b):
    return lax.dot_general(a, b, (((1,), (1,)), ((), ())), preferred_element_type=F32)


def _dot_tn(a, b):
    return lax.dot_general(a, b, (((0,), (0,)), ((), ())), preferred_element_type=F32)


def _dot3(xf, g_ref):
    g = g_ref[...]
    hi = xf.astype(BF16)
    lo = (xf - hi.astype(F32)).astype(BF16)
    return _dot(hi, g) + _dot(lo, g)


def _sigmoid(x):
    return 1.0 / (1.0 + jnp.exp(-x))


def _rowsum(x):
    return jnp.sum(x, axis=0, keepdims=True)


def _ln_norm(z):
    mu = jnp.mean(z, axis=1, keepdims=True)
    zc = z - mu
    var = jnp.mean(zc * zc, axis=1, keepdims=True)
    rstd = lax.rsqrt(var + LN_EPS)
    return zc * rstd, rstd


def _ln_bwd(dxh, xh, rstd):
    m1 = jnp.mean(dxh, axis=1, keepdims=True)
    m2 = jnp.mean(dxh * xh, axis=1, keepdims=True)
    return rstd * (dxh - m1 - xh * m2)


def _adamw(w, g, m, v):
    m2 = ADAM_B1 * m + (1.0 - ADAM_B1) * g
    v2 = ADAM_B2 * v + (1.0 - ADAM_B2) * (g * g)
    m_hat = m2 / (1.0 - ADAM_B1 ** ADAM_STEP)
    v_hat = v2 / (1.0 - ADAM_B2 ** ADAM_STEP)
    delta = -ADAM_LR * (m_hat / (jnp.sqrt(v_hat) + ADAM_EPS) + ADAM_WD * w)
    return delta, m2, v2


def _params(sem=None, vmem=None):
    kw = {}
    if sem is not None:
        kw["dimension_semantics"] = sem
    if vmem is not None:
        kw["vmem_limit_bytes"] = vmem
    return pltpu.CompilerParams(**kw)


def _coords():
    return lax.axis_index("x"), lax.axis_index("y"), lax.axis_index("c")


def _flip(v, bit):
    return 1 - v if bit else v


def _acc_rows(ref, r, val):
    ref[r:r + 1, :] = ref[r:r + 1, :] + val


def _gather_small(v, name):
    r, n = v.shape

    def body(v_ref, out_ref, send_sems, recv_sems, local_sem):
        x, y, c = _coords()
        me = 4 * x + 2 * y + c
        mine = pltpu.make_async_copy(v_ref, out_ref.at[me], local_sem)
        mine.start()
        sends = []
        for k in range(1, N_DEV):
            tgt = (_flip(x, (k >> 2) & 1), _flip(y, (k >> 1) & 1), _flip(c, k & 1))
            cp = pltpu.make_async_remote_copy(
                src_ref=v_ref, dst_ref=out_ref.at[me], send_sem=send_sems.at[k - 1],
                recv_sem=recv_sems.at[k - 1], device_id=tgt, device_id_type=MESH)
            cp.start()
            sends.append(cp)
        for k in range(1, N_DEV):
            px, py, pc = _flip(x, (k >> 2) & 1), _flip(y, (k >> 1) & 1), _flip(c, k & 1)
            peer = 4 * px + 2 * py + pc
            pltpu.make_async_remote_copy(
                src_ref=v_ref, dst_ref=out_ref.at[peer], send_sem=send_sems.at[k - 1],
                recv_sem=recv_sems.at[k - 1], device_id=(px, py, pc), device_id_type=MESH).wait_recv()
        for cp in sends:
            cp.wait_send()
        mine.wait()

    return pl.pallas_call(
        body, name=name,
        out_shape=jax.ShapeDtypeStruct((N_DEV, r, n), v.dtype),
        in_specs=[VMEM_WHOLE], out_specs=VMEM_WHOLE,
        scratch_shapes=[pltpu.SemaphoreType.DMA((N_DEV - 1,)), pltpu.SemaphoreType.DMA((N_DEV - 1,)),
                        pltpu.SemaphoreType.DMA],
    )(v)


def _gather_weights(shards):
    n = len(shards)

    def body(*refs):
        ins, outs = refs[:n], refs[n:2 * n]
        send_sems, recv_sems, local_sems = refs[2 * n:]
        x, y, c = _coords()
        me, sibling = (x, y, c), (x, y, 1 - c)
        chips = [(1 - x, y), (x, 1 - y), (1 - x, 1 - y)]

        def slot(a, pos):
            return outs[a].at[4 * pos[0] + 2 * pos[1] + pos[2]]

        def copy(a, k, block, to, src=None):
            return pltpu.make_async_remote_copy(
                src_ref=slot(a, block) if src is None else src, dst_ref=slot(a, block),
                send_sem=send_sems.at[7 * a + k], recv_sem=recv_sems.at[7 * a + k],
                device_id=to, device_id_type=MESH)

        started = []
        local = []
        for a in range(n):
            mine = pltpu.make_async_copy(ins[a], slot(a, me), local_sems.at[a])
            mine.start()
            local.append(mine)
        for j, chip in enumerate(chips):
            for a in range(n):
                cp = copy(a, 1 + j, me, (*chip, c), src=ins[a])
                cp.start()
                started.append(cp)
        for a in range(n):
            cp = copy(a, 0, me, sibling, src=ins[a])
            cp.start()
            started.append(cp)
        for j, chip in enumerate(chips):
            for a in range(n):
                copy(a, 1 + j, (*chip, c), me).wait_recv()
                cp = copy(a, 4 + j, (*chip, c), sibling)
                cp.start()
                started.append(cp)
        for a in range(n):
            copy(a, 0, sibling, me).wait_recv()
        for j, chip in enumerate(chips):
            for a in range(n):
                copy(a, 4 + j, (*chip, 1 - c), me).wait_recv()
        for cp in started:
            cp.wait_send()
        for mine in local:
            mine.wait()

    return pl.pallas_call(
        body, name="gather_weights",
        out_shape=[jax.ShapeDtypeStruct((N_DEV,) + s.shape, s.dtype) for s in shards],
        in_specs=[ANY_SPACE] * n, out_specs=[ANY_SPACE] * n,
        scratch_shapes=[pltpu.SemaphoreType.DMA((7 * n,)), pltpu.SemaphoreType.DMA((7 * n,)),
                        pltpu.SemaphoreType.DMA((n,))],
    )(*shards)


def _exchange(ins, outs, send_sems, recv_sems, local_sems, gather):
    n = len(ins)
    x, y, c = _coords()
    me = 4 * x + 2 * y + c
    local = [pltpu.make_async_copy(ins[a] if gather else ins[a].at[me], outs[a].at[me], local_sems.at[a])
             for a in range(n)]
    sends, recvs = [], []
    for k in range(1, N_DEV):
        px, py, pc = _flip(x, (k >> 2) & 1), _flip(y, (k >> 1) & 1), _flip(c, k & 1)
        peer = 4 * px + 2 * py + pc
        for a in range(n):
            src = ins[a] if gather else ins[a].at[peer]
            sems = dict(send_sem=send_sems.at[7 * a + k - 1], recv_sem=recv_sems.at[7 * a + k - 1],
                        device_id=(px, py, pc), device_id_type=MESH)
            sends.append(pltpu.make_async_remote_copy(src_ref=src, dst_ref=outs[a].at[me], **sems))
            recvs.append(pltpu.make_async_remote_copy(src_ref=src, dst_ref=outs[a].at[peer], **sems))

    def start():
        for cp in local + sends:
            cp.start()

    def wait():
        for cp in recvs:
            cp.wait_recv()
        for cp in sends:
            cp.wait_send()
        for cp in local:
            cp.wait()

    return start, wait


def _exchange_scratch(n):
    return [pltpu.SemaphoreType.DMA((7 * n,)), pltpu.SemaphoreType.DMA((7 * n,)), pltpu.SemaphoreType.DMA((n,))]


def _scatter_grads(grads):
    n = len(grads)

    def body(*refs):
        start, wait = _exchange(refs[:n], refs[n:2 * n], *refs[2 * n:], gather=False)
        start()
        wait()

    return pl.pallas_call(
        body, name="scatter_grads",
        out_shape=[jax.ShapeDtypeStruct(g.shape, g.dtype) for g in grads],
        in_specs=[ANY_SPACE] * n, out_specs=[ANY_SPACE] * n,
        scratch_shapes=_exchange_scratch(n),
    )(*grads)


def _ada_part(c_all, w_ada, b_shard):
    def body(c_ref, w_ref, b_ref, o_ref):
        cv = c_ref[...]
        sc = (cv * _sigmoid(cv)).astype(BF16)
        o_ref[...] = _dot(sc, w_ref[...].astype(BF16)) + b_ref[...]

    return pl.pallas_call(
        body, name="ada_part", out_shape=jax.ShapeDtypeStruct((N_DEV, ADA_SHARD), F32),
        in_specs=[VMEM_WHOLE] * 3, out_specs=VMEM_WHOLE,
        compiler_params=_params(vmem=VMEM_MID),
    )(c_all, w_ada, b_shard)


def _in_proj(x, mod, wqkv, wag, wft, tm):
    s = x.shape[0]

    def body(x_ref, mod_ref, wqkv_ref, wag_ref, wft_ref, qkv_ref, ag_ref, ft_ref, u_ref):
        u = x_ref[...] * (1.0 + mod_ref[1:2, :]) + mod_ref[0:1, :]
        ub = u.astype(BF16)
        u_ref[...] = ub
        qkv_ref[...] = _dot(ub, wqkv_ref[...]).astype(BF16)
        ag_ref[...] = _dot(ub, wag_ref[...])
        ft_ref[...] = _dot_nt(wft_ref[...], ub)

    return pl.pallas_call(
        body, name="in_proj", grid=(s // tm,),
        in_specs=[pl.BlockSpec((tm, D_MODEL), lambda i: (i, 0)), pl.BlockSpec((8, D_MODEL), lambda i: (0, 0)),
                  VMEM_WHOLE, VMEM_WHOLE, VMEM_WHOLE],
        out_specs=[pl.BlockSpec((tm, 3 * ATTN_W), lambda i: (i, 0)),
                   pl.BlockSpec((tm, 2 * CONV_W), lambda i: (i, 0)),
                   pl.BlockSpec((8, tm), lambda i: (0, i)),
                   pl.BlockSpec((tm, D_MODEL), lambda i: (i, 0))],
        out_shape=[jax.ShapeDtypeStruct((s, 3 * ATTN_W), BF16), jax.ShapeDtypeStruct((s, 2 * CONV_W), F32),
                   jax.ShapeDtypeStruct((8, s), F32), jax.ShapeDtypeStruct((s, D_MODEL), BF16)],
        compiler_params=_params(("parallel",), VMEM_MID),
    )(x, mod, wqkv, wag, wft)


def _fgate(ft, bcol):
    s = ft.shape[1]

    def body(f_ref, b_ref, cum_ref, sneg_ref):
        z = f_ref[...] + b_ref[...]
        e = jnp.exp(-jnp.abs(z))
        l1p = jnp.where(e < 1e-2, e * (1.0 - e * (0.5 - e * (1.0 / 3.0))), jnp.log(1.0 + e))
        logf = jnp.minimum(z, 0.0) - l1p
        r = 1.0 / (1.0 + e)
        sneg_ref[...] = jnp.where(z >= 0, e * r, r)
        lane = lax.broadcasted_iota(jnp.int32, (8, s), 1)
        acc = logf
        sh = 1
        while sh < s:
            acc = acc + jnp.where(lane >= sh, pltpu.roll(acc, sh, axis=1), 0.0)
            sh *= 2
        cum_ref[...] = acc

    return pl.pallas_call(
        body, name="fgate", out_shape=[jax.ShapeDtypeStruct((8, s), F32)] * 2,
        in_specs=[VMEM_WHOLE] * 2, out_specs=[VMEM_WHOLE] * 2,
    )(ft, bcol)


def _attn_fwd(qkv, cum_t, tq, tk, shards):
    s = qkv.shape[0]
    nq = s // tq
    r = tq // tk
    n = len(shards)

    def body(q_ref, k_ref, v_ref, cum_ref, *rest):
        o_ref, lse_ref = rest[n:n + 2]
        hp = pl.program_id(0)
        qi = pl.program_id(1)
        start, wait = _exchange(rest[:n], rest[n + 2:2 * n + 2], *rest[2 * n + 2:], gather=True)
        pl.when(jnp.logical_and(hp == 0, qi == 0))(start)
        q2 = q_ref[...]
        lane = lax.broadcasted_iota(jnp.int32, (1, 128), 1)
        hmask = (lane < 64, lane >= 64)
        aux = (64, 0)
        qm = [jnp.where(hmask[h], q2, jnp.zeros_like(q2)) * 0.125 for h in range(2)]

        def step(kb, carry, r0=None):
            lo = 0 if r0 is None else r0
            kw = tq if r0 is None else tk
            k0 = pl.multiple_of(kb * kw, kw)
            k2 = k_ref[pl.ds(k0, kw), :]
            v2 = v_ref[pl.ds(k0, kw), :]
            out = []
            for h in range(2):
                m_all, acc_all = carry[2 * h], carry[2 * h + 1]
                m, acc = m_all[lo:], acc_all[lo:]
                sc = _dot_nt(qm[h][lo:], k2) - cum_ref[pl.ds(2 * hp + h, 1), pl.ds(k0, kw)]
                if r0 is not None:
                    visible = (lax.broadcasted_iota(jnp.int32, (tq - lo, tk), 1)
                               <= lax.broadcasted_iota(jnp.int32, (tq - lo, tk), 0))
                    sc = jnp.where(visible, sc, -jnp.inf)
                m_new = jnp.maximum(m, jnp.max(sc, axis=1, keepdims=True))
                p = jnp.exp(sc - m_new)
                corr = jnp.exp(m - m_new)
                va = jnp.where(hmask[h], v2, jnp.where(lane == aux[h], 1.0, 0.0).astype(v2.dtype))
                acc_new = acc * corr + _dot(p.astype(BF16), va)
                if lo:
                    m_new = jnp.concatenate([m_all[:lo], m_new], axis=0)
                    acc_new = jnp.concatenate([acc_all[:lo], acc_new], axis=0)
                out += [m_new, acc_new]
            return tuple(out)

        init = (jnp.full((tq, 1), -jnp.inf, F32), jnp.zeros((tq, 128), F32)) * 2
        carry = lax.fori_loop(0, qi, step, init)
        for d in range(r):
            carry = step(qi * r + d, carry, r0=d * tk)
        m0, acc0, m1, acc1 = carry
        l0 = acc0[:, aux[0]:aux[0] + 1]
        l1 = acc1[:, aux[1]:aux[1] + 1]
        o_ref[...] = jnp.where(hmask[0], acc0 / l0, acc1 / l1)
        lse_ref[:, 0:1] = m0 + jnp.log(l0)
        lse_ref[:, 1:2] = m1 + jnp.log(l1)
        pl.when(jnp.logical_and(hp == 3, qi == nq - 1))(wait)

    return pl.pallas_call(
        body, name="attn_fwd", grid=(4, nq),
        in_specs=[pl.BlockSpec((tq, 128), lambda h, i: (i, h)),
                  pl.BlockSpec((s, 128), lambda h, i: (0, 4 + h)),
                  pl.BlockSpec((s, 128), lambda h, i: (0, 8 + h)),
                  pl.BlockSpec((8, s), lambda h, i: (0, 0))] + [ANY_SPACE] * n,
        out_specs=[pl.BlockSpec((tq, 128), lambda h, i: (i, h)),
                   pl.BlockSpec((None, tq, 2), lambda h, i: (h, i, 0))] + [ANY_SPACE] * n,
        out_shape=[jax.ShapeDtypeStruct((s, ATTN_W), F32), jax.ShapeDtypeStruct((4, s, 2), F32)]
        + [jax.ShapeDtypeStruct((N_DEV,) + w.shape, w.dtype) for w in shards],
        scratch_shapes=_exchange_scratch(n),
        compiler_params=_params(("arbitrary", "arbitrary"), VMEM_BIG),
    )(qkv, qkv, qkv, cum_t, *shards)


def _gn_fwd(y, vec_ref, g_ref):
    mean = _dot3(y, g_ref)
    yc = y - mean
    var = _dot3(yc * yc, g_ref)
    rstd = lax.rsqrt(var + LN_EPS)
    yh = yc * rstd
    yn = yh * vec_ref[1:2, :] + vec_ref[2:3, :]
    return yh, rstd, yn


SHIFT_ROWS = HALO - 8
CONV_CHUNK = 32


def _shifted_copies(buf, sh, tm):
    for b in range(1, 8):
        sh[b - 1] = buf[pl.ds(b, tm + SHIFT_ROWS), :]


def _window(buf, sh, off, rows, r0=0):
    a, b = divmod(off, 8)
    if b == 0:
        return buf[pl.ds(8 * a + r0, rows), :]
    return sh[b - 1, pl.ds(8 * a + r0, rows), :]


def _conv_fwd(ag, wdw, vecs, gmat, tm):
    s = ag.shape[0]
    hb = tm // HALO

    def body(cur_ref, halo_ref, w_ref, vec_ref, g_ref, ypre_ref, conv_ref, buf, sh):
        i = pl.program_id(0)
        ug = cur_ref[:, 0:CONV_W] * _sigmoid(cur_ref[:, CONV_W:2 * CONV_W])
        ugh = halo_ref[:, 0:CONV_W] * _sigmoid(halo_ref[:, CONV_W:2 * CONV_W])
        buf[0:HALO, :] = jnp.where(i > 0, ugh, 0.0)
        buf[HALO:HALO + tm, :] = ug
        _shifted_copies(buf, sh, tm)
        y = jnp.broadcast_to(vec_ref[0:1, :], (tm, CONV_W))
        for j in range(CONV_K):
            y = y + w_ref[j:j + 1, :] * _window(buf, sh, HALO - (CONV_K - 1) + j, tm)
        ypre_ref[...] = y
        _, _, yn = _gn_fwd(y, vec_ref, g_ref)
        conv_ref[...] = yn * _sigmoid(yn)

    return pl.pallas_call(
        body, name="conv_fwd", grid=(s // tm,),
        in_specs=[pl.BlockSpec((tm, 2 * CONV_W), lambda i: (i, 0)),
                  pl.BlockSpec((HALO, 2 * CONV_W), lambda i: (jnp.maximum(i * hb - 1, 0), 0)),
                  pl.BlockSpec((32, CONV_W), lambda i: (0, 0)),
                  pl.BlockSpec((8, CONV_W), lambda i: (0, 0)),
                  pl.BlockSpec((CONV_W, CONV_W), lambda i: (0, 0))],
        out_specs=[pl.BlockSpec((tm, CONV_W), lambda i: (i, 0))] * 2,
        out_shape=[jax.ShapeDtypeStruct((s, CONV_W), F32)] * 2,
        scratch_shapes=[pltpu.VMEM((HALO + tm, CONV_W), F32), pltpu.VMEM((7, tm + SHIFT_ROWS, CONV_W), F32)],
        compiler_params=_params(("parallel",), VMEM_MID),
    )(ag, ag, wdw, vecs, gmat)


def _mix_fwd(o, conv, x, w_out, gvec, mod, lnv, tm):
    s = x.shape[0]

    def body(o_ref, c_ref, x_ref, w_ref, g_ref, mod_ref, ln_ref, cat_ref, mixed_ref, z1_ref, x1_ref):
        ov = o_ref[...]
        cv = c_ref[...]
        ra = ov * lax.rsqrt(jnp.mean(ov * ov, axis=1, keepdims=True) + LN_EPS) * g_ref[0:1, :]
        rc = cv * lax.rsqrt(jnp.mean(cv * cv, axis=1, keepdims=True) + LN_EPS) * g_ref[1:2, :]
        rab = ra.astype(BF16)
        rcb = rc.astype(BF16)
        cat_ref[:, 0:ATTN_W] = rab
        cat_ref[:, ATTN_W:D_MODEL] = rcb
        mixed = _dot(rab, w_ref[0:ATTN_W, :]) + _dot(rcb, w_ref[ATTN_W:D_MODEL, :])
        mixed_ref[...] = mixed
        z1 = ALPHA * x_ref[...] + (1.0 + mod_ref[2:3, :]) * mixed
        z1_ref[...] = z1
        xh, _ = _ln_norm(z1)
        x1_ref[...] = xh * ln_ref[0:1, :] + ln_ref[1:2, :]

    row = lambda w: pl.BlockSpec((tm, w), lambda i: (i, 0))
    return pl.pallas_call(
        body, name="mix_fwd", grid=(s // tm,),
        in_specs=[row(ATTN_W), row(CONV_W), row(D_MODEL), VMEM_WHOLE,
                  pl.BlockSpec((8, ATTN_W), lambda i: (0, 0)), pl.BlockSpec((8, D_MODEL), lambda i: (0, 0)),
                  pl.BlockSpec((8, D_MODEL), lambda i: (0, 0))],
        out_specs=[row(D_MODEL)] * 4,
        out_shape=[jax.ShapeDtypeStruct((s, D_MODEL), BF16)] + [jax.ShapeDtypeStruct((s, D_MODEL), F32)] * 3,
        compiler_params=_params(("parallel",), VMEM_MID),
    )(o, conv, x, w_out, gvec, mod, lnv)


def _ff_fwd(x1, tgt, w1, w2, mod, lnv, tm):
    s = x1.shape[0]

    def body(x1_ref, t_ref, w1_ref, w2_ref, mod_ref, ln_ref, u2_ref, hid_ref, dff_ref, dxa_ref, st_ref):
        i = pl.program_id(0)
        x1v = x1_ref[...]
        u2 = (x1v * (1.0 + mod_ref[4:5, :]) + mod_ref[3:4, :]).astype(BF16)
        u2_ref[...] = u2
        ff = jnp.zeros((tm, D_MODEL), F32)
        for d in range(N_DEV):
            h = _dot(u2, w1_ref[d])
            hr = jnp.maximum(h, 0.0)
            hb = (hr * hr).astype(BF16)
            hid_ref[:, d * FF_SHARD:(d + 1) * FF_SHARD] = hb
            ff = ff + _dot(hb, w2_ref[d * FF_SHARD:(d + 1) * FF_SHARD, :])
        gate = 1.0 + mod_ref[5:6, :]
        z2 = ALPHA * x1v + gate * ff
        xh, rstd = _ln_norm(z2)
        y = xh * ln_ref[2:3, :] + ln_ref[3:4, :]
        diff = y - t_ref[...]
        dy = diff * (1.0 / D_MODEL)
        dz2 = _ln_bwd(dy * ln_ref[2:3, :], xh, rstd)
        dff_ref[...] = (gate * dz2).astype(BF16)
        dxa_ref[...] = ALPHA * dz2

        @pl.when(i == 0)
        def _():
            st_ref[...] = jnp.zeros_like(st_ref)

        _acc_rows(st_ref, 0, _rowsum(dy * xh))
        _acc_rows(st_ref, 1, _rowsum(dy))
        _acc_rows(st_ref, 2, _rowsum(dz2 * ff))
        sq = _rowsum(jnp.sum(diff * diff, axis=1, keepdims=True))
        _acc_rows(st_ref, 3, jnp.broadcast_to(sq * (0.5 / D_MODEL), (1, D_MODEL)))

    row = lambda w: pl.BlockSpec((tm, w), lambda i: (i, 0))
    vec = pl.BlockSpec((8, D_MODEL), lambda i: (0, 0))
    return pl.pallas_call(
        body, name="ff_fwd", grid=(s // tm,),
        in_specs=[row(D_MODEL), row(D_MODEL), VMEM_WHOLE, VMEM_WHOLE, vec, vec],
        out_specs=[row(D_MODEL), row(D_FF), row(D_MODEL), row(D_MODEL), vec],
        out_shape=[jax.ShapeDtypeStruct((s, D_MODEL), BF16), jax.ShapeDtypeStruct((s, D_FF), BF16),
                   jax.ShapeDtypeStruct((s, D_MODEL), BF16), jax.ShapeDtypeStruct((s, D_MODEL), F32),
                   jax.ShapeDtypeStruct((8, D_MODEL), F32)],
        compiler_params=_params(("arbitrary",), VMEM_BIG),
    )(x1, tgt, w1, w2, mod, lnv)


def _ff_bwd(dff, hid, w1, w2, dxa, x1, z1, mixed, mod, lnv, tm):
    s = x1.shape[0]

    def body(dff_ref, hid_ref, w1_ref, w2_ref, dxa_ref, x1_ref, z1_ref, mixed_ref, mod_ref, ln_ref,
             dh_ref, dxb_ref, dmix_ref, st_ref):
        i = pl.program_id(0)
        dffv = dff_ref[...]
        du2 = jnp.zeros((tm, D_MODEL), F32)
        for d in range(N_DEV):
            cols = slice(d * FF_SHARD, (d + 1) * FF_SHARD)
            dhid = _dot_nt(dffv, w2_ref[cols, :])
            dh = (dhid * (2.0 * jnp.sqrt(hid_ref[:, cols].astype(F32)))).astype(BF16)
            dh_ref[:, cols] = dh
            du2 = du2 + _dot_nt(dh, w1_ref[d])
        x1v = x1_ref[...]
        dx1 = dxa_ref[...] + du2 * (1.0 + mod_ref[4:5, :])
        xh, rstd = _ln_norm(z1_ref[...])
        dz1 = _ln_bwd(dx1 * ln_ref[0:1, :], xh, rstd)
        dxb_ref[...] = ALPHA * dz1
        dmix_ref[...] = ((1.0 + mod_ref[2:3, :]) * dz1).astype(BF16)

        @pl.when(i == 0)
        def _():
            st_ref[...] = jnp.zeros_like(st_ref)

        _acc_rows(st_ref, 0, _rowsum(du2 * x1v))
        _acc_rows(st_ref, 1, _rowsum(du2))
        _acc_rows(st_ref, 2, _rowsum(dx1 * xh))
        _acc_rows(st_ref, 3, _rowsum(dx1))
        _acc_rows(st_ref, 4, _rowsum(dz1 * mixed_ref[...]))

    row = lambda w: pl.BlockSpec((tm, w), lambda i: (i, 0))
    vec = pl.BlockSpec((8, D_MODEL), lambda i: (0, 0))
    return pl.pallas_call(
        body, name="ff_bwd", grid=(s // tm,),
        in_specs=[row(D_MODEL), row(D_FF), VMEM_WHOLE, VMEM_WHOLE, row(D_MODEL), row(D_MODEL), row(D_MODEL),
                  row(D_MODEL), vec, vec],
        out_specs=[row(D_FF), row(D_MODEL), row(D_MODEL), vec],
        out_shape=[jax.ShapeDtypeStruct((s, D_FF), BF16), jax.ShapeDtypeStruct((s, D_MODEL), F32),
                   jax.ShapeDtypeStruct((s, D_MODEL), BF16), jax.ShapeDtypeStruct((8, D_MODEL), F32)],
        compiler_params=_params(("arbitrary",), VMEM_BIG),
    )(dff, hid, w1, w2, dxa, x1, z1, mixed, mod, lnv)


def _mix_bwd(dmix, w_out, o, conv, gvec, tm):
    s = o.shape[0]

    def body(dm_ref, w_ref, o_ref, c_ref, g_ref, do_ref, dc_ref, st_ref):
        i = pl.program_id(0)
        dmv = dm_ref[...]

        @pl.when(i == 0)
        def _():
            st_ref[...] = jnp.zeros_like(st_ref)

        for part, (src, dst) in enumerate(((o_ref, do_ref), (c_ref, dc_ref))):
            dr = _dot_nt(dmv, w_ref[part * ATTN_W:(part + 1) * ATTN_W, :])
            v = src[...]
            rr = lax.rsqrt(jnp.mean(v * v, axis=1, keepdims=True) + LN_EPS)
            vh = v * rr
            _acc_rows(st_ref, part, _rowsum(dr * vh))
            t = dr * g_ref[part:part + 1, :]
            dst[...] = rr * (t - vh * jnp.mean(t * vh, axis=1, keepdims=True))

    row = lambda w: pl.BlockSpec((tm, w), lambda i: (i, 0))
    vec = pl.BlockSpec((8, ATTN_W), lambda i: (0, 0))
    return pl.pallas_call(
        body, name="mix_bwd", grid=(s // tm,),
        in_specs=[row(D_MODEL), VMEM_WHOLE, row(ATTN_W), row(CONV_W), vec],
        out_specs=[row(ATTN_W), row(CONV_W), vec],
        out_shape=[jax.ShapeDtypeStruct((s, ATTN_W), F32), jax.ShapeDtypeStruct((s, CONV_W), F32),
                   jax.ShapeDtypeStruct((8, ATTN_W), F32)],
        compiler_params=_params(("arbitrary",), VMEM_MID),
    )(dmix, w_out, o, conv, gvec)


def _conv_bwd(dconv, ypre, ag, wdw, vecs, gmat, tm):
    s = ag.shape[0]
    hb = tm // HALO
    nt = s // tm
    last_halo = s // HALO - 1

    def body(dc_ref, dch_ref, yp_ref, yph_ref, cur_ref, w_ref, vec_ref, g_ref,
             dag_ref, st_ref, dw_ref, bufd, shd, dwacc):
        i = pl.program_id(0)

        def dyc_of(yp, dc):
            yh, rstd, yn = _gn_fwd(yp, vec_ref, g_ref)
            sg = _sigmoid(yn)
            dyn = dc * (sg * (1.0 + yn * (1.0 - sg)))
            dyh = dyn * vec_ref[1:2, :]
            dyc = rstd * (dyh - _dot3(dyh, g_ref) - yh * _dot3(dyh * yh, g_ref))
            return dyc, dyn, yh

        dyc, dyn, yh = dyc_of(yp_ref[...], dc_ref[...])
        dych, _, _ = dyc_of(yph_ref[...], dch_ref[...])
        bufd[0:tm, :] = dyc
        bufd[tm:tm + HALO, :] = jnp.where(i < nt - 1, dych, 0.0)

        @pl.when(i == 0)
        def _():
            st_ref[...] = jnp.zeros_like(st_ref)
            dw_ref[...] = jnp.zeros_like(dw_ref)

        _acc_rows(st_ref, 0, _rowsum(dyc))
        _acc_rows(st_ref, 1, _rowsum(dyn * yh))
        _acc_rows(st_ref, 2, _rowsum(dyn))

        _shifted_copies(bufd, shd, tm)
        dwacc[...] = jnp.zeros_like(dwacc)

        def chunk(c, carry):
            r0 = pl.multiple_of(c * CONV_CHUNK, CONV_CHUNK)
            a_c = cur_ref[pl.ds(r0, CONV_CHUNK), 0:CONV_W]
            sg_c = _sigmoid(cur_ref[pl.ds(r0, CONV_CHUNK), CONV_W:2 * CONV_W])
            ug_c = a_c * sg_c
            dug = jnp.zeros((CONV_CHUNK, CONV_W), F32)
            for j in range(CONV_K):
                win = _window(bufd, shd, CONV_K - 1 - j, CONV_CHUNK, r0)
                dug = dug + w_ref[j:j + 1, :] * win
                prod = ug_c * win
                part = prod[0:8]
                for g8 in range(1, CONV_CHUNK // 8):
                    part = part + prod[8 * g8:8 * g8 + 8]
                dwacc[j] = dwacc[j] + part
            dag_ref[pl.ds(r0, CONV_CHUNK), 0:CONV_W] = (dug * sg_c).astype(BF16)
            dag_ref[pl.ds(r0, CONV_CHUNK), CONV_W:2 * CONV_W] = (dug * a_c * sg_c * (1.0 - sg_c)).astype(BF16)
            return carry

        lax.fori_loop(0, tm // CONV_CHUNK, chunk, 0)
        for j in range(CONV_K):
            _acc_rows(dw_ref, j, _rowsum(dwacc[j]))

    row = lambda w: pl.BlockSpec((tm, w), lambda i: (i, 0))
    nxt = lambda w: pl.BlockSpec((HALO, w), lambda i: (jnp.minimum((i + 1) * hb, last_halo), 0))
    return pl.pallas_call(
        body, name="conv_bwd", grid=(nt,),
        in_specs=[row(CONV_W), nxt(CONV_W), row(CONV_W), nxt(CONV_W), row(2 * CONV_W),
                  pl.BlockSpec((32, CONV_W), lambda i: (0, 0)),
                  pl.BlockSpec((8, CONV_W), lambda i: (0, 0)),
                  pl.BlockSpec((CONV_W, CONV_W), lambda i: (0, 0))],
        out_specs=[row(2 * CONV_W), pl.BlockSpec((8, CONV_W), lambda i: (0, 0)),
                   pl.BlockSpec((32, CONV_W), lambda i: (0, 0))],
        out_shape=[jax.ShapeDtypeStruct((s, 2 * CONV_W), BF16), jax.ShapeDtypeStruct((8, CONV_W), F32),
                   jax.ShapeDtypeStruct((32, CONV_W), F32)],
        scratch_shapes=[pltpu.VMEM((tm + HALO, CONV_W), F32), pltpu.VMEM((7, tm + SHIFT_ROWS, CONV_W), F32),
                        pltpu.VMEM((32, 8, CONV_W), F32)],
        compiler_params=_params(("arbitrary",), VMEM_MID),
    )(dconv, dconv, ypre, ypre, ag, wdw, vecs, gmat)


def _attn_bwd(qkv, cum_t, o, do, lse, tq, tk, grads):
    s = qkv.shape[0]
    nq = s // tq
    r = tq // tk
    n = len(grads)

    def body(q_ref, k_ref, v_ref, cum_ref, o_ref, do_ref, lse_ref, *rest):
        dq_ref, dkt_ref, dvt_ref, dcum_ref, drow_ref = rest[n:n + 5]
        hp = pl.program_id(0)
        qi = pl.program_id(1)
        start, wait = _exchange(rest[:n], rest[n + 5:2 * n + 5], *rest[2 * n + 5:], gather=False)
        pl.when(jnp.logical_and(hp == 0, qi == 0))(start)

        @pl.when(qi == 0)
        def _():
            dkt_ref[...] = jnp.zeros_like(dkt_ref)
            dvt_ref[...] = jnp.zeros_like(dvt_ref)
            dcum_ref[...] = jnp.zeros_like(dcum_ref)

        qs = q_ref[...] * 0.125
        ov = o_ref[...]
        dob = do_ref[...].astype(BF16)
        lane = lax.broadcasted_iota(jnp.int32, (1, 128), 1)
        subl = lax.broadcasted_iota(jnp.int32, (128, 1), 0)
        hmask = (lane < 64, lane >= 64)
        aux = (64, 0)
        ones_aux = [jnp.where(lane == aux[h], 1.0, 0.0) for h in range(2)]
        qm, dom, delta, lse_h, qat, dot_t = [], [], [], [], [], []
        for h in range(2):
            qm.append(jnp.where(hmask[h], qs, jnp.zeros_like(qs)))
            dom.append(jnp.where(hmask[h], dob, jnp.zeros_like(dob)))
            delta.append(jnp.sum(dom[h].astype(F32) * ov, axis=1, keepdims=True))
            lse_h.append(lse_ref[:, h:h + 1])
            qat.append(jnp.where(hmask[h], qs.astype(F32), ones_aux[h]).T.astype(BF16))
            dot_t.append(dom[h].astype(F32).T.astype(BF16))

        def step(kb, carry, r0=None):
            lo = 0 if r0 is None else r0
            k0 = pl.multiple_of(kb * tk, tk)
            k2 = k_ref[pl.ds(k0, tk), :]
            v2 = v_ref[pl.ds(k0, tk), :]
            dq_new, a, b = [], [], []
            for h in range(2):
                cs = cum_ref[pl.ds(2 * hp + h, 1), pl.ds(k0, tk)]
                sc = _dot_nt(qm[h][lo:], k2) - cs
                p = jnp.exp(sc - lse_h[h][lo:])
                if r0 is not None:
                    visible = (lax.broadcasted_iota(jnp.int32, (tq - lo, tk), 1)
                               <= lax.broadcasted_iota(jnp.int32, (tq - lo, tk), 0))
                    p = jnp.where(visible, p, 0.0)
                dp = _dot_nt(dom[h][lo:], v2)
                ds = p * (dp - delta[h][lo:])
                pb = p.astype(BF16)
                dsb = ds.astype(BF16)
                a.append(_dot(qat[h][:, lo:], dsb))
                b.append(_dot(dot_t[h][:, lo:], pb))
                ka = jnp.where(hmask[h], k2, ones_aux[h].astype(k2.dtype))
                dq_h = carry[h][lo:] + _dot(dsb, ka)
                dq_new.append(jnp.concatenate([carry[h][:lo], dq_h], axis=0) if lo else dq_h)
            dkt_ref[:, pl.ds(k0, tk)] = dkt_ref[:, pl.ds(k0, tk)] + jnp.where(subl < 64, a[0], a[1])
            dvt_ref[:, pl.ds(k0, tk)] = dvt_ref[:, pl.ds(k0, tk)] + (b[0] + b[1])
            for h in range(2):
                dcum_ref[pl.ds(h, 1), pl.ds(k0, tk)] = (dcum_ref[pl.ds(h, 1), pl.ds(k0, tk)]
                                                        - a[h][aux[h]:aux[h] + 1, :])
            return tuple(dq_new)

        init = (jnp.zeros((tq, 128), F32),) * 2
        carry = lax.fori_loop(0, qi * r, step, init)
        for d in range(r):
            carry = step(qi * r + d, carry, r0=d * tk)
        dq0, dq1 = carry
        dq_ref[...] = (jnp.where(hmask[0], dq0, dq1) * 0.125).astype(BF16)
        drow_ref[:, 0:1] = dq0[:, aux[0]:aux[0] + 1]
        drow_ref[:, 1:2] = dq1[:, aux[1]:aux[1] + 1]
        pl.when(jnp.logical_and(hp == 3, qi == nq - 1))(wait)

    return pl.pallas_call(
        body, name="attn_bwd", grid=(4, nq),
        in_specs=[pl.BlockSpec((tq, 128), lambda h, i: (i, h)),
                  pl.BlockSpec((s, 128), lambda h, i: (0, 4 + h)),
                  pl.BlockSpec((s, 128), lambda h, i: (0, 8 + h)),
                  pl.BlockSpec((8, s), lambda h, i: (0, 0)),
                  pl.BlockSpec((tq, 128), lambda h, i: (i, h)),
                  pl.BlockSpec((tq, 128), lambda h, i: (i, h)),
                  pl.BlockSpec((None, tq, 2), lambda h, i: (h, i, 0))] + [ANY_SPACE] * n,
        out_specs=[pl.BlockSpec((tq, 128), lambda h, i: (i, h)),
                   pl.BlockSpec((128, s), lambda h, i: (h, 0)),
                   pl.BlockSpec((128, s), lambda h, i: (h, 0)),
                   pl.BlockSpec((None, 8, s), lambda h, i: (h, 0, 0)),
                   pl.BlockSpec((None, tq, 2), lambda h, i: (h, i, 0))] + [ANY_SPACE] * n,
        out_shape=[jax.ShapeDtypeStruct((s, ATTN_W), BF16), jax.ShapeDtypeStruct((ATTN_W, s), F32),
                   jax.ShapeDtypeStruct((ATTN_W, s), F32), jax.ShapeDtypeStruct((4, 8, s), F32),
                   jax.ShapeDtypeStruct((4, s, 2), F32)]
        + [jax.ShapeDtypeStruct(g.shape, g.dtype) for g in grads],
        scratch_shapes=_exchange_scratch(n),
        compiler_params=_params(("arbitrary", "arbitrary"), VMEM_BIG),
    )(qkv, qkv, qkv, cum_t, o, do, lse, *grads)


def _fgate_bwd(dcum_t, drow_t, sneg_t):
    s = dcum_t.shape[1]

    def body(dc_ref, dr_ref, sn_ref, df_ref, db_ref):
        lane = lax.broadcasted_iota(jnp.int32, (8, s), 1)
        acc = dc_ref[...] + dr_ref[...]
        sh = 1
        while sh < s:
            acc = acc + jnp.where(lane < s - sh, pltpu.roll(acc, s - sh, axis=1), 0.0)
            sh *= 2
        df = acc * sn_ref[...]
        df_ref[...] = df
        db_ref[...] = jnp.broadcast_to(jnp.sum(df, axis=1, keepdims=True), (8, 128))

    return pl.pallas_call(
        body, name="fgate_bwd",
        out_shape=[jax.ShapeDtypeStruct((8, s), F32), jax.ShapeDtypeStruct((8, 128), F32)],
        in_specs=[VMEM_WHOLE] * 3, out_specs=[VMEM_WHOLE] * 2,
    )(dcum_t, drow_t, sneg_t)


def _in_proj_bwd(dq, dkt, dvt, dag, dft, wqkv, wkvt, wag, wft, x, dxb, mod, tm):
    s = x.shape[0]

    def body(dq_ref, dkt_ref, dvt_ref, dag_ref, dft_ref, wqkv_ref, wkvt_ref, wag_ref, wft_ref, x_ref, dxb_ref,
             mod_ref, dkvt_ref, dx_ref, st_ref):
        i = pl.program_id(0)
        dkb = dkt_ref[...].astype(BF16)
        dvb = dvt_ref[...].astype(BF16)
        dkvt_ref[0:ATTN_W, :] = dkb
        dkvt_ref[ATTN_W:2 * ATTN_W, :] = dvb
        du = _dot_nt(dq_ref[...], wqkv_ref[:, 0:ATTN_W])
        du = du + _dot_tn(dkb, wkvt_ref[0:ATTN_W, :])
        du = du + _dot_tn(dvb, wkvt_ref[ATTN_W:2 * ATTN_W, :])
        du = du + _dot_nt(dag_ref[...], wag_ref[...])
        du = du + _dot_tn(dft_ref[...].astype(BF16), wft_ref[...])
        xv = x_ref[...]
        dx_ref[...] = dxb_ref[...] + du * (1.0 + mod_ref[1:2, :])

        @pl.when(i == 0)
        def _():
            st_ref[...] = jnp.zeros_like(st_ref)

        _acc_rows(st_ref, 0, _rowsum(du * xv))
        _acc_rows(st_ref, 1, _rowsum(du))

    row = lambda w: pl.BlockSpec((tm, w), lambda i: (i, 0))
    col = lambda h: pl.BlockSpec((h, tm), lambda i: (0, i))
    vec = pl.BlockSpec((8, D_MODEL), lambda i: (0, 0))
    return pl.pallas_call(
        body, name="in_proj_bwd", grid=(s // tm,),
        in_specs=[row(ATTN_W), col(ATTN_W), col(ATTN_W), row(2 * CONV_W), col(8),
                  VMEM_WHOLE, VMEM_WHOLE, VMEM_WHOLE, VMEM_WHOLE, row(D_MODEL), row(D_MODEL), vec],
        out_specs=[col(2 * ATTN_W), row(D_MODEL), vec],
        out_shape=[jax.ShapeDtypeStruct((2 * ATTN_W, s), BF16), jax.ShapeDtypeStruct((s, D_MODEL), F32),
                   jax.ShapeDtypeStruct((8, D_MODEL), F32)],
        compiler_params=_params(("arbitrary",), VMEM_MID),
    )(dq, dkt, dvt, dag, dft, wqkv, wkvt, wag, wft, x, dxb, mod)


def _wgrad_nn(a, b, name, tm, tn, tk):
    m, s = a.shape
    n = b.shape[1]
    nk = s // tk

    def body(a_ref, b_ref, o_ref, acc_ref):
        k = pl.program_id(2)

        @pl.when(k == 0)
        def _():
            acc_ref[...] = jnp.zeros_like(acc_ref)

        acc_ref[...] += _dot(a_ref[...], b_ref[...])

        @pl.when(k == nk - 1)
        def _():
            o_ref[...] = acc_ref[...].astype(o_ref.dtype)

    return pl.pallas_call(
        body, name=name, grid=(m // tm, n // tn, nk),
        in_specs=[pl.BlockSpec((tm, tk), lambda i, j, k: (i, k)), pl.BlockSpec((tk, tn), lambda i, j, k: (k, j))],
        out_specs=pl.BlockSpec((tm, tn), lambda i, j, k: (i, j)),
        out_shape=jax.ShapeDtypeStruct((m, n), BF16),
        scratch_shapes=[pltpu.VMEM((tm, tn), F32)],
        compiler_params=_params(("parallel", "parallel", "arbitrary"), VMEM_MID),
    )(a, b)


def _wgrad_tn(a, b, name, tm, tn, tk, out_shape, out_spec):
    s, m = a.shape
    n = b.shape[1]
    nk = s // tk

    def body(a_ref, b_ref, o_ref, acc_ref):
        k = pl.program_id(2)

        @pl.when(k == 0)
        def _():
            acc_ref[...] = jnp.zeros_like(acc_ref)

        acc_ref[...] += _dot_tn(a_ref[...], b_ref[...])

        @pl.when(k == nk - 1)
        def _():
            o_ref[...] = acc_ref[...].astype(o_ref.dtype)

    return pl.pallas_call(
        body, name=name, grid=(m // tm, n // tn, nk),
        in_specs=[pl.BlockSpec((tk, tm), lambda i, j, k: (k, i)), pl.BlockSpec((tk, tn), lambda i, j, k: (k, j))],
        out_specs=out_spec, out_shape=out_shape,
        scratch_shapes=[pltpu.VMEM((tm, tn), F32)],
        compiler_params=_params(("parallel", "parallel", "arbitrary"), VMEM_MID),
    )(a, b)


def _wgrad_f(dft, u, tk):
    s = u.shape[0]
    nk = s // tk

    def body(d_ref, u_ref, o_ref, acc_ref):
        k = pl.program_id(0)

        @pl.when(k == 0)
        def _():
            acc_ref[...] = jnp.zeros_like(acc_ref)

        acc_ref[...] += _dot(d_ref[...].astype(BF16), u_ref[...])

        @pl.when(k == nk - 1)
        def _():
            o_ref[...] = acc_ref[...].astype(BF16)

    return pl.pallas_call(
        body, name="wgrad_f", grid=(nk,),
        in_specs=[pl.BlockSpec((8, tk), lambda k: (0, k)), pl.BlockSpec((tk, D_MODEL), lambda k: (k, 0))],
        out_specs=pl.BlockSpec((8, D_MODEL), lambda k: (0, 0)),
        out_shape=jax.ShapeDtypeStruct((8, D_MODEL), BF16),
        scratch_shapes=[pltpu.VMEM((8, D_MODEL), F32)],
        compiler_params=_params(("arbitrary",)),
    )(dft, u)


def _adamw_big(recv, w, m, v, name, tr):
    r, cdim = w.shape

    def body(g_ref, w_ref, m_ref, v_ref, go_ref, d_ref, mo_ref, vo_ref):
        g = g_ref[0].astype(F32)
        for k in range(1, N_DEV):
            g = g + g_ref[k].astype(F32)
        delta, m2, v2 = _adamw(w_ref[...], g, m_ref[...], v_ref[...])
        go_ref[...] = g
        d_ref[...] = delta
        mo_ref[...] = m2
        vo_ref[...] = v2

    blk = pl.BlockSpec((tr, cdim), lambda i: (i, 0))
    return pl.pallas_call(
        body, name=name, grid=(r // tr,),
        in_specs=[pl.BlockSpec((N_DEV, tr, cdim), lambda i: (0, i, 0)), blk, blk, blk],
        out_specs=[blk] * 4, out_shape=[jax.ShapeDtypeStruct((r, cdim), F32)] * 4,
        compiler_params=_params(("parallel",), VMEM_MID),
    )(recv, w, m, v)


def _adamw_ada(c_all, dada, w, m, v, tr):
    r, cdim = w.shape

    def body(c_ref, d_ref, w_ref, m_ref, v_ref, go_ref, dl_ref, mo_ref, vo_ref):
        cv = c_ref[...]
        sc = (cv * _sigmoid(cv)).astype(BF16)
        g = _dot_tn(sc, d_ref[...].astype(BF16))
        delta, m2, v2 = _adamw(w_ref[...], g, m_ref[...], v_ref[...])
        go_ref[...] = g
        dl_ref[...] = delta
        mo_ref[...] = m2
        vo_ref[...] = v2

    blk = pl.BlockSpec((tr, cdim), lambda i: (i, 0))
    return pl.pallas_call(
        body, name="adamw_ada", grid=(r // tr,),
        in_specs=[pl.BlockSpec((N_DEV, tr), lambda i: (0, i)), pl.BlockSpec((N_DEV, cdim), lambda i: (0, 0)),
                  blk, blk, blk],
        out_specs=[blk] * 4, out_shape=[jax.ShapeDtypeStruct((r, cdim), F32)] * 4,
        compiler_params=_params(("parallel",), VMEM_MID),
    )(c_all, dada, w, m, v)


def _adamw_small(parts, w, m, v):
    n = w.shape[1]

    def body(g_ref, w_ref, m_ref, v_ref, go_ref, d_ref, mo_ref, vo_ref):
        g = g_ref[0]
        for k in range(1, N_DEV):
            g = g + g_ref[k]
        delta, m2, v2 = _adamw(w_ref[...], g, m_ref[...], v_ref[...])
        go_ref[...] = g
        d_ref[...] = delta
        mo_ref[...] = m2
        vo_ref[...] = v2

    return pl.pallas_call(
        body, name="adamw_small", out_shape=[jax.ShapeDtypeStruct((1, n), F32)] * 4,
        in_specs=[VMEM_WHOLE] * 4, out_specs=[VMEM_WHOLE] * 4,
    )(parts, w, m, v)


def _pad_cols(a, n):
    return jnp.pad(a, ((0, 0), (0, n - a.shape[1])))


def _pad_rows(a, n):
    return jnp.pad(a, ((0, n - a.shape[0]), (0, 0)))


def kernel(x, c, w_ada, b_ada, w_in, b_forget, w_dw, b_dw, gn_g, gn_b, g_attn_out, g_conv_out, w_out, ln1_g, ln1_b, w_ff1, w_ff2, ln2_g, ln2_b, loss_target, m_w_ada, m_b_ada, m_w_in, m_b_forget, m_w_dw, m_b_dw, m_gn_g, m_gn_b, m_g_attn_out, m_g_conv_out, m_w_out, m_ln1_g, m_ln1_b, m_w_ff1, m_w_ff2, m_ln2_g, m_ln2_b, v_w_ada, v_b_ada, v_w_in, v_b_forget, v_w_dw, v_b_dw, v_gn_g, v_gn_b, v_g_attn_out, v_g_conv_out, v_w_out, v_ln1_g, v_ln1_b, v_w_ff1, v_w_ff2, v_ln2_g, v_ln2_b):
    s = x.shape[1]
    tm = min(ROW_TILE, s)
    tk_att = min(ATT_TILE, s)
    tq_att = min(ATT_Q_BLOCKS * tk_att, s)
    me = 4 * lax.axis_index("x") + 2 * lax.axis_index("y") + lax.axis_index("c")
    xs = x[0]
    tgt = loss_target[0]

    dw_n = CONV_K * DW_SHARD
    pack1 = jnp.concatenate([c, w_dw[0].reshape(1, dw_n), jnp.zeros((1, 2048 - dw_n), F32)], axis=1)
    g1 = _gather_small(pack1, "gather_c")[:, 0, :]
    c_all = g1[:, :D_MODEL]
    wdw_full = g1[:, D_MODEL:D_MODEL + dw_n].reshape(N_DEV, CONV_K, DW_SHARD)
    wdw_full = _pad_rows(wdw_full.transpose(1, 0, 2).reshape(CONV_K, CONV_W), 32)

    b_shard = lax.dynamic_slice(b_ada, (0, me * ADA_SHARD), (1, ADA_SHARD))
    ada_all = _gather_small(_ada_part(c_all, w_ada[0], b_shard), "gather_ada")
    ada = lax.dynamic_index_in_dim(ada_all, me, axis=1, keepdims=False).reshape(6, D_MODEL)
    mod = _pad_rows(ada, 8)

    gw_in, = _gather_weights([_pad_cols(w_in[0], IN_SHARD_PAD).astype(BF16)])
    w_in_full = gw_in[:, :, :IN_SHARD].transpose(1, 0, 2).reshape(D_MODEL, N_IN)
    wqkv = w_in_full[:, :3 * ATTN_W]
    wft = w_in_full[:, 3 * ATTN_W:3 * ATTN_W + N_HEADS].T
    wag = w_in_full[:, 3 * ATTN_W + N_HEADS:]

    qkv, ag, ft, u = _in_proj(xs, mod, wqkv, wag, wft, tm)
    cum_t, sneg_t = _fgate(ft, b_forget.reshape(N_HEADS, 1))
    o, lse, gw_out, gw_ff1, gw_ff2 = _attn_fwd(
        qkv, cum_t, tq_att, tk_att, [w_out[0].astype(BF16), w_ff1[0].astype(BF16), w_ff2[0].astype(BF16)])
    w_out_full = gw_out.reshape(D_MODEL, D_MODEL)
    w2_full = gw_ff2.reshape(D_FF, D_MODEL)
    cvec = _pad_rows(jnp.concatenate([b_dw, gn_g, gn_b], axis=0), 8)
    grp = jnp.arange(CONV_W) // 64
    gmat = jnp.where(grp[:, None] == grp[None, :], 1.0 / 64.0, 0.0).astype(BF16)
    ypre, conv = _conv_fwd(ag, wdw_full, cvec, gmat, tm)
    gvec = _pad_rows(jnp.concatenate([g_attn_out, g_conv_out], axis=0), 8)
    lnv = _pad_rows(jnp.concatenate([ln1_g, ln1_b, ln2_g, ln2_b], axis=0), 8)
    cat, mixed, z1, x1 = _mix_fwd(o, conv, xs, w_out_full, gvec, mod, lnv, tm)
    tf = min(FF_TILE, s)
    u2, hid, dff, dxa, st_ff = _ff_fwd(x1, tgt, gw_ff1, w2_full, mod, lnv, min(2 * FF_TILE, s))

    dh, dxb, dmix, st_fb = _ff_bwd(dff, hid, gw_ff1, w2_full, dxa, x1, z1, mixed, mod, lnv, tf)
    do, dconv, st_mix = _mix_bwd(dmix, w_out_full, o, conv, gvec, tm)
    dag, st_conv, dwdw = _conv_bwd(dconv, ypre, ag, wdw_full, cvec, gmat, tm)

    tk = min(WGRAD_K_TILE, s)
    g_ff2 = _wgrad_tn(hid, dff, "wgrad_ff2", 512, 1024, tk,
                      jax.ShapeDtypeStruct((D_FF, D_MODEL), BF16),
                      pl.BlockSpec((512, 1024), lambda i, j, k: (i, j))).reshape(N_DEV, FF_SHARD, D_MODEL)
    g_ff1 = _wgrad_tn(u2, dh, "wgrad_ff1", 1024, 512, tk,
                      jax.ShapeDtypeStruct((N_DEV, D_MODEL, FF_SHARD), BF16),
                      pl.BlockSpec((None, 1024, 512), lambda i, j, k: (j, i, 0)))
    g_out = _wgrad_tn(cat, dmix, "wgrad_out", 512, 1024, tk,
                      jax.ShapeDtypeStruct((D_MODEL, D_MODEL), BF16),
                      pl.BlockSpec((512, 1024), lambda i, j, k: (i, j))).reshape(N_DEV, OUT_SHARD, D_MODEL)
    dq, dkt, dvt, dcum, drow, r_out, r_ff1, r_ff2 = _attn_bwd(
        qkv, cum_t, o, do, lse, tq_att, tk_att, [g_out, g_ff1, g_ff2])
    dft, dbf = _fgate_bwd(dcum[:, :2, :].reshape(N_HEADS, s), drow.transpose(0, 2, 1).reshape(N_HEADS, s), sneg_t)
    wkvt = wqkv[:, ATTN_W:3 * ATTN_W].T
    dkvt, grad_x, st_in = _in_proj_bwd(dq, dkt, dvt, dag, dft, wqkv, wkvt, wag, wft, xs, dxb, mod, tm)

    g_q = _wgrad_tn(u, dq, "wgrad_q", 1024, 512, tk,
                    jax.ShapeDtypeStruct((D_MODEL, ATTN_W), BF16),
                    pl.BlockSpec((1024, 512), lambda i, j, k: (i, j)))
    g_kvt = _wgrad_nn(dkvt, u, "wgrad_kv", 512, 1024, tk)
    g_qkv = jnp.concatenate([g_q, g_kvt.T], axis=1)
    g_ag = _wgrad_tn(u, dag, "wgrad_ag", 1024, 512, tk,
                     jax.ShapeDtypeStruct((D_MODEL, 2 * CONV_W), BF16),
                     pl.BlockSpec((1024, 512), lambda i, j, k: (i, j)))
    g_ft = _wgrad_f(dft, u, tk)
    g_in = jnp.concatenate([g_qkv, g_ft.T, g_ag], axis=1)
    g_in = jnp.pad(g_in.reshape(D_MODEL, N_DEV, IN_SHARD).transpose(1, 0, 2),
                   ((0, 0), (0, 0), (0, IN_SHARD_PAD - IN_SHARD)))

    r_in, = _scatter_grads([g_in])

    dada = jnp.concatenate([st_in[1:2], st_in[0:1], st_fb[4:5], st_fb[1:2], st_fb[0:1], st_ff[2:3]], axis=1)
    loss_part = st_ff[3:4, 0:1]
    pack2 = jnp.concatenate([
        dada,
        _pad_cols(dbf[:, 0].reshape(1, N_HEADS), 128),
        dwdw[:CONV_K].reshape(1, CONV_K * CONV_W),
        st_conv[0:1], st_conv[1:2], st_conv[2:3], st_mix[0:1], st_mix[1:2],
        st_fb[2:3], st_fb[3:4], st_ff[0:1], st_ff[1:2],
        _pad_cols(loss_part, 128)], axis=1)
    g2 = _gather_small(pack2, "gather_small_grads")[:, 0, :]
    o_bf = 6 * D_MODEL
    o_dw = o_bf + 128
    o_v = o_dw + CONV_K * CONV_W
    o_ln = o_v + 5 * CONV_W
    o_loss = o_ln + 4 * D_MODEL
    dw_parts = lax.dynamic_slice_in_dim(
        g2[:, o_dw:o_v].reshape(N_DEV, CONV_K, N_DEV, DW_SHARD), me, 1, axis=2).reshape(N_DEV, dw_n)
    parts = jnp.concatenate([
        g2[:, :o_dw], _pad_cols(dw_parts, 2048), g2[:, o_v:o_loss + 128]], axis=1)[:, None, :]

    def pack_w(b_ada_, b_forget_, w_dw_, smalls):
        return jnp.concatenate([b_ada_, _pad_cols(b_forget_, 128), _pad_cols(w_dw_[0].reshape(1, dw_n), 2048)]
                               + smalls + [jnp.zeros((1, 128), F32)], axis=1)

    pw = pack_w(b_ada, b_forget, w_dw, [b_dw, gn_g, gn_b, g_attn_out, g_conv_out, ln1_g, ln1_b, ln2_g, ln2_b])
    pm = pack_w(m_b_ada, m_b_forget, m_w_dw, [m_b_dw, m_gn_g, m_gn_b, m_g_attn_out, m_g_conv_out,
                                              m_ln1_g, m_ln1_b, m_ln2_g, m_ln2_b])
    pv = pack_w(v_b_ada, v_b_forget, v_w_dw, [v_b_dw, v_gn_g, v_gn_b, v_g_attn_out, v_g_conv_out,
                                              v_ln1_g, v_ln1_b, v_ln2_g, v_ln2_b])
    small = _adamw_small(parts, pw, pm, pv)

    p_bf = 6 * D_MODEL
    p_dw = p_bf + 128
    p_v = p_dw + 2048
    p_ln = p_v + 5 * CONV_W
    p_loss = p_ln + 4 * D_MODEL

    def unpack(t):
        outs = {"b_ada": t[:, :p_bf], "b_forget": t[:, p_bf:p_bf + N_HEADS],
                "w_dw": t[:, p_dw:p_dw + dw_n].reshape(1, CONV_K, 1, DW_SHARD)}
        for k, nm in enumerate(["b_dw", "gn_g", "gn_b", "g_attn_out", "g_conv_out"]):
            outs[nm] = t[:, p_v + k * CONV_W:p_v + (k + 1) * CONV_W]
        for k, nm in enumerate(["ln1_g", "ln1_b", "ln2_g", "ln2_b"]):
            outs[nm] = t[:, p_ln + k * D_MODEL:p_ln + (k + 1) * D_MODEL]
        return outs

    sm = [unpack(t) for t in small]
    loss = small[0][0, p_loss]

    dada_all = g2[:, :6 * D_MODEL]
    dada_shard = lax.dynamic_slice_in_dim(dada_all, me * ADA_SHARD, ADA_SHARD, axis=1)
    big = {
        "w_ada": _adamw_ada(c_all, dada_shard, w_ada[0], m_w_ada[0], v_w_ada[0], 256),
        "w_in": [t[:, :IN_SHARD] for t in _adamw_big(
            r_in, _pad_cols(w_in[0], IN_SHARD_PAD), _pad_cols(m_w_in[0], IN_SHARD_PAD),
            _pad_cols(v_w_in[0], IN_SHARD_PAD), "adamw_in", 256)],
        "w_out": _adamw_big(r_out, w_out[0], m_w_out[0], v_w_out[0], "adamw_out", 128),
        "w_ff1": _adamw_big(r_ff1, w_ff1[0], m_w_ff1[0], v_w_ff1[0], "adamw_ff1", 256),
        "w_ff2": _adamw_big(r_ff2, w_ff2[0], m_w_ff2[0], v_w_ff2[0], "adamw_ff2", 256),
    }

    names = ["w_ada", "b_ada", "w_in", "b_forget", "w_dw", "b_dw", "gn_g", "gn_b", "g_attn_out", "g_conv_out",
             "w_out", "ln1_g", "ln1_b", "w_ff1", "w_ff2", "ln2_g", "ln2_b"]

    def leaf(kind, nm):
        if nm in big:
            return big[nm][kind][None]
        return sm[kind][nm]

    outs = [loss, grad_x[None]]
    for kind in range(4):
        outs += [leaf(kind, nm) for nm in names]
    return tuple(outs)
```

```python
import jax
import jax.numpy as jnp
from jax import lax
from jax.experimental import pallas as pl
from jax.experimental.pallas import tpu as pltpu

F32 = jnp.float32
BF16 = jnp.bfloat16

D_MODEL = 1024
ATTN_W = 512
CONV_W = 512
N_HEADS = 8
CONV_K = 31
D_FF = 4096
N_DEV = 8
N_IN = 3 * ATTN_W + N_HEADS + 2 * CONV_W
IN_SHARD = N_IN // N_DEV
IN_SHARD_PAD = 384
ADA_SHARD = 6 * D_MODEL // N_DEV
FF_SHARD = D_FF // N_DEV
OUT_SHARD = D_MODEL // N_DEV
DW_SHARD = CONV_W // N_DEV
HALO = 32
LN_EPS = 1e-5
ALPHA = 2.0 ** 0.25
ADAM_LR, ADAM_B1, ADAM_B2, ADAM_EPS, ADAM_WD, ADAM_STEP = 0.001, 0.9, 0.999, 1e-08, 0.01, 10

ROW_TILE = 512
FF_TILE = 256
WGRAD_K_TILE = 2048
ATT_TILE = 512
ATT_Q_BLOCKS = 2
VMEM_BIG = 56 * 1024 * 1024
VMEM_MID = 40 * 1024 * 1024

MESH = pl.DeviceIdType.MESH
VMEM_WHOLE = pl.BlockSpec(memory_space=pltpu.VMEM)
ANY_SPACE = pl.BlockSpec(memory_space=pl.ANY)


def _dot(a, b):
    return jnp.dot(a, b, preferred_element_type=F32)


def _dot_nt(a, b):
    return lax.dot_general(a, b, (((1,), (1,)), ((), ())), preferred_element_type=F32)


def _dot_tn(a, b):
    return lax.dot_general(a, b, (((0,), (0,)), ((), ())), preferred_element_type=F32)


def _dot3(xf, g_ref):
    g = g_ref[...]
    hi = xf.astype(BF16)
    lo = (xf - hi.astype(F32)).astype(BF16)
    return _dot(hi, g) + _dot(lo, g)


def _sigmoid(x):
    return 1.0 / (1.0 + jnp.exp(-x))


def _rowsum(x):
    return jnp.sum(x, axis=0, keepdims=True)


def _ln_norm(z):
    mu = jnp.mean(z, axis=1, keepdims=True)
    zc = z - mu
    var = jnp.mean(zc * zc, axis=1, keepdims=True)
    rstd = lax.rsqrt(var + LN_EPS)
    return zc * rstd, rstd


def _ln_bwd(dxh, xh, rstd):
    m1 = jnp.mean(dxh, axis=1, keepdims=True)
    m2 = jnp.mean(dxh * xh, axis=1, keepdims=True)
    return rstd * (dxh - m1 - xh * m2)


def _adamw(w, g, m, v):
    m2 = ADAM_B1 * m + (1.0 - ADAM_B1) * g
    v2 = ADAM_B2 * v + (1.0 - ADAM_B2) * (g * g)
    m_hat = m2 / (1.0 - ADAM_B1 ** ADAM_STEP)
    v_hat = v2 / (1.0 - ADAM_B2 ** ADAM_STEP)
    delta = -ADAM_LR * (m_hat / (jnp.sqrt(v_hat) + ADAM_EPS) + ADAM_WD * w)
    return delta, m2, v2


def _params(sem=None, vmem=None):
    kw = {}
    if sem is not None:
        kw["dimension_semantics"] = sem
    if vmem is not None:
        kw["vmem_limit_bytes"] = vmem
    return pltpu.CompilerParams(**kw)


def _coords():
    return lax.axis_index("x"), lax.axis_index("y"), lax.axis_index("c")


def _flip(v, bit):
    return 1 - v if bit else v


def _acc_rows(ref, r, val):
    ref[r:r + 1, :] = ref[r:r + 1, :] + val


def _small_gather(v_ref, out_ref, send_sems, recv_sems, local_sem):
    x, y, c = _coords()
    me = 4 * x + 2 * y + c
    mine = pltpu.make_async_copy(v_ref, out_ref.at[me], local_sem)
    sends, recvs = [], []
    for k in range(1, N_DEV):
        px, py, pc = _flip(x, (k >> 2) & 1), _flip(y, (k >> 1) & 1), _flip(c, k & 1)
        peer = 4 * px + 2 * py + pc
        sems = dict(send_sem=send_sems.at[k - 1], recv_sem=recv_sems.at[k - 1],
                    device_id=(px, py, pc), device_id_type=MESH)
        sends.append(pltpu.make_async_remote_copy(src_ref=v_ref, dst_ref=out_ref.at[me], **sems))
        recvs.append(pltpu.make_async_remote_copy(src_ref=v_ref, dst_ref=out_ref.at[peer], **sems))

    def start():
        mine.start()
        for cp in sends:
            cp.start()

    def wait():
        for cp in recvs:
            cp.wait_recv()
        for cp in sends:
            cp.wait_send()
        mine.wait()

    return start, wait


SMALL_GATHER_SCRATCH = [pltpu.SemaphoreType.DMA((N_DEV - 1,)), pltpu.SemaphoreType.DMA((N_DEV - 1,)),
                        pltpu.SemaphoreType.DMA]


def _gather_small(v, name):
    r, n = v.shape

    def body(v_ref, out_ref, send_sems, recv_sems, local_sem):
        start, wait = _small_gather(v_ref, out_ref, send_sems, recv_sems, local_sem)
        start()
        wait()

    return pl.pallas_call(
        body, name=name,
        out_shape=jax.ShapeDtypeStruct((N_DEV, r, n), v.dtype),
        in_specs=[VMEM_WHOLE], out_specs=VMEM_WHOLE,
        scratch_shapes=SMALL_GATHER_SCRATCH,
    )(v)


def _gather_weights(shards, small):
    n = len(shards)

    def body(*refs):
        ins, small_ref = refs[:n], refs[n]
        outs, small_out = refs[n + 1:2 * n + 1], refs[2 * n + 1]
        send_sems, recv_sems, local_sems = refs[2 * n + 2:2 * n + 5]
        start_small, wait_small = _small_gather(small_ref, small_out, *refs[2 * n + 5:])
        start_small()
        x, y, c = _coords()
        me, sibling = (x, y, c), (x, y, 1 - c)
        chips = [(1 - x, y), (x, 1 - y), (1 - x, 1 - y)]

        def slot(a, pos):
            return outs[a].at[4 * pos[0] + 2 * pos[1] + pos[2]]

        def copy(a, k, block, to, src=None):
            return pltpu.make_async_remote_copy(
                src_ref=slot(a, block) if src is None else src, dst_ref=slot(a, block),
                send_sem=send_sems.at[7 * a + k], recv_sem=recv_sems.at[7 * a + k],
                device_id=to, device_id_type=MESH)

        started = []
        local = []
        for a in range(n):
            mine = pltpu.make_async_copy(ins[a], slot(a, me), local_sems.at[a])
            mine.start()
            local.append(mine)
        for j, chip in enumerate(chips):
            for a in range(n):
                cp = copy(a, 1 + j, me, (*chip, c), src=ins[a])
                cp.start()
                started.append(cp)
        for a in range(n):
            cp = copy(a, 0, me, sibling, src=ins[a])
            cp.start()
            started.append(cp)
        for j, chip in enumerate(chips):
            for a in range(n):
                copy(a, 1 + j, (*chip, c), me).wait_recv()
                cp = copy(a, 4 + j, (*chip, c), sibling)
                cp.start()
                started.append(cp)
        for a in range(n):
            copy(a, 0, sibling, me).wait_recv()
        for j, chip in enumerate(chips):
            for a in range(n):
                copy(a, 4 + j, (*chip, 1 - c), me).wait_recv()
        for cp in started:
            cp.wait_send()
        for mine in local:
            mine.wait()
        wait_small()

    return pl.pallas_call(
        body, name="gather_weights",
        out_shape=[jax.ShapeDtypeStruct((N_DEV,) + s.shape, s.dtype) for s in shards]
        + [jax.ShapeDtypeStruct((N_DEV,) + small.shape, small.dtype)],
        in_specs=[ANY_SPACE] * n + [VMEM_WHOLE], out_specs=[ANY_SPACE] * n + [VMEM_WHOLE],
        scratch_shapes=[pltpu.SemaphoreType.DMA((7 * n,)), pltpu.SemaphoreType.DMA((7 * n,)),
                        pltpu.SemaphoreType.DMA((n,))] + SMALL_GATHER_SCRATCH,
    )(*shards, small)


def _exchange(ins, outs, send_sems, recv_sems, local_sems, gather):
    n = len(ins)
    x, y, c = _coords()
    me = 4 * x + 2 * y + c
    local = [pltpu.make_async_copy(ins[a] if gather else ins[a].at[me], outs[a].at[me], local_sems.at[a])
             for a in range(n)]
    sends, recvs = [], []
    for k in range(1, N_DEV):
        px, py, pc = _flip(x, (k >> 2) & 1), _flip(y, (k >> 1) & 1), _flip(c, k & 1)
        peer = 4 * px + 2 * py + pc
        for a in range(n):
            src = ins[a] if gather else ins[a].at[peer]
            sems = dict(send_sem=send_sems.at[7 * a + k - 1], recv_sem=recv_sems.at[7 * a + k - 1],
                        device_id=(px, py, pc), device_id_type=MESH)
            sends.append(pltpu.make_async_remote_copy(src_ref=src, dst_ref=outs[a].at[me], **sems))
            recvs.append(pltpu.make_async_remote_copy(src_ref=src, dst_ref=outs[a].at[peer], **sems))

    def start():
        for cp in local + sends:
            cp.start()

    def wait():
        for cp in recvs:
            cp.wait_recv()
        for cp in sends:
            cp.wait_send()
        for cp in local:
            cp.wait()

    return start, wait


def _exchange_scratch(n):
    return [pltpu.SemaphoreType.DMA((7 * n,)), pltpu.SemaphoreType.DMA((7 * n,)), pltpu.SemaphoreType.DMA((n,))]


def _scatter_grads(grads, small):
    n = len(grads)

    def body(*refs):
        start, wait = _exchange(refs[:n], refs[n + 1:2 * n + 1], *refs[2 * n + 2:2 * n + 5], gather=False)
        start_small, wait_small = _small_gather(refs[n], refs[2 * n + 1], *refs[2 * n + 5:])
        start()
        start_small()
        wait_small()
        wait()

    return pl.pallas_call(
        body, name="scatter_grads",
        out_shape=[jax.ShapeDtypeStruct(g.shape, g.dtype) for g in grads]
        + [jax.ShapeDtypeStruct((N_DEV,) + small.shape, small.dtype)],
        in_specs=[ANY_SPACE] * n + [VMEM_WHOLE], out_specs=[ANY_SPACE] * n + [VMEM_WHOLE],
        scratch_shapes=_exchange_scratch(n) + SMALL_GATHER_SCRATCH,
    )(*grads, small)


def _ada_part(c_all, w_ada, b_shard):
    def body(c_ref, w_ref, b_ref, o_ref):
        cv = c_ref[...]
        sc = (cv * _sigmoid(cv)).astype(BF16)
        o_ref[...] = _dot(sc, w_ref[...].astype(BF16)) + b_ref[...]

    return pl.pallas_call(
        body, name="ada_part", out_shape=jax.ShapeDtypeStruct((N_DEV, ADA_SHARD), F32),
        in_specs=[VMEM_WHOLE] * 3, out_specs=VMEM_WHOLE,
        compiler_params=_params(vmem=VMEM_MID),
    )(c_all, w_ada, b_shard)


def _in_proj(x, mod, wqkv, wag, wft, tm):
    s = x.shape[0]

    def body(x_ref, mod_ref, wqkv_ref, wag_ref, wft_ref, qkv_ref, ag_ref, ft_ref, u_ref):
        u = x_ref[...] * (1.0 + mod_ref[1:2, :]) + mod_ref[0:1, :]
        ub = u.astype(BF16)
        u_ref[...] = ub
        qkv_ref[...] = _dot(ub, wqkv_ref[...]).astype(BF16)
        ag_ref[...] = _dot(ub, wag_ref[...])
        ft_ref[...] = _dot_nt(wft_ref[...], ub)

    return pl.pallas_call(
        body, name="in_proj", grid=(s // tm,),
        in_specs=[pl.BlockSpec((tm, D_MODEL), lambda i: (i, 0)), pl.BlockSpec((8, D_MODEL), lambda i: (0, 0)),
                  VMEM_WHOLE, VMEM_WHOLE, VMEM_WHOLE],
        out_specs=[pl.BlockSpec((tm, 3 * ATTN_W), lambda i: (i, 0)),
                   pl.BlockSpec((tm, 2 * CONV_W), lambda i: (i, 0)),
                   pl.BlockSpec((8, tm), lambda i: (0, i)),
                   pl.BlockSpec((tm, D_MODEL), lambda i: (i, 0))],
        out_shape=[jax.ShapeDtypeStruct((s, 3 * ATTN_W), BF16), jax.ShapeDtypeStruct((s, 2 * CONV_W), F32),
                   jax.ShapeDtypeStruct((8, s), F32), jax.ShapeDtypeStruct((s, D_MODEL), BF16)],
        compiler_params=_params(("parallel",), VMEM_MID),
    )(x, mod, wqkv, wag, wft)


def _fgate(ft, bcol):
    s = ft.shape[1]

    def body(f_ref, b_ref, cum_ref, sneg_ref):
        z = f_ref[...] + b_ref[...]
        e = jnp.exp(-jnp.abs(z))
        l1p = jnp.where(e < 1e-2, e * (1.0 - e * (0.5 - e * (1.0 / 3.0))), jnp.log(1.0 + e))
        logf = jnp.minimum(z, 0.0) - l1p
        r = 1.0 / (1.0 + e)
        sneg_ref[...] = jnp.where(z >= 0, e * r, r)
        lane = lax.broadcasted_iota(jnp.int32, (8, s), 1)
        acc = logf
        sh = 1
        while sh < s:
            acc = acc + jnp.where(lane >= sh, pltpu.roll(acc, sh, axis=1), 0.0)
            sh *= 2
        cum_ref[...] = acc

    return pl.pallas_call(
        body, name="fgate", out_shape=[jax.ShapeDtypeStruct((8, s), F32)] * 2,
        in_specs=[VMEM_WHOLE] * 2, out_specs=[VMEM_WHOLE] * 2,
    )(ft, bcol)


def _attn_fwd(qkv, cum_t, tq, tk, shards):
    s = qkv.shape[0]
    nq = s // tq
    r = tq // tk
    n = len(shards)

    def body(q_ref, k_ref, v_ref, cum_ref, *rest):
        o_ref, lse_ref = rest[n:n + 2]
        hp = pl.program_id(0)
        qi = pl.program_id(1)
        start, wait = _exchange(rest[:n], rest[n + 2:2 * n + 2], *rest[2 * n + 2:], gather=True)
        pl.when(jnp.logical_and(hp == 0, qi == 0))(start)
        q2 = q_ref[...]
        lane = lax.broadcasted_iota(jnp.int32, (1, 128), 1)
        hmask = (lane < 64, lane >= 64)
        aux = (64, 0)
        qm = [jnp.where(hmask[h], q2, jnp.zeros_like(q2)) * 0.125 for h in range(2)]

        def step(kb, carry, r0=None):
            lo = 0 if r0 is None else r0
            kw = tq if r0 is None else tk
            k0 = pl.multiple_of(kb * kw, kw)
            k2 = k_ref[pl.ds(k0, kw), :]
            v2 = v_ref[pl.ds(k0, kw), :]
            out = []
            for h in range(2):
                m_all, acc_all = carry[2 * h], carry[2 * h + 1]
                m, acc = m_all[lo:], acc_all[lo:]
                sc = _dot_nt(qm[h][lo:], k2) - cum_ref[pl.ds(2 * hp + h, 1), pl.ds(k0, kw)]
                if r0 is not None:
                    visible = (lax.broadcasted_iota(jnp.int32, (tq - lo, tk), 1)
                               <= lax.broadcasted_iota(jnp.int32, (tq - lo, tk), 0))
                    sc = jnp.where(visible, sc, -jnp.inf)
                m_new = jnp.maximum(m, jnp.max(sc, axis=1, keepdims=True))
                p = jnp.exp(sc - m_new)
                corr = jnp.exp(m - m_new)
                va = jnp.where(hmask[h], v2, jnp.where(lane == aux[h], 1.0, 0.0).astype(v2.dtype))
                acc_new = acc * corr + _dot(p.astype(BF16), va)
                if lo:
                    m_new = jnp.concatenate([m_all[:lo], m_new], axis=0)
                    acc_new = jnp.concatenate([acc_all[:lo], acc_new], axis=0)
                out += [m_new, acc_new]
            return tuple(out)

        init = (jnp.full((tq, 1), -jnp.inf, F32), jnp.zeros((tq, 128), F32)) * 2
        carry = lax.fori_loop(0, qi, step, init)
        for d in range(r):
            carry = step(qi * r + d, carry, r0=d * tk)
        m0, acc0, m1, acc1 = carry
        l0 = acc0[:, aux[0]:aux[0] + 1]
        l1 = acc1[:, aux[1]:aux[1] + 1]
        o_ref[...] = jnp.where(hmask[0], acc0 / l0, acc1 / l1)
        lse_ref[:, 0:1] = m0 + jnp.log(l0)
        lse_ref[:, 1:2] = m1 + jnp.log(l1)
        pl.when(jnp.logical_and(hp == 3, qi == nq - 1))(wait)

    return pl.pallas_call(
        body, name="attn_fwd", grid=(4, nq),
        in_specs=[pl.BlockSpec((tq, 128), lambda h, i: (i, h)),
                  pl.BlockSpec((s, 128), lambda h, i: (0, 4 + h)),
                  pl.BlockSpec((s, 128), lambda h, i: (0, 8 + h)),
                  pl.BlockSpec((8, s), lambda h, i: (0, 0))] + [ANY_SPACE] * n,
        out_specs=[pl.BlockSpec((tq, 128), lambda h, i: (i, h)),
                   pl.BlockSpec((None, tq, 2), lambda h, i: (h, i, 0))] + [ANY_SPACE] * n,
        out_shape=[jax.ShapeDtypeStruct((s, ATTN_W), F32), jax.ShapeDtypeStruct((4, s, 2), F32)]
        + [jax.ShapeDtypeStruct((N_DEV,) + w.shape, w.dtype) for w in shards],
        scratch_shapes=_exchange_scratch(n),
        compiler_params=_params(("arbitrary", "arbitrary"), VMEM_BIG),
    )(qkv, qkv, qkv, cum_t, *shards)


def _gn_fwd(y, vec_ref, g_ref):
    mean = _dot3(y, g_ref)
    yc = y - mean
    var = _dot3(yc * yc, g_ref)
    rstd = lax.rsqrt(var + LN_EPS)
    yh = yc * rstd
    yn = yh * vec_ref[1:2, :] + vec_ref[2:3, :]
    return yh, rstd, yn


SHIFT_ROWS = HALO - 8
CONV_CHUNK = 32


def _shifted_copies(buf, sh, tm):
    for b in range(1, 8):
        sh[b - 1] = buf[pl.ds(b, tm + SHIFT_ROWS), :]


def _window(buf, sh, off, rows, r0=0):
    a, b = divmod(off, 8)
    if b == 0:
        return buf[pl.ds(8 * a + r0, rows), :]
    return sh[b - 1, pl.ds(8 * a + r0, rows), :]


def _conv_fwd(ag, wdw, vecs, gmat, tm):
    s = ag.shape[0]
    hb = tm // HALO

    def body(cur_ref, halo_ref, w_ref, vec_ref, g_ref, ypre_ref, conv_ref, buf, sh):
        i = pl.program_id(0)
        ug = cur_ref[:, 0:CONV_W] * _sigmoid(cur_ref[:, CONV_W:2 * CONV_W])
        ugh = halo_ref[:, 0:CONV_W] * _sigmoid(halo_ref[:, CONV_W:2 * CONV_W])
        buf[0:HALO, :] = jnp.where(i > 0, ugh, 0.0)
        buf[HALO:HALO + tm, :] = ug
        _shifted_copies(buf, sh, tm)
        y = jnp.broadcast_to(vec_ref[0:1, :], (tm, CONV_W))
        for j in range(CONV_K):
            y = y + w_ref[j:j + 1, :] * _window(buf, sh, HALO - (CONV_K - 1) + j, tm)
        ypre_ref[...] = y
        _, _, yn = _gn_fwd(y, vec_ref, g_ref)
        conv_ref[...] = yn * _sigmoid(yn)

    return pl.pallas_call(
        body, name="conv_fwd", grid=(s // tm,),
        in_specs=[pl.BlockSpec((tm, 2 * CONV_W), lambda i: (i, 0)),
                  pl.BlockSpec((HALO, 2 * CONV_W), lambda i: (jnp.maximum(i * hb - 1, 0), 0)),
                  pl.BlockSpec((32, CONV_W), lambda i: (0, 0)),
                  pl.BlockSpec((8, CONV_W), lambda i: (0, 0)),
                  pl.BlockSpec((CONV_W, CONV_W), lambda i: (0, 0))],
        out_specs=[pl.BlockSpec((tm, CONV_W), lambda i: (i, 0))] * 2,
        out_shape=[jax.ShapeDtypeStruct((s, CONV_W), F32)] * 2,
        scratch_shapes=[pltpu.VMEM((HALO + tm, CONV_W), F32), pltpu.VMEM((7, tm + SHIFT_ROWS, CONV_W), F32)],
        compiler_params=_params(("parallel",), VMEM_MID),
    )(ag, ag, wdw, vecs, gmat)


def _mix_fwd(o, conv, x, w_out, gvec, mod, lnv, tm):
    s = x.shape[0]

    def body(o_ref, c_ref, x_ref, w_ref, g_ref, mod_ref, ln_ref, cat_ref, mixed_ref, z1_ref, x1_ref):
        ov = o_ref[...]
        cv = c_ref[...]
        ra = ov * lax.rsqrt(jnp.mean(ov * ov, axis=1, keepdims=True) + LN_EPS) * g_ref[0:1, :]
        rc = cv * lax.rsqrt(jnp.mean(cv * cv, axis=1, keepdims=True) + LN_EPS) * g_ref[1:2, :]
        rab = ra.astype(BF16)
        rcb = rc.astype(BF16)
        cat_ref[:, 0:ATTN_W] = rab
        cat_ref[:, ATTN_W:D_MODEL] = rcb
        mixed = _dot(rab, w_ref[0:ATTN_W, :]) + _dot(rcb, w_ref[ATTN_W:D_MODEL, :])
        mixed_ref[...] = mixed
        z1 = ALPHA * x_ref[...] + (1.0 + mod_ref[2:3, :]) * mixed
        z1_ref[...] = z1
        xh, _ = _ln_norm(z1)
        x1_ref[...] = xh * ln_ref[0:1, :] + ln_ref[1:2, :]

    row = lambda w: pl.BlockSpec((tm, w), lambda i: (i, 0))
    return pl.pallas_call(
        body, name="mix_fwd", grid=(s // tm,),
        in_specs=[row(ATTN_W), row(CONV_W), row(D_MODEL), VMEM_WHOLE,
                  pl.BlockSpec((8, ATTN_W), lambda i: (0, 0)), pl.BlockSpec((8, D_MODEL), lambda i: (0, 0)),
                  pl.BlockSpec((8, D_MODEL), lambda i: (0, 0))],
        out_specs=[row(D_MODEL)] * 4,
        out_shape=[jax.ShapeDtypeStruct((s, D_MODEL), BF16)] + [jax.ShapeDtypeStruct((s, D_MODEL), F32)] * 3,
        compiler_params=_params(("parallel",), VMEM_MID),
    )(o, conv, x, w_out, gvec, mod, lnv)


def _ff_fwd(x1, tgt, w1, w2, mod, lnv, tm):
    s = x1.shape[0]

    def body(x1_ref, t_ref, w1_ref, w2_ref, mod_ref, ln_ref, u2_ref, hid_ref, dff_ref, dxa_ref, st_ref):
        i = pl.program_id(0)
        x1v = x1_ref[...]
        u2 = (x1v * (1.0 + mod_ref[4:5, :]) + mod_ref[3:4, :]).astype(BF16)
        u2_ref[...] = u2
        ff = jnp.zeros((tm, D_MODEL), F32)
        for d in range(N_DEV):
            h = _dot(u2, w1_ref[d])
            hr = jnp.maximum(h, 0.0)
            hb = (hr * hr).astype(BF16)
            hid_ref[:, d * FF_SHARD:(d + 1) * FF_SHARD] = hb
            ff = ff + _dot(hb, w2_ref[d * FF_SHARD:(d + 1) * FF_SHARD, :])
        gate = 1.0 + mod_ref[5:6, :]
        z2 = ALPHA * x1v + gate * ff
        xh, rstd = _ln_norm(z2)
        y = xh * ln_ref[2:3, :] + ln_ref[3:4, :]
        diff = y - t_ref[...]
        dy = diff * (1.0 / D_MODEL)
        dz2 = _ln_bwd(dy * ln_ref[2:3, :], xh, rstd)
        dff_ref[...] = (gate * dz2).astype(BF16)
        dxa_ref[...] = ALPHA * dz2

        @pl.when(i == 0)
        def _():
            st_ref[...] = jnp.zeros_like(st_ref)

        _acc_rows(st_ref, 0, _rowsum(dy * xh))
        _acc_rows(st_ref, 1, _rowsum(dy))
        _acc_rows(st_ref, 2, _rowsum(dz2 * ff))
        sq = _rowsum(jnp.sum(diff * diff, axis=1, keepdims=True))
        _acc_rows(st_ref, 3, jnp.broadcast_to(sq * (0.5 / D_MODEL), (1, D_MODEL)))

    row = lambda w: pl.BlockSpec((tm, w), lambda i: (i, 0))
    vec = pl.BlockSpec((8, D_MODEL), lambda i: (0, 0))
    return pl.pallas_call(
        body, name="ff_fwd", grid=(s // tm,),
        in_specs=[row(D_MODEL), row(D_MODEL), VMEM_WHOLE, VMEM_WHOLE, vec, vec],
        out_specs=[row(D_MODEL), row(D_FF), row(D_MODEL), row(D_MODEL), vec],
        out_shape=[jax.ShapeDtypeStruct((s, D_MODEL), BF16), jax.ShapeDtypeStruct((s, D_FF), BF16),
                   jax.ShapeDtypeStruct((s, D_MODEL), BF16), jax.ShapeDtypeStruct((s, D_MODEL), F32),
                   jax.ShapeDtypeStruct((8, D_MODEL), F32)],
        compiler_params=_params(("arbitrary",), VMEM_BIG),
    )(x1, tgt, w1, w2, mod, lnv)


def _ff_bwd(dff, hid, w1, w2, dxa, x1, z1, mixed, mod, lnv, tm):
    s = x1.shape[0]

    def body(dff_ref, hid_ref, w1_ref, w2_ref, dxa_ref, x1_ref, z1_ref, mixed_ref, mod_ref, ln_ref,
             dh_ref, dxb_ref, dmix_ref, st_ref):
        i = pl.program_id(0)
        dffv = dff_ref[...]
        du2 = jnp.zeros((tm, D_MODEL), F32)
        for d in range(N_DEV):
            cols = slice(d * FF_SHARD, (d + 1) * FF_SHARD)
            dhid = _dot_nt(dffv, w2_ref[cols, :])
            dh = (dhid * (2.0 * jnp.sqrt(hid_ref[:, cols].astype(F32)))).astype(BF16)
            dh_ref[:, cols] = dh
            du2 = du2 + _dot_nt(dh, w1_ref[d])
        x1v = x1_ref[...]
        dx1 = dxa_ref[...] + du2 * (1.0 + mod_ref[4:5, :])
        xh, rstd = _ln_norm(z1_ref[...])
        dz1 = _ln_bwd(dx1 * ln_ref[0:1, :], xh, rstd)
        dxb_ref[...] = ALPHA * dz1
        dmix_ref[...] = ((1.0 + mod_ref[2:3, :]) * dz1).astype(BF16)

        @pl.when(i == 0)
        def _():
            st_ref[...] = jnp.zeros_like(st_ref)

        _acc_rows(st_ref, 0, _rowsum(du2 * x1v))
        _acc_rows(st_ref, 1, _rowsum(du2))
        _acc_rows(st_ref, 2, _rowsum(dx1 * xh))
        _acc_rows(st_ref, 3, _rowsum(dx1))
        _acc_rows(st_ref, 4, _rowsum(dz1 * mixed_ref[...]))

    row = lambda w: pl.BlockSpec((tm, w), lambda i: (i, 0))
    vec = pl.BlockSpec((8, D_MODEL), lambda i: (0, 0))
    return pl.pallas_call(
        body, name="ff_bwd", grid=(s // tm,),
        in_specs=[row(D_MODEL), row(D_FF), VMEM_WHOLE, VMEM_WHOLE, row(D_MODEL), row(D_MODEL), row(D_MODEL),
                  row(D_MODEL), vec, vec],
        out_specs=[row(D_FF), row(D_MODEL), row(D_MODEL), vec],
        out_shape=[jax.ShapeDtypeStruct((s, D_FF), BF16), jax.ShapeDtypeStruct((s, D_MODEL), F32),
                   jax.ShapeDtypeStruct((s, D_MODEL), BF16), jax.ShapeDtypeStruct((8, D_MODEL), F32)],
        compiler_params=_params(("arbitrary",), VMEM_BIG),
    )(dff, hid, w1, w2, dxa, x1, z1, mixed, mod, lnv)


def _mix_bwd(dmix, w_out, o, conv, gvec, tm):
    s = o.shape[0]

    def body(dm_ref, w_ref, o_ref, c_ref, g_ref, do_ref, dc_ref, st_ref):
        i = pl.program_id(0)
        dmv = dm_ref[...]

        @pl.when(i == 0)
        def _():
            st_ref[...] = jnp.zeros_like(st_ref)

        for part, (src, dst) in enumerate(((o_ref, do_ref), (c_ref, dc_ref))):
            dr = _dot_nt(dmv, w_ref[part * ATTN_W:(part + 1) * ATTN_W, :])
            v = src[...]
            rr = lax.rsqrt(jnp.mean(v * v, axis=1, keepdims=True) + LN_EPS)
            vh = v * rr
            _acc_rows(st_ref, part, _rowsum(dr * vh))
            t = dr * g_ref[part:part + 1, :]
            dst[...] = rr * (t - vh * jnp.mean(t * vh, axis=1, keepdims=True))

    row = lambda w: pl.BlockSpec((tm, w), lambda i: (i, 0))
    vec = pl.BlockSpec((8, ATTN_W), lambda i: (0, 0))
    return pl.pallas_call(
        body, name="mix_bwd", grid=(s // tm,),
        in_specs=[row(D_MODEL), VMEM_WHOLE, row(ATTN_W), row(CONV_W), vec],
        out_specs=[row(ATTN_W), row(CONV_W), vec],
        out_shape=[jax.ShapeDtypeStruct((s, ATTN_W), F32), jax.ShapeDtypeStruct((s, CONV_W), F32),
                   jax.ShapeDtypeStruct((8, ATTN_W), F32)],
        compiler_params=_params(("arbitrary",), VMEM_MID),
    )(dmix, w_out, o, conv, gvec)


def _conv_bwd(dconv, ypre, ag, wdw, vecs, gmat, tm):
    s = ag.shape[0]
    hb = tm // HALO
    nt = s // tm
    last_halo = s // HALO - 1

    def body(dc_ref, dch_ref, yp_ref, yph_ref, cur_ref, w_ref, vec_ref, g_ref,
             dag_ref, st_ref, dw_ref, bufd, shd, dwacc):
        i = pl.program_id(0)

        def dyc_of(yp, dc):
            yh, rstd, yn = _gn_fwd(yp, vec_ref, g_ref)
            sg = _sigmoid(yn)
            dyn = dc * (sg * (1.0 + yn * (1.0 - sg)))
            dyh = dyn * vec_ref[1:2, :]
            dyc = rstd * (dyh - _dot3(dyh, g_ref) - yh * _dot3(dyh * yh, g_ref))
            return dyc, dyn, yh

        dyc, dyn, yh = dyc_of(yp_ref[...], dc_ref[...])
        dych, _, _ = dyc_of(yph_ref[...], dch_ref[...])
        bufd[0:tm, :] = dyc
        bufd[tm:tm + HALO, :] = jnp.where(i < nt - 1, dych, 0.0)

        @pl.when(i == 0)
        def _():
            st_ref[...] = jnp.zeros_like(st_ref)
            dw_ref[...] = jnp.zeros_like(dw_ref)

        _acc_rows(st_ref, 0, _rowsum(dyc))
        _acc_rows(st_ref, 1, _rowsum(dyn * yh))
        _acc_rows(st_ref, 2, _rowsum(dyn))

        _shifted_copies(bufd, shd, tm)
        dwacc[...] = jnp.zeros_like(dwacc)

        def chunk(c, carry):
            r0 = pl.multiple_of(c * CONV_CHUNK, CONV_CHUNK)
            a_c = cur_ref[pl.ds(r0, CONV_CHUNK), 0:CONV_W]
            sg_c = _sigmoid(cur_ref[pl.ds(r0, CONV_CHUNK), CONV_W:2 * CONV_W])
            ug_c = a_c * sg_c
            dug = jnp.zeros((CONV_CHUNK, CONV_W), F32)
            for j in range(CONV_K):
                win = _window(bufd, shd, CONV_K - 1 - j, CONV_CHUNK, r0)
                dug = dug + w_ref[j:j + 1, :] * win
                prod = ug_c * win
                part = prod[0:8]
                for g8 in range(1, CONV_CHUNK // 8):
                    part = part + prod[8 * g8:8 * g8 + 8]
                dwacc[j] = dwacc[j] + part
            dag_ref[pl.ds(r0, CONV_CHUNK), 0:CONV_W] = (dug * sg_c).astype(BF16)
            dag_ref[pl.ds(r0, CONV_CHUNK), CONV_W:2 * CONV_W] = (dug * a_c * sg_c * (1.0 - sg_c)).astype(BF16)
            return carry

        lax.fori_loop(0, tm // CONV_CHUNK, chunk, 0)
        for j in range(CONV_K):
            _acc_rows(dw_ref, j, _rowsum(dwacc[j]))

    row = lambda w: pl.BlockSpec((tm, w), lambda i: (i, 0))
    nxt = lambda w: pl.BlockSpec((HALO, w), lambda i: (jnp.minimum((i + 1) * hb, last_halo), 0))
    return pl.pallas_call(
        body, name="conv_bwd", grid=(nt,),
        in_specs=[row(CONV_W), nxt(CONV_W), row(CONV_W), nxt(CONV_W), row(2 * CONV_W),
                  pl.BlockSpec((32, CONV_W), lambda i: (0, 0)),
                  pl.BlockSpec((8, CONV_W), lambda i: (0, 0)),
                  pl.BlockSpec((CONV_W, CONV_W), lambda i: (0, 0))],
        out_specs=[row(2 * CONV_W), pl.BlockSpec((8, CONV_W), lambda i: (0, 0)),
                   pl.BlockSpec((32, CONV_W), lambda i: (0, 0))],
        out_shape=[jax.ShapeDtypeStruct((s, 2 * CONV_W), BF16), jax.ShapeDtypeStruct((8, CONV_W), F32),
                   jax.ShapeDtypeStruct((32, CONV_W), F32)],
        scratch_shapes=[pltpu.VMEM((tm + HALO, CONV_W), F32), pltpu.VMEM((7, tm + SHIFT_ROWS, CONV_W), F32),
                        pltpu.VMEM((32, 8, CONV_W), F32)],
        compiler_params=_params(("arbitrary",), VMEM_MID),
    )(dconv, dconv, ypre, ypre, ag, wdw, vecs, gmat)


def _attn_bwd(qkv, cum_t, o, do, lse, tq, tk, grads):
    s = qkv.shape[0]
    nq = s // tq
    r = tq // tk
    n = len(grads)

    def body(q_ref, k_ref, v_ref, cum_ref, o_ref, do_ref, lse_ref, *rest):
        dq_ref, dkt_ref, dvt_ref, dcum_ref, drow_ref = rest[n:n + 5]
        hp = pl.program_id(0)
        qi = pl.program_id(1)
        start, wait = _exchange(rest[:n], rest[n + 5:2 * n + 5], *rest[2 * n + 5:], gather=False)
        pl.when(jnp.logical_and(hp == 0, qi == 0))(start)

        @pl.when(qi == 0)
        def _():
            dkt_ref[...] = jnp.zeros_like(dkt_ref)
            dvt_ref[...] = jnp.zeros_like(dvt_ref)
            dcum_ref[...] = jnp.zeros_like(dcum_ref)

        qs = q_ref[...] * 0.125
        ov = o_ref[...]
        dob = do_ref[...].astype(BF16)
        lane = lax.broadcasted_iota(jnp.int32, (1, 128), 1)
        subl = lax.broadcasted_iota(jnp.int32, (128, 1), 0)
        hmask = (lane < 64, lane >= 64)
        aux = (64, 0)
        ones_aux = [jnp.where(lane == aux[h], 1.0, 0.0) for h in range(2)]
        qm, dom, delta, lse_h, qat, dot_t = [], [], [], [], [], []
        for h in range(2):
            qm.append(jnp.where(hmask[h], qs, jnp.zeros_like(qs)))
            dom.append(jnp.where(hmask[h], dob, jnp.zeros_like(dob)))
            delta.append(jnp.sum(dom[h].astype(F32) * ov, axis=1, keepdims=True))
            lse_h.append(lse_ref[:, h:h + 1])
            qat.append(jnp.where(hmask[h], qs.astype(F32), ones_aux[h]).T.astype(BF16))
            dot_t.append(dom[h].astype(F32).T.astype(BF16))

        def step(kb, carry, r0=None):
            lo = 0 if r0 is None else r0
            k0 = pl.multiple_of(kb * tk, tk)
            k2 = k_ref[pl.ds(k0, tk), :]
            v2 = v_ref[pl.ds(k0, tk), :]
            dq_new, a, b = [], [], []
            for h in range(2):
                cs = cum_ref[pl.ds(2 * hp + h, 1), pl.ds(k0, tk)]
                sc = _dot_nt(qm[h][lo:], k2) - cs
                p = jnp.exp(sc - lse_h[h][lo:])
                if r0 is not None:
                    visible = (lax.broadcasted_iota(jnp.int32, (tq - lo, tk), 1)
                               <= lax.broadcasted_iota(jnp.int32, (tq - lo, tk), 0))
                    p = jnp.where(visible, p, 0.0)
                dp = _dot_nt(dom[h][lo:], v2)
                ds = p * (dp - delta[h][lo:])
                pb = p.astype(BF16)
                dsb = ds.astype(BF16)
                a.append(_dot(qat[h][:, lo:], dsb))
                b.append(_dot(dot_t[h][:, lo:], pb))
                ka = jnp.where(hmask[h], k2, ones_aux[h].astype(k2.dtype))
                dq_h = carry[h][lo:] + _dot(dsb, ka)
                dq_new.append(jnp.concatenate([carry[h][:lo], dq_h], axis=0) if lo else dq_h)
            dkt_ref[:, pl.ds(k0, tk)] = dkt_ref[:, pl.ds(k0, tk)] + jnp.where(subl < 64, a[0], a[1])
            dvt_ref[:, pl.ds(k0, tk)] = dvt_ref[:, pl.ds(k0, tk)] + (b[0] + b[1])
            for h in range(2):
                dcum_ref[pl.ds(h, 1), pl.ds(k0, tk)] = (dcum_ref[pl.ds(h, 1), pl.ds(k0, tk)]
                                                        - a[h][aux[h]:aux[h] + 1, :])
            return tuple(dq_new)

        init = (jnp.zeros((tq, 128), F32),) * 2
        carry = lax.fori_loop(0, qi * r, step, init)
        for d in range(r):
            carry = step(qi * r + d, carry, r0=d * tk)
        dq0, dq1 = carry
        dq_ref[...] = (jnp.where(hmask[0], dq0, dq1) * 0.125).astype(BF16)
        drow_ref[:, 0:1] = dq0[:, aux[0]:aux[0] + 1]
        drow_ref[:, 1:2] = dq1[:, aux[1]:aux[1] + 1]
        pl.when(jnp.logical_and(hp == 3, qi == nq - 1))(wait)

    return pl.pallas_call(
        body, name="attn_bwd", grid=(4, nq),
        in_specs=[pl.BlockSpec((tq, 128), lambda h, i: (i, h)),
                  pl.BlockSpec((s, 128), lambda h, i: (0, 4 + h)),
                  pl.BlockSpec((s, 128), lambda h, i: (0, 8 + h)),
                  pl.BlockSpec((8, s), lambda h, i: (0, 0)),
                  pl.BlockSpec((tq, 128), lambda h, i: (i, h)),
                  pl.BlockSpec((tq, 128), lambda h, i: (i, h)),
                  pl.BlockSpec((None, tq, 2), lambda h, i: (h, i, 0))] + [ANY_SPACE] * n,
        out_specs=[pl.BlockSpec((tq, 128), lambda h, i: (i, h)),
                   pl.BlockSpec((128, s), lambda h, i: (h, 0)),
                   pl.BlockSpec((128, s), lambda h, i: (h, 0)),
                   pl.BlockSpec((None, 8, s), lambda h, i: (h, 0, 0)),
                   pl.BlockSpec((None, tq, 2), lambda h, i: (h, i, 0))] + [ANY_SPACE] * n,
        out_shape=[jax.ShapeDtypeStruct((s, ATTN_W), BF16), jax.ShapeDtypeStruct((ATTN_W, s), F32),
                   jax.ShapeDtypeStruct((ATTN_W, s), F32), jax.ShapeDtypeStruct((4, 8, s), F32),
                   jax.ShapeDtypeStruct((4, s, 2), F32)]
        + [jax.ShapeDtypeStruct(g.shape, g.dtype) for g in grads],
        scratch_shapes=_exchange_scratch(n),
        compiler_params=_params(("arbitrary", "arbitrary"), VMEM_BIG),
    )(qkv, qkv, qkv, cum_t, o, do, lse, *grads)


def _fgate_bwd(dcum_t, drow_t, sneg_t):
    s = dcum_t.shape[1]

    def body(dc_ref, dr_ref, sn_ref, df_ref, db_ref):
        lane = lax.broadcasted_iota(jnp.int32, (8, s), 1)
        acc = dc_ref[...] + dr_ref[...]
        sh = 1
        while sh < s:
            acc = acc + jnp.where(lane < s - sh, pltpu.roll(acc, s - sh, axis=1), 0.0)
            sh *= 2
        df = acc * sn_ref[...]
        df_ref[...] = df
        db_ref[...] = jnp.broadcast_to(jnp.sum(df, axis=1, keepdims=True), (8, 128))

    return pl.pallas_call(
        body, name="fgate_bwd",
        out_shape=[jax.ShapeDtypeStruct((8, s), F32), jax.ShapeDtypeStruct((8, 128), F32)],
        in_specs=[VMEM_WHOLE] * 3, out_specs=[VMEM_WHOLE] * 2,
    )(dcum_t, drow_t, sneg_t)


def _in_proj_bwd(dq, dkt, dvt, dag, dft, wqkv, wkvt, wag, wft, x, dxb, mod, tm):
    s = x.shape[0]

    def body(dq_ref, dkt_ref, dvt_ref, dag_ref, dft_ref, wqkv_ref, wkvt_ref, wag_ref, wft_ref, x_ref, dxb_ref,
             mod_ref, dkvt_ref, dx_ref, st_ref):
        i = pl.program_id(0)
        dkb = dkt_ref[...].astype(BF16)
        dvb = dvt_ref[...].astype(BF16)
        dkvt_ref[0:ATTN_W, :] = dkb
        dkvt_ref[ATTN_W:2 * ATTN_W, :] = dvb
        du = _dot_nt(dq_ref[...], wqkv_ref[:, 0:ATTN_W])
        du = du + _dot_tn(dkb, wkvt_ref[0:ATTN_W, :])
        du = du + _dot_tn(dvb, wkvt_ref[ATTN_W:2 * ATTN_W, :])
        du = du + _dot_nt(dag_ref[...], wag_ref[...])
        du = du + _dot_tn(dft_ref[...].astype(BF16), wft_ref[...])
        xv = x_ref[...]
        dx_ref[...] = dxb_ref[...] + du * (1.0 + mod_ref[1:2, :])

        @pl.when(i == 0)
        def _():
            st_ref[...] = jnp.zeros_like(st_ref)

        _acc_rows(st_ref, 0, _rowsum(du * xv))
        _acc_rows(st_ref, 1, _rowsum(du))

    row = lambda w: pl.BlockSpec((tm, w), lambda i: (i, 0))
    col = lambda h: pl.BlockSpec((h, tm), lambda i: (0, i))
    vec = pl.BlockSpec((8, D_MODEL), lambda i: (0, 0))
    return pl.pallas_call(
        body, name="in_proj_bwd", grid=(s // tm,),
        in_specs=[row(ATTN_W), col(ATTN_W), col(ATTN_W), row(2 * CONV_W), col(8),
                  VMEM_WHOLE, VMEM_WHOLE, VMEM_WHOLE, VMEM_WHOLE, row(D_MODEL), row(D_MODEL), vec],
        out_specs=[col(2 * ATTN_W), row(D_MODEL), vec],
        out_shape=[jax.ShapeDtypeStruct((2 * ATTN_W, s), BF16), jax.ShapeDtypeStruct((s, D_MODEL), F32),
                   jax.ShapeDtypeStruct((8, D_MODEL), F32)],
        compiler_params=_params(("arbitrary",), VMEM_MID),
    )(dq, dkt, dvt, dag, dft, wqkv, wkvt, wag, wft, x, dxb, mod)


def _wgrad_nn(a, b, name, tm, tn, tk):
    m, s = a.shape
    n = b.shape[1]
    nk = s // tk

    def body(a_ref, b_ref, o_ref, acc_ref):
        k = pl.program_id(2)

        @pl.when(k == 0)
        def _():
            acc_ref[...] = jnp.zeros_like(acc_ref)

        acc_ref[...] += _dot(a_ref[...], b_ref[...])

        @pl.when(k == nk - 1)
        def _():
            o_ref[...] = acc_ref[...].astype(o_ref.dtype)

    return pl.pallas_call(
        body, name=name, grid=(m // tm, n // tn, nk),
        in_specs=[pl.BlockSpec((tm, tk), lambda i, j, k: (i, k)), pl.BlockSpec((tk, tn), lambda i, j, k: (k, j))],
        out_specs=pl.BlockSpec((tm, tn), lambda i, j, k: (i, j)),
        out_shape=jax.ShapeDtypeStruct((m, n), BF16),
        scratch_shapes=[pltpu.VMEM((tm, tn), F32)],
        compiler_params=_params(("parallel", "parallel", "arbitrary"), VMEM_MID),
    )(a, b)


def _wgrad_tn(a, b, name, tm, tn, tk, out_shape, out_spec):
    s, m = a.shape
    n = b.shape[1]
    nk = s // tk

    def body(a_ref, b_ref, o_ref, acc_ref):
        k = pl.program_id(2)

        @pl.when(k == 0)
        def _():
            acc_ref[...] = jnp.zeros_like(acc_ref)

        acc_ref[...] += _dot_tn(a_ref[...], b_ref[...])

        @pl.when(k == nk - 1)
        def _():
            o_ref[...] = acc_ref[...].astype(o_ref.dtype)

    return pl.pallas_call(
        body, name=name, grid=(m // tm, n // tn, nk),
        in_specs=[pl.BlockSpec((tk, tm), lambda i, j, k: (k, i)), pl.BlockSpec((tk, tn), lambda i, j, k: (k, j))],
        out_specs=out_spec, out_shape=out_shape,
        scratch_shapes=[pltpu.VMEM((tm, tn), F32)],
        compiler_params=_params(("parallel", "parallel", "arbitrary"), VMEM_MID),
    )(a, b)


def _wgrad_f(dft, u, tk):
    s = u.shape[0]
    nk = s // tk

    def body(d_ref, u_ref, o_ref, acc_ref):
        k = pl.program_id(0)

        @pl.when(k == 0)
        def _():
            acc_ref[...] = jnp.zeros_like(acc_ref)

        acc_ref[...] += _dot(d_ref[...].astype(BF16), u_ref[...])

        @pl.when(k == nk - 1)
        def _():
            o_ref[...] = acc_ref[...].astype(BF16)

    return pl.pallas_call(
        body, name="wgrad_f", grid=(nk,),
        in_specs=[pl.BlockSpec((8, tk), lambda k: (0, k)), pl.BlockSpec((tk, D_MODEL), lambda k: (k, 0))],
        out_specs=pl.BlockSpec((8, D_MODEL), lambda k: (0, 0)),
        out_shape=jax.ShapeDtypeStruct((8, D_MODEL), BF16),
        scratch_shapes=[pltpu.VMEM((8, D_MODEL), F32)],
        compiler_params=_params(("arbitrary",)),
    )(dft, u)


def _adamw_big(recv, w, m, v, name, tr):
    r, cdim = w.shape

    def body(g_ref, w_ref, m_ref, v_ref, go_ref, d_ref, mo_ref, vo_ref):
        g = g_ref[0].astype(F32)
        for k in range(1, N_DEV):
            g = g + g_ref[k].astype(F32)
        delta, m2, v2 = _adamw(w_ref[...], g, m_ref[...], v_ref[...])
        go_ref[...] = g
        d_ref[...] = delta
        mo_ref[...] = m2
        vo_ref[...] = v2

    blk = pl.BlockSpec((tr, cdim), lambda i: (i, 0))
    return pl.pallas_call(
        body, name=name, grid=(r // tr,),
        in_specs=[pl.BlockSpec((N_DEV, tr, cdim), lambda i: (0, i, 0)), blk, blk, blk],
        out_specs=[blk] * 4, out_shape=[jax.ShapeDtypeStruct((r, cdim), F32)] * 4,
        compiler_params=_params(("parallel",), VMEM_MID),
    )(recv, w, m, v)


def _adamw_ada(c_all, dada, w, m, v, tr):
    r, cdim = w.shape

    def body(c_ref, d_ref, w_ref, m_ref, v_ref, go_ref, dl_ref, mo_ref, vo_ref):
        cv = c_ref[...]
        sc = (cv * _sigmoid(cv)).astype(BF16)
        g = _dot_tn(sc, d_ref[...].astype(BF16))
        delta, m2, v2 = _adamw(w_ref[...], g, m_ref[...], v_ref[...])
        go_ref[...] = g
        dl_ref[...] = delta
        mo_ref[...] = m2
        vo_ref[...] = v2

    blk = pl.BlockSpec((tr, cdim), lambda i: (i, 0))
    return pl.pallas_call(
        body, name="adamw_ada", grid=(r // tr,),
        in_specs=[pl.BlockSpec((N_DEV, tr), lambda i: (0, i)), pl.BlockSpec((N_DEV, cdim), lambda i: (0, 0)),
                  blk, blk, blk],
        out_specs=[blk] * 4, out_shape=[jax.ShapeDtypeStruct((r, cdim), F32)] * 4,
        compiler_params=_params(("parallel",), VMEM_MID),
    )(c_all, dada, w, m, v)


def _adamw_small(parts, w, m, v):
    n = w.shape[1]

    def body(g_ref, w_ref, m_ref, v_ref, go_ref, d_ref, mo_ref, vo_ref):
        g = g_ref[0]
        for k in range(1, N_DEV):
            g = g + g_ref[k]
        delta, m2, v2 = _adamw(w_ref[...], g, m_ref[...], v_ref[...])
        go_ref[...] = g
        d_ref[...] = delta
        mo_ref[...] = m2
        vo_ref[...] = v2

    return pl.pallas_call(
        body, name="adamw_small", out_shape=[jax.ShapeDtypeStruct((1, n), F32)] * 4,
        in_specs=[VMEM_WHOLE] * 4, out_specs=[VMEM_WHOLE] * 4,
    )(parts, w, m, v)


def _pad_cols(a, n):
    return jnp.pad(a, ((0, 0), (0, n - a.shape[1])))


def _pad_rows(a, n):
    return jnp.pad(a, ((0, n - a.shape[0]), (0, 0)))


def kernel(x, c, w_ada, b_ada, w_in, b_forget, w_dw, b_dw, gn_g, gn_b, g_attn_out, g_conv_out, w_out, ln1_g, ln1_b, w_ff1, w_ff2, ln2_g, ln2_b, loss_target, m_w_ada, m_b_ada, m_w_in, m_b_forget, m_w_dw, m_b_dw, m_gn_g, m_gn_b, m_g_attn_out, m_g_conv_out, m_w_out, m_ln1_g, m_ln1_b, m_w_ff1, m_w_ff2, m_ln2_g, m_ln2_b, v_w_ada, v_b_ada, v_w_in, v_b_forget, v_w_dw, v_b_dw, v_gn_g, v_gn_b, v_g_attn_out, v_g_conv_out, v_w_out, v_ln1_g, v_ln1_b, v_w_ff1, v_w_ff2, v_ln2_g, v_ln2_b):
    s = x.shape[1]
    tm = min(ROW_TILE, s)
    tk_att = min(ATT_TILE, s)
    tq_att = min(ATT_Q_BLOCKS * tk_att, s)
    me = 4 * lax.axis_index("x") + 2 * lax.axis_index("y") + lax.axis_index("c")
    xs = x[0]
    tgt = loss_target[0]

    dw_n = CONV_K * DW_SHARD
    pack1 = jnp.concatenate([c, w_dw[0].reshape(1, dw_n), jnp.zeros((1, 2048 - dw_n), F32)], axis=1)
    gw_in, g1 = _gather_weights([_pad_cols(w_in[0], IN_SHARD_PAD).astype(BF16)], pack1)
    g1 = g1[:, 0, :]
    c_all = g1[:, :D_MODEL]
    wdw_full = g1[:, D_MODEL:D_MODEL + dw_n].reshape(N_DEV, CONV_K, DW_SHARD)
    wdw_full = _pad_rows(wdw_full.transpose(1, 0, 2).reshape(CONV_K, CONV_W), 32)

    b_shard = lax.dynamic_slice(b_ada, (0, me * ADA_SHARD), (1, ADA_SHARD))
    ada_all = _gather_small(_ada_part(c_all, w_ada[0], b_shard), "gather_ada")
    ada = lax.dynamic_index_in_dim(ada_all, me, axis=1, keepdims=False).reshape(6, D_MODEL)
    mod = _pad_rows(ada, 8)

    w_in_full = gw_in[:, :, :IN_SHARD].transpose(1, 0, 2).reshape(D_MODEL, N_IN)
    wqkv = w_in_full[:, :3 * ATTN_W]
    wft = w_in_full[:, 3 * ATTN_W:3 * ATTN_W + N_HEADS].T
    wag = w_in_full[:, 3 * ATTN_W + N_HEADS:]

    qkv, ag, ft, u = _in_proj(xs, mod, wqkv, wag, wft, tm)
    cum_t, sneg_t = _fgate(ft, b_forget.reshape(N_HEADS, 1))
    o, lse, gw_out, gw_ff1, gw_ff2 = _attn_fwd(
        qkv, cum_t, tq_att, tk_att, [w_out[0].astype(BF16), w_ff1[0].astype(BF16), w_ff2[0].astype(BF16)])
    w_out_full = gw_out.reshape(D_MODEL, D_MODEL)
    w2_full = gw_ff2.reshape(D_FF, D_MODEL)
    cvec = _pad_rows(jnp.concatenate([b_dw, gn_g, gn_b], axis=0), 8)
    grp = jnp.arange(CONV_W) // 64
    gmat = jnp.where(grp[:, None] == grp[None, :], 1.0 / 64.0, 0.0).astype(BF16)
    ypre, conv = _conv_fwd(ag, wdw_full, cvec, gmat, tm)
    gvec = _pad_rows(jnp.concatenate([g_attn_out, g_conv_out], axis=0), 8)
    lnv = _pad_rows(jnp.concatenate([ln1_g, ln1_b, ln2_g, ln2_b], axis=0), 8)
    cat, mixed, z1, x1 = _mix_fwd(o, conv, xs, w_out_full, gvec, mod, lnv, tm)
    tf = min(FF_TILE, s)
    u2, hid, dff, dxa, st_ff = _ff_fwd(x1, tgt, gw_ff1, w2_full, mod, lnv, min(2 * FF_TILE, s))

    dh, dxb, dmix, st_fb = _ff_bwd(dff, hid, gw_ff1, w2_full, dxa, x1, z1, mixed, mod, lnv, tf)
    do, dconv, st_mix = _mix_bwd(dmix, w_out_full, o, conv, gvec, tm)
    dag, st_conv, dwdw = _conv_bwd(dconv, ypre, ag, wdw_full, cvec, gmat, tm)

    tk = min(WGRAD_K_TILE, s)
    g_ff2 = _wgrad_tn(hid, dff, "wgrad_ff2", 512, 1024, tk,
                      jax.ShapeDtypeStruct((D_FF, D_MODEL), BF16),
                      pl.BlockSpec((512, 1024), lambda i, j, k: (i, j))).reshape(N_DEV, FF_SHARD, D_MODEL)
    g_ff1 = _wgrad_tn(u2, dh, "wgrad_ff1", 1024, 512, tk,
                      jax.ShapeDtypeStruct((N_DEV, D_MODEL, FF_SHARD), BF16),
                      pl.BlockSpec((None, 1024, 512), lambda i, j, k: (j, i, 0)))
    g_out = _wgrad_tn(cat, dmix, "wgrad_out", 512, 1024, tk,
                      jax.ShapeDtypeStruct((D_MODEL, D_MODEL), BF16),
                      pl.BlockSpec((512, 1024), lambda i, j, k: (i, j))).reshape(N_DEV, OUT_SHARD, D_MODEL)
    dq, dkt, dvt, dcum, drow, r_out, r_ff1, r_ff2 = _attn_bwd(
        qkv, cum_t, o, do, lse, tq_att, tk_att, [g_out, g_ff1, g_ff2])
    dft, dbf = _fgate_bwd(dcum[:, :2, :].reshape(N_HEADS, s), drow.transpose(0, 2, 1).reshape(N_HEADS, s), sneg_t)
    wkvt = wqkv[:, ATTN_W:3 * ATTN_W].T
    dkvt, grad_x, st_in = _in_proj_bwd(dq, dkt, dvt, dag, dft, wqkv, wkvt, wag, wft, xs, dxb, mod, tm)

    g_q = _wgrad_tn(u, dq, "wgrad_q", 1024, 512, tk,
                    jax.ShapeDtypeStruct((D_MODEL, ATTN_W), BF16),
                    pl.BlockSpec((1024, 512), lambda i, j, k: (i, j)))
    g_kvt = _wgrad_nn(dkvt, u, "wgrad_kv", 512, 1024, tk)
    g_qkv = jnp.concatenate([g_q, g_kvt.T], axis=1)
    g_ag = _wgrad_tn(u, dag, "wgrad_ag", 1024, 512, tk,
                     jax.ShapeDtypeStruct((D_MODEL, 2 * CONV_W), BF16),
                     pl.BlockSpec((1024, 512), lambda i, j, k: (i, j)))
    g_ft = _wgrad_f(dft, u, tk)
    g_in = jnp.concatenate([g_qkv, g_ft.T, g_ag], axis=1)
    g_in = jnp.pad(g_in.reshape(D_MODEL, N_DEV, IN_SHARD).transpose(1, 0, 2),
                   ((0, 0), (0, 0), (0, IN_SHARD_PAD - IN_SHARD)))

    dada = jnp.concatenate([st_in[1:2], st_in[0:1], st_fb[4:5], st_fb[1:2], st_fb[0:1], st_ff[2:3]], axis=1)
    loss_part = st_ff[3:4, 0:1]
    pack2 = jnp.concatenate([
        dada,
        _pad_cols(dbf[:, 0].reshape(1, N_HEADS), 128),
        dwdw[:CONV_K].reshape(1, CONV_K * CONV_W),
        st_conv[0:1], st_conv[1:2], st_conv[2:3], st_mix[0:1], st_mix[1:2],
        st_fb[2:3], st_fb[3:4], st_ff[0:1], st_ff[1:2],
        _pad_cols(loss_part, 128)], axis=1)
    r_in, g2 = _scatter_grads([g_in], pack2)
    g2 = g2[:, 0, :]
    o_bf = 6 * D_MODEL
    o_dw = o_bf + 128
    o_v = o_dw + CONV_K * CONV_W
    o_ln = o_v + 5 * CONV_W
    o_loss = o_ln + 4 * D_MODEL
    dw_parts = lax.dynamic_slice_in_dim(
        g2[:, o_dw:o_v].reshape(N_DEV, CONV_K, N_DEV, DW_SHARD), me, 1, axis=2).reshape(N_DEV, dw_n)
    parts = jnp.concatenate([
        g2[:, :o_dw], _pad_cols(dw_parts, 2048), g2[:, o_v:o_loss + 128]], axis=1)[:, None, :]

    def pack_w(b_ada_, b_forget_, w_dw_, smalls):
        return jnp.concatenate([b_ada_, _pad_cols(b_forget_, 128), _pad_cols(w_dw_[0].reshape(1, dw_n), 2048)]
                               + smalls + [jnp.zeros((1, 128), F32)], axis=1)

    pw = pack_w(b_ada, b_forget, w_dw, [b_dw, gn_g, gn_b, g_attn_out, g_conv_out, ln1_g, ln1_b, ln2_g, ln2_b])
    pm = pack_w(m_b_ada, m_b_forget, m_w_dw, [m_b_dw, m_gn_g, m_gn_b, m_g_attn_out, m_g_conv_out,
                                              m_ln1_g, m_ln1_b, m_ln2_g, m_ln2_b])
    pv = pack_w(v_b_ada, v_b_forget, v_w_dw, [v_b_dw, v_gn_g, v_gn_b, v_g_attn_out, v_g_conv_out,
                                              v_ln1_g, v_ln1_b, v_ln2_g, v_ln2_b])
    small = _adamw_small(parts, pw, pm, pv)

    p_bf = 6 * D_MODEL
    p_dw = p_bf + 128
    p_v = p_dw + 2048
    p_ln = p_v + 5 * CONV_W
    p_loss = p_ln + 4 * D_MODEL

    def unpack(t):
        outs = {"b_ada": t[:, :p_bf], "b_forget": t[:, p_bf:p_bf + N_HEADS],
                "w_dw": t[:, p_dw:p_dw + dw_n].reshape(1, CONV_K, 1, DW_SHARD)}
        for k, nm in enumerate(["b_dw", "gn_g", "gn_b", "g_attn_out", "g_conv_out"]):
            outs[nm] = t[:, p_v + k * CONV_W:p_v + (k + 1) * CONV_W]
        for k, nm in enumerate(["ln1_g", "ln1_b", "ln2_g", "ln2_b"]):
            outs[nm] = t[:, p_ln + k * D_MODEL:p_ln + (k + 1) * D_MODEL]
        return outs

    sm = [unpack(t) for t in small]
    loss = small[0][0, p_loss]

    dada_all = g2[:, :6 * D_MODEL]
    dada_shard = lax.dynamic_slice_in_dim(dada_all, me * ADA_SHARD, ADA_SHARD, axis=1)
    big = {
        "w_ada": _adamw_ada(c_all, dada_shard, w_ada[0], m_w_ada[0], v_w_ada[0], 256),
        "w_in": [t[:, :IN_SHARD] for t in _adamw_big(
            r_in, _pad_cols(w_in[0], IN_SHARD_PAD), _pad_cols(m_w_in[0], IN_SHARD_PAD),
            _pad_cols(v_w_in[0], IN_SHARD_PAD), "adamw_in", 256)],
        "w_out": _adamw_big(r_out, w_out[0], m_w_out[0], v_w_out[0], "adamw_out", 128),
        "w_ff1": _adamw_big(r_ff1, w_ff1[0], m_w_ff1[0], v_w_ff1[0], "adamw_ff1", 256),
        "w_ff2": _adamw_big(r_ff2, w_ff2[0], m_w_ff2[0], v_w_ff2[0], "adamw_ff2", 256),
    }

    names = ["w_ada", "b_ada", "w_in", "b_forget", "w_dw", "b_dw", "gn_g", "gn_b", "g_attn_out", "g_conv_out",
             "w_out", "ln1_g", "ln1_b", "w_ff1", "w_ff2", "ln2_g", "ln2_b"]

    def leaf(kind, nm):
        if nm in big:
            return big[nm][kind][None]
        return sm[kind][nm]

    outs = [loss, grad_x[None]]
    for kind in range(4):
        outs += [leaf(kind, nm) for nm in names]
    return tuple(outs)
```

```python
import jax
import jax.numpy as jnp
from jax import lax
from jax.experimental import pallas as pl
from jax.experimental.pallas import tpu as pltpu

F32 = jnp.float32
BF16 = jnp.bfloat16

D_MODEL = 1024
ATTN_W = 512
CONV_W = 512
N_HEADS = 8
CONV_K = 31
D_FF = 4096
N_DEV = 8
N_IN = 3 * ATTN_W + N_HEADS + 2 * CONV_W
IN_SHARD = N_IN // N_DEV
IN_SHARD_PAD = 384
ADA_SHARD = 6 * D_MODEL // N_DEV
FF_SHARD = D_FF // N_DEV
OUT_SHARD = D_MODEL // N_DEV
DW_SHARD = CONV_W // N_DEV
HALO = 32
LN_EPS = 1e-5
ALPHA = 2.0 ** 0.25
ADAM_LR, ADAM_B1, ADAM_B2, ADAM_EPS, ADAM_WD, ADAM_STEP = 0.001, 0.9, 0.999, 1e-08, 0.01, 10

ROW_TILE = 512
FF_TILE = 256
WGRAD_K_TILE = 2048
ATT_TILE = 512
ATT_Q_BLOCKS = 2
VMEM_BIG = 56 * 1024 * 1024
VMEM_MID = 40 * 1024 * 1024

MESH = pl.DeviceIdType.MESH
VMEM_WHOLE = pl.BlockSpec(memory_space=pltpu.VMEM)
ANY_SPACE = pl.BlockSpec(memory_space=pl.ANY)


def _dot(a, b):
    return jnp.dot(a, b, preferred_element_type=F32)


def _dot_nt(a, b):
    return lax.dot_general(a, b, (((1,), (1,)), ((), ())), preferred_element_type=F32)


def _dot_tn(a, b):
    return lax.dot_general(a, b, (((0,), (0,)), ((), ())), preferred_element_type=F32)


def _dot3(xf, g_ref):
    g = g_ref[...]
    hi = xf.astype(BF16)
    lo = (xf - hi.astype(F32)).astype(BF16)
    return _dot(hi, g) + _dot(lo, g)


def _sigmoid(x):
    return 1.0 / (1.0 + jnp.exp(-x))


def _rowsum(x):
    return jnp.sum(x, axis=0, keepdims=True)


def _ln_norm(z):
    mu = jnp.mean(z, axis=1, keepdims=True)
    zc = z - mu
    var = jnp.mean(zc * zc, axis=1, keepdims=True)
    rstd = lax.rsqrt(var + LN_EPS)
    return zc * rstd, rstd


def _ln_bwd(dxh, xh, rstd):
    m1 = jnp.mean(dxh, axis=1, keepdims=True)
    m2 = jnp.mean(dxh * xh, axis=1, keepdims=True)
    return rstd * (dxh - m1 - xh * m2)


def _adamw(w, g, m, v):
    m2 = ADAM_B1 * m + (1.0 - ADAM_B1) * g
    v2 = ADAM_B2 * v + (1.0 - ADAM_B2) * (g * g)
    m_hat = m2 / (1.0 - ADAM_B1 ** ADAM_STEP)
    v_hat = v2 / (1.0 - ADAM_B2 ** ADAM_STEP)
    delta = -ADAM_LR * (m_hat / (jnp.sqrt(v_hat) + ADAM_EPS) + ADAM_WD * w)
    return delta, m2, v2


def _params(sem=None, vmem=None):
    kw = {}
    if sem is not None:
        kw["dimension_semantics"] = sem
    if vmem is not None:
        kw["vmem_limit_bytes"] = vmem
    return pltpu.CompilerParams(**kw)


def _coords():
    return lax.axis_index("x"), lax.axis_index("y"), lax.axis_index("c")


def _flip(v, bit):
    return 1 - v if bit else v


def _acc_rows(ref, r, val):
    ref[r:r + 1, :] = ref[r:r + 1, :] + val


def _small_gather(v_ref, out_ref, send_sems, recv_sems, local_sem):
    x, y, c = _coords()
    me = 4 * x + 2 * y + c
    mine = pltpu.make_async_copy(v_ref, out_ref.at[me], local_sem)
    sends, recvs = [], []
    for k in range(1, N_DEV):
        px, py, pc = _flip(x, (k >> 2) & 1), _flip(y, (k >> 1) & 1), _flip(c, k & 1)
        peer = 4 * px + 2 * py + pc
        sems = dict(send_sem=send_sems.at[k - 1], recv_sem=recv_sems.at[k - 1],
                    device_id=(px, py, pc), device_id_type=MESH)
        sends.append(pltpu.make_async_remote_copy(src_ref=v_ref, dst_ref=out_ref.at[me], **sems))
        recvs.append(pltpu.make_async_remote_copy(src_ref=v_ref, dst_ref=out_ref.at[peer], **sems))

    def start():
        mine.start()
        for cp in sends:
            cp.start()

    def wait():
        for cp in recvs:
            cp.wait_recv()
        for cp in sends:
            cp.wait_send()
        mine.wait()

    return start, wait


SMALL_GATHER_SCRATCH = [pltpu.SemaphoreType.DMA((N_DEV - 1,)), pltpu.SemaphoreType.DMA((N_DEV - 1,)),
                        pltpu.SemaphoreType.DMA]


def _gather_small(v, name):
    r, n = v.shape

    def body(v_ref, out_ref, send_sems, recv_sems, local_sem):
        start, wait = _small_gather(v_ref, out_ref, send_sems, recv_sems, local_sem)
        start()
        wait()

    return pl.pallas_call(
        body, name=name,
        out_shape=jax.ShapeDtypeStruct((N_DEV, r, n), v.dtype),
        in_specs=[VMEM_WHOLE], out_specs=VMEM_WHOLE,
        scratch_shapes=SMALL_GATHER_SCRATCH,
    )(v)


def _gather_weights(shards, small):
    n = len(shards)

    def body(*refs):
        ins, small_ref = refs[:n], refs[n]
        outs, small_out = refs[n + 1:2 * n + 1], refs[2 * n + 1]
        send_sems, recv_sems, local_sems = refs[2 * n + 2:2 * n + 5]
        start_small, wait_small = _small_gather(small_ref, small_out, *refs[2 * n + 5:])
        start_small()
        x, y, c = _coords()
        me, sibling = (x, y, c), (x, y, 1 - c)
        chips = [(1 - x, y), (x, 1 - y), (1 - x, 1 - y)]

        def slot(a, pos):
            return outs[a].at[4 * pos[0] + 2 * pos[1] + pos[2]]

        def copy(a, k, block, to, src=None):
            return pltpu.make_async_remote_copy(
                src_ref=slot(a, block) if src is None else src, dst_ref=slot(a, block),
                send_sem=send_sems.at[7 * a + k], recv_sem=recv_sems.at[7 * a + k],
                device_id=to, device_id_type=MESH)

        started = []
        local = []
        for a in range(n):
            mine = pltpu.make_async_copy(ins[a], slot(a, me), local_sems.at[a])
            mine.start()
            local.append(mine)
        for j, chip in enumerate(chips):
            for a in range(n):
                cp = copy(a, 1 + j, me, (*chip, c), src=ins[a])
                cp.start()
                started.append(cp)
        for a in range(n):
            cp = copy(a, 0, me, sibling, src=ins[a])
            cp.start()
            started.append(cp)
        for j, chip in enumerate(chips):
            for a in range(n):
                copy(a, 1 + j, (*chip, c), me).wait_recv()
                cp = copy(a, 4 + j, (*chip, c), sibling)
                cp.start()
                started.append(cp)
        for a in range(n):
            copy(a, 0, sibling, me).wait_recv()
        for j, chip in enumerate(chips):
            for a in range(n):
                copy(a, 4 + j, (*chip, 1 - c), me).wait_recv()
        for cp in started:
            cp.wait_send()
        for mine in local:
            mine.wait()
        wait_small()

    return pl.pallas_call(
        body, name="gather_weights",
        out_shape=[jax.ShapeDtypeStruct((N_DEV,) + s.shape, s.dtype) for s in shards]
        + [jax.ShapeDtypeStruct((N_DEV,) + small.shape, small.dtype)],
        in_specs=[ANY_SPACE] * n + [VMEM_WHOLE], out_specs=[ANY_SPACE] * n + [VMEM_WHOLE],
        scratch_shapes=[pltpu.SemaphoreType.DMA((7 * n,)), pltpu.SemaphoreType.DMA((7 * n,)),
                        pltpu.SemaphoreType.DMA((n,))] + SMALL_GATHER_SCRATCH,
    )(*shards, small)


def _exchange(ins, outs, send_sems, recv_sems, local_sems, gather):
    n = len(ins)
    x, y, c = _coords()
    me = 4 * x + 2 * y + c
    local = [pltpu.make_async_copy(ins[a] if gather else ins[a].at[me], outs[a].at[me], local_sems.at[a])
             for a in range(n)]
    sends, recvs = [], []
    for k in range(1, N_DEV):
        px, py, pc = _flip(x, (k >> 2) & 1), _flip(y, (k >> 1) & 1), _flip(c, k & 1)
        peer = 4 * px + 2 * py + pc
        for a in range(n):
            src = ins[a] if gather else ins[a].at[peer]
            sems = dict(send_sem=send_sems.at[7 * a + k - 1], recv_sem=recv_sems.at[7 * a + k - 1],
                        device_id=(px, py, pc), device_id_type=MESH)
            sends.append(pltpu.make_async_remote_copy(src_ref=src, dst_ref=outs[a].at[me], **sems))
            recvs.append(pltpu.make_async_remote_copy(src_ref=src, dst_ref=outs[a].at[peer], **sems))

    def start():
        for cp in local + sends:
            cp.start()

    def wait():
        for cp in recvs:
            cp.wait_recv()
        for cp in sends:
            cp.wait_send()
        for cp in local:
            cp.wait()

    return start, wait


def _exchange_scratch(n):
    return [pltpu.SemaphoreType.DMA((7 * n,)), pltpu.SemaphoreType.DMA((7 * n,)), pltpu.SemaphoreType.DMA((n,))]


def _scatter_grads(g, small):
    _, rows, cols = g.shape
    halves = 4

    def body(g_ref, small_ref, out_ref, small_out, sib_buf, own_buf, ps_buf,
             d2d_send, d2d_recv, ici_send, ici_recv, load_sems, store_sem, *small_sems):
        start_small, wait_small = _small_gather(small_ref, small_out, *small_sems)
        start_small()
        x, y, c = _coords()
        sibling = (x, y, 1 - c)
        chips = [(x, y), (1 - x, y), (x, 1 - y), (1 - x, 1 - y)]
        swaps, loads = [], []
        for j, (px, py) in enumerate(chips):
            cp = pltpu.make_async_remote_copy(
                src_ref=g_ref.at[4 * px + 2 * py + (1 - c)], dst_ref=sib_buf.at[j],
                send_sem=d2d_send.at[j], recv_sem=d2d_recv.at[j], device_id=sibling, device_id_type=MESH)
            cp.start()
            swaps.append(cp)
            ld = pltpu.make_async_copy(g_ref.at[4 * px + 2 * py + c], own_buf.at[j], load_sems.at[j])
            ld.start()
            loads.append(ld)
        for j in range(halves):
            swaps[j].wait_recv()
            loads[j].wait()
            ps_buf[j] = (own_buf[j].astype(F32) + sib_buf[j].astype(F32)).astype(ps_buf.dtype)
        mine = pltpu.make_async_copy(ps_buf.at[0], out_ref.at[0], store_sem)
        mine.start()
        sends = []
        for j in range(1, halves):
            cp = pltpu.make_async_remote_copy(
                src_ref=ps_buf.at[j], dst_ref=out_ref.at[j], send_sem=ici_send.at[j - 1],
                recv_sem=ici_recv.at[j - 1], device_id=(*chips[j], c), device_id_type=MESH)
            cp.start()
            sends.append(cp)
        for cp in sends:
            cp.wait_recv()
        for cp in sends + swaps:
            cp.wait_send()
        mine.wait()
        wait_small()

    buf = pltpu.VMEM((halves, rows, cols), g.dtype)
    return pl.pallas_call(
        body, name="scatter_grads",
        out_shape=[jax.ShapeDtypeStruct((halves, rows, cols), g.dtype),
                   jax.ShapeDtypeStruct((N_DEV,) + small.shape, small.dtype)],
        in_specs=[ANY_SPACE, VMEM_WHOLE], out_specs=[ANY_SPACE, VMEM_WHOLE],
        scratch_shapes=[buf, buf, buf, pltpu.SemaphoreType.DMA((halves,)), pltpu.SemaphoreType.DMA((halves,)),
                        pltpu.SemaphoreType.DMA((halves - 1,)), pltpu.SemaphoreType.DMA((halves - 1,)),
                        pltpu.SemaphoreType.DMA((halves,)), pltpu.SemaphoreType.DMA] + SMALL_GATHER_SCRATCH,
        compiler_params=_params(vmem=VMEM_MID),
    )(g, small)


def _ada_part(c_all, w_ada, b_shard):
    def body(c_ref, w_ref, b_ref, o_ref):
        cv = c_ref[...]
        sc = (cv * _sigmoid(cv)).astype(BF16)
        o_ref[...] = _dot(sc, w_ref[...].astype(BF16)) + b_ref[...]

    return pl.pallas_call(
        body, name="ada_part", out_shape=jax.ShapeDtypeStruct((N_DEV, ADA_SHARD), F32),
        in_specs=[VMEM_WHOLE] * 3, out_specs=VMEM_WHOLE,
        compiler_params=_params(vmem=VMEM_MID),
    )(c_all, w_ada, b_shard)


def _in_proj(x, mod, wqkv, wag, wft, tm):
    s = x.shape[0]

    def body(x_ref, mod_ref, wqkv_ref, wag_ref, wft_ref, qkv_ref, ag_ref, ft_ref, u_ref):
        u = x_ref[...] * (1.0 + mod_ref[1:2, :]) + mod_ref[0:1, :]
        ub = u.astype(BF16)
        u_ref[...] = ub
        qkv_ref[...] = _dot(ub, wqkv_ref[...]).astype(BF16)
        ag_ref[...] = _dot(ub, wag_ref[...])
        ft_ref[...] = _dot_nt(wft_ref[...], ub)

    return pl.pallas_call(
        body, name="in_proj", grid=(s // tm,),
        in_specs=[pl.BlockSpec((tm, D_MODEL), lambda i: (i, 0)), pl.BlockSpec((8, D_MODEL), lambda i: (0, 0)),
                  VMEM_WHOLE, VMEM_WHOLE, VMEM_WHOLE],
        out_specs=[pl.BlockSpec((tm, 3 * ATTN_W), lambda i: (i, 0)),
                   pl.BlockSpec((tm, 2 * CONV_W), lambda i: (i, 0)),
                   pl.BlockSpec((8, tm), lambda i: (0, i)),
                   pl.BlockSpec((tm, D_MODEL), lambda i: (i, 0))],
        out_shape=[jax.ShapeDtypeStruct((s, 3 * ATTN_W), BF16), jax.ShapeDtypeStruct((s, 2 * CONV_W), F32),
                   jax.ShapeDtypeStruct((8, s), F32), jax.ShapeDtypeStruct((s, D_MODEL), BF16)],
        compiler_params=_params(("parallel",), VMEM_MID),
    )(x, mod, wqkv, wag, wft)


def _fgate(ft, bcol):
    s = ft.shape[1]

    def body(f_ref, b_ref, cum_ref, sneg_ref):
        z = f_ref[...] + b_ref[...]
        e = jnp.exp(-jnp.abs(z))
        l1p = jnp.where(e < 1e-2, e * (1.0 - e * (0.5 - e * (1.0 / 3.0))), jnp.log(1.0 + e))
        logf = jnp.minimum(z, 0.0) - l1p
        r = 1.0 / (1.0 + e)
        sneg_ref[...] = jnp.where(z >= 0, e * r, r)
        lane = lax.broadcasted_iota(jnp.int32, (8, s), 1)
        acc = logf
        sh = 1
        while sh < s:
            acc = acc + jnp.where(lane >= sh, pltpu.roll(acc, sh, axis=1), 0.0)
            sh *= 2
        cum_ref[...] = acc

    return pl.pallas_call(
        body, name="fgate", out_shape=[jax.ShapeDtypeStruct((8, s), F32)] * 2,
        in_specs=[VMEM_WHOLE] * 2, out_specs=[VMEM_WHOLE] * 2,
    )(ft, bcol)


def _attn_fwd(qkv, cum_t, tq, tk, shards):
    s = qkv.shape[0]
    nq = s // tq
    r = tq // tk
    n = len(shards)

    def body(q_ref, k_ref, v_ref, cum_ref, *rest):
        o_ref, lse_ref = rest[n:n + 2]
        hp = pl.program_id(0)
        qi = pl.program_id(1)
        start, wait = _exchange(rest[:n], rest[n + 2:2 * n + 2], *rest[2 * n + 2:], gather=True)
        pl.when(jnp.logical_and(hp == 0, qi == 0))(start)
        q2 = q_ref[...]
        lane = lax.broadcasted_iota(jnp.int32, (1, 128), 1)
        hmask = (lane < 64, lane >= 64)
        aux = (64, 0)
        qm = [jnp.where(hmask[h], q2, jnp.zeros_like(q2)) * 0.125 for h in range(2)]

        def step(kb, carry, r0=None):
            lo = 0 if r0 is None else r0
            kw = tq if r0 is None else tk
            k0 = pl.multiple_of(kb * kw, kw)
            k2 = k_ref[pl.ds(k0, kw), :]
            v2 = v_ref[pl.ds(k0, kw), :]
            out = []
            for h in range(2):
                m_all, acc_all = carry[2 * h], carry[2 * h + 1]
                m, acc = m_all[lo:], acc_all[lo:]
                sc = _dot_nt(qm[h][lo:], k2) - cum_ref[pl.ds(2 * hp + h, 1), pl.ds(k0, kw)]
                if r0 is not None:
                    visible = (lax.broadcasted_iota(jnp.int32, (tq - lo, tk), 1)
                               <= lax.broadcasted_iota(jnp.int32, (tq - lo, tk), 0))
                    sc = jnp.where(visible, sc, -jnp.inf)
                m_new = jnp.maximum(m, jnp.max(sc, axis=1, keepdims=True))
                p = jnp.exp(sc - m_new)
                corr = jnp.exp(m - m_new)
                va = jnp.where(hmask[h], v2, jnp.where(lane == aux[h], 1.0, 0.0).astype(v2.dtype))
                acc_new = acc * corr + _dot(p.astype(BF16), va)
                if lo:
                    m_new = jnp.concatenate([m_all[:lo], m_new], axis=0)
                    acc_new = jnp.concatenate([acc_all[:lo], acc_new], axis=0)
                out += [m_new, acc_new]
            return tuple(out)

        init = (jnp.full((tq, 1), -jnp.inf, F32), jnp.zeros((tq, 128), F32)) * 2
        carry = lax.fori_loop(0, qi, step, init)
        for d in range(r):
            carry = step(qi * r + d, carry, r0=d * tk)
        m0, acc0, m1, acc1 = carry
        l0 = acc0[:, aux[0]:aux[0] + 1]
        l1 = acc1[:, aux[1]:aux[1] + 1]
        o_ref[...] = jnp.where(hmask[0], acc0 / l0, acc1 / l1)
        lse_ref[:, 0:1] = m0 + jnp.log(l0)
        lse_ref[:, 1:2] = m1 + jnp.log(l1)
        pl.when(jnp.logical_and(hp == 3, qi == nq - 1))(wait)

    return pl.pallas_call(
        body, name="attn_fwd", grid=(4, nq),
        in_specs=[pl.BlockSpec((tq, 128), lambda h, i: (i, h)),
                  pl.BlockSpec((s, 128), lambda h, i: (0, 4 + h)),
                  pl.BlockSpec((s, 128), lambda h, i: (0, 8 + h)),
                  pl.BlockSpec((8, s), lambda h, i: (0, 0))] + [ANY_SPACE] * n,
        out_specs=[pl.BlockSpec((tq, 128), lambda h, i: (i, h)),
                   pl.BlockSpec((None, tq, 2), lambda h, i: (h, i, 0))] + [ANY_SPACE] * n,
        out_shape=[jax.ShapeDtypeStruct((s, ATTN_W), F32), jax.ShapeDtypeStruct((4, s, 2), F32)]
        + [jax.ShapeDtypeStruct((N_DEV,) + w.shape, w.dtype) for w in shards],
        scratch_shapes=_exchange_scratch(n),
        compiler_params=_params(("arbitrary", "arbitrary"), VMEM_BIG),
    )(qkv, qkv, qkv, cum_t, *shards)


def _gn_fwd(y, vec_ref, g_ref):
    mean = _dot3(y, g_ref)
    yc = y - mean
    var = _dot3(yc * yc, g_ref)
    rstd = lax.rsqrt(var + LN_EPS)
    yh = yc * rstd
    yn = yh * vec_ref[1:2, :] + vec_ref[2:3, :]
    return yh, rstd, yn


SHIFT_ROWS = HALO - 8
CONV_CHUNK = 32


def _shifted_copies(buf, sh, tm):
    for b in range(1, 8):
        sh[b - 1] = buf[pl.ds(b, tm + SHIFT_ROWS), :]


def _window(buf, sh, off, rows, r0=0):
    a, b = divmod(off, 8)
    if b == 0:
        return buf[pl.ds(8 * a + r0, rows), :]
    return sh[b - 1, pl.ds(8 * a + r0, rows), :]


def _conv_fwd(ag, wdw, vecs, gmat, tm):
    s = ag.shape[0]
    hb = tm // HALO

    def body(cur_ref, halo_ref, w_ref, vec_ref, g_ref, ypre_ref, conv_ref, buf, sh):
        i = pl.program_id(0)
        ug = cur_ref[:, 0:CONV_W] * _sigmoid(cur_ref[:, CONV_W:2 * CONV_W])
        ugh = halo_ref[:, 0:CONV_W] * _sigmoid(halo_ref[:, CONV_W:2 * CONV_W])
        buf[0:HALO, :] = jnp.where(i > 0, ugh, 0.0)
        buf[HALO:HALO + tm, :] = ug
        _shifted_copies(buf, sh, tm)
        y = jnp.broadcast_to(vec_ref[0:1, :], (tm, CONV_W))
        for j in range(CONV_K):
            y = y + w_ref[j:j + 1, :] * _window(buf, sh, HALO - (CONV_K - 1) + j, tm)
        ypre_ref[...] = y
        _, _, yn = _gn_fwd(y, vec_ref, g_ref)
        conv_ref[...] = yn * _sigmoid(yn)

    return pl.pallas_call(
        body, name="conv_fwd", grid=(s // tm,),
        in_specs=[pl.BlockSpec((tm, 2 * CONV_W), lambda i: (i, 0)),
                  pl.BlockSpec((HALO, 2 * CONV_W), lambda i: (jnp.maximum(i * hb - 1, 0), 0)),
                  pl.BlockSpec((32, CONV_W), lambda i: (0, 0)),
                  pl.BlockSpec((8, CONV_W), lambda i: (0, 0)),
                  pl.BlockSpec((CONV_W, CONV_W), lambda i: (0, 0))],
        out_specs=[pl.BlockSpec((tm, CONV_W), lambda i: (i, 0))] * 2,
        out_shape=[jax.ShapeDtypeStruct((s, CONV_W), F32)] * 2,
        scratch_shapes=[pltpu.VMEM((HALO + tm, CONV_W), F32), pltpu.VMEM((7, tm + SHIFT_ROWS, CONV_W), F32)],
        compiler_params=_params(("parallel",), VMEM_MID),
    )(ag, ag, wdw, vecs, gmat)


def _mix_fwd(o, conv, x, w_out, gvec, mod, lnv, tm):
    s = x.shape[0]

    def body(o_ref, c_ref, x_ref, w_ref, g_ref, mod_ref, ln_ref, cat_ref, mixed_ref, z1_ref, x1_ref):
        ov = o_ref[...]
        cv = c_ref[...]
        ra = ov * lax.rsqrt(jnp.mean(ov * ov, axis=1, keepdims=True) + LN_EPS) * g_ref[0:1, :]
        rc = cv * lax.rsqrt(jnp.mean(cv * cv, axis=1, keepdims=True) + LN_EPS) * g_ref[1:2, :]
        rab = ra.astype(BF16)
        rcb = rc.astype(BF16)
        cat_ref[:, 0:ATTN_W] = rab
        cat_ref[:, ATTN_W:D_MODEL] = rcb
        mixed = _dot(rab, w_ref[0:ATTN_W, :]) + _dot(rcb, w_ref[ATTN_W:D_MODEL, :])
        mixed_ref[...] = mixed
        z1 = ALPHA * x_ref[...] + (1.0 + mod_ref[2:3, :]) * mixed
        z1_ref[...] = z1
        xh, _ = _ln_norm(z1)
        x1_ref[...] = xh * ln_ref[0:1, :] + ln_ref[1:2, :]

    row = lambda w: pl.BlockSpec((tm, w), lambda i: (i, 0))
    return pl.pallas_call(
        body, name="mix_fwd", grid=(s // tm,),
        in_specs=[row(ATTN_W), row(CONV_W), row(D_MODEL), VMEM_WHOLE,
                  pl.BlockSpec((8, ATTN_W), lambda i: (0, 0)), pl.BlockSpec((8, D_MODEL), lambda i: (0, 0)),
                  pl.BlockSpec((8, D_MODEL), lambda i: (0, 0))],
        out_specs=[row(D_MODEL)] * 4,
        out_shape=[jax.ShapeDtypeStruct((s, D_MODEL), BF16)] + [jax.ShapeDtypeStruct((s, D_MODEL), F32)] * 3,
        compiler_params=_params(("parallel",), VMEM_MID),
    )(o, conv, x, w_out, gvec, mod, lnv)


def _ff_fwd(x1, tgt, w1, w2, mod, lnv, tm):
    s = x1.shape[0]

    def body(x1_ref, t_ref, w1_ref, w2_ref, mod_ref, ln_ref, u2_ref, hid_ref, dff_ref, dxa_ref, st_ref):
        i = pl.program_id(0)
        x1v = x1_ref[...]
        u2 = (x1v * (1.0 + mod_ref[4:5, :]) + mod_ref[3:4, :]).astype(BF16)
        u2_ref[...] = u2
        ff = jnp.zeros((tm, D_MODEL), F32)
        for d in range(N_DEV):
            h = _dot(u2, w1_ref[d])
            hr = jnp.maximum(h, 0.0)
            hb = (hr * hr).astype(BF16)
            hid_ref[:, d * FF_SHARD:(d + 1) * FF_SHARD] = hb
            ff = ff + _dot(hb, w2_ref[d * FF_SHARD:(d + 1) * FF_SHARD, :])
        gate = 1.0 + mod_ref[5:6, :]
        z2 = ALPHA * x1v + gate * ff
        xh, rstd = _ln_norm(z2)
        y = xh * ln_ref[2:3, :] + ln_ref[3:4, :]
        diff = y - t_ref[...]
        dy = diff * (1.0 / D_MODEL)
        dz2 = _ln_bwd(dy * ln_ref[2:3, :], xh, rstd)
        dff_ref[...] = (gate * dz2).astype(BF16)
        dxa_ref[...] = ALPHA * dz2

        @pl.when(i == 0)
        def _():
            st_ref[...] = jnp.zeros_like(st_ref)

        _acc_rows(st_ref, 0, _rowsum(dy * xh))
        _acc_rows(st_ref, 1, _rowsum(dy))
        _acc_rows(st_ref, 2, _rowsum(dz2 * ff))
        sq = _rowsum(jnp.sum(diff * diff, axis=1, keepdims=True))
        _acc_rows(st_ref, 3, jnp.broadcast_to(sq * (0.5 / D_MODEL), (1, D_MODEL)))

    row = lambda w: pl.BlockSpec((tm, w), lambda i: (i, 0))
    vec = pl.BlockSpec((8, D_MODEL), lambda i: (0, 0))
    return pl.pallas_call(
        body, name="ff_fwd", grid=(s // tm,),
        in_specs=[row(D_MODEL), row(D_MODEL), VMEM_WHOLE, VMEM_WHOLE, vec, vec],
        out_specs=[row(D_MODEL), row(D_FF), row(D_MODEL), row(D_MODEL), vec],
        out_shape=[jax.ShapeDtypeStruct((s, D_MODEL), BF16), jax.ShapeDtypeStruct((s, D_FF), BF16),
                   jax.ShapeDtypeStruct((s, D_MODEL), BF16), jax.ShapeDtypeStruct((s, D_MODEL), F32),
                   jax.ShapeDtypeStruct((8, D_MODEL), F32)],
        compiler_params=_params(("arbitrary",), VMEM_BIG),
    )(x1, tgt, w1, w2, mod, lnv)


def _ff_bwd(dff, hid, w1, w2, dxa, x1, z1, mixed, mod, lnv, tm):
    s = x1.shape[0]

    def body(dff_ref, hid_ref, w1_ref, w2_ref, dxa_ref, x1_ref, z1_ref, mixed_ref, mod_ref, ln_ref,
             dh_ref, dxb_ref, dmix_ref, st_ref):
        i = pl.program_id(0)
        dffv = dff_ref[...]
        du2 = jnp.zeros((tm, D_MODEL), F32)
        for d in range(N_DEV):
            cols = slice(d * FF_SHARD, (d + 1) * FF_SHARD)
            dhid = _dot_nt(dffv, w2_ref[cols, :])
            dh = (dhid * (2.0 * jnp.sqrt(hid_ref[:, cols].astype(F32)))).astype(BF16)
            dh_ref[:, cols] = dh
            du2 = du2 + _dot_nt(dh, w1_ref[d])
        x1v = x1_ref[...]
        dx1 = dxa_ref[...] + du2 * (1.0 + mod_ref[4:5, :])
        xh, rstd = _ln_norm(z1_ref[...])
        dz1 = _ln_bwd(dx1 * ln_ref[0:1, :], xh, rstd)
        dxb_ref[...] = ALPHA * dz1
        dmix_ref[...] = ((1.0 + mod_ref[2:3, :]) * dz1).astype(BF16)

        @pl.when(i == 0)
        def _():
            st_ref[...] = jnp.zeros_like(st_ref)

        _acc_rows(st_ref, 0, _rowsum(du2 * x1v))
        _acc_rows(st_ref, 1, _rowsum(du2))
        _acc_rows(st_ref, 2, _rowsum(dx1 * xh))
        _acc_rows(st_ref, 3, _rowsum(dx1))
        _acc_rows(st_ref, 4, _rowsum(dz1 * mixed_ref[...]))

    row = lambda w: pl.BlockSpec((tm, w), lambda i: (i, 0))
    vec = pl.BlockSpec((8, D_MODEL), lambda i: (0, 0))
    return pl.pallas_call(
        body, name="ff_bwd", grid=(s // tm,),
        in_specs=[row(D_MODEL), row(D_FF), VMEM_WHOLE, VMEM_WHOLE, row(D_MODEL), row(D_MODEL), row(D_MODEL),
                  row(D_MODEL), vec, vec],
        out_specs=[row(D_FF), row(D_MODEL), row(D_MODEL), vec],
        out_shape=[jax.ShapeDtypeStruct((s, D_FF), BF16), jax.ShapeDtypeStruct((s, D_MODEL), F32),
                   jax.ShapeDtypeStruct((s, D_MODEL), BF16), jax.ShapeDtypeStruct((8, D_MODEL), F32)],
        compiler_params=_params(("arbitrary",), VMEM_BIG),
    )(dff, hid, w1, w2, dxa, x1, z1, mixed, mod, lnv)


def _mix_bwd(dmix, w_out, o, conv, gvec, tm):
    s = o.shape[0]

    def body(dm_ref, w_ref, o_ref, c_ref, g_ref, do_ref, dc_ref, st_ref):
        i = pl.program_id(0)
        dmv = dm_ref[...]

        @pl.when(i == 0)
        def _():
            st_ref[...] = jnp.zeros_like(st_ref)

        for part, (src, dst) in enumerate(((o_ref, do_ref), (c_ref, dc_ref))):
            dr = _dot_nt(dmv, w_ref[part * ATTN_W:(part + 1) * ATTN_W, :])
            v = src[...]
            rr = lax.rsqrt(jnp.mean(v * v, axis=1, keepdims=True) + LN_EPS)
            vh = v * rr
            _acc_rows(st_ref, part, _rowsum(dr * vh))
            t = dr * g_ref[part:part + 1, :]
            dst[...] = rr * (t - vh * jnp.mean(t * vh, axis=1, keepdims=True))

    row = lambda w: pl.BlockSpec((tm, w), lambda i: (i, 0))
    vec = pl.BlockSpec((8, ATTN_W), lambda i: (0, 0))
    return pl.pallas_call(
        body, name="mix_bwd", grid=(s // tm,),
        in_specs=[row(D_MODEL), VMEM_WHOLE, row(ATTN_W), row(CONV_W), vec],
        out_specs=[row(ATTN_W), row(CONV_W), vec],
        out_shape=[jax.ShapeDtypeStruct((s, ATTN_W), F32), jax.ShapeDtypeStruct((s, CONV_W), F32),
                   jax.ShapeDtypeStruct((8, ATTN_W), F32)],
        compiler_params=_params(("arbitrary",), VMEM_MID),
    )(dmix, w_out, o, conv, gvec)


def _conv_bwd(dconv, ypre, ag, wdw, vecs, gmat, tm):
    s = ag.shape[0]
    hb = tm // HALO
    nt = s // tm
    last_halo = s // HALO - 1

    def body(dc_ref, dch_ref, yp_ref, yph_ref, cur_ref, w_ref, vec_ref, g_ref,
             dag_ref, st_ref, dw_ref, bufd, shd, dwacc):
        i = pl.program_id(0)

        def dyc_of(yp, dc):
            yh, rstd, yn = _gn_fwd(yp, vec_ref, g_ref)
            sg = _sigmoid(yn)
            dyn = dc * (sg * (1.0 + yn * (1.0 - sg)))
            dyh = dyn * vec_ref[1:2, :]
            dyc = rstd * (dyh - _dot3(dyh, g_ref) - yh * _dot3(dyh * yh, g_ref))
            return dyc, dyn, yh

        dyc, dyn, yh = dyc_of(yp_ref[...], dc_ref[...])
        dych, _, _ = dyc_of(yph_ref[...], dch_ref[...])
        bufd[0:tm, :] = dyc
        bufd[tm:tm + HALO, :] = jnp.where(i < nt - 1, dych, 0.0)

        @pl.when(i == 0)
        def _():
            st_ref[...] = jnp.zeros_like(st_ref)
            dw_ref[...] = jnp.zeros_like(dw_ref)

        _acc_rows(st_ref, 0, _rowsum(dyc))
        _acc_rows(st_ref, 1, _rowsum(dyn * yh))
        _acc_rows(st_ref, 2, _rowsum(dyn))

        _shifted_copies(bufd, shd, tm)
        dwacc[...] = jnp.zeros_like(dwacc)

        def chunk(c, carry):
            r0 = pl.multiple_of(c * CONV_CHUNK, CONV_CHUNK)
            a_c = cur_ref[pl.ds(r0, CONV_CHUNK), 0:CONV_W]
            sg_c = _sigmoid(cur_ref[pl.ds(r0, CONV_CHUNK), CONV_W:2 * CONV_W])
            ug_c = a_c * sg_c
            dug = jnp.zeros((CONV_CHUNK, CONV_W), F32)
            for j in range(CONV_K):
                win = _window(bufd, shd, CONV_K - 1 - j, CONV_CHUNK, r0)
                dug = dug + w_ref[j:j + 1, :] * win
                prod = ug_c * win
                part = prod[0:8]
                for g8 in range(1, CONV_CHUNK // 8):
                    part = part + prod[8 * g8:8 * g8 + 8]
                dwacc[j] = dwacc[j] + part
            dag_ref[pl.ds(r0, CONV_CHUNK), 0:CONV_W] = (dug * sg_c).astype(BF16)
            dag_ref[pl.ds(r0, CONV_CHUNK), CONV_W:2 * CONV_W] = (dug * a_c * sg_c * (1.0 - sg_c)).astype(BF16)
            return carry

        lax.fori_loop(0, tm // CONV_CHUNK, chunk, 0)
        for j in range(CONV_K):
            _acc_rows(dw_ref, j, _rowsum(dwacc[j]))

    row = lambda w: pl.BlockSpec((tm, w), lambda i: (i, 0))
    nxt = lambda w: pl.BlockSpec((HALO, w), lambda i: (jnp.minimum((i + 1) * hb, last_halo), 0))
    return pl.pallas_call(
        body, name="conv_bwd", grid=(nt,),
        in_specs=[row(CONV_W), nxt(CONV_W), row(CONV_W), nxt(CONV_W), row(2 * CONV_W),
                  pl.BlockSpec((32, CONV_W), lambda i: (0, 0)),
                  pl.BlockSpec((8, CONV_W), lambda i: (0, 0)),
                  pl.BlockSpec((CONV_W, CONV_W), lambda i: (0, 0))],
        out_specs=[row(2 * CONV_W), pl.BlockSpec((8, CONV_W), lambda i: (0, 0)),
                   pl.BlockSpec((32, CONV_W), lambda i: (0, 0))],
        out_shape=[jax.ShapeDtypeStruct((s, 2 * CONV_W), BF16), jax.ShapeDtypeStruct((8, CONV_W), F32),
                   jax.ShapeDtypeStruct((32, CONV_W), F32)],
        scratch_shapes=[pltpu.VMEM((tm + HALO, CONV_W), F32), pltpu.VMEM((7, tm + SHIFT_ROWS, CONV_W), F32),
                        pltpu.VMEM((32, 8, CONV_W), F32)],
        compiler_params=_params(("arbitrary",), VMEM_MID),
    )(dconv, dconv, ypre, ypre, ag, wdw, vecs, gmat)


def _attn_bwd(qkv, cum_t, o, do, lse, tq, tk, grads):
    s = qkv.shape[0]
    nq = s // tq
    r = tq // tk
    n = len(grads)

    def body(q_ref, k_ref, v_ref, cum_ref, o_ref, do_ref, lse_ref, *rest):
        dq_ref, dkt_ref, dvt_ref, dcum_ref, drow_ref = rest[n:n + 5]
        hp = pl.program_id(0)
        qi = pl.program_id(1)
        start, wait = _exchange(rest[:n], rest[n + 5:2 * n + 5], *rest[2 * n + 5:], gather=False)
        pl.when(jnp.logical_and(hp == 0, qi == 0))(start)

        @pl.when(qi == 0)
        def _():
            dkt_ref[...] = jnp.zeros_like(dkt_ref)
            dvt_ref[...] = jnp.zeros_like(dvt_ref)
            dcum_ref[...] = jnp.zeros_like(dcum_ref)

        qs = q_ref[...] * 0.125
        ov = o_ref[...]
        dob = do_ref[...].astype(BF16)
        lane = lax.broadcasted_iota(jnp.int32, (1, 128), 1)
        subl = lax.broadcasted_iota(jnp.int32, (128, 1), 0)
        hmask = (lane < 64, lane >= 64)
        aux = (64, 0)
        ones_aux = [jnp.where(lane == aux[h], 1.0, 0.0) for h in range(2)]
        qm, dom, delta, lse_h, qat, dot_t = [], [], [], [], [], []
        for h in range(2):
            qm.append(jnp.where(hmask[h], qs, jnp.zeros_like(qs)))
            dom.append(jnp.where(hmask[h], dob, jnp.zeros_like(dob)))
            delta.append(jnp.sum(dom[h].astype(F32) * ov, axis=1, keepdims=True))
            lse_h.append(lse_ref[:, h:h + 1])
            qat.append(jnp.where(hmask[h], qs.astype(F32), ones_aux[h]).T.astype(BF16))
            dot_t.append(dom[h].astype(F32).T.astype(BF16))

        def step(kb, carry, r0=None):
            lo = 0 if r0 is None else r0
            k0 = pl.multiple_of(kb * tk, tk)
            k2 = k_ref[pl.ds(k0, tk), :]
            v2 = v_ref[pl.ds(k0, tk), :]
            dq_new, a, b = [], [], []
            for h in range(2):
                cs = cum_ref[pl.ds(2 * hp + h, 1), pl.ds(k0, tk)]
                sc = _dot_nt(qm[h][lo:], k2) - cs
                p = jnp.exp(sc - lse_h[h][lo:])
                if r0 is not None:
                    visible = (lax.broadcasted_iota(jnp.int32, (tq - lo, tk), 1)
                               <= lax.broadcasted_iota(jnp.int32, (tq - lo, tk), 0))
                    p = jnp.where(visible, p, 0.0)
                dp = _dot_nt(dom[h][lo:], v2)
                ds = p * (dp - delta[h][lo:])
                pb = p.astype(BF16)
                dsb = ds.astype(BF16)
                a.append(_dot(qat[h][:, lo:], dsb))
                b.append(_dot(dot_t[h][:, lo:], pb))
                ka = jnp.where(hmask[h], k2, ones_aux[h].astype(k2.dtype))
                dq_h = carry[h][lo:] + _dot(dsb, ka)
                dq_new.append(jnp.concatenate([carry[h][:lo], dq_h], axis=0) if lo else dq_h)
            dkt_ref[:, pl.ds(k0, tk)] = dkt_ref[:, pl.ds(k0, tk)] + jnp.where(subl < 64, a[0], a[1])
            dvt_ref[:, pl.ds(k0, tk)] = dvt_ref[:, pl.ds(k0, tk)] + (b[0] + b[1])
            for h in range(2):
                dcum_ref[pl.ds(h, 1), pl.ds(k0, tk)] = (dcum_ref[pl.ds(h, 1), pl.ds(k0, tk)]
                                                        - a[h][aux[h]:aux[h] + 1, :])
            return tuple(dq_new)

        init = (jnp.zeros((tq, 128), F32),) * 2
        carry = lax.fori_loop(0, qi * r, step, init)
        for d in range(r):
            carry = step(qi * r + d, carry, r0=d * tk)
        dq0, dq1 = carry
        dq_ref[...] = (jnp.where(hmask[0], dq0, dq1) * 0.125).astype(BF16)
        drow_ref[:, 0:1] = dq0[:, aux[0]:aux[0] + 1]
        drow_ref[:, 1:2] = dq1[:, aux[1]:aux[1] + 1]
        pl.when(jnp.logical_and(hp == 3, qi == nq - 1))(wait)

    return pl.pallas_call(
        body, name="attn_bwd", grid=(4, nq),
        in_specs=[pl.BlockSpec((tq, 128), lambda h, i: (i, h)),
                  pl.BlockSpec((s, 128), lambda h, i: (0, 4 + h)),
                  pl.BlockSpec((s, 128), lambda h, i: (0, 8 + h)),
                  pl.BlockSpec((8, s), lambda h, i: (0, 0)),
                  pl.BlockSpec((tq, 128), lambda h, i: (i, h)),
                  pl.BlockSpec((tq, 128), lambda h, i: (i, h)),
                  pl.BlockSpec((None, tq, 2), lambda h, i: (h, i, 0))] + [ANY_SPACE] * n,
        out_specs=[pl.BlockSpec((tq, 128), lambda h, i: (i, h)),
                   pl.BlockSpec((128, s), lambda h, i: (h, 0)),
                   pl.BlockSpec((128, s), lambda h, i: (h, 0)),
                   pl.BlockSpec((None, 8, s), lambda h, i: (h, 0, 0)),
                   pl.BlockSpec((None, tq, 2), lambda h, i: (h, i, 0))] + [ANY_SPACE] * n,
        out_shape=[jax.ShapeDtypeStruct((s, ATTN_W), BF16), jax.ShapeDtypeStruct((ATTN_W, s), F32),
                   jax.ShapeDtypeStruct((ATTN_W, s), F32), jax.ShapeDtypeStruct((4, 8, s), F32),
                   jax.ShapeDtypeStruct((4, s, 2), F32)]
        + [jax.ShapeDtypeStruct(g.shape, g.dtype) for g in grads],
        scratch_shapes=_exchange_scratch(n),
        compiler_params=_params(("arbitrary", "arbitrary"), VMEM_BIG),
    )(qkv, qkv, qkv, cum_t, o, do, lse, *grads)


def _fgate_bwd(dcum_t, drow_t, sneg_t):
    s = dcum_t.shape[1]

    def body(dc_ref, dr_ref, sn_ref, df_ref, db_ref):
        lane = lax.broadcasted_iota(jnp.int32, (8, s), 1)
        acc = dc_ref[...] + dr_ref[...]
        sh = 1
        while sh < s:
            acc = acc + jnp.where(lane < s - sh, pltpu.roll(acc, s - sh, axis=1), 0.0)
            sh *= 2
        df = acc * sn_ref[...]
        df_ref[...] = df
        db_ref[...] = jnp.broadcast_to(jnp.sum(df, axis=1, keepdims=True), (8, 128))

    return pl.pallas_call(
        body, name="fgate_bwd",
        out_shape=[jax.ShapeDtypeStruct((8, s), F32), jax.ShapeDtypeStruct((8, 128), F32)],
        in_specs=[VMEM_WHOLE] * 3, out_specs=[VMEM_WHOLE] * 2,
    )(dcum_t, drow_t, sneg_t)


def _in_proj_bwd(dq, dkt, dvt, dag, dft, wqkv, wkvt, wag, wft, x, dxb, mod, tm):
    s = x.shape[0]

    def body(dq_ref, dkt_ref, dvt_ref, dag_ref, dft_ref, wqkv_ref, wkvt_ref, wag_ref, wft_ref, x_ref, dxb_ref,
             mod_ref, dkvt_ref, dx_ref, st_ref):
        i = pl.program_id(0)
        dkb = dkt_ref[...].astype(BF16)
        dvb = dvt_ref[...].astype(BF16)
        dkvt_ref[0:ATTN_W, :] = dkb
        dkvt_ref[ATTN_W:2 * ATTN_W, :] = dvb
        du = _dot_nt(dq_ref[...], wqkv_ref[:, 0:ATTN_W])
        du = du + _dot_tn(dkb, wkvt_ref[0:ATTN_W, :])
        du = du + _dot_tn(dvb, wkvt_ref[ATTN_W:2 * ATTN_W, :])
        du = du + _dot_nt(dag_ref[...], wag_ref[...])
        du = du + _dot_tn(dft_ref[...].astype(BF16), wft_ref[...])
        xv = x_ref[...]
        dx_ref[...] = dxb_ref[...] + du * (1.0 + mod_ref[1:2, :])

        @pl.when(i == 0)
        def _():
            st_ref[...] = jnp.zeros_like(st_ref)

        _acc_rows(st_ref, 0, _rowsum(du * xv))
        _acc_rows(st_ref, 1, _rowsum(du))

    row = lambda w: pl.BlockSpec((tm, w), lambda i: (i, 0))
    col = lambda h: pl.BlockSpec((h, tm), lambda i: (0, i))
    vec = pl.BlockSpec((8, D_MODEL), lambda i: (0, 0))
    return pl.pallas_call(
        body, name="in_proj_bwd", grid=(s // tm,),
        in_specs=[row(ATTN_W), col(ATTN_W), col(ATTN_W), row(2 * CONV_W), col(8),
                  VMEM_WHOLE, VMEM_WHOLE, VMEM_WHOLE, VMEM_WHOLE, row(D_MODEL), row(D_MODEL), vec],
        out_specs=[col(2 * ATTN_W), row(D_MODEL), vec],
        out_shape=[jax.ShapeDtypeStruct((2 * ATTN_W, s), BF16), jax.ShapeDtypeStruct((s, D_MODEL), F32),
                   jax.ShapeDtypeStruct((8, D_MODEL), F32)],
        compiler_params=_params(("arbitrary",), VMEM_MID),
    )(dq, dkt, dvt, dag, dft, wqkv, wkvt, wag, wft, x, dxb, mod)


def _wgrad_nn(a, b, name, tm, tn, tk):
    m, s = a.shape
    n = b.shape[1]
    nk = s // tk

    def body(a_ref, b_ref, o_ref, acc_ref):
        k = pl.program_id(2)

        @pl.when(k == 0)
        def _():
            acc_ref[...] = jnp.zeros_like(acc_ref)

        acc_ref[...] += _dot(a_ref[...], b_ref[...])

        @pl.when(k == nk - 1)
        def _():
            o_ref[...] = acc_ref[...].astype(o_ref.dtype)

    return pl.pallas_call(
        body, name=name, grid=(m // tm, n // tn, nk),
        in_specs=[pl.BlockSpec((tm, tk), lambda i, j, k: (i, k)), pl.BlockSpec((tk, tn), lambda i, j, k: (k, j))],
        out_specs=pl.BlockSpec((tm, tn), lambda i, j, k: (i, j)),
        out_shape=jax.ShapeDtypeStruct((m, n), BF16),
        scratch_shapes=[pltpu.VMEM((tm, tn), F32)],
        compiler_params=_params(("parallel", "parallel", "arbitrary"), VMEM_MID),
    )(a, b)


def _wgrad_tn(a, b, name, tm, tn, tk, out_shape, out_spec):
    s, m = a.shape
    n = b.shape[1]
    nk = s // tk

    def body(a_ref, b_ref, o_ref, acc_ref):
        k = pl.program_id(2)

        @pl.when(k == 0)
        def _():
            acc_ref[...] = jnp.zeros_like(acc_ref)

        acc_ref[...] += _dot_tn(a_ref[...], b_ref[...])

        @pl.when(k == nk - 1)
        def _():
            o_ref[...] = acc_ref[...].astype(o_ref.dtype)

    return pl.pallas_call(
        body, name=name, grid=(m // tm, n // tn, nk),
        in_specs=[pl.BlockSpec((tk, tm), lambda i, j, k: (k, i)), pl.BlockSpec((tk, tn), lambda i, j, k: (k, j))],
        out_specs=out_spec, out_shape=out_shape,
        scratch_shapes=[pltpu.VMEM((tm, tn), F32)],
        compiler_params=_params(("parallel", "parallel", "arbitrary"), VMEM_MID),
    )(a, b)


def _wgrad_f(dft, u, tk):
    s = u.shape[0]
    nk = s // tk

    def body(d_ref, u_ref, o_ref, acc_ref):
        k = pl.program_id(0)

        @pl.when(k == 0)
        def _():
            acc_ref[...] = jnp.zeros_like(acc_ref)

        acc_ref[...] += _dot(d_ref[...].astype(BF16), u_ref[...])

        @pl.when(k == nk - 1)
        def _():
            o_ref[...] = acc_ref[...].astype(BF16)

    return pl.pallas_call(
        body, name="wgrad_f", grid=(nk,),
        in_specs=[pl.BlockSpec((8, tk), lambda k: (0, k)), pl.BlockSpec((tk, D_MODEL), lambda k: (k, 0))],
        out_specs=pl.BlockSpec((8, D_MODEL), lambda k: (0, 0)),
        out_shape=jax.ShapeDtypeStruct((8, D_MODEL), BF16),
        scratch_shapes=[pltpu.VMEM((8, D_MODEL), F32)],
        compiler_params=_params(("arbitrary",)),
    )(dft, u)


def _adamw_big(recv, w, m, v, name, tr):
    r, cdim = w.shape
    slots = recv.shape[0]

    def body(g_ref, w_ref, m_ref, v_ref, go_ref, d_ref, mo_ref, vo_ref):
        g = g_ref[0].astype(F32)
        for k in range(1, slots):
            g = g + g_ref[k].astype(F32)
        delta, m2, v2 = _adamw(w_ref[...], g, m_ref[...], v_ref[...])
        go_ref[...] = g
        d_ref[...] = delta
        mo_ref[...] = m2
        vo_ref[...] = v2

    blk = pl.BlockSpec((tr, cdim), lambda i: (i, 0))
    return pl.pallas_call(
        body, name=name, grid=(r // tr,),
        in_specs=[pl.BlockSpec((slots, tr, cdim), lambda i: (0, i, 0)), blk, blk, blk],
        out_specs=[blk] * 4, out_shape=[jax.ShapeDtypeStruct((r, cdim), F32)] * 4,
        compiler_params=_params(("parallel",), VMEM_MID),
    )(recv, w, m, v)


def _adamw_ada(c_all, dada, w, m, v, tr):
    r, cdim = w.shape

    def body(c_ref, d_ref, w_ref, m_ref, v_ref, go_ref, dl_ref, mo_ref, vo_ref):
        cv = c_ref[...]
        sc = (cv * _sigmoid(cv)).astype(BF16)
        g = _dot_tn(sc, d_ref[...].astype(BF16))
        delta, m2, v2 = _adamw(w_ref[...], g, m_ref[...], v_ref[...])
        go_ref[...] = g
        dl_ref[...] = delta
        mo_ref[...] = m2
        vo_ref[...] = v2

    blk = pl.BlockSpec((tr, cdim), lambda i: (i, 0))
    return pl.pallas_call(
        body, name="adamw_ada", grid=(r // tr,),
        in_specs=[pl.BlockSpec((N_DEV, tr), lambda i: (0, i)), pl.BlockSpec((N_DEV, cdim), lambda i: (0, 0)),
                  blk, blk, blk],
        out_specs=[blk] * 4, out_shape=[jax.ShapeDtypeStruct((r, cdim), F32)] * 4,
        compiler_params=_params(("parallel",), VMEM_MID),
    )(c_all, dada, w, m, v)


def _adamw_small(parts, w, m, v):
    n = w.shape[1]

    def body(g_ref, w_ref, m_ref, v_ref, go_ref, d_ref, mo_ref, vo_ref):
        g = g_ref[0]
        for k in range(1, N_DEV):
            g = g + g_ref[k]
        delta, m2, v2 = _adamw(w_ref[...], g, m_ref[...], v_ref[...])
        go_ref[...] = g
        d_ref[...] = delta
        mo_ref[...] = m2
        vo_ref[...] = v2

    return pl.pallas_call(
        body, name="adamw_small", out_shape=[jax.ShapeDtypeStruct((1, n), F32)] * 4,
        in_specs=[VMEM_WHOLE] * 4, out_specs=[VMEM_WHOLE] * 4,
    )(parts, w, m, v)


def _pad_cols(a, n):
    return jnp.pad(a, ((0, 0), (0, n - a.shape[1])))


def _pad_rows(a, n):
    return jnp.pad(a, ((0, n - a.shape[0]), (0, 0)))


def kernel(x, c, w_ada, b_ada, w_in, b_forget, w_dw, b_dw, gn_g, gn_b, g_attn_out, g_conv_out, w_out, ln1_g, ln1_b, w_ff1, w_ff2, ln2_g, ln2_b, loss_target, m_w_ada, m_b_ada, m_w_in, m_b_forget, m_w_dw, m_b_dw, m_gn_g, m_gn_b, m_g_attn_out, m_g_conv_out, m_w_out, m_ln1_g, m_ln1_b, m_w_ff1, m_w_ff2, m_ln2_g, m_ln2_b, v_w_ada, v_b_ada, v_w_in, v_b_forget, v_w_dw, v_b_dw, v_gn_g, v_gn_b, v_g_attn_out, v_g_conv_out, v_w_out, v_ln1_g, v_ln1_b, v_w_ff1, v_w_ff2, v_ln2_g, v_ln2_b):
    s = x.shape[1]
    tm = min(ROW_TILE, s)
    tk_att = min(ATT_TILE, s)
    tq_att = min(ATT_Q_BLOCKS * tk_att, s)
    me = 4 * lax.axis_index("x") + 2 * lax.axis_index("y") + lax.axis_index("c")
    xs = x[0]
    tgt = loss_target[0]

    dw_n = CONV_K * DW_SHARD
    pack1 = jnp.concatenate([c, w_dw[0].reshape(1, dw_n), jnp.zeros((1, 2048 - dw_n), F32)], axis=1)
    gw_in, g1 = _gather_weights([_pad_cols(w_in[0], IN_SHARD_PAD).astype(BF16)], pack1)
    g1 = g1[:, 0, :]
    c_all = g1[:, :D_MODEL]
    wdw_full = g1[:, D_MODEL:D_MODEL + dw_n].reshape(N_DEV, CONV_K, DW_SHARD)
    wdw_full = _pad_rows(wdw_full.transpose(1, 0, 2).reshape(CONV_K, CONV_W), 32)

    b_shard = lax.dynamic_slice(b_ada, (0, me * ADA_SHARD), (1, ADA_SHARD))
    ada_all = _gather_small(_ada_part(c_all, w_ada[0], b_shard), "gather_ada")
    ada = lax.dynamic_index_in_dim(ada_all, me, axis=1, keepdims=False).reshape(6, D_MODEL)
    mod = _pad_rows(ada, 8)

    w_in_full = gw_in[:, :, :IN_SHARD].transpose(1, 0, 2).reshape(D_MODEL, N_IN)
    wqkv = w_in_full[:, :3 * ATTN_W]
    wft = w_in_full[:, 3 * ATTN_W:3 * ATTN_W + N_HEADS].T
    wag = w_in_full[:, 3 * ATTN_W + N_HEADS:]

    qkv, ag, ft, u = _in_proj(xs, mod, wqkv, wag, wft, tm)
    cum_t, sneg_t = _fgate(ft, b_forget.reshape(N_HEADS, 1))
    o, lse, gw_out, gw_ff1, gw_ff2 = _attn_fwd(
        qkv, cum_t, tq_att, tk_att, [w_out[0].astype(BF16), w_ff1[0].astype(BF16), w_ff2[0].astype(BF16)])
    w_out_full = gw_out.reshape(D_MODEL, D_MODEL)
    w2_full = gw_ff2.reshape(D_FF, D_MODEL)
    cvec = _pad_rows(jnp.concatenate([b_dw, gn_g, gn_b], axis=0), 8)
    grp = jnp.arange(CONV_W) // 64
    gmat = jnp.where(grp[:, None] == grp[None, :], 1.0 / 64.0, 0.0).astype(BF16)
    ypre, conv = _conv_fwd(ag, wdw_full, cvec, gmat, tm)
    gvec = _pad_rows(jnp.concatenate([g_attn_out, g_conv_out], axis=0), 8)
    lnv = _pad_rows(jnp.concatenate([ln1_g, ln1_b, ln2_g, ln2_b], axis=0), 8)
    cat, mixed, z1, x1 = _mix_fwd(o, conv, xs, w_out_full, gvec, mod, lnv, tm)
    tf = min(FF_TILE, s)
    u2, hid, dff, dxa, st_ff = _ff_fwd(x1, tgt, gw_ff1, w2_full, mod, lnv, min(2 * FF_TILE, s))

    dh, dxb, dmix, st_fb = _ff_bwd(dff, hid, gw_ff1, w2_full, dxa, x1, z1, mixed, mod, lnv, tf)
    do, dconv, st_mix = _mix_bwd(dmix, w_out_full, o, conv, gvec, tm)
    dag, st_conv, dwdw = _conv_bwd(dconv, ypre, ag, wdw_full, cvec, gmat, tm)

    tk = min(WGRAD_K_TILE, s)
    g_ff2 = _wgrad_tn(hid, dff, "wgrad_ff2", 512, 1024, tk,
                      jax.ShapeDtypeStruct((D_FF, D_MODEL), BF16),
                      pl.BlockSpec((512, 1024), lambda i, j, k: (i, j))).reshape(N_DEV, FF_SHARD, D_MODEL)
    g_ff1 = _wgrad_tn(u2, dh, "wgrad_ff1", 1024, 512, tk,
                      jax.ShapeDtypeStruct((N_DEV, D_MODEL, FF_SHARD), BF16),
                      pl.BlockSpec((None, 1024, 512), lambda i, j, k: (j, i, 0)))
    g_out = _wgrad_tn(cat, dmix, "wgrad_out", 512, 1024, tk,
                      jax.ShapeDtypeStruct((D_MODEL, D_MODEL), BF16),
                      pl.BlockSpec((512, 1024), lambda i, j, k: (i, j))).reshape(N_DEV, OUT_SHARD, D_MODEL)
    dq, dkt, dvt, dcum, drow, r_out, r_ff1, r_ff2 = _attn_bwd(
        qkv, cum_t, o, do, lse, tq_att, tk_att, [g_out, g_ff1, g_ff2])
    dft, dbf = _fgate_bwd(dcum[:, :2, :].reshape(N_HEADS, s), drow.transpose(0, 2, 1).reshape(N_HEADS, s), sneg_t)
    wkvt = wqkv[:, ATTN_W:3 * ATTN_W].T
    dkvt, grad_x, st_in = _in_proj_bwd(dq, dkt, dvt, dag, dft, wqkv, wkvt, wag, wft, xs, dxb, mod, tm)

    g_q = _wgrad_tn(u, dq, "wgrad_q", 1024, 512, tk,
                    jax.ShapeDtypeStruct((D_MODEL, ATTN_W), BF16),
                    pl.BlockSpec((1024, 512), lambda i, j, k: (i, j)))
    g_kvt = _wgrad_nn(dkvt, u, "wgrad_kv", 512, 1024, tk)
    g_qkv = jnp.concatenate([g_q, g_kvt.T], axis=1)
    g_ag = _wgrad_tn(u, dag, "wgrad_ag", 1024, 512, tk,
                     jax.ShapeDtypeStruct((D_MODEL, 2 * CONV_W), BF16),
                     pl.BlockSpec((1024, 512), lambda i, j, k: (i, j)))
    g_ft = _wgrad_f(dft, u, tk)
    g_in = jnp.concatenate([g_qkv, g_ft.T, g_ag], axis=1)
    g_in = jnp.pad(g_in.reshape(D_MODEL, N_DEV, IN_SHARD).transpose(1, 0, 2),
                   ((0, 0), (0, 0), (0, IN_SHARD_PAD - IN_SHARD)))

    dada = jnp.concatenate([st_in[1:2], st_in[0:1], st_fb[4:5], st_fb[1:2], st_fb[0:1], st_ff[2:3]], axis=1)
    loss_part = st_ff[3:4, 0:1]
    pack2 = jnp.concatenate([
        dada,
        _pad_cols(dbf[:, 0].reshape(1, N_HEADS), 128),
        dwdw[:CONV_K].reshape(1, CONV_K * CONV_W),
        st_conv[0:1], st_conv[1:2], st_conv[2:3], st_mix[0:1], st_mix[1:2],
        st_fb[2:3], st_fb[3:4], st_ff[0:1], st_ff[1:2],
        _pad_cols(loss_part, 128)], axis=1)
    r_in, g2 = _scatter_grads(g_in, pack2)
    g2 = g2[:, 0, :]
    o_bf = 6 * D_MODEL
    o_dw = o_bf + 128
    o_v = o_dw + CONV_K * CONV_W
    o_ln = o_v + 5 * CONV_W
    o_loss = o_ln + 4 * D_MODEL
    dw_parts = lax.dynamic_slice_in_dim(
        g2[:, o_dw:o_v].reshape(N_DEV, CONV_K, N_DEV, DW_SHARD), me, 1, axis=2).reshape(N_DEV, dw_n)
    parts = jnp.concatenate([
        g2[:, :o_dw], _pad_cols(dw_parts, 2048), g2[:, o_v:o_loss + 128]], axis=1)[:, None, :]

    def pack_w(b_ada_, b_forget_, w_dw_, smalls):
        return jnp.concatenate([b_ada_, _pad_cols(b_forget_, 128), _pad_cols(w_dw_[0].reshape(1, dw_n), 2048)]
                               + smalls + [jnp.zeros((1, 128), F32)], axis=1)

    pw = pack_w(b_ada, b_forget, w_dw, [b_dw, gn_g, gn_b, g_attn_out, g_conv_out, ln1_g, ln1_b, ln2_g, ln2_b])
    pm = pack_w(m_b_ada, m_b_forget, m_w_dw, [m_b_dw, m_gn_g, m_gn_b, m_g_attn_out, m_g_conv_out,
                                              m_ln1_g, m_ln1_b, m_ln2_g, m_ln2_b])
    pv = pack_w(v_b_ada, v_b_forget, v_w_dw, [v_b_dw, v_gn_g, v_gn_b, v_g_attn_out, v_g_conv_out,
                                              v_ln1_g, v_ln1_b, v_ln2_g, v_ln2_b])
    small = _adamw_small(parts, pw, pm, pv)

    p_bf = 6 * D_MODEL
    p_dw = p_bf + 128
    p_v = p_dw + 2048
    p_ln = p_v + 5 * CONV_W
    p_loss = p_ln + 4 * D_MODEL

    def unpack(t):
        outs = {"b_ada": t[:, :p_bf], "b_forget": t[:, p_bf:p_bf + N_HEADS],
                "w_dw": t[:, p_dw:p_dw + dw_n].reshape(1, CONV_K, 1, DW_SHARD)}
        for k, nm in enumerate(["b_dw", "gn_g", "gn_b", "g_attn_out", "g_conv_out"]):
            outs[nm] = t[:, p_v + k * CONV_W:p_v + (k + 1) * CONV_W]
        for k, nm in enumerate(["ln1_g", "ln1_b", "ln2_g", "ln2_b"]):
            outs[nm] = t[:, p_ln + k * D_MODEL:p_ln + (k + 1) * D_MODEL]
        return outs

    sm = [unpack(t) for t in small]
    loss = small[0][0, p_loss]

    dada_all = g2[:, :6 * D_MODEL]
    dada_shard = lax.dynamic_slice_in_dim(dada_all, me * ADA_SHARD, ADA_SHARD, axis=1)
    big = {
        "w_ada": _adamw_ada(c_all, dada_shard, w_ada[0], m_w_ada[0], v_w_ada[0], 256),
        "w_in": [t[:, :IN_SHARD] for t in _adamw_big(
            r_in, _pad_cols(w_in[0], IN_SHARD_PAD), _pad_cols(m_w_in[0], IN_SHARD_PAD),
            _pad_cols(v_w_in[0], IN_SHARD_PAD), "adamw_in", 256)],
        "w_out": _adamw_big(r_out, w_out[0], m_w_out[0], v_w_out[0], "adamw_out", 128),
        "w_ff1": _adamw_big(r_ff1, w_ff1[0], m_w_ff1[0], v_w_ff1[0], "adamw_ff1", 256),
        "w_ff2": _adamw_big(r_ff2, w_ff2[0], m_w_ff2[0], v_w_ff2[0], "adamw_ff2", 256),
    }

    names = ["w_ada", "b_ada", "w_in", "b_forget", "w_dw", "b_dw", "gn_g", "gn_b", "g_attn_out", "g_conv_out",
             "w_out", "ln1_g", "ln1_b", "w_ff1", "w_ff2", "ln2_g", "ln2_b"]

    def leaf(kind, nm):
        if nm in big:
            return big[nm][kind][None]
        return sm[kind][nm]

    outs = [loss, grad_x[None]]
    for kind in range(4):
        outs += [leaf(kind, nm) for nm in names]
    return tuple(outs)
```

```python
import jax
import jax.numpy as jnp
from jax import lax
from jax.experimental import pallas as pl
from jax.experimental.pallas import tpu as pltpu

F32 = jnp.float32
BF16 = jnp.bfloat16

D_MODEL = 1024
ATTN_W = 512
CONV_W = 512
N_HEADS = 8
CONV_K = 31
D_FF = 4096
N_DEV = 8
N_IN = 3 * ATTN_W + N_HEADS + 2 * CONV_W
IN_SHARD = N_IN // N_DEV
IN_SHARD_PAD = 384
ADA_SHARD = 6 * D_MODEL // N_DEV
FF_SHARD = D_FF // N_DEV
OUT_SHARD = D_MODEL // N_DEV
DW_SHARD = CONV_W // N_DEV
HALO = 32
LN_EPS = 1e-5
ALPHA = 2.0 ** 0.25
ADAM_LR, ADAM_B1, ADAM_B2, ADAM_EPS, ADAM_WD, ADAM_STEP = 0.001, 0.9, 0.999, 1e-08, 0.01, 10

ROW_TILE = 512
FF_TILE = 256
WGRAD_K_TILE = 4096
ATT_TILE = 512
ATT_Q_BLOCKS = 2
VMEM_BIG = 56 * 1024 * 1024
VMEM_MID = 40 * 1024 * 1024

MESH = pl.DeviceIdType.MESH
VMEM_WHOLE = pl.BlockSpec(memory_space=pltpu.VMEM)
ANY_SPACE = pl.BlockSpec(memory_space=pl.ANY)


def _dot(a, b):
    return jnp.dot(a, b, preferred_element_type=F32)


def _dot_nt(a, b):
    return lax.dot_general(a, b, (((1,), (1,)), ((), ())), preferred_element_type=F32)


def _dot_tn(a, b):
    return lax.dot_general(a, b, (((0,), (0,)), ((), ())), preferred_element_type=F32)


def _dot3(xf, g_ref):
    g = g_ref[...]
    hi = xf.astype(BF16)
    lo = (xf - hi.astype(F32)).astype(BF16)
    return _dot(hi, g) + _dot(lo, g)


def _sigmoid(x):
    return 1.0 / (1.0 + jnp.exp(-x))


def _rowsum(x):
    return jnp.sum(x, axis=0, keepdims=True)


def _ln_norm(z):
    mu = jnp.mean(z, axis=1, keepdims=True)
    zc = z - mu
    var = jnp.mean(zc * zc, axis=1, keepdims=True)
    rstd = lax.rsqrt(var + LN_EPS)
    return zc * rstd, rstd


def _ln_bwd(dxh, xh, rstd):
    m1 = jnp.mean(dxh, axis=1, keepdims=True)
    m2 = jnp.mean(dxh * xh, axis=1, keepdims=True)
    return rstd * (dxh - m1 - xh * m2)


def _adamw(w, g, m, v):
    m2 = ADAM_B1 * m + (1.0 - ADAM_B1) * g
    v2 = ADAM_B2 * v + (1.0 - ADAM_B2) * (g * g)
    m_hat = m2 / (1.0 - ADAM_B1 ** ADAM_STEP)
    v_hat = v2 / (1.0 - ADAM_B2 ** ADAM_STEP)
    delta = -ADAM_LR * (m_hat / (jnp.sqrt(v_hat) + ADAM_EPS) + ADAM_WD * w)
    return delta, m2, v2


def _params(sem=None, vmem=None):
    kw = {}
    if sem is not None:
        kw["dimension_semantics"] = sem
    if vmem is not None:
        kw["vmem_limit_bytes"] = vmem
    return pltpu.CompilerParams(**kw)


def _coords():
    return lax.axis_index("x"), lax.axis_index("y"), lax.axis_index("c")


def _flip(v, bit):
    return 1 - v if bit else v


def _acc_rows(ref, r, val):
    ref[r:r + 1, :] = ref[r:r + 1, :] + val


def _small_gather(v_ref, out_ref, send_sems, recv_sems, local_sem):
    x, y, c = _coords()
    me = 4 * x + 2 * y + c
    mine = pltpu.make_async_copy(v_ref, out_ref.at[me], local_sem)
    sends, recvs = [], []
    for k in range(1, N_DEV):
        px, py, pc = _flip(x, (k >> 2) & 1), _flip(y, (k >> 1) & 1), _flip(c, k & 1)
        peer = 4 * px + 2 * py + pc
        sems = dict(send_sem=send_sems.at[k - 1], recv_sem=recv_sems.at[k - 1],
                    device_id=(px, py, pc), device_id_type=MESH)
        sends.append(pltpu.make_async_remote_copy(src_ref=v_ref, dst_ref=out_ref.at[me], **sems))
        recvs.append(pltpu.make_async_remote_copy(src_ref=v_ref, dst_ref=out_ref.at[peer], **sems))

    def start():
        mine.start()
        for cp in sends:
            cp.start()

    def wait():
        for cp in recvs:
            cp.wait_recv()
        for cp in sends:
            cp.wait_send()
        mine.wait()

    return start, wait


SMALL_GATHER_SCRATCH = [pltpu.SemaphoreType.DMA((N_DEV - 1,)), pltpu.SemaphoreType.DMA((N_DEV - 1,)),
                        pltpu.SemaphoreType.DMA]


def _gather_small(v, name):
    r, n = v.shape

    def body(v_ref, out_ref, send_sems, recv_sems, local_sem):
        start, wait = _small_gather(v_ref, out_ref, send_sems, recv_sems, local_sem)
        start()
        wait()

    return pl.pallas_call(
        body, name=name,
        out_shape=jax.ShapeDtypeStruct((N_DEV, r, n), v.dtype),
        in_specs=[VMEM_WHOLE], out_specs=VMEM_WHOLE,
        scratch_shapes=SMALL_GATHER_SCRATCH,
    )(v)


def _gather_weights(shards, small):
    n = len(shards)

    def body(*refs):
        ins, small_ref = refs[:n], refs[n]
        outs, small_out = refs[n + 1:2 * n + 1], refs[2 * n + 1]
        send_sems, recv_sems, local_sems = refs[2 * n + 2:2 * n + 5]
        start_small, wait_small = _small_gather(small_ref, small_out, *refs[2 * n + 5:])
        start_small()
        x, y, c = _coords()
        me, sibling = (x, y, c), (x, y, 1 - c)
        chips = [(1 - x, y), (x, 1 - y), (1 - x, 1 - y)]

        def slot(a, pos):
            return outs[a].at[4 * pos[0] + 2 * pos[1] + pos[2]]

        def copy(a, k, block, to, src=None):
            return pltpu.make_async_remote_copy(
                src_ref=slot(a, block) if src is None else src, dst_ref=slot(a, block),
                send_sem=send_sems.at[7 * a + k], recv_sem=recv_sems.at[7 * a + k],
                device_id=to, device_id_type=MESH)

        started = []
        local = []
        for a in range(n):
            mine = pltpu.make_async_copy(ins[a], slot(a, me), local_sems.at[a])
            mine.start()
            local.append(mine)
        for j, chip in enumerate(chips):
            for a in range(n):
                cp = copy(a, 1 + j, me, (*chip, c), src=ins[a])
                cp.start()
                started.append(cp)
        for a in range(n):
            cp = copy(a, 0, me, sibling, src=ins[a])
            cp.start()
            started.append(cp)
        for j, chip in enumerate(chips):
            for a in range(n):
                copy(a, 1 + j, (*chip, c), me).wait_recv()
                cp = copy(a, 4 + j, (*chip, c), sibling)
                cp.start()
                started.append(cp)
        for a in range(n):
            copy(a, 0, sibling, me).wait_recv()
        for j, chip in enumerate(chips):
            for a in range(n):
                copy(a, 4 + j, (*chip, 1 - c), me).wait_recv()
        for cp in started:
            cp.wait_send()
        for mine in local:
            mine.wait()
        wait_small()

    return pl.pallas_call(
        body, name="gather_weights",
        out_shape=[jax.ShapeDtypeStruct((N_DEV,) + s.shape, s.dtype) for s in shards]
        + [jax.ShapeDtypeStruct((N_DEV,) + small.shape, small.dtype)],
        in_specs=[ANY_SPACE] * n + [VMEM_WHOLE], out_specs=[ANY_SPACE] * n + [VMEM_WHOLE],
        scratch_shapes=[pltpu.SemaphoreType.DMA((7 * n,)), pltpu.SemaphoreType.DMA((7 * n,)),
                        pltpu.SemaphoreType.DMA((n,))] + SMALL_GATHER_SCRATCH,
    )(*shards, small)


def _exchange(ins, outs, send_sems, recv_sems, local_sems, gather):
    n = len(ins)
    x, y, c = _coords()
    me = 4 * x + 2 * y + c
    local = [pltpu.make_async_copy(ins[a] if gather else ins[a].at[me], outs[a].at[me], local_sems.at[a])
             for a in range(n)]
    sends, recvs = [], []
    for k in range(1, N_DEV):
        px, py, pc = _flip(x, (k >> 2) & 1), _flip(y, (k >> 1) & 1), _flip(c, k & 1)
        peer = 4 * px + 2 * py + pc
        for a in range(n):
            src = ins[a] if gather else ins[a].at[peer]
            sems = dict(send_sem=send_sems.at[7 * a + k - 1], recv_sem=recv_sems.at[7 * a + k - 1],
                        device_id=(px, py, pc), device_id_type=MESH)
            sends.append(pltpu.make_async_remote_copy(src_ref=src, dst_ref=outs[a].at[me], **sems))
            recvs.append(pltpu.make_async_remote_copy(src_ref=src, dst_ref=outs[a].at[peer], **sems))

    def start():
        for cp in local + sends:
            cp.start()

    def wait():
        for cp in recvs:
            cp.wait_recv()
        for cp in sends:
            cp.wait_send()
        for cp in local:
            cp.wait()

    return start, wait


def _exchange_scratch(n):
    return [pltpu.SemaphoreType.DMA((7 * n,)), pltpu.SemaphoreType.DMA((7 * n,)), pltpu.SemaphoreType.DMA((n,))]


def _scatter_grads(g, small):
    _, rows, cols = g.shape
    halves = 4

    def body(g_ref, small_ref, out_ref, small_out, sib_buf, own_buf, ps_buf,
             d2d_send, d2d_recv, ici_send, ici_recv, load_sems, store_sem, *small_sems):
        start_small, wait_small = _small_gather(small_ref, small_out, *small_sems)
        start_small()
        x, y, c = _coords()
        sibling = (x, y, 1 - c)
        chips = [(x, y), (1 - x, y), (x, 1 - y), (1 - x, 1 - y)]
        swaps, loads = [], []
        for j, (px, py) in enumerate(chips):
            cp = pltpu.make_async_remote_copy(
                src_ref=g_ref.at[4 * px + 2 * py + (1 - c)], dst_ref=sib_buf.at[j],
                send_sem=d2d_send.at[j], recv_sem=d2d_recv.at[j], device_id=sibling, device_id_type=MESH)
            cp.start()
            swaps.append(cp)
            ld = pltpu.make_async_copy(g_ref.at[4 * px + 2 * py + c], own_buf.at[j], load_sems.at[j])
            ld.start()
            loads.append(ld)
        for j in range(halves):
            swaps[j].wait_recv()
            loads[j].wait()
            ps_buf[j] = (own_buf[j].astype(F32) + sib_buf[j].astype(F32)).astype(ps_buf.dtype)
        mine = pltpu.make_async_copy(ps_buf.at[0], out_ref.at[0], store_sem)
        mine.start()
        sends = []
        for j in range(1, halves):
            cp = pltpu.make_async_remote_copy(
                src_ref=ps_buf.at[j], dst_ref=out_ref.at[j], send_sem=ici_send.at[j - 1],
                recv_sem=ici_recv.at[j - 1], device_id=(*chips[j], c), device_id_type=MESH)
            cp.start()
            sends.append(cp)
        for cp in sends:
            cp.wait_recv()
        for cp in sends + swaps:
            cp.wait_send()
        mine.wait()
        wait_small()

    buf = pltpu.VMEM((halves, rows, cols), g.dtype)
    return pl.pallas_call(
        body, name="scatter_grads",
        out_shape=[jax.ShapeDtypeStruct((halves, rows, cols), g.dtype),
                   jax.ShapeDtypeStruct((N_DEV,) + small.shape, small.dtype)],
        in_specs=[ANY_SPACE, VMEM_WHOLE], out_specs=[ANY_SPACE, VMEM_WHOLE],
        scratch_shapes=[buf, buf, buf, pltpu.SemaphoreType.DMA((halves,)), pltpu.SemaphoreType.DMA((halves,)),
                        pltpu.SemaphoreType.DMA((halves - 1,)), pltpu.SemaphoreType.DMA((halves - 1,)),
                        pltpu.SemaphoreType.DMA((halves,)), pltpu.SemaphoreType.DMA] + SMALL_GATHER_SCRATCH,
        compiler_params=_params(vmem=VMEM_MID),
    )(g, small)


def _ada_part(c_all, w_ada, b_shard):
    def body(c_ref, w_ref, b_ref, o_ref):
        cv = c_ref[...]
        sc = (cv * _sigmoid(cv)).astype(BF16)
        o_ref[...] = _dot(sc, w_ref[...].astype(BF16)) + b_ref[...]

    return pl.pallas_call(
        body, name="ada_part", out_shape=jax.ShapeDtypeStruct((N_DEV, ADA_SHARD), F32),
        in_specs=[VMEM_WHOLE] * 3, out_specs=VMEM_WHOLE,
        compiler_params=_params(vmem=VMEM_MID),
    )(c_all, w_ada, b_shard)


def _in_proj(x, mod, wqkv, wag, wft, tm):
    s = x.shape[0]

    def body(x_ref, mod_ref, wqkv_ref, wag_ref, wft_ref, qkv_ref, ag_ref, ft_ref, u_ref):
        u = x_ref[...] * (1.0 + mod_ref[1:2, :]) + mod_ref[0:1, :]
        ub = u.astype(BF16)
        u_ref[...] = ub
        qkv_ref[...] = _dot(ub, wqkv_ref[...]).astype(BF16)
        ag_ref[...] = _dot(ub, wag_ref[...])
        ft_ref[...] = _dot_nt(wft_ref[...], ub)

    return pl.pallas_call(
        body, name="in_proj", grid=(s // tm,),
        in_specs=[pl.BlockSpec((tm, D_MODEL), lambda i: (i, 0)), pl.BlockSpec((8, D_MODEL), lambda i: (0, 0)),
                  VMEM_WHOLE, VMEM_WHOLE, VMEM_WHOLE],
        out_specs=[pl.BlockSpec((tm, 3 * ATTN_W), lambda i: (i, 0)),
                   pl.BlockSpec((tm, 2 * CONV_W), lambda i: (i, 0)),
                   pl.BlockSpec((8, tm), lambda i: (0, i)),
                   pl.BlockSpec((tm, D_MODEL), lambda i: (i, 0))],
        out_shape=[jax.ShapeDtypeStruct((s, 3 * ATTN_W), BF16), jax.ShapeDtypeStruct((s, 2 * CONV_W), F32),
                   jax.ShapeDtypeStruct((8, s), F32), jax.ShapeDtypeStruct((s, D_MODEL), BF16)],
        compiler_params=_params(("parallel",), VMEM_MID),
    )(x, mod, wqkv, wag, wft)


def _fgate(ft, bcol):
    s = ft.shape[1]

    def body(f_ref, b_ref, cum_ref, sneg_ref):
        z = f_ref[...] + b_ref[...]
        e = jnp.exp(-jnp.abs(z))
        l1p = jnp.where(e < 1e-2, e * (1.0 - e * (0.5 - e * (1.0 / 3.0))), jnp.log(1.0 + e))
        logf = jnp.minimum(z, 0.0) - l1p
        r = 1.0 / (1.0 + e)
        sneg_ref[...] = jnp.where(z >= 0, e * r, r)
        lane = lax.broadcasted_iota(jnp.int32, (8, s), 1)
        acc = logf
        sh = 1
        while sh < s:
            acc = acc + jnp.where(lane >= sh, pltpu.roll(acc, sh, axis=1), 0.0)
            sh *= 2
        cum_ref[...] = acc

    return pl.pallas_call(
        body, name="fgate", out_shape=[jax.ShapeDtypeStruct((8, s), F32)] * 2,
        in_specs=[VMEM_WHOLE] * 2, out_specs=[VMEM_WHOLE] * 2,
    )(ft, bcol)


def _attn_fwd(qkv, cum_t, tq, tk, shards):
    s = qkv.shape[0]
    nq = s // tq
    r = tq // tk
    n = len(shards)

    def body(q_ref, k_ref, v_ref, cum_ref, *rest):
        o_ref, lse_ref = rest[n:n + 2]
        hp = pl.program_id(0)
        qi = pl.program_id(1)
        start, wait = _exchange(rest[:n], rest[n + 2:2 * n + 2], *rest[2 * n + 2:], gather=True)
        pl.when(jnp.logical_and(hp == 0, qi == 0))(start)
        q2 = q_ref[...]
        lane = lax.broadcasted_iota(jnp.int32, (1, 128), 1)
        hmask = (lane < 64, lane >= 64)
        aux = (64, 0)
        qm = [jnp.where(hmask[h], q2, jnp.zeros_like(q2)) * 0.125 for h in range(2)]

        def step(kb, carry, r0=None):
            lo = 0 if r0 is None else r0
            kw = tq if r0 is None else tk
            k0 = pl.multiple_of(kb * kw, kw)
            k2 = k_ref[pl.ds(k0, kw), :]
            v2 = v_ref[pl.ds(k0, kw), :]
            out = []
            for h in range(2):
                m_all, acc_all = carry[2 * h], carry[2 * h + 1]
                m, acc = m_all[lo:], acc_all[lo:]
                sc = _dot_nt(qm[h][lo:], k2) - cum_ref[pl.ds(2 * hp + h, 1), pl.ds(k0, kw)]
                if r0 is not None:
                    visible = (lax.broadcasted_iota(jnp.int32, (tq - lo, tk), 1)
                               <= lax.broadcasted_iota(jnp.int32, (tq - lo, tk), 0))
                    sc = jnp.where(visible, sc, -jnp.inf)
                m_new = jnp.maximum(m, jnp.max(sc, axis=1, keepdims=True))
                p = jnp.exp(sc - m_new)
                corr = jnp.exp(m - m_new)
                va = jnp.where(hmask[h], v2, jnp.where(lane == aux[h], 1.0, 0.0).astype(v2.dtype))
                acc_new = acc * corr + _dot(p.astype(BF16), va)
                if lo:
                    m_new = jnp.concatenate([m_all[:lo], m_new], axis=0)
                    acc_new = jnp.concatenate([acc_all[:lo], acc_new], axis=0)
                out += [m_new, acc_new]
            return tuple(out)

        init = (jnp.full((tq, 1), -jnp.inf, F32), jnp.zeros((tq, 128), F32)) * 2
        carry = lax.fori_loop(0, qi, step, init)
        for d in range(r):
            carry = step(qi * r + d, carry, r0=d * tk)
        m0, acc0, m1, acc1 = carry
        l0 = acc0[:, aux[0]:aux[0] + 1]
        l1 = acc1[:, aux[1]:aux[1] + 1]
        o_ref[...] = jnp.where(hmask[0], acc0 / l0, acc1 / l1)
        lse_ref[:, 0:1] = m0 + jnp.log(l0)
        lse_ref[:, 1:2] = m1 + jnp.log(l1)
        pl.when(jnp.logical_and(hp == 3, qi == nq - 1))(wait)

    return pl.pallas_call(
        body, name="attn_fwd", grid=(4, nq),
        in_specs=[pl.BlockSpec((tq, 128), lambda h, i: (i, h)),
                  pl.BlockSpec((s, 128), lambda h, i: (0, 4 + h)),
                  pl.BlockSpec((s, 128), lambda h, i: (0, 8 + h)),
                  pl.BlockSpec((8, s), lambda h, i: (0, 0))] + [ANY_SPACE] * n,
        out_specs=[pl.BlockSpec((tq, 128), lambda h, i: (i, h)),
                   pl.BlockSpec((None, tq, 2), lambda h, i: (h, i, 0))] + [ANY_SPACE] * n,
        out_shape=[jax.ShapeDtypeStruct((s, ATTN_W), F32), jax.ShapeDtypeStruct((4, s, 2), F32)]
        + [jax.ShapeDtypeStruct((N_DEV,) + w.shape, w.dtype) for w in shards],
        scratch_shapes=_exchange_scratch(n),
        compiler_params=_params(("arbitrary", "arbitrary"), VMEM_BIG),
    )(qkv, qkv, qkv, cum_t, *shards)


def _gn_fwd(y, vec_ref, g_ref):
    mean = _dot3(y, g_ref)
    yc = y - mean
    var = _dot3(yc * yc, g_ref)
    rstd = lax.rsqrt(var + LN_EPS)
    yh = yc * rstd
    yn = yh * vec_ref[1:2, :] + vec_ref[2:3, :]
    return yh, rstd, yn


SHIFT_ROWS = HALO - 8
CONV_CHUNK = 32


def _shifted_copies(buf, sh, tm):
    for b in range(1, 8):
        sh[b - 1] = buf[pl.ds(b, tm + SHIFT_ROWS), :]


def _window(buf, sh, off, rows, r0=0):
    a, b = divmod(off, 8)
    if b == 0:
        return buf[pl.ds(8 * a + r0, rows), :]
    return sh[b - 1, pl.ds(8 * a + r0, rows), :]


def _conv_fwd(ag, wdw, vecs, gmat, tm):
    s = ag.shape[0]
    hb = tm // HALO

    def body(cur_ref, halo_ref, w_ref, vec_ref, g_ref, ypre_ref, conv_ref, buf, sh):
        i = pl.program_id(0)
        ug = cur_ref[:, 0:CONV_W] * _sigmoid(cur_ref[:, CONV_W:2 * CONV_W])
        ugh = halo_ref[:, 0:CONV_W] * _sigmoid(halo_ref[:, CONV_W:2 * CONV_W])
        buf[0:HALO, :] = jnp.where(i > 0, ugh, 0.0)
        buf[HALO:HALO + tm, :] = ug
        _shifted_copies(buf, sh, tm)
        y = jnp.broadcast_to(vec_ref[0:1, :], (tm, CONV_W))
        for j in range(CONV_K):
            y = y + w_ref[j:j + 1, :] * _window(buf, sh, HALO - (CONV_K - 1) + j, tm)
        ypre_ref[...] = y
        _, _, yn = _gn_fwd(y, vec_ref, g_ref)
        conv_ref[...] = yn * _sigmoid(yn)

    return pl.pallas_call(
        body, name="conv_fwd", grid=(s // tm,),
        in_specs=[pl.BlockSpec((tm, 2 * CONV_W), lambda i: (i, 0)),
                  pl.BlockSpec((HALO, 2 * CONV_W), lambda i: (jnp.maximum(i * hb - 1, 0), 0)),
                  pl.BlockSpec((32, CONV_W), lambda i: (0, 0)),
                  pl.BlockSpec((8, CONV_W), lambda i: (0, 0)),
                  pl.BlockSpec((CONV_W, CONV_W), lambda i: (0, 0))],
        out_specs=[pl.BlockSpec((tm, CONV_W), lambda i: (i, 0))] * 2,
        out_shape=[jax.ShapeDtypeStruct((s, CONV_W), F32)] * 2,
        scratch_shapes=[pltpu.VMEM((HALO + tm, CONV_W), F32), pltpu.VMEM((7, tm + SHIFT_ROWS, CONV_W), F32)],
        compiler_params=_params(("parallel",), VMEM_MID),
    )(ag, ag, wdw, vecs, gmat)


def _mix_fwd(o, conv, x, w_out, gvec, mod, lnv, tm):
    s = x.shape[0]

    def body(o_ref, c_ref, x_ref, w_ref, g_ref, mod_ref, ln_ref, cat_ref, mixed_ref, z1_ref, x1_ref):
        ov = o_ref[...]
        cv = c_ref[...]
        ra = ov * lax.rsqrt(jnp.mean(ov * ov, axis=1, keepdims=True) + LN_EPS) * g_ref[0:1, :]
        rc = cv * lax.rsqrt(jnp.mean(cv * cv, axis=1, keepdims=True) + LN_EPS) * g_ref[1:2, :]
        rab = ra.astype(BF16)
        rcb = rc.astype(BF16)
        cat_ref[:, 0:ATTN_W] = rab
        cat_ref[:, ATTN_W:D_MODEL] = rcb
        mixed = _dot(rab, w_ref[0:ATTN_W, :]) + _dot(rcb, w_ref[ATTN_W:D_MODEL, :])
        mixed_ref[...] = mixed
        z1 = ALPHA * x_ref[...] + (1.0 + mod_ref[2:3, :]) * mixed
        z1_ref[...] = z1
        xh, _ = _ln_norm(z1)
        x1_ref[...] = xh * ln_ref[0:1, :] + ln_ref[1:2, :]

    row = lambda w: pl.BlockSpec((tm, w), lambda i: (i, 0))
    return pl.pallas_call(
        body, name="mix_fwd", grid=(s // tm,),
        in_specs=[row(ATTN_W), row(CONV_W), row(D_MODEL), VMEM_WHOLE,
                  pl.BlockSpec((8, ATTN_W), lambda i: (0, 0)), pl.BlockSpec((8, D_MODEL), lambda i: (0, 0)),
                  pl.BlockSpec((8, D_MODEL), lambda i: (0, 0))],
        out_specs=[row(D_MODEL)] * 4,
        out_shape=[jax.ShapeDtypeStruct((s, D_MODEL), BF16)] + [jax.ShapeDtypeStruct((s, D_MODEL), F32)] * 3,
        compiler_params=_params(("parallel",), VMEM_MID),
    )(o, conv, x, w_out, gvec, mod, lnv)


def _ff_fwd(x1, tgt, w1, w2, mod, lnv, tm):
    s = x1.shape[0]

    def body(x1_ref, t_ref, w1_ref, w2_ref, mod_ref, ln_ref, u2_ref, hid_ref, dff_ref, dxa_ref, st_ref):
        i = pl.program_id(0)
        x1v = x1_ref[...]
        u2 = (x1v * (1.0 + mod_ref[4:5, :]) + mod_ref[3:4, :]).astype(BF16)
        u2_ref[...] = u2
        ff = jnp.zeros((tm, D_MODEL), F32)
        for d in range(N_DEV):
            h = _dot(u2, w1_ref[d])
            hr = jnp.maximum(h, 0.0)
            hb = (hr * hr).astype(BF16)
            hid_ref[:, d * FF_SHARD:(d + 1) * FF_SHARD] = hb
            ff = ff + _dot(hb, w2_ref[d * FF_SHARD:(d + 1) * FF_SHARD, :])
        gate = 1.0 + mod_ref[5:6, :]
        z2 = ALPHA * x1v + gate * ff
        xh, rstd = _ln_norm(z2)
        y = xh * ln_ref[2:3, :] + ln_ref[3:4, :]
        diff = y - t_ref[...]
        dy = diff * (1.0 / D_MODEL)
        dz2 = _ln_bwd(dy * ln_ref[2:3, :], xh, rstd)
        dff_ref[...] = (gate * dz2).astype(BF16)
        dxa_ref[...] = ALPHA * dz2

        @pl.when(i == 0)
        def _():
            st_ref[...] = jnp.zeros_like(st_ref)

        _acc_rows(st_ref, 0, _rowsum(dy * xh))
        _acc_rows(st_ref, 1, _rowsum(dy))
        _acc_rows(st_ref, 2, _rowsum(dz2 * ff))
        sq = _rowsum(jnp.sum(diff * diff, axis=1, keepdims=True))
        _acc_rows(st_ref, 3, jnp.broadcast_to(sq * (0.5 / D_MODEL), (1, D_MODEL)))

    row = lambda w: pl.BlockSpec((tm, w), lambda i: (i, 0))
    vec = pl.BlockSpec((8, D_MODEL), lambda i: (0, 0))
    return pl.pallas_call(
        body, name="ff_fwd", grid=(s // tm,),
        in_specs=[row(D_MODEL), row(D_MODEL), VMEM_WHOLE, VMEM_WHOLE, vec, vec],
        out_specs=[row(D_MODEL), row(D_FF), row(D_MODEL), row(D_MODEL), vec],
        out_shape=[jax.ShapeDtypeStruct((s, D_MODEL), BF16), jax.ShapeDtypeStruct((s, D_FF), BF16),
                   jax.ShapeDtypeStruct((s, D_MODEL), BF16), jax.ShapeDtypeStruct((s, D_MODEL), F32),
                   jax.ShapeDtypeStruct((8, D_MODEL), F32)],
        compiler_params=_params(("arbitrary",), VMEM_BIG),
    )(x1, tgt, w1, w2, mod, lnv)


def _ff_bwd(dff, hid, w1, w2, dxa, x1, z1, mixed, mod, lnv, tm):
    s = x1.shape[0]

    def body(dff_ref, hid_ref, w1_ref, w2_ref, dxa_ref, x1_ref, z1_ref, mixed_ref, mod_ref, ln_ref,
             dh_ref, dxb_ref, dmix_ref, st_ref):
        i = pl.program_id(0)
        dffv = dff_ref[...]
        du2 = jnp.zeros((tm, D_MODEL), F32)
        for d in range(N_DEV):
            cols = slice(d * FF_SHARD, (d + 1) * FF_SHARD)
            dhid = _dot_nt(dffv, w2_ref[cols, :])
            dh = (dhid * (2.0 * jnp.sqrt(hid_ref[:, cols].astype(F32)))).astype(BF16)
            dh_ref[:, cols] = dh
            du2 = du2 + _dot_nt(dh, w1_ref[d])
        x1v = x1_ref[...]
        dx1 = dxa_ref[...] + du2 * (1.0 + mod_ref[4:5, :])
        xh, rstd = _ln_norm(z1_ref[...])
        dz1 = _ln_bwd(dx1 * ln_ref[0:1, :], xh, rstd)
        dxb_ref[...] = ALPHA * dz1
        dmix_ref[...] = ((1.0 + mod_ref[2:3, :]) * dz1).astype(BF16)

        @pl.when(i == 0)
        def _():
            st_ref[...] = jnp.zeros_like(st_ref)

        _acc_rows(st_ref, 0, _rowsum(du2 * x1v))
        _acc_rows(st_ref, 1, _rowsum(du2))
        _acc_rows(st_ref, 2, _rowsum(dx1 * xh))
        _acc_rows(st_ref, 3, _rowsum(dx1))
        _acc_rows(st_ref, 4, _rowsum(dz1 * mixed_ref[...]))

    row = lambda w: pl.BlockSpec((tm, w), lambda i: (i, 0))
    vec = pl.BlockSpec((8, D_MODEL), lambda i: (0, 0))
    return pl.pallas_call(
        body, name="ff_bwd", grid=(s // tm,),
        in_specs=[row(D_MODEL), row(D_FF), VMEM_WHOLE, VMEM_WHOLE, row(D_MODEL), row(D_MODEL), row(D_MODEL),
                  row(D_MODEL), vec, vec],
        out_specs=[row(D_FF), row(D_MODEL), row(D_MODEL), vec],
        out_shape=[jax.ShapeDtypeStruct((s, D_FF), BF16), jax.ShapeDtypeStruct((s, D_MODEL), F32),
                   jax.ShapeDtypeStruct((s, D_MODEL), BF16), jax.ShapeDtypeStruct((8, D_MODEL), F32)],
        compiler_params=_params(("arbitrary",), VMEM_BIG),
    )(dff, hid, w1, w2, dxa, x1, z1, mixed, mod, lnv)


def _mix_bwd(dmix, w_out, o, conv, gvec, tm):
    s = o.shape[0]

    def body(dm_ref, w_ref, o_ref, c_ref, g_ref, do_ref, dc_ref, st_ref):
        i = pl.program_id(0)
        dmv = dm_ref[...]

        @pl.when(i == 0)
        def _():
            st_ref[...] = jnp.zeros_like(st_ref)

        for part, (src, dst) in enumerate(((o_ref, do_ref), (c_ref, dc_ref))):
            dr = _dot_nt(dmv, w_ref[part * ATTN_W:(part + 1) * ATTN_W, :])
            v = src[...]
            rr = lax.rsqrt(jnp.mean(v * v, axis=1, keepdims=True) + LN_EPS)
            vh = v * rr
            _acc_rows(st_ref, part, _rowsum(dr * vh))
            t = dr * g_ref[part:part + 1, :]
            dst[...] = rr * (t - vh * jnp.mean(t * vh, axis=1, keepdims=True))

    row = lambda w: pl.BlockSpec((tm, w), lambda i: (i, 0))
    vec = pl.BlockSpec((8, ATTN_W), lambda i: (0, 0))
    return pl.pallas_call(
        body, name="mix_bwd", grid=(s // tm,),
        in_specs=[row(D_MODEL), VMEM_WHOLE, row(ATTN_W), row(CONV_W), vec],
        out_specs=[row(ATTN_W), row(CONV_W), vec],
        out_shape=[jax.ShapeDtypeStruct((s, ATTN_W), F32), jax.ShapeDtypeStruct((s, CONV_W), F32),
                   jax.ShapeDtypeStruct((8, ATTN_W), F32)],
        compiler_params=_params(("arbitrary",), VMEM_MID),
    )(dmix, w_out, o, conv, gvec)


def _conv_bwd(dconv, ypre, ag, wdw, vecs, gmat, tm):
    s = ag.shape[0]
    hb = tm // HALO
    nt = s // tm
    last_halo = s // HALO - 1

    def body(dc_ref, dch_ref, yp_ref, yph_ref, cur_ref, w_ref, vec_ref, g_ref,
             dag_ref, st_ref, dw_ref, bufd, shd, dwacc):
        i = pl.program_id(0)

        def dyc_of(yp, dc):
            yh, rstd, yn = _gn_fwd(yp, vec_ref, g_ref)
            sg = _sigmoid(yn)
            dyn = dc * (sg * (1.0 + yn * (1.0 - sg)))
            dyh = dyn * vec_ref[1:2, :]
            dyc = rstd * (dyh - _dot3(dyh, g_ref) - yh * _dot3(dyh * yh, g_ref))
            return dyc, dyn, yh

        dyc, dyn, yh = dyc_of(yp_ref[...], dc_ref[...])
        dych, _, _ = dyc_of(yph_ref[...], dch_ref[...])
        bufd[0:tm, :] = dyc
        bufd[tm:tm + HALO, :] = jnp.where(i < nt - 1, dych, 0.0)

        @pl.when(i == 0)
        def _():
            st_ref[...] = jnp.zeros_like(st_ref)
            dw_ref[...] = jnp.zeros_like(dw_ref)

        _acc_rows(st_ref, 0, _rowsum(dyc))
        _acc_rows(st_ref, 1, _rowsum(dyn * yh))
        _acc_rows(st_ref, 2, _rowsum(dyn))

        _shifted_copies(bufd, shd, tm)
        dwacc[...] = jnp.zeros_like(dwacc)

        def chunk(c, carry):
            r0 = pl.multiple_of(c * CONV_CHUNK, CONV_CHUNK)
            a_c = cur_ref[pl.ds(r0, CONV_CHUNK), 0:CONV_W]
            sg_c = _sigmoid(cur_ref[pl.ds(r0, CONV_CHUNK), CONV_W:2 * CONV_W])
            ug_c = a_c * sg_c
            dug = jnp.zeros((CONV_CHUNK, CONV_W), F32)
            for j in range(CONV_K):
                win = _window(bufd, shd, CONV_K - 1 - j, CONV_CHUNK, r0)
                dug = dug + w_ref[j:j + 1, :] * win
                prod = ug_c * win
                part = prod[0:8]
                for g8 in range(1, CONV_CHUNK // 8):
                    part = part + prod[8 * g8:8 * g8 + 8]
                dwacc[j] = dwacc[j] + part
            dag_ref[pl.ds(r0, CONV_CHUNK), 0:CONV_W] = (dug * sg_c).astype(BF16)
            dag_ref[pl.ds(r0, CONV_CHUNK), CONV_W:2 * CONV_W] = (dug * a_c * sg_c * (1.0 - sg_c)).astype(BF16)
            return carry

        lax.fori_loop(0, tm // CONV_CHUNK, chunk, 0)
        for j in range(CONV_K):
            _acc_rows(dw_ref, j, _rowsum(dwacc[j]))

    row = lambda w: pl.BlockSpec((tm, w), lambda i: (i, 0))
    nxt = lambda w: pl.BlockSpec((HALO, w), lambda i: (jnp.minimum((i + 1) * hb, last_halo), 0))
    return pl.pallas_call(
        body, name="conv_bwd", grid=(nt,),
        in_specs=[row(CONV_W), nxt(CONV_W), row(CONV_W), nxt(CONV_W), row(2 * CONV_W),
                  pl.BlockSpec((32, CONV_W), lambda i: (0, 0)),
                  pl.BlockSpec((8, CONV_W), lambda i: (0, 0)),
                  pl.BlockSpec((CONV_W, CONV_W), lambda i: (0, 0))],
        out_specs=[row(2 * CONV_W), pl.BlockSpec((8, CONV_W), lambda i: (0, 0)),
                   pl.BlockSpec((32, CONV_W), lambda i: (0, 0))],
        out_shape=[jax.ShapeDtypeStruct((s, 2 * CONV_W), BF16), jax.ShapeDtypeStruct((8, CONV_W), F32),
                   jax.ShapeDtypeStruct((32, CONV_W), F32)],
        scratch_shapes=[pltpu.VMEM((tm + HALO, CONV_W), F32), pltpu.VMEM((7, tm + SHIFT_ROWS, CONV_W), F32),
                        pltpu.VMEM((32, 8, CONV_W), F32)],
        compiler_params=_params(("arbitrary",), VMEM_MID),
    )(dconv, dconv, ypre, ypre, ag, wdw, vecs, gmat)


def _attn_bwd(qkv, cum_t, o, do, lse, tq, tk, grads):
    s = qkv.shape[0]
    nq = s // tq
    r = tq // tk
    n = len(grads)

    def body(q_ref, k_ref, v_ref, cum_ref, o_ref, do_ref, lse_ref, *rest):
        dq_ref, dkt_ref, dvt_ref, dcum_ref, drow_ref = rest[n:n + 5]
        hp = pl.program_id(0)
        qi = pl.program_id(1)
        start, wait = _exchange(rest[:n], rest[n + 5:2 * n + 5], *rest[2 * n + 5:], gather=False)
        pl.when(jnp.logical_and(hp == 0, qi == 0))(start)

        @pl.when(qi == 0)
        def _():
            dkt_ref[...] = jnp.zeros_like(dkt_ref)
            dvt_ref[...] = jnp.zeros_like(dvt_ref)
            dcum_ref[...] = jnp.zeros_like(dcum_ref)

        qs = q_ref[...] * 0.125
        ov = o_ref[...]
        dob = do_ref[...].astype(BF16)
        lane = lax.broadcasted_iota(jnp.int32, (1, 128), 1)
        subl = lax.broadcasted_iota(jnp.int32, (128, 1), 0)
        hmask = (lane < 64, lane >= 64)
        aux = (64, 0)
        ones_aux = [jnp.where(lane == aux[h], 1.0, 0.0) for h in range(2)]
        qm, dom, delta, lse_h, qat, dot_t = [], [], [], [], [], []
        for h in range(2):
            qm.append(jnp.where(hmask[h], qs, jnp.zeros_like(qs)))
            dom.append(jnp.where(hmask[h], dob, jnp.zeros_like(dob)))
            delta.append(jnp.sum(dom[h].astype(F32) * ov, axis=1, keepdims=True))
            lse_h.append(lse_ref[:, h:h + 1])
            qat.append(jnp.where(hmask[h], qs.astype(F32), ones_aux[h]).T.astype(BF16))
            dot_t.append(dom[h].astype(F32).T.astype(BF16))

        def step(kb, carry, r0=None):
            lo = 0 if r0 is None else r0
            k0 = pl.multiple_of(kb * tk, tk)
            k2 = k_ref[pl.ds(k0, tk), :]
            v2 = v_ref[pl.ds(k0, tk), :]
            dq_new, a, b = [], [], []
            for h in range(2):
                cs = cum_ref[pl.ds(2 * hp + h, 1), pl.ds(k0, tk)]
                sc = _dot_nt(qm[h][lo:], k2) - cs
                p = jnp.exp(sc - lse_h[h][lo:])
                if r0 is not None:
                    visible = (lax.broadcasted_iota(jnp.int32, (tq - lo, tk), 1)
                               <= lax.broadcasted_iota(jnp.int32, (tq - lo, tk), 0))
                    p = jnp.where(visible, p, 0.0)
                dp = _dot_nt(dom[h][lo:], v2)
                ds = p * (dp - delta[h][lo:])
                pb = p.astype(BF16)
                dsb = ds.astype(BF16)
                a.append(_dot(qat[h][:, lo:], dsb))
                b.append(_dot(dot_t[h][:, lo:], pb))
                ka = jnp.where(hmask[h], k2, ones_aux[h].astype(k2.dtype))
                dq_h = carry[h][lo:] + _dot(dsb, ka)
                dq_new.append(jnp.concatenate([carry[h][:lo], dq_h], axis=0) if lo else dq_h)
            dkt_ref[:, pl.ds(k0, tk)] = dkt_ref[:, pl.ds(k0, tk)] + jnp.where(subl < 64, a[0], a[1])
            dvt_ref[:, pl.ds(k0, tk)] = dvt_ref[:, pl.ds(k0, tk)] + (b[0] + b[1])
            for h in range(2):
                dcum_ref[pl.ds(h, 1), pl.ds(k0, tk)] = (dcum_ref[pl.ds(h, 1), pl.ds(k0, tk)]
                                                        - a[h][aux[h]:aux[h] + 1, :])
            return tuple(dq_new)

        init = (jnp.zeros((tq, 128), F32),) * 2
        carry = lax.fori_loop(0, qi * r, step, init)
        for d in range(r):
            carry = step(qi * r + d, carry, r0=d * tk)
        dq0, dq1 = carry
        dq_ref[...] = (jnp.where(hmask[0], dq0, dq1) * 0.125).astype(BF16)
        drow_ref[:, 0:1] = dq0[:, aux[0]:aux[0] + 1]
        drow_ref[:, 1:2] = dq1[:, aux[1]:aux[1] + 1]
        pl.when(jnp.logical_and(hp == 3, qi == nq - 1))(wait)

    return pl.pallas_call(
        body, name="attn_bwd", grid=(4, nq),
        in_specs=[pl.BlockSpec((tq, 128), lambda h, i: (i, h)),
                  pl.BlockSpec((s, 128), lambda h, i: (0, 4 + h)),
                  pl.BlockSpec((s, 128), lambda h, i: (0, 8 + h)),
                  pl.BlockSpec((8, s), lambda h, i: (0, 0)),
                  pl.BlockSpec((tq, 128), lambda h, i: (i, h)),
                  pl.BlockSpec((tq, 128), lambda h, i: (i, h)),
                  pl.BlockSpec((None, tq, 2), lambda h, i: (h, i, 0))] + [ANY_SPACE] * n,
        out_specs=[pl.BlockSpec((tq, 128), lambda h, i: (i, h)),
                   pl.BlockSpec((128, s), lambda h, i: (h, 0)),
                   pl.BlockSpec((128, s), lambda h, i: (h, 0)),
                   pl.BlockSpec((None, 8, s), lambda h, i: (h, 0, 0)),
                   pl.BlockSpec((None, tq, 2), lambda h, i: (h, i, 0))] + [ANY_SPACE] * n,
        out_shape=[jax.ShapeDtypeStruct((s, ATTN_W), BF16), jax.ShapeDtypeStruct((ATTN_W, s), F32),
                   jax.ShapeDtypeStruct((ATTN_W, s), F32), jax.ShapeDtypeStruct((4, 8, s), F32),
                   jax.ShapeDtypeStruct((4, s, 2), F32)]
        + [jax.ShapeDtypeStruct(g.shape, g.dtype) for g in grads],
        scratch_shapes=_exchange_scratch(n),
        compiler_params=_params(("arbitrary", "arbitrary"), VMEM_BIG),
    )(qkv, qkv, qkv, cum_t, o, do, lse, *grads)


def _fgate_bwd(dcum_t, drow_t, sneg_t):
    s = dcum_t.shape[1]

    def body(dc_ref, dr_ref, sn_ref, df_ref, db_ref):
        lane = lax.broadcasted_iota(jnp.int32, (8, s), 1)
        acc = dc_ref[...] + dr_ref[...]
        sh = 1
        while sh < s:
            acc = acc + jnp.where(lane < s - sh, pltpu.roll(acc, s - sh, axis=1), 0.0)
            sh *= 2
        df = acc * sn_ref[...]
        df_ref[...] = df
        db_ref[...] = jnp.broadcast_to(jnp.sum(df, axis=1, keepdims=True), (8, 128))

    return pl.pallas_call(
        body, name="fgate_bwd",
        out_shape=[jax.ShapeDtypeStruct((8, s), F32), jax.ShapeDtypeStruct((8, 128), F32)],
        in_specs=[VMEM_WHOLE] * 3, out_specs=[VMEM_WHOLE] * 2,
    )(dcum_t, drow_t, sneg_t)


def _in_proj_bwd(dq, dkt, dvt, dag, dft, wqkv, wkvt, wag, wft, x, dxb, mod, tm):
    s = x.shape[0]

    def body(dq_ref, dkt_ref, dvt_ref, dag_ref, dft_ref, wqkv_ref, wkvt_ref, wag_ref, wft_ref, x_ref, dxb_ref,
             mod_ref, dkvt_ref, dx_ref, st_ref):
        i = pl.program_id(0)
        dkb = dkt_ref[...].astype(BF16)
        dvb = dvt_ref[...].astype(BF16)
        dkvt_ref[0:ATTN_W, :] = dkb
        dkvt_ref[ATTN_W:2 * ATTN_W, :] = dvb
        du = _dot_nt(dq_ref[...], wqkv_ref[:, 0:ATTN_W])
        du = du + _dot_tn(dkb, wkvt_ref[0:ATTN_W, :])
        du = du + _dot_tn(dvb, wkvt_ref[ATTN_W:2 * ATTN_W, :])
        du = du + _dot_nt(dag_ref[...], wag_ref[...])
        du = du + _dot_tn(dft_ref[...].astype(BF16), wft_ref[...])
        xv = x_ref[...]
        dx_ref[...] = dxb_ref[...] + du * (1.0 + mod_ref[1:2, :])

        @pl.when(i == 0)
        def _():
            st_ref[...] = jnp.zeros_like(st_ref)

        _acc_rows(st_ref, 0, _rowsum(du * xv))
        _acc_rows(st_ref, 1, _rowsum(du))

    row = lambda w: pl.BlockSpec((tm, w), lambda i: (i, 0))
    col = lambda h: pl.BlockSpec((h, tm), lambda i: (0, i))
    vec = pl.BlockSpec((8, D_MODEL), lambda i: (0, 0))
    return pl.pallas_call(
        body, name="in_proj_bwd", grid=(s // tm,),
        in_specs=[row(ATTN_W), col(ATTN_W), col(ATTN_W), row(2 * CONV_W), col(8),
                  VMEM_WHOLE, VMEM_WHOLE, VMEM_WHOLE, VMEM_WHOLE, row(D_MODEL), row(D_MODEL), vec],
        out_specs=[col(2 * ATTN_W), row(D_MODEL), vec],
        out_shape=[jax.ShapeDtypeStruct((2 * ATTN_W, s), BF16), jax.ShapeDtypeStruct((s, D_MODEL), F32),
                   jax.ShapeDtypeStruct((8, D_MODEL), F32)],
        compiler_params=_params(("arbitrary",), VMEM_MID),
    )(dq, dkt, dvt, dag, dft, wqkv, wkvt, wag, wft, x, dxb, mod)


def _wgrad_nn(a, b, name, tm, tn, tk):
    m, s = a.shape
    n = b.shape[1]
    nk = s // tk

    def body(a_ref, b_ref, o_ref, acc_ref):
        k = pl.program_id(2)

        @pl.when(k == 0)
        def _():
            acc_ref[...] = jnp.zeros_like(acc_ref)

        acc_ref[...] += _dot(a_ref[...], b_ref[...])

        @pl.when(k == nk - 1)
        def _():
            o_ref[...] = acc_ref[...].astype(o_ref.dtype)

    return pl.pallas_call(
        body, name=name, grid=(m // tm, n // tn, nk),
        in_specs=[pl.BlockSpec((tm, tk), lambda i, j, k: (i, k)), pl.BlockSpec((tk, tn), lambda i, j, k: (k, j))],
        out_specs=pl.BlockSpec((tm, tn), lambda i, j, k: (i, j)),
        out_shape=jax.ShapeDtypeStruct((m, n), BF16),
        scratch_shapes=[pltpu.VMEM((tm, tn), F32)],
        compiler_params=_params(("parallel", "parallel", "arbitrary"), VMEM_MID),
    )(a, b)


def _wgrad_tn(a, b, name, tm, tn, tk, out_shape, out_spec):
    s, m = a.shape
    n = b.shape[1]
    nk = s // tk

    def body(a_ref, b_ref, o_ref, acc_ref):
        k = pl.program_id(2)

        @pl.when(k == 0)
        def _():
            acc_ref[...] = jnp.zeros_like(acc_ref)

        acc_ref[...] += _dot_tn(a_ref[...], b_ref[...])

        @pl.when(k == nk - 1)
        def _():
            o_ref[...] = acc_ref[...].astype(o_ref.dtype)

    return pl.pallas_call(
        body, name=name, grid=(m // tm, n // tn, nk),
        in_specs=[pl.BlockSpec((tk, tm), lambda i, j, k: (k, i)), pl.BlockSpec((tk, tn), lambda i, j, k: (k, j))],
        out_specs=out_spec, out_shape=out_shape,
        scratch_shapes=[pltpu.VMEM((tm, tn), F32)],
        compiler_params=_params(("parallel", "parallel", "arbitrary"), VMEM_MID),
    )(a, b)


def _wgrad_f(dft, u, tk):
    s = u.shape[0]
    nk = s // tk

    def body(d_ref, u_ref, o_ref, acc_ref):
        k = pl.program_id(0)

        @pl.when(k == 0)
        def _():
            acc_ref[...] = jnp.zeros_like(acc_ref)

        acc_ref[...] += _dot(d_ref[...].astype(BF16), u_ref[...])

        @pl.when(k == nk - 1)
        def _():
            o_ref[...] = acc_ref[...].astype(BF16)

    return pl.pallas_call(
        body, name="wgrad_f", grid=(nk,),
        in_specs=[pl.BlockSpec((8, tk), lambda k: (0, k)), pl.BlockSpec((tk, D_MODEL), lambda k: (k, 0))],
        out_specs=pl.BlockSpec((8, D_MODEL), lambda k: (0, 0)),
        out_shape=jax.ShapeDtypeStruct((8, D_MODEL), BF16),
        scratch_shapes=[pltpu.VMEM((8, D_MODEL), F32)],
        compiler_params=_params(("arbitrary",)),
    )(dft, u)


def _adamw_big(recv, w, m, v, name, tr):
    r, cdim = w.shape
    slots = recv.shape[0]

    def body(g_ref, w_ref, m_ref, v_ref, go_ref, d_ref, mo_ref, vo_ref):
        g = g_ref[0].astype(F32)
        for k in range(1, slots):
            g = g + g_ref[k].astype(F32)
        delta, m2, v2 = _adamw(w_ref[...], g, m_ref[...], v_ref[...])
        go_ref[...] = g
        d_ref[...] = delta
        mo_ref[...] = m2
        vo_ref[...] = v2

    blk = pl.BlockSpec((tr, cdim), lambda i: (i, 0))
    return pl.pallas_call(
        body, name=name, grid=(r // tr,),
        in_specs=[pl.BlockSpec((slots, tr, cdim), lambda i: (0, i, 0)), blk, blk, blk],
        out_specs=[blk] * 4, out_shape=[jax.ShapeDtypeStruct((r, cdim), F32)] * 4,
        compiler_params=_params(("parallel",), VMEM_MID),
    )(recv, w, m, v)


def _adamw_ada(c_all, dada, w, m, v, tr):
    r, cdim = w.shape

    def body(c_ref, d_ref, w_ref, m_ref, v_ref, go_ref, dl_ref, mo_ref, vo_ref):
        cv = c_ref[...]
        sc = (cv * _sigmoid(cv)).astype(BF16)
        g = _dot_tn(sc, d_ref[...].astype(BF16))
        delta, m2, v2 = _adamw(w_ref[...], g, m_ref[...], v_ref[...])
        go_ref[...] = g
        dl_ref[...] = delta
        mo_ref[...] = m2
        vo_ref[...] = v2

    blk = pl.BlockSpec((tr, cdim), lambda i: (i, 0))
    return pl.pallas_call(
        body, name="adamw_ada", grid=(r // tr,),
        in_specs=[pl.BlockSpec((N_DEV, tr), lambda i: (0, i)), pl.BlockSpec((N_DEV, cdim), lambda i: (0, 0)),
                  blk, blk, blk],
        out_specs=[blk] * 4, out_shape=[jax.ShapeDtypeStruct((r, cdim), F32)] * 4,
        compiler_params=_params(("parallel",), VMEM_MID),
    )(c_all, dada, w, m, v)


def _adamw_small(parts, w, m, v):
    n = w.shape[1]

    def body(g_ref, w_ref, m_ref, v_ref, go_ref, d_ref, mo_ref, vo_ref):
        g = g_ref[0]
        for k in range(1, N_DEV):
            g = g + g_ref[k]
        delta, m2, v2 = _adamw(w_ref[...], g, m_ref[...], v_ref[...])
        go_ref[...] = g
        d_ref[...] = delta
        mo_ref[...] = m2
        vo_ref[...] = v2

    return pl.pallas_call(
        body, name="adamw_small", out_shape=[jax.ShapeDtypeStruct((1, n), F32)] * 4,
        in_specs=[VMEM_WHOLE] * 4, out_specs=[VMEM_WHOLE] * 4,
    )(parts, w, m, v)


def _pad_cols(a, n):
    return jnp.pad(a, ((0, 0), (0, n - a.shape[1])))


def _pad_rows(a, n):
    return jnp.pad(a, ((0, n - a.shape[0]), (0, 0)))


def kernel(x, c, w_ada, b_ada, w_in, b_forget, w_dw, b_dw, gn_g, gn_b, g_attn_out, g_conv_out, w_out, ln1_g, ln1_b, w_ff1, w_ff2, ln2_g, ln2_b, loss_target, m_w_ada, m_b_ada, m_w_in, m_b_forget, m_w_dw, m_b_dw, m_gn_g, m_gn_b, m_g_attn_out, m_g_conv_out, m_w_out, m_ln1_g, m_ln1_b, m_w_ff1, m_w_ff2, m_ln2_g, m_ln2_b, v_w_ada, v_b_ada, v_w_in, v_b_forget, v_w_dw, v_b_dw, v_gn_g, v_gn_b, v_g_attn_out, v_g_conv_out, v_w_out, v_ln1_g, v_ln1_b, v_w_ff1, v_w_ff2, v_ln2_g, v_ln2_b):
    s = x.shape[1]
    tm = min(ROW_TILE, s)
    tk_att = min(ATT_TILE, s)
    tq_att = min(ATT_Q_BLOCKS * tk_att, s)
    me = 4 * lax.axis_index("x") + 2 * lax.axis_index("y") + lax.axis_index("c")
    xs = x[0]
    tgt = loss_target[0]

    dw_n = CONV_K * DW_SHARD
    pack1 = jnp.concatenate([c, w_dw[0].reshape(1, dw_n), jnp.zeros((1, 2048 - dw_n), F32)], axis=1)
    gw_in, g1 = _gather_weights([_pad_cols(w_in[0], IN_SHARD_PAD).astype(BF16)], pack1)
    g1 = g1[:, 0, :]
    c_all = g1[:, :D_MODEL]
    wdw_full = g1[:, D_MODEL:D_MODEL + dw_n].reshape(N_DEV, CONV_K, DW_SHARD)
    wdw_full = _pad_rows(wdw_full.transpose(1, 0, 2).reshape(CONV_K, CONV_W), 32)

    b_shard = lax.dynamic_slice(b_ada, (0, me * ADA_SHARD), (1, ADA_SHARD))
    ada_all = _gather_small(_ada_part(c_all, w_ada[0], b_shard), "gather_ada")
    ada = lax.dynamic_index_in_dim(ada_all, me, axis=1, keepdims=False).reshape(6, D_MODEL)
    mod = _pad_rows(ada, 8)

    w_in_full = gw_in[:, :, :IN_SHARD].transpose(1, 0, 2).reshape(D_MODEL, N_IN)
    wqkv = w_in_full[:, :3 * ATTN_W]
    wft = w_in_full[:, 3 * ATTN_W:3 * ATTN_W + N_HEADS].T
    wag = w_in_full[:, 3 * ATTN_W + N_HEADS:]

    qkv, ag, ft, u = _in_proj(xs, mod, wqkv, wag, wft, tm)
    cum_t, sneg_t = _fgate(ft, b_forget.reshape(N_HEADS, 1))
    o, lse, gw_out, gw_ff1, gw_ff2 = _attn_fwd(
        qkv, cum_t, tq_att, tk_att, [w_out[0].astype(BF16), w_ff1[0].astype(BF16), w_ff2[0].astype(BF16)])
    w_out_full = gw_out.reshape(D_MODEL, D_MODEL)
    w2_full = gw_ff2.reshape(D_FF, D_MODEL)
    cvec = _pad_rows(jnp.concatenate([b_dw, gn_g, gn_b], axis=0), 8)
    grp = jnp.arange(CONV_W) // 64
    gmat = jnp.where(grp[:, None] == grp[None, :], 1.0 / 64.0, 0.0).astype(BF16)
    ypre, conv = _conv_fwd(ag, wdw_full, cvec, gmat, tm)
    gvec = _pad_rows(jnp.concatenate([g_attn_out, g_conv_out], axis=0), 8)
    lnv = _pad_rows(jnp.concatenate([ln1_g, ln1_b, ln2_g, ln2_b], axis=0), 8)
    cat, mixed, z1, x1 = _mix_fwd(o, conv, xs, w_out_full, gvec, mod, lnv, tm)
    tf = min(FF_TILE, s)
    u2, hid, dff, dxa, st_ff = _ff_fwd(x1, tgt, gw_ff1, w2_full, mod, lnv, min(2 * FF_TILE, s))

    dh, dxb, dmix, st_fb = _ff_bwd(dff, hid, gw_ff1, w2_full, dxa, x1, z1, mixed, mod, lnv, tf)
    do, dconv, st_mix = _mix_bwd(dmix, w_out_full, o, conv, gvec, tm)
    dag, st_conv, dwdw = _conv_bwd(dconv, ypre, ag, wdw_full, cvec, gmat, tm)

    tk = min(WGRAD_K_TILE, s)
    g_ff2 = _wgrad_tn(hid, dff, "wgrad_ff2", 512, 1024, tk,
                      jax.ShapeDtypeStruct((D_FF, D_MODEL), BF16),
                      pl.BlockSpec((512, 1024), lambda i, j, k: (i, j))).reshape(N_DEV, FF_SHARD, D_MODEL)
    g_ff1 = _wgrad_tn(u2, dh, "wgrad_ff1", 1024, 512, tk,
                      jax.ShapeDtypeStruct((N_DEV, D_MODEL, FF_SHARD), BF16),
                      pl.BlockSpec((None, 1024, 512), lambda i, j, k: (j, i, 0)))
    g_out = _wgrad_tn(cat, dmix, "wgrad_out", 512, 1024, tk,
                      jax.ShapeDtypeStruct((D_MODEL, D_MODEL), BF16),
                      pl.BlockSpec((512, 1024), lambda i, j, k: (i, j))).reshape(N_DEV, OUT_SHARD, D_MODEL)
    dq, dkt, dvt, dcum, drow, r_out, r_ff1, r_ff2 = _attn_bwd(
        qkv, cum_t, o, do, lse, tq_att, tk_att, [g_out, g_ff1, g_ff2])
    dft, dbf = _fgate_bwd(dcum[:, :2, :].reshape(N_HEADS, s), drow.transpose(0, 2, 1).reshape(N_HEADS, s), sneg_t)
    wkvt = wqkv[:, ATTN_W:3 * ATTN_W].T
    dkvt, grad_x, st_in = _in_proj_bwd(dq, dkt, dvt, dag, dft, wqkv, wkvt, wag, wft, xs, dxb, mod, tm)

    g_q = _wgrad_tn(u, dq, "wgrad_q", 1024, 512, tk,
                    jax.ShapeDtypeStruct((D_MODEL, ATTN_W), BF16),
                    pl.BlockSpec((1024, 512), lambda i, j, k: (i, j)))
    g_kvt = _wgrad_nn(dkvt, u, "wgrad_kv", 512, 1024, tk)
    g_qkv = jnp.concatenate([g_q, g_kvt.T], axis=1)
    g_ag = _wgrad_tn(u, dag, "wgrad_ag", 1024, 512, tk,
                     jax.ShapeDtypeStruct((D_MODEL, 2 * CONV_W), BF16),
                     pl.BlockSpec((1024, 512), lambda i, j, k: (i, j)))
    g_ft = _wgrad_f(dft, u, tk)
    g_in = jnp.concatenate([g_qkv, g_ft.T, g_ag], axis=1)
    g_in = jnp.pad(g_in.reshape(D_MODEL, N_DEV, IN_SHARD).transpose(1, 0, 2),
                   ((0, 0), (0, 0), (0, IN_SHARD_PAD - IN_SHARD)))

    dada = jnp.concatenate([st_in[1:2], st_in[0:1], st_fb[4:5], st_fb[1:2], st_fb[0:1], st_ff[2:3]], axis=1)
    loss_part = st_ff[3:4, 0:1]
    pack2 = jnp.concatenate([
        dada,
        _pad_cols(dbf[:, 0].reshape(1, N_HEADS), 128),
        dwdw[:CONV_K].reshape(1, CONV_K * CONV_W),
        st_conv[0:1], st_conv[1:2], st_conv[2:3], st_mix[0:1], st_mix[1:2],
        st_fb[2:3], st_fb[3:4], st_ff[0:1], st_ff[1:2],
        _pad_cols(loss_part, 128)], axis=1)
    r_in, g2 = _scatter_grads(g_in, pack2)
    g2 = g2[:, 0, :]
    o_bf = 6 * D_MODEL
    o_dw = o_bf + 128
    o_v = o_dw + CONV_K * CONV_W
    o_ln = o_v + 5 * CONV_W
    o_loss = o_ln + 4 * D_MODEL
    dw_parts = lax.dynamic_slice_in_dim(
        g2[:, o_dw:o_v].reshape(N_DEV, CONV_K, N_DEV, DW_SHARD), me, 1, axis=2).reshape(N_DEV, dw_n)
    parts = jnp.concatenate([
        g2[:, :o_dw], _pad_cols(dw_parts, 2048), g2[:, o_v:o_loss + 128]], axis=1)[:, None, :]

    def pack_w(b_ada_, b_forget_, w_dw_, smalls):
        return jnp.concatenate([b_ada_, _pad_cols(b_forget_, 128), _pad_cols(w_dw_[0].reshape(1, dw_n), 2048)]
                               + smalls + [jnp.zeros((1, 128), F32)], axis=1)

    pw = pack_w(b_ada, b_forget, w_dw, [b_dw, gn_g, gn_b, g_attn_out, g_conv_out, ln1_g, ln1_b, ln2_g, ln2_b])
    pm = pack_w(m_b_ada, m_b_forget, m_w_dw, [m_b_dw, m_gn_g, m_gn_b, m_g_attn_out, m_g_conv_out,
                                              m_ln1_g, m_ln1_b, m_ln2_g, m_ln2_b])
    pv = pack_w(v_b_ada, v_b_forget, v_w_dw, [v_b_dw, v_gn_g, v_gn_b, v_g_attn_out, v_g_conv_out,
                                              v_ln1_g, v_ln1_b, v_ln2_g, v_ln2_b])
    small = _adamw_small(parts, pw, pm, pv)

    p_bf = 6 * D_MODEL
    p_dw = p_bf + 128
    p_v = p_dw + 2048
    p_ln = p_v + 5 * CONV_W
    p_loss = p_ln + 4 * D_MODEL

    def unpack(t):
        outs = {"b_ada": t[:, :p_bf], "b_forget": t[:, p_bf:p_bf + N_HEADS],
                "w_dw": t[:, p_dw:p_dw + dw_n].reshape(1, CONV_K, 1, DW_SHARD)}
        for k, nm in enumerate(["b_dw", "gn_g", "gn_b", "g_attn_out", "g_conv_out"]):
            outs[nm] = t[:, p_v + k * CONV_W:p_v + (k + 1) * CONV_W]
        for k, nm in enumerate(["ln1_g", "ln1_b", "ln2_g", "ln2_b"]):
            outs[nm] = t[:, p_ln + k * D_MODEL:p_ln + (k + 1) * D_MODEL]
        return outs

    sm = [unpack(t) for t in small]
    loss = small[0][0, p_loss]

    dada_all = g2[:, :6 * D_MODEL]
    dada_shard = lax.dynamic_slice_in_dim(dada_all, me * ADA_SHARD, ADA_SHARD, axis=1)
    big = {
        "w_ada": _adamw_ada(c_all, dada_shard, w_ada[0], m_w_ada[0], v_w_ada[0], 256),
        "w_in": [t[:, :IN_SHARD] for t in _adamw_big(
            r_in, _pad_cols(w_in[0], IN_SHARD_PAD), _pad_cols(m_w_in[0], IN_SHARD_PAD),
            _pad_cols(v_w_in[0], IN_SHARD_PAD), "adamw_in", 256)],
        "w_out": _adamw_big(r_out, w_out[0], m_w_out[0], v_w_out[0], "adamw_out", 128),
        "w_ff1": _adamw_big(r_ff1, w_ff1[0], m_w_ff1[0], v_w_ff1[0], "adamw_ff1", 256),
        "w_ff2": _adamw_big(r_ff2, w_ff2[0], m_w_ff2[0], v_w_ff2[0], "adamw_ff2", 256),
    }

    names = ["w_ada", "b_ada", "w_in", "b_forget", "w_dw", "b_dw", "gn_g", "gn_b", "g_attn_out", "g_conv_out",
             "w_out", "ln1_g", "ln1_b", "w_ff1", "w_ff2", "ln2_g", "ln2_b"]

    def leaf(kind, nm):
        if nm in big:
            return big[nm][kind][None]
        return sm[kind][nm]

    outs = [loss, grad_x[None]]
    for kind in range(4):
        outs += [leaf(kind, nm) for nm in names]
    return tuple(outs)
```

```python
import jax
import jax.numpy as jnp
from jax import lax
from jax.experimental import pallas as pl
from jax.experimental.pallas import tpu as pltpu

F32 = jnp.float32
BF16 = jnp.bfloat16

D_MODEL = 1024
ATTN_W = 512
CONV_W = 512
N_HEADS = 8
CONV_K = 31
D_FF = 4096
N_DEV = 8
N_IN = 3 * ATTN_W + N_HEADS + 2 * CONV_W
IN_SHARD = N_IN // N_DEV
IN_SHARD_PAD = 384
ADA_SHARD = 6 * D_MODEL // N_DEV
FF_SHARD = D_FF // N_DEV
OUT_SHARD = D_MODEL // N_DEV
DW_SHARD = CONV_W // N_DEV
HALO = 32
LN_EPS = 1e-5
ALPHA = 2.0 ** 0.25
ADAM_LR, ADAM_B1, ADAM_B2, ADAM_EPS, ADAM_WD, ADAM_STEP = 0.001, 0.9, 0.999, 1e-08, 0.01, 10

ROW_TILE = 512
FF_TILE = 256
WGRAD_K_TILE = 4096
ATT_TILE = 512
ATT_Q_BLOCKS = 2
VMEM_BIG = 56 * 1024 * 1024
VMEM_MID = 40 * 1024 * 1024

MESH = pl.DeviceIdType.MESH
VMEM_WHOLE = pl.BlockSpec(memory_space=pltpu.VMEM)
ANY_SPACE = pl.BlockSpec(memory_space=pl.ANY)


def _dot(a, b):
    return jnp.dot(a, b, preferred_element_type=F32)


def _dot_nt(a, b):
    return lax.dot_general(a, b, (((1,), (1,)), ((), ())), preferred_element_type=F32)


def _dot_tn(a, b):
    return lax.dot_general(a, b, (((0,), (0,)), ((), ())), preferred_element_type=F32)


def _dot_split(xf, g_ref):
    g = g_ref[...]
    hi = xf.astype(BF16)
    lo = (xf - hi.astype(F32)).astype(BF16)
    return _dot(hi, g) + _dot(lo, g)


def _sigmoid(x):
    return 1.0 / (1.0 + jnp.exp(-x))


def _rowsum(x):
    return jnp.sum(x, axis=0, keepdims=True)


def _ln_norm(z):
    mu = jnp.mean(z, axis=1, keepdims=True)
    zc = z - mu
    var = jnp.mean(zc * zc, axis=1, keepdims=True)
    rstd = lax.rsqrt(var + LN_EPS)
    return zc * rstd, rstd


def _ln_bwd(dxh, xh, rstd):
    m1 = jnp.mean(dxh, axis=1, keepdims=True)
    m2 = jnp.mean(dxh * xh, axis=1, keepdims=True)
    return rstd * (dxh - m1 - xh * m2)


def _adamw(w, g, m, v):
    m2 = ADAM_B1 * m + (1.0 - ADAM_B1) * g
    v2 = ADAM_B2 * v + (1.0 - ADAM_B2) * (g * g)
    m_hat = m2 / (1.0 - ADAM_B1 ** ADAM_STEP)
    v_hat = v2 / (1.0 - ADAM_B2 ** ADAM_STEP)
    delta = -ADAM_LR * (m_hat / (jnp.sqrt(v_hat) + ADAM_EPS) + ADAM_WD * w)
    return delta, m2, v2


def _params(sem=None, vmem=None):
    kw = {}
    if sem is not None:
        kw["dimension_semantics"] = sem
    if vmem is not None:
        kw["vmem_limit_bytes"] = vmem
    return pltpu.CompilerParams(**kw)


def _coords():
    return lax.axis_index("x"), lax.axis_index("y"), lax.axis_index("c")


def _flip(v, bit):
    return 1 - v if bit else v


def _acc_rows(ref, r, val):
    ref[r:r + 1, :] = ref[r:r + 1, :] + val


def _small_gather(v_ref, out_ref, send_sems, recv_sems, local_sem):
    x, y, c = _coords()
    me = 4 * x + 2 * y + c
    mine = pltpu.make_async_copy(v_ref, out_ref.at[me], local_sem)
    sends, recvs = [], []
    for k in range(1, N_DEV):
        px, py, pc = _flip(x, (k >> 2) & 1), _flip(y, (k >> 1) & 1), _flip(c, k & 1)
        peer = 4 * px + 2 * py + pc
        sems = dict(send_sem=send_sems.at[k - 1], recv_sem=recv_sems.at[k - 1],
                    device_id=(px, py, pc), device_id_type=MESH)
        sends.append(pltpu.make_async_remote_copy(src_ref=v_ref, dst_ref=out_ref.at[me], **sems))
        recvs.append(pltpu.make_async_remote_copy(src_ref=v_ref, dst_ref=out_ref.at[peer], **sems))

    def start():
        mine.start()
        for cp in sends:
            cp.start()

    def wait():
        for cp in recvs:
            cp.wait_recv()
        for cp in sends:
            cp.wait_send()
        mine.wait()

    return start, wait


SMALL_GATHER_SCRATCH = [pltpu.SemaphoreType.DMA((N_DEV - 1,)), pltpu.SemaphoreType.DMA((N_DEV - 1,)),
                        pltpu.SemaphoreType.DMA]


def _gather_small(v, name):
    r, n = v.shape

    def body(v_ref, out_ref, send_sems, recv_sems, local_sem):
        start, wait = _small_gather(v_ref, out_ref, send_sems, recv_sems, local_sem)
        start()
        wait()

    return pl.pallas_call(
        body, name=name,
        out_shape=jax.ShapeDtypeStruct((N_DEV, r, n), v.dtype),
        in_specs=[VMEM_WHOLE], out_specs=VMEM_WHOLE,
        scratch_shapes=SMALL_GATHER_SCRATCH,
    )(v)


def _gather_weights(shards, small):
    n = len(shards)

    def body(*refs):
        ins, small_ref = refs[:n], refs[n]
        outs, small_out = refs[n + 1:2 * n + 1], refs[2 * n + 1]
        send_sems, recv_sems, local_sems = refs[2 * n + 2:2 * n + 5]
        start_small, wait_small = _small_gather(small_ref, small_out, *refs[2 * n + 5:])
        start_small()
        x, y, c = _coords()
        me, sibling = (x, y, c), (x, y, 1 - c)
        chips = [(1 - x, y), (x, 1 - y), (1 - x, 1 - y)]

        def slot(a, pos):
            return outs[a].at[4 * pos[0] + 2 * pos[1] + pos[2]]

        def copy(a, k, block, to, src=None):
            return pltpu.make_async_remote_copy(
                src_ref=slot(a, block) if src is None else src, dst_ref=slot(a, block),
                send_sem=send_sems.at[7 * a + k], recv_sem=recv_sems.at[7 * a + k],
                device_id=to, device_id_type=MESH)

        started = []
        local = []
        for a in range(n):
            mine = pltpu.make_async_copy(ins[a], slot(a, me), local_sems.at[a])
            mine.start()
            local.append(mine)
        for j, chip in enumerate(chips):
            for a in range(n):
                cp = copy(a, 1 + j, me, (*chip, c), src=ins[a])
                cp.start()
                started.append(cp)
        for a in range(n):
            cp = copy(a, 0, me, sibling, src=ins[a])
            cp.start()
            started.append(cp)
        for j, chip in enumerate(chips):
            for a in range(n):
                copy(a, 1 + j, (*chip, c), me).wait_recv()
                cp = copy(a, 4 + j, (*chip, c), sibling)
                cp.start()
                started.append(cp)
        for a in range(n):
            copy(a, 0, sibling, me).wait_recv()
        for j, chip in enumerate(chips):
            for a in range(n):
                copy(a, 4 + j, (*chip, 1 - c), me).wait_recv()
        for cp in started:
            cp.wait_send()
        for mine in local:
            mine.wait()
        wait_small()

    return pl.pallas_call(
        body, name="gather_weights",
        out_shape=[jax.ShapeDtypeStruct((N_DEV,) + s.shape, s.dtype) for s in shards]
        + [jax.ShapeDtypeStruct((N_DEV,) + small.shape, small.dtype)],
        in_specs=[ANY_SPACE] * n + [VMEM_WHOLE], out_specs=[ANY_SPACE] * n + [VMEM_WHOLE],
        scratch_shapes=[pltpu.SemaphoreType.DMA((7 * n,)), pltpu.SemaphoreType.DMA((7 * n,)),
                        pltpu.SemaphoreType.DMA((n,))] + SMALL_GATHER_SCRATCH,
    )(*shards, small)


def _exchange(ins, outs, send_sems, recv_sems, local_sems, gather):
    n = len(ins)
    x, y, c = _coords()
    me = 4 * x + 2 * y + c
    local = [pltpu.make_async_copy(ins[a] if gather else ins[a].at[me], outs[a].at[me], local_sems.at[a])
             for a in range(n)]
    sends, recvs = [], []
    for k in range(1, N_DEV):
        px, py, pc = _flip(x, (k >> 2) & 1), _flip(y, (k >> 1) & 1), _flip(c, k & 1)
        peer = 4 * px + 2 * py + pc
        for a in range(n):
            src = ins[a] if gather else ins[a].at[peer]
            sems = dict(send_sem=send_sems.at[7 * a + k - 1], recv_sem=recv_sems.at[7 * a + k - 1],
                        device_id=(px, py, pc), device_id_type=MESH)
            sends.append(pltpu.make_async_remote_copy(src_ref=src, dst_ref=outs[a].at[me], **sems))
            recvs.append(pltpu.make_async_remote_copy(src_ref=src, dst_ref=outs[a].at[peer], **sems))

    def start():
        for cp in local + sends:
            cp.start()

    def wait():
        for cp in recvs:
            cp.wait_recv()
        for cp in sends:
            cp.wait_send()
        for cp in local:
            cp.wait()

    return start, wait


def _exchange_scratch(n):
    return [pltpu.SemaphoreType.DMA((7 * n,)), pltpu.SemaphoreType.DMA((7 * n,)), pltpu.SemaphoreType.DMA((n,))]


def _scatter_grads(g, small):
    _, rows, cols = g.shape
    halves = 4

    def body(g_ref, small_ref, out_ref, small_out, sib_buf, own_buf, ps_buf,
             d2d_send, d2d_recv, ici_send, ici_recv, load_sems, store_sem, *small_sems):
        start_small, wait_small = _small_gather(small_ref, small_out, *small_sems)
        start_small()
        x, y, c = _coords()
        sibling = (x, y, 1 - c)
        chips = [(x, y), (1 - x, y), (x, 1 - y), (1 - x, 1 - y)]
        swaps, loads = [], []
        for j, (px, py) in enumerate(chips):
            cp = pltpu.make_async_remote_copy(
                src_ref=g_ref.at[4 * px + 2 * py + (1 - c)], dst_ref=sib_buf.at[j],
                send_sem=d2d_send.at[j], recv_sem=d2d_recv.at[j], device_id=sibling, device_id_type=MESH)
            cp.start()
            swaps.append(cp)
            ld = pltpu.make_async_copy(g_ref.at[4 * px + 2 * py + c], own_buf.at[j], load_sems.at[j])
            ld.start()
            loads.append(ld)
        for j in range(halves):
            swaps[j].wait_recv()
            loads[j].wait()
            ps_buf[j] = (own_buf[j].astype(F32) + sib_buf[j].astype(F32)).astype(ps_buf.dtype)
        mine = pltpu.make_async_copy(ps_buf.at[0], out_ref.at[0], store_sem)
        mine.start()
        sends = []
        for j in range(1, halves):
            cp = pltpu.make_async_remote_copy(
                src_ref=ps_buf.at[j], dst_ref=out_ref.at[j], send_sem=ici_send.at[j - 1],
                recv_sem=ici_recv.at[j - 1], device_id=(*chips[j], c), device_id_type=MESH)
            cp.start()
            sends.append(cp)
        for cp in sends:
            cp.wait_recv()
        for cp in sends + swaps:
            cp.wait_send()
        mine.wait()
        wait_small()

    buf = pltpu.VMEM((halves, rows, cols), g.dtype)
    return pl.pallas_call(
        body, name="scatter_grads",
        out_shape=[jax.ShapeDtypeStruct((halves, rows, cols), g.dtype),
                   jax.ShapeDtypeStruct((N_DEV,) + small.shape, small.dtype)],
        in_specs=[ANY_SPACE, VMEM_WHOLE], out_specs=[ANY_SPACE, VMEM_WHOLE],
        scratch_shapes=[buf, buf, buf, pltpu.SemaphoreType.DMA((halves,)), pltpu.SemaphoreType.DMA((halves,)),
                        pltpu.SemaphoreType.DMA((halves - 1,)), pltpu.SemaphoreType.DMA((halves - 1,)),
                        pltpu.SemaphoreType.DMA((halves,)), pltpu.SemaphoreType.DMA] + SMALL_GATHER_SCRATCH,
        compiler_params=_params(vmem=VMEM_MID),
    )(g, small)


def _ada_part(c_all, w_ada, b_shard):
    def body(c_ref, w_ref, b_ref, o_ref):
        cv = c_ref[...]
        sc = (cv * _sigmoid(cv)).astype(BF16)
        o_ref[...] = _dot(sc, w_ref[...].astype(BF16)) + b_ref[...]

    return pl.pallas_call(
        body, name="ada_part", out_shape=jax.ShapeDtypeStruct((N_DEV, ADA_SHARD), F32),
        in_specs=[VMEM_WHOLE] * 3, out_specs=VMEM_WHOLE,
        compiler_params=_params(vmem=VMEM_MID),
    )(c_all, w_ada, b_shard)


def _in_proj(x, mod, wqkv, wag, wft, tm):
    s = x.shape[0]

    def body(x_ref, mod_ref, wqkv_ref, wag_ref, wft_ref, qkv_ref, ag_ref, ft_ref, u_ref):
        u = x_ref[...] * (1.0 + mod_ref[1:2, :]) + mod_ref[0:1, :]
        ub = u.astype(BF16)
        u_ref[...] = ub
        qkv_ref[...] = _dot(ub, wqkv_ref[...]).astype(BF16)
        ag_ref[...] = _dot(ub, wag_ref[...])
        ft_ref[...] = _dot_nt(wft_ref[...], ub)

    return pl.pallas_call(
        body, name="in_proj", grid=(s // tm,),
        in_specs=[pl.BlockSpec((tm, D_MODEL), lambda i: (i, 0)), pl.BlockSpec((8, D_MODEL), lambda i: (0, 0)),
                  VMEM_WHOLE, VMEM_WHOLE, VMEM_WHOLE],
        out_specs=[pl.BlockSpec((tm, 3 * ATTN_W), lambda i: (i, 0)),
                   pl.BlockSpec((tm, 2 * CONV_W), lambda i: (i, 0)),
                   pl.BlockSpec((8, tm), lambda i: (0, i)),
                   pl.BlockSpec((tm, D_MODEL), lambda i: (i, 0))],
        out_shape=[jax.ShapeDtypeStruct((s, 3 * ATTN_W), BF16), jax.ShapeDtypeStruct((s, 2 * CONV_W), F32),
                   jax.ShapeDtypeStruct((8, s), F32), jax.ShapeDtypeStruct((s, D_MODEL), BF16)],
        compiler_params=_params(("parallel",), VMEM_MID),
    )(x, mod, wqkv, wag, wft)


def _fgate(ft, bcol):
    s = ft.shape[1]

    def body(f_ref, b_ref, cum_ref, sneg_ref):
        z = f_ref[...] + b_ref[...]
        e = jnp.exp(-jnp.abs(z))
        l1p = jnp.where(e < 1e-2, e * (1.0 - e * (0.5 - e * (1.0 / 3.0))), jnp.log(1.0 + e))
        logf = jnp.minimum(z, 0.0) - l1p
        r = 1.0 / (1.0 + e)
        sneg_ref[...] = jnp.where(z >= 0, e * r, r)
        lane = lax.broadcasted_iota(jnp.int32, (8, s), 1)
        acc = logf
        sh = 1
        while sh < s:
            acc = acc + jnp.where(lane >= sh, pltpu.roll(acc, sh, axis=1), 0.0)
            sh *= 2
        cum_ref[...] = acc

    return pl.pallas_call(
        body, name="fgate", out_shape=[jax.ShapeDtypeStruct((8, s), F32)] * 2,
        in_specs=[VMEM_WHOLE] * 2, out_specs=[VMEM_WHOLE] * 2,
    )(ft, bcol)


def _attn_fwd(qkv, cum_t, tq, tk, shards):
    s = qkv.shape[0]
    nq = s // tq
    r = tq // tk
    n = len(shards)

    def body(q_ref, k_ref, v_ref, cum_ref, *rest):
        o_ref, lse_ref = rest[n:n + 2]
        hp = pl.program_id(0)
        qi = pl.program_id(1)
        start, wait = _exchange(rest[:n], rest[n + 2:2 * n + 2], *rest[2 * n + 2:], gather=True)
        pl.when(jnp.logical_and(hp == 0, qi == 0))(start)
        q2 = q_ref[...]
        lane = lax.broadcasted_iota(jnp.int32, (1, 128), 1)
        hmask = (lane < 64, lane >= 64)
        aux = (64, 0)
        qm = [jnp.where(hmask[h], q2, jnp.zeros_like(q2)) * 0.125 for h in range(2)]

        def step(kb, carry, r0=None):
            lo = 0 if r0 is None else r0
            kw = tq if r0 is None else tk
            k0 = pl.multiple_of(kb * kw, kw)
            k2 = k_ref[pl.ds(k0, kw), :]
            v2 = v_ref[pl.ds(k0, kw), :]
            out = []
            for h in range(2):
                m_all, acc_all = carry[2 * h], carry[2 * h + 1]
                m, acc = m_all[lo:], acc_all[lo:]
                sc = _dot_nt(qm[h][lo:], k2) - cum_ref[pl.ds(2 * hp + h, 1), pl.ds(k0, kw)]
                if r0 is not None:
                    visible = (lax.broadcasted_iota(jnp.int32, (tq - lo, tk), 1)
                               <= lax.broadcasted_iota(jnp.int32, (tq - lo, tk), 0))
                    sc = jnp.where(visible, sc, -jnp.inf)
                m_new = jnp.maximum(m, jnp.max(sc, axis=1, keepdims=True))
                p = jnp.exp(sc - m_new)
                corr = jnp.exp(m - m_new)
                va = jnp.where(hmask[h], v2, jnp.where(lane == aux[h], 1.0, 0.0).astype(v2.dtype))
                acc_new = acc * corr + _dot(p.astype(BF16), va)
                if lo:
                    m_new = jnp.concatenate([m_all[:lo], m_new], axis=0)
                    acc_new = jnp.concatenate([acc_all[:lo], acc_new], axis=0)
                out += [m_new, acc_new]
            return tuple(out)

        init = (jnp.full((tq, 1), -jnp.inf, F32), jnp.zeros((tq, 128), F32)) * 2
        carry = lax.fori_loop(0, qi, step, init)
        for d in range(r):
            carry = step(qi * r + d, carry, r0=d * tk)
        m0, acc0, m1, acc1 = carry
        l0 = acc0[:, aux[0]:aux[0] + 1]
        l1 = acc1[:, aux[1]:aux[1] + 1]
        o_ref[...] = jnp.where(hmask[0], acc0 / l0, acc1 / l1)
        lse_ref[:, 0:1] = m0 + jnp.log(l0)
        lse_ref[:, 1:2] = m1 + jnp.log(l1)
        pl.when(jnp.logical_and(hp == 3, qi == nq - 1))(wait)

    return pl.pallas_call(
        body, name="attn_fwd", grid=(4, nq),
        in_specs=[pl.BlockSpec((tq, 128), lambda h, i: (i, h)),
                  pl.BlockSpec((s, 128), lambda h, i: (0, 4 + h)),
                  pl.BlockSpec((s, 128), lambda h, i: (0, 8 + h)),
                  pl.BlockSpec((8, s), lambda h, i: (0, 0))] + [ANY_SPACE] * n,
        out_specs=[pl.BlockSpec((tq, 128), lambda h, i: (i, h)),
                   pl.BlockSpec((None, tq, 2), lambda h, i: (h, i, 0))] + [ANY_SPACE] * n,
        out_shape=[jax.ShapeDtypeStruct((s, ATTN_W), F32), jax.ShapeDtypeStruct((4, s, 2), F32)]
        + [jax.ShapeDtypeStruct((N_DEV,) + w.shape, w.dtype) for w in shards],
        scratch_shapes=_exchange_scratch(n),
        compiler_params=_params(("arbitrary", "arbitrary"), VMEM_BIG),
    )(qkv, qkv, qkv, cum_t, *shards)


def _gn_fwd(y, vec_ref, g_ref):
    mean = _dot_split(y, g_ref)
    yc = y - mean
    var = _dot_split(yc * yc, g_ref)
    rstd = lax.rsqrt(var + LN_EPS)
    yh = yc * rstd
    yn = yh * vec_ref[1:2, :] + vec_ref[2:3, :]
    return yh, rstd, yn


SHIFT_ROWS = HALO - 8
CONV_CHUNK = 32


def _shifted_copies(buf, sh, tm):
    for b in range(1, 8):
        sh[b - 1] = buf[pl.ds(b, tm + SHIFT_ROWS), :]


def _window(buf, sh, off, rows, r0=0):
    a, b = divmod(off, 8)
    if b == 0:
        return buf[pl.ds(8 * a + r0, rows), :]
    return sh[b - 1, pl.ds(8 * a + r0, rows), :]


def _conv_fwd(ag, wdw, vecs, gmat, tm):
    s = ag.shape[0]
    hb = tm // HALO

    def body(cur_ref, halo_ref, w_ref, vec_ref, g_ref, ypre_ref, conv_ref, buf, sh):
        i = pl.program_id(0)
        ug = cur_ref[:, 0:CONV_W] * _sigmoid(cur_ref[:, CONV_W:2 * CONV_W])
        ugh = halo_ref[:, 0:CONV_W] * _sigmoid(halo_ref[:, CONV_W:2 * CONV_W])
        buf[0:HALO, :] = jnp.where(i > 0, ugh, 0.0)
        buf[HALO:HALO + tm, :] = ug
        _shifted_copies(buf, sh, tm)
        y = jnp.broadcast_to(vec_ref[0:1, :], (tm, CONV_W))
        for j in range(CONV_K):
            y = y + w_ref[j:j + 1, :] * _window(buf, sh, HALO - (CONV_K - 1) + j, tm)
        ypre_ref[...] = y
        _, _, yn = _gn_fwd(y, vec_ref, g_ref)
        conv_ref[...] = yn * _sigmoid(yn)

    return pl.pallas_call(
        body, name="conv_fwd", grid=(s // tm,),
        in_specs=[pl.BlockSpec((tm, 2 * CONV_W), lambda i: (i, 0)),
                  pl.BlockSpec((HALO, 2 * CONV_W), lambda i: (jnp.maximum(i * hb - 1, 0), 0)),
                  pl.BlockSpec((32, CONV_W), lambda i: (0, 0)),
                  pl.BlockSpec((8, CONV_W), lambda i: (0, 0)),
                  pl.BlockSpec((CONV_W, CONV_W), lambda i: (0, 0))],
        out_specs=[pl.BlockSpec((tm, CONV_W), lambda i: (i, 0))] * 2,
        out_shape=[jax.ShapeDtypeStruct((s, CONV_W), F32)] * 2,
        scratch_shapes=[pltpu.VMEM((HALO + tm, CONV_W), F32), pltpu.VMEM((7, tm + SHIFT_ROWS, CONV_W), F32)],
        compiler_params=_params(("parallel",), VMEM_MID),
    )(ag, ag, wdw, vecs, gmat)


def _mix_fwd(o, conv, x, w_out, gvec, mod, lnv, tm):
    s = x.shape[0]

    def body(o_ref, c_ref, x_ref, w_ref, g_ref, mod_ref, ln_ref, cat_ref, mixed_ref, z1_ref, x1_ref):
        ov = o_ref[...]
        cv = c_ref[...]
        ra = ov * lax.rsqrt(jnp.mean(ov * ov, axis=1, keepdims=True) + LN_EPS) * g_ref[0:1, :]
        rc = cv * lax.rsqrt(jnp.mean(cv * cv, axis=1, keepdims=True) + LN_EPS) * g_ref[1:2, :]
        rab = ra.astype(BF16)
        rcb = rc.astype(BF16)
        cat_ref[:, 0:ATTN_W] = rab
        cat_ref[:, ATTN_W:D_MODEL] = rcb
        mixed = _dot(rab, w_ref[0:ATTN_W, :]) + _dot(rcb, w_ref[ATTN_W:D_MODEL, :])
        mixed_ref[...] = mixed
        z1 = ALPHA * x_ref[...] + (1.0 + mod_ref[2:3, :]) * mixed
        z1_ref[...] = z1
        xh, _ = _ln_norm(z1)
        x1_ref[...] = xh * ln_ref[0:1, :] + ln_ref[1:2, :]

    row = lambda w: pl.BlockSpec((tm, w), lambda i: (i, 0))
    return pl.pallas_call(
        body, name="mix_fwd", grid=(s // tm,),
        in_specs=[row(ATTN_W), row(CONV_W), row(D_MODEL), VMEM_WHOLE,
                  pl.BlockSpec((8, ATTN_W), lambda i: (0, 0)), pl.BlockSpec((8, D_MODEL), lambda i: (0, 0)),
                  pl.BlockSpec((8, D_MODEL), lambda i: (0, 0))],
        out_specs=[row(D_MODEL)] * 4,
        out_shape=[jax.ShapeDtypeStruct((s, D_MODEL), BF16)] + [jax.ShapeDtypeStruct((s, D_MODEL), F32)] * 3,
        compiler_params=_params(("parallel",), VMEM_MID),
    )(o, conv, x, w_out, gvec, mod, lnv)


def _ff_fwd(x1, tgt, w1, w2, mod, lnv, tm):
    s = x1.shape[0]

    def body(x1_ref, t_ref, w1_ref, w2_ref, mod_ref, ln_ref, u2_ref, hid_ref, dff_ref, dxa_ref, st_ref):
        i = pl.program_id(0)
        x1v = x1_ref[...]
        u2 = (x1v * (1.0 + mod_ref[4:5, :]) + mod_ref[3:4, :]).astype(BF16)
        u2_ref[...] = u2
        ff = jnp.zeros((tm, D_MODEL), F32)
        for d in range(N_DEV):
            h = _dot(u2, w1_ref[d])
            hr = jnp.maximum(h, 0.0)
            hb = (hr * hr).astype(BF16)
            hid_ref[:, d * FF_SHARD:(d + 1) * FF_SHARD] = hb
            ff = ff + _dot(hb, w2_ref[d * FF_SHARD:(d + 1) * FF_SHARD, :])
        gate = 1.0 + mod_ref[5:6, :]
        z2 = ALPHA * x1v + gate * ff
        xh, rstd = _ln_norm(z2)
        y = xh * ln_ref[2:3, :] + ln_ref[3:4, :]
        diff = y - t_ref[...]
        dy = diff * (1.0 / D_MODEL)
        dz2 = _ln_bwd(dy * ln_ref[2:3, :], xh, rstd)
        dff_ref[...] = (gate * dz2).astype(BF16)
        dxa_ref[...] = ALPHA * dz2

        @pl.when(i == 0)
        def _():
            st_ref[...] = jnp.zeros_like(st_ref)

        _acc_rows(st_ref, 0, _rowsum(dy * xh))
        _acc_rows(st_ref, 1, _rowsum(dy))
        _acc_rows(st_ref, 2, _rowsum(dz2 * ff))
        sq = _rowsum(jnp.sum(diff * diff, axis=1, keepdims=True))
        _acc_rows(st_ref, 3, jnp.broadcast_to(sq * (0.5 / D_MODEL), (1, D_MODEL)))

    row = lambda w: pl.BlockSpec((tm, w), lambda i: (i, 0))
    vec = pl.BlockSpec((8, D_MODEL), lambda i: (0, 0))
    return pl.pallas_call(
        body, name="ff_fwd", grid=(s // tm,),
        in_specs=[row(D_MODEL), row(D_MODEL), VMEM_WHOLE, VMEM_WHOLE, vec, vec],
        out_specs=[row(D_MODEL), row(D_FF), row(D_MODEL), row(D_MODEL), vec],
        out_shape=[jax.ShapeDtypeStruct((s, D_MODEL), BF16), jax.ShapeDtypeStruct((s, D_FF), BF16),
                   jax.ShapeDtypeStruct((s, D_MODEL), BF16), jax.ShapeDtypeStruct((s, D_MODEL), F32),
                   jax.ShapeDtypeStruct((8, D_MODEL), F32)],
        compiler_params=_params(("arbitrary",), VMEM_BIG),
    )(x1, tgt, w1, w2, mod, lnv)


def _ff_bwd(dff, hid, w1, w2, dxa, x1, z1, mixed, mod, lnv, tm):
    s = x1.shape[0]

    def body(dff_ref, hid_ref, w1_ref, w2_ref, dxa_ref, x1_ref, z1_ref, mixed_ref, mod_ref, ln_ref,
             dh_ref, dxb_ref, dmix_ref, st_ref):
        i = pl.program_id(0)
        dffv = dff_ref[...]
        du2 = jnp.zeros((tm, D_MODEL), F32)
        for d in range(N_DEV):
            cols = slice(d * FF_SHARD, (d + 1) * FF_SHARD)
            dhid = _dot_nt(dffv, w2_ref[cols, :])
            dh = (dhid * (2.0 * jnp.sqrt(hid_ref[:, cols].astype(F32)))).astype(BF16)
            dh_ref[:, cols] = dh
            du2 = du2 + _dot_nt(dh, w1_ref[d])
        x1v = x1_ref[...]
        dx1 = dxa_ref[...] + du2 * (1.0 + mod_ref[4:5, :])
        xh, rstd = _ln_norm(z1_ref[...])
        dz1 = _ln_bwd(dx1 * ln_ref[0:1, :], xh, rstd)
        dxb_ref[...] = ALPHA * dz1
        dmix_ref[...] = ((1.0 + mod_ref[2:3, :]) * dz1).astype(BF16)

        @pl.when(i == 0)
        def _():
            st_ref[...] = jnp.zeros_like(st_ref)

        _acc_rows(st_ref, 0, _rowsum(du2 * x1v))
        _acc_rows(st_ref, 1, _rowsum(du2))
        _acc_rows(st_ref, 2, _rowsum(dx1 * xh))
        _acc_rows(st_ref, 3, _rowsum(dx1))
        _acc_rows(st_ref, 4, _rowsum(dz1 * mixed_ref[...]))

    row = lambda w: pl.BlockSpec((tm, w), lambda i: (i, 0))
    vec = pl.BlockSpec((8, D_MODEL), lambda i: (0, 0))
    return pl.pallas_call(
        body, name="ff_bwd", grid=(s // tm,),
        in_specs=[row(D_MODEL), row(D_FF), VMEM_WHOLE, VMEM_WHOLE, row(D_MODEL), row(D_MODEL), row(D_MODEL),
                  row(D_MODEL), vec, vec],
        out_specs=[row(D_FF), row(D_MODEL), row(D_MODEL), vec],
        out_shape=[jax.ShapeDtypeStruct((s, D_FF), BF16), jax.ShapeDtypeStruct((s, D_MODEL), F32),
                   jax.ShapeDtypeStruct((s, D_MODEL), BF16), jax.ShapeDtypeStruct((8, D_MODEL), F32)],
        compiler_params=_params(("arbitrary",), VMEM_BIG),
    )(dff, hid, w1, w2, dxa, x1, z1, mixed, mod, lnv)


def _mix_bwd(dmix, w_out, o, conv, gvec, tm):
    s = o.shape[0]

    def body(dm_ref, w_ref, o_ref, c_ref, g_ref, do_ref, dc_ref, st_ref):
        i = pl.program_id(0)
        dmv = dm_ref[...]

        @pl.when(i == 0)
        def _():
            st_ref[...] = jnp.zeros_like(st_ref)

        for part, (src, dst) in enumerate(((o_ref, do_ref), (c_ref, dc_ref))):
            dr = _dot_nt(dmv, w_ref[part * ATTN_W:(part + 1) * ATTN_W, :])
            v = src[...]
            rr = lax.rsqrt(jnp.mean(v * v, axis=1, keepdims=True) + LN_EPS)
            vh = v * rr
            _acc_rows(st_ref, part, _rowsum(dr * vh))
            t = dr * g_ref[part:part + 1, :]
            dst[...] = rr * (t - vh * jnp.mean(t * vh, axis=1, keepdims=True))

    row = lambda w: pl.BlockSpec((tm, w), lambda i: (i, 0))
    vec = pl.BlockSpec((8, ATTN_W), lambda i: (0, 0))
    return pl.pallas_call(
        body, name="mix_bwd", grid=(s // tm,),
        in_specs=[row(D_MODEL), VMEM_WHOLE, row(ATTN_W), row(CONV_W), vec],
        out_specs=[row(ATTN_W), row(CONV_W), vec],
        out_shape=[jax.ShapeDtypeStruct((s, ATTN_W), F32), jax.ShapeDtypeStruct((s, CONV_W), F32),
                   jax.ShapeDtypeStruct((8, ATTN_W), F32)],
        compiler_params=_params(("arbitrary",), VMEM_MID),
    )(dmix, w_out, o, conv, gvec)


def _conv_bwd(dconv, ypre, ag, wdw, vecs, gmat, tm):
    s = ag.shape[0]
    hb = tm // HALO
    nt = s // tm
    last_halo = s // HALO - 1

    def body(dc_ref, dch_ref, yp_ref, yph_ref, cur_ref, w_ref, vec_ref, g_ref,
             dag_ref, st_ref, dw_ref, bufd, shd, dwacc):
        i = pl.program_id(0)

        def dyc_of(yp, dc):
            yh, rstd, yn = _gn_fwd(yp, vec_ref, g_ref)
            sg = _sigmoid(yn)
            dyn = dc * (sg * (1.0 + yn * (1.0 - sg)))
            dyh = dyn * vec_ref[1:2, :]
            dyc = rstd * (dyh - _dot_split(dyh, g_ref) - yh * _dot_split(dyh * yh, g_ref))
            return dyc, dyn, yh

        dyc, dyn, yh = dyc_of(yp_ref[...], dc_ref[...])
        dych, _, _ = dyc_of(yph_ref[...], dch_ref[...])
        bufd[0:tm, :] = dyc
        bufd[tm:tm + HALO, :] = jnp.where(i < nt - 1, dych, 0.0)

        @pl.when(i == 0)
        def _():
            st_ref[...] = jnp.zeros_like(st_ref)
            dw_ref[...] = jnp.zeros_like(dw_ref)

        _acc_rows(st_ref, 0, _rowsum(dyc))
        _acc_rows(st_ref, 1, _rowsum(dyn * yh))
        _acc_rows(st_ref, 2, _rowsum(dyn))

        _shifted_copies(bufd, shd, tm)
        dwacc[...] = jnp.zeros_like(dwacc)

        def chunk(c, carry):
            r0 = pl.multiple_of(c * CONV_CHUNK, CONV_CHUNK)
            a_c = cur_ref[pl.ds(r0, CONV_CHUNK), 0:CONV_W]
            sg_c = _sigmoid(cur_ref[pl.ds(r0, CONV_CHUNK), CONV_W:2 * CONV_W])
            ug_c = a_c * sg_c
            dug = jnp.zeros((CONV_CHUNK, CONV_W), F32)
            for j in range(CONV_K):
                win = _window(bufd, shd, CONV_K - 1 - j, CONV_CHUNK, r0)
                dug = dug + w_ref[j:j + 1, :] * win
                prod = ug_c * win
                part = prod[0:8]
                for g8 in range(1, CONV_CHUNK // 8):
                    part = part + prod[8 * g8:8 * g8 + 8]
                dwacc[j] = dwacc[j] + part
            dag_ref[pl.ds(r0, CONV_CHUNK), 0:CONV_W] = (dug * sg_c).astype(BF16)
            dag_ref[pl.ds(r0, CONV_CHUNK), CONV_W:2 * CONV_W] = (dug * a_c * sg_c * (1.0 - sg_c)).astype(BF16)
            return carry

        lax.fori_loop(0, tm // CONV_CHUNK, chunk, 0)
        for j in range(CONV_K):
            _acc_rows(dw_ref, j, _rowsum(dwacc[j]))

    row = lambda w: pl.BlockSpec((tm, w), lambda i: (i, 0))
    nxt = lambda w: pl.BlockSpec((HALO, w), lambda i: (jnp.minimum((i + 1) * hb, last_halo), 0))
    return pl.pallas_call(
        body, name="conv_bwd", grid=(nt,),
        in_specs=[row(CONV_W), nxt(CONV_W), row(CONV_W), nxt(CONV_W), row(2 * CONV_W),
                  pl.BlockSpec((32, CONV_W), lambda i: (0, 0)),
                  pl.BlockSpec((8, CONV_W), lambda i: (0, 0)),
                  pl.BlockSpec((CONV_W, CONV_W), lambda i: (0, 0))],
        out_specs=[row(2 * CONV_W), pl.BlockSpec((8, CONV_W), lambda i: (0, 0)),
                   pl.BlockSpec((32, CONV_W), lambda i: (0, 0))],
        out_shape=[jax.ShapeDtypeStruct((s, 2 * CONV_W), BF16), jax.ShapeDtypeStruct((8, CONV_W), F32),
                   jax.ShapeDtypeStruct((32, CONV_W), F32)],
        scratch_shapes=[pltpu.VMEM((tm + HALO, CONV_W), F32), pltpu.VMEM((7, tm + SHIFT_ROWS, CONV_W), F32),
                        pltpu.VMEM((32, 8, CONV_W), F32)],
        compiler_params=_params(("arbitrary",), VMEM_MID),
    )(dconv, dconv, ypre, ypre, ag, wdw, vecs, gmat)


def _attn_bwd(qkv, cum_t, o, do, lse, tq, tk, grads):
    s = qkv.shape[0]
    nq = s // tq
    r = tq // tk
    n = len(grads)

    def body(q_ref, k_ref, v_ref, cum_ref, o_ref, do_ref, lse_ref, *rest):
        dq_ref, dkt_ref, dvt_ref, dcum_ref, drow_ref = rest[n:n + 5]
        hp = pl.program_id(0)
        qi = pl.program_id(1)
        start, wait = _exchange(rest[:n], rest[n + 5:2 * n + 5], *rest[2 * n + 5:], gather=False)
        pl.when(jnp.logical_and(hp == 0, qi == 0))(start)

        @pl.when(qi == 0)
        def _():
            dkt_ref[...] = jnp.zeros_like(dkt_ref)
            dvt_ref[...] = jnp.zeros_like(dvt_ref)
            dcum_ref[...] = jnp.zeros_like(dcum_ref)

        qs = q_ref[...] * 0.125
        ov = o_ref[...]
        dob = do_ref[...].astype(BF16)
        lane = lax.broadcasted_iota(jnp.int32, (1, 128), 1)
        subl = lax.broadcasted_iota(jnp.int32, (128, 1), 0)
        hmask = (lane < 64, lane >= 64)
        aux = (64, 0)
        ones_aux = [jnp.where(lane == aux[h], 1.0, 0.0) for h in range(2)]
        eye = jnp.where(subl == lane, 1.0, 0.0).astype(BF16)
        qm, dom, delta, lse_h, qat, dot_t = [], [], [], [], [], []
        for h in range(2):
            qm.append(jnp.where(hmask[h], qs, jnp.zeros_like(qs)))
            dom.append(jnp.where(hmask[h], dob, jnp.zeros_like(dob)))
            delta.append(jnp.sum(dom[h].astype(F32) * ov, axis=1, keepdims=True))
            lse_h.append(lse_ref[:, h:h + 1])
            qa = jnp.where(hmask[h], qs, ones_aux[h].astype(qs.dtype))
            qat.append(_dot_nt(eye, qa).astype(BF16))
            dot_t.append(_dot_nt(eye, dom[h]).astype(BF16))

        def step(kb, carry, r0=None):
            lo = 0 if r0 is None else r0
            k0 = pl.multiple_of(kb * tk, tk)
            k2 = k_ref[pl.ds(k0, tk), :]
            v2 = v_ref[pl.ds(k0, tk), :]
            dq_new, a, b = [], [], []
            for h in range(2):
                cs = cum_ref[pl.ds(2 * hp + h, 1), pl.ds(k0, tk)]
                sc = _dot_nt(qm[h][lo:], k2) - cs
                p = jnp.exp(sc - lse_h[h][lo:])
                if r0 is not None:
                    visible = (lax.broadcasted_iota(jnp.int32, (tq - lo, tk), 1)
                               <= lax.broadcasted_iota(jnp.int32, (tq - lo, tk), 0))
                    p = jnp.where(visible, p, 0.0)
                dp = _dot_nt(dom[h][lo:], v2)
                ds = p * (dp - delta[h][lo:])
                pb = p.astype(BF16)
                dsb = ds.astype(BF16)
                a.append(_dot(qat[h][:, lo:], dsb))
                b.append(_dot(dot_t[h][:, lo:], pb))
                ka = jnp.where(hmask[h], k2, ones_aux[h].astype(k2.dtype))
                dq_h = carry[h][lo:] + _dot(dsb, ka)
                dq_new.append(jnp.concatenate([carry[h][:lo], dq_h], axis=0) if lo else dq_h)
            dkt_ref[:, pl.ds(k0, tk)] = dkt_ref[:, pl.ds(k0, tk)] + jnp.where(subl < 64, a[0], a[1])
            dvt_ref[:, pl.ds(k0, tk)] = dvt_ref[:, pl.ds(k0, tk)] + (b[0] + b[1])
            for h in range(2):
                dcum_ref[pl.ds(h, 1), pl.ds(k0, tk)] = (dcum_ref[pl.ds(h, 1), pl.ds(k0, tk)]
                                                        - a[h][aux[h]:aux[h] + 1, :])
            return tuple(dq_new)

        init = (jnp.zeros((tq, 128), F32),) * 2
        carry = lax.fori_loop(0, qi * r, step, init)
        for d in range(r):
            carry = step(qi * r + d, carry, r0=d * tk)
        dq0, dq1 = carry
        dq_ref[...] = (jnp.where(hmask[0], dq0, dq1) * 0.125).astype(BF16)
        drow_ref[:, 0:1] = dq0[:, aux[0]:aux[0] + 1]
        drow_ref[:, 1:2] = dq1[:, aux[1]:aux[1] + 1]
        pl.when(jnp.logical_and(hp == 3, qi == nq - 1))(wait)

    return pl.pallas_call(
        body, name="attn_bwd", grid=(4, nq),
        in_specs=[pl.BlockSpec((tq, 128), lambda h, i: (i, h)),
                  pl.BlockSpec((s, 128), lambda h, i: (0, 4 + h)),
                  pl.BlockSpec((s, 128), lambda h, i: (0, 8 + h)),
                  pl.BlockSpec((8, s), lambda h, i: (0, 0)),
                  pl.BlockSpec((tq, 128), lambda h, i: (i, h)),
                  pl.BlockSpec((tq, 128), lambda h, i: (i, h)),
                  pl.BlockSpec((None, tq, 2), lambda h, i: (h, i, 0))] + [ANY_SPACE] * n,
        out_specs=[pl.BlockSpec((tq, 128), lambda h, i: (i, h)),
                   pl.BlockSpec((128, s), lambda h, i: (h, 0)),
                   pl.BlockSpec((128, s), lambda h, i: (h, 0)),
                   pl.BlockSpec((None, 8, s), lambda h, i: (h, 0, 0)),
                   pl.BlockSpec((None, tq, 2), lambda h, i: (h, i, 0))] + [ANY_SPACE] * n,
        out_shape=[jax.ShapeDtypeStruct((s, ATTN_W), BF16), jax.ShapeDtypeStruct((ATTN_W, s), F32),
                   jax.ShapeDtypeStruct((ATTN_W, s), F32), jax.ShapeDtypeStruct((4, 8, s), F32),
                   jax.ShapeDtypeStruct((4, s, 2), F32)]
        + [jax.ShapeDtypeStruct(g.shape, g.dtype) for g in grads],
        scratch_shapes=_exchange_scratch(n),
        compiler_params=_params(("arbitrary", "arbitrary"), VMEM_BIG),
    )(qkv, qkv, qkv, cum_t, o, do, lse, *grads)


def _fgate_bwd(dcum_t, drow_t, sneg_t):
    s = dcum_t.shape[1]

    def body(dc_ref, dr_ref, sn_ref, df_ref, db_ref):
        lane = lax.broadcasted_iota(jnp.int32, (8, s), 1)
        acc = dc_ref[...] + dr_ref[...]
        sh = 1
        while sh < s:
            acc = acc + jnp.where(lane < s - sh, pltpu.roll(acc, s - sh, axis=1), 0.0)
            sh *= 2
        df = acc * sn_ref[...]
        df_ref[...] = df
        db_ref[...] = jnp.broadcast_to(jnp.sum(df, axis=1, keepdims=True), (8, 128))

    return pl.pallas_call(
        body, name="fgate_bwd",
        out_shape=[jax.ShapeDtypeStruct((8, s), F32), jax.ShapeDtypeStruct((8, 128), F32)],
        in_specs=[VMEM_WHOLE] * 3, out_specs=[VMEM_WHOLE] * 2,
    )(dcum_t, drow_t, sneg_t)


def _in_proj_bwd(dq, dkt, dvt, dag, dft, wqkv, wkvt, wag, wft, x, dxb, mod, tm):
    s = x.shape[0]

    def body(dq_ref, dkt_ref, dvt_ref, dag_ref, dft_ref, wqkv_ref, wkvt_ref, wag_ref, wft_ref, x_ref, dxb_ref,
             mod_ref, dkvt_ref, dx_ref, st_ref):
        i = pl.program_id(0)
        dkb = dkt_ref[...].astype(BF16)
        dvb = dvt_ref[...].astype(BF16)
        dkvt_ref[0:ATTN_W, :] = dkb
        dkvt_ref[ATTN_W:2 * ATTN_W, :] = dvb
        du = _dot_nt(dq_ref[...], wqkv_ref[:, 0:ATTN_W])
        du = du + _dot_tn(dkb, wkvt_ref[0:ATTN_W, :])
        du = du + _dot_tn(dvb, wkvt_ref[ATTN_W:2 * ATTN_W, :])
        du = du + _dot_nt(dag_ref[...], wag_ref[...])
        du = du + _dot_tn(dft_ref[...].astype(BF16), wft_ref[...])
        xv = x_ref[...]
        dx_ref[...] = dxb_ref[...] + du * (1.0 + mod_ref[1:2, :])

        @pl.when(i == 0)
        def _():
            st_ref[...] = jnp.zeros_like(st_ref)

        _acc_rows(st_ref, 0, _rowsum(du * xv))
        _acc_rows(st_ref, 1, _rowsum(du))

    row = lambda w: pl.BlockSpec((tm, w), lambda i: (i, 0))
    col = lambda h: pl.BlockSpec((h, tm), lambda i: (0, i))
    vec = pl.BlockSpec((8, D_MODEL), lambda i: (0, 0))
    return pl.pallas_call(
        body, name="in_proj_bwd", grid=(s // tm,),
        in_specs=[row(ATTN_W), col(ATTN_W), col(ATTN_W), row(2 * CONV_W), col(8),
                  VMEM_WHOLE, VMEM_WHOLE, VMEM_WHOLE, VMEM_WHOLE, row(D_MODEL), row(D_MODEL), vec],
        out_specs=[col(2 * ATTN_W), row(D_MODEL), vec],
        out_shape=[jax.ShapeDtypeStruct((2 * ATTN_W, s), BF16), jax.ShapeDtypeStruct((s, D_MODEL), F32),
                   jax.ShapeDtypeStruct((8, D_MODEL), F32)],
        compiler_params=_params(("arbitrary",), VMEM_MID),
    )(dq, dkt, dvt, dag, dft, wqkv, wkvt, wag, wft, x, dxb, mod)


def _wgrad_nn(a, b, name, tm, tn, tk):
    m, s = a.shape
    n = b.shape[1]
    nk = s // tk

    def body(a_ref, b_ref, o_ref, acc_ref):
        k = pl.program_id(2)

        @pl.when(k == 0)
        def _():
            acc_ref[...] = jnp.zeros_like(acc_ref)

        acc_ref[...] += _dot(a_ref[...], b_ref[...])

        @pl.when(k == nk - 1)
        def _():
            o_ref[...] = acc_ref[...].astype(o_ref.dtype)

    return pl.pallas_call(
        body, name=name, grid=(m // tm, n // tn, nk),
        in_specs=[pl.BlockSpec((tm, tk), lambda i, j, k: (i, k)), pl.BlockSpec((tk, tn), lambda i, j, k: (k, j))],
        out_specs=pl.BlockSpec((tm, tn), lambda i, j, k: (i, j)),
        out_shape=jax.ShapeDtypeStruct((m, n), BF16),
        scratch_shapes=[pltpu.VMEM((tm, tn), F32)],
        compiler_params=_params(("parallel", "parallel", "arbitrary"), VMEM_MID),
    )(a, b)


def _wgrad_tn(a, b, name, tm, tn, tk, out_shape, out_spec):
    s, m = a.shape
    n = b.shape[1]
    nk = s // tk

    def body(a_ref, b_ref, o_ref, acc_ref):
        k = pl.program_id(2)

        @pl.when(k == 0)
        def _():
            acc_ref[...] = jnp.zeros_like(acc_ref)

        acc_ref[...] += _dot_tn(a_ref[...], b_ref[...])

        @pl.when(k == nk - 1)
        def _():
            o_ref[...] = acc_ref[...].astype(o_ref.dtype)

    return pl.pallas_call(
        body, name=name, grid=(m // tm, n // tn, nk),
        in_specs=[pl.BlockSpec((tk, tm), lambda i, j, k: (k, i)), pl.BlockSpec((tk, tn), lambda i, j, k: (k, j))],
        out_specs=out_spec, out_shape=out_shape,
        scratch_shapes=[pltpu.VMEM((tm, tn), F32)],
        compiler_params=_params(("parallel", "parallel", "arbitrary"), VMEM_MID),
    )(a, b)


def _wgrad_f(dft, u, tk):
    s = u.shape[0]
    nk = s // tk

    def body(d_ref, u_ref, o_ref, acc_ref):
        k = pl.program_id(0)

        @pl.when(k == 0)
        def _():
            acc_ref[...] = jnp.zeros_like(acc_ref)

        acc_ref[...] += _dot(d_ref[...].astype(BF16), u_ref[...])

        @pl.when(k == nk - 1)
        def _():
            o_ref[...] = acc_ref[...].astype(BF16)

    return pl.pallas_call(
        body, name="wgrad_f", grid=(nk,),
        in_specs=[pl.BlockSpec((8, tk), lambda k: (0, k)), pl.BlockSpec((tk, D_MODEL), lambda k: (k, 0))],
        out_specs=pl.BlockSpec((8, D_MODEL), lambda k: (0, 0)),
        out_shape=jax.ShapeDtypeStruct((8, D_MODEL), BF16),
        scratch_shapes=[pltpu.VMEM((8, D_MODEL), F32)],
        compiler_params=_params(("arbitrary",)),
    )(dft, u)


def _adamw_big(recv, w, m, v, name, tr):
    r, cdim = w.shape
    slots = recv.shape[0]

    def body(g_ref, w_ref, m_ref, v_ref, go_ref, d_ref, mo_ref, vo_ref):
        g = g_ref[0].astype(F32)
        for k in range(1, slots):
            g = g + g_ref[k].astype(F32)
        delta, m2, v2 = _adamw(w_ref[...], g, m_ref[...], v_ref[...])
        go_ref[...] = g
        d_ref[...] = delta
        mo_ref[...] = m2
        vo_ref[...] = v2

    blk = pl.BlockSpec((tr, cdim), lambda i: (i, 0))
    return pl.pallas_call(
        body, name=name, grid=(r // tr,),
        in_specs=[pl.BlockSpec((slots, tr, cdim), lambda i: (0, i, 0)), blk, blk, blk],
        out_specs=[blk] * 4, out_shape=[jax.ShapeDtypeStruct((r, cdim), F32)] * 4,
        compiler_params=_params(("parallel",), VMEM_MID),
    )(recv, w, m, v)


def _adamw_ada(c_all, dada, w, m, v, tr):
    r, cdim = w.shape

    def body(c_ref, d_ref, w_ref, m_ref, v_ref, go_ref, dl_ref, mo_ref, vo_ref):
        cv = c_ref[...]
        sc = (cv * _sigmoid(cv)).astype(BF16)
        g = _dot_tn(sc, d_ref[...].astype(BF16))
        delta, m2, v2 = _adamw(w_ref[...], g, m_ref[...], v_ref[...])
        go_ref[...] = g
        dl_ref[...] = delta
        mo_ref[...] = m2
        vo_ref[...] = v2

    blk = pl.BlockSpec((tr, cdim), lambda i: (i, 0))
    return pl.pallas_call(
        body, name="adamw_ada", grid=(r // tr,),
        in_specs=[pl.BlockSpec((N_DEV, tr), lambda i: (0, i)), pl.BlockSpec((N_DEV, cdim), lambda i: (0, 0)),
                  blk, blk, blk],
        out_specs=[blk] * 4, out_shape=[jax.ShapeDtypeStruct((r, cdim), F32)] * 4,
        compiler_params=_params(("parallel",), VMEM_MID),
    )(c_all, dada, w, m, v)


def _adamw_small(parts, w, m, v):
    n = w.shape[1]

    def body(g_ref, w_ref, m_ref, v_ref, go_ref, d_ref, mo_ref, vo_ref):
        g = g_ref[0]
        for k in range(1, N_DEV):
            g = g + g_ref[k]
        delta, m2, v2 = _adamw(w_ref[...], g, m_ref[...], v_ref[...])
        go_ref[...] = g
        d_ref[...] = delta
        mo_ref[...] = m2
        vo_ref[...] = v2

    return pl.pallas_call(
        body, name="adamw_small", out_shape=[jax.ShapeDtypeStruct((1, n), F32)] * 4,
        in_specs=[VMEM_WHOLE] * 4, out_specs=[VMEM_WHOLE] * 4,
    )(parts, w, m, v)


def _pad_cols(a, n):
    return jnp.pad(a, ((0, 0), (0, n - a.shape[1])))


def _pad_rows(a, n):
    return jnp.pad(a, ((0, n - a.shape[0]), (0, 0)))


def kernel(x, c, w_ada, b_ada, w_in, b_forget, w_dw, b_dw, gn_g, gn_b, g_attn_out, g_conv_out, w_out, ln1_g, ln1_b, w_ff1, w_ff2, ln2_g, ln2_b, loss_target, m_w_ada, m_b_ada, m_w_in, m_b_forget, m_w_dw, m_b_dw, m_gn_g, m_gn_b, m_g_attn_out, m_g_conv_out, m_w_out, m_ln1_g, m_ln1_b, m_w_ff1, m_w_ff2, m_ln2_g, m_ln2_b, v_w_ada, v_b_ada, v_w_in, v_b_forget, v_w_dw, v_b_dw, v_gn_g, v_gn_b, v_g_attn_out, v_g_conv_out, v_w_out, v_ln1_g, v_ln1_b, v_w_ff1, v_w_ff2, v_ln2_g, v_ln2_b):
    s = x.shape[1]
    tm = min(ROW_TILE, s)
    tk_att = min(ATT_TILE, s)
    tq_att = min(ATT_Q_BLOCKS * tk_att, s)
    me = 4 * lax.axis_index("x") + 2 * lax.axis_index("y") + lax.axis_index("c")
    xs = x[0]
    tgt = loss_target[0]

    dw_n = CONV_K * DW_SHARD
    pack1 = jnp.concatenate([c, w_dw[0].reshape(1, dw_n), jnp.zeros((1, 2048 - dw_n), F32)], axis=1)
    gw_in, g1 = _gather_weights([_pad_cols(w_in[0], IN_SHARD_PAD).astype(BF16)], pack1)
    g1 = g1[:, 0, :]
    c_all = g1[:, :D_MODEL]
    wdw_full = g1[:, D_MODEL:D_MODEL + dw_n].reshape(N_DEV, CONV_K, DW_SHARD)
    wdw_full = _pad_rows(wdw_full.transpose(1, 0, 2).reshape(CONV_K, CONV_W), 32)

    b_shard = lax.dynamic_slice(b_ada, (0, me * ADA_SHARD), (1, ADA_SHARD))
    ada_all = _gather_small(_ada_part(c_all, w_ada[0], b_shard), "gather_ada")
    ada = lax.dynamic_index_in_dim(ada_all, me, axis=1, keepdims=False).reshape(6, D_MODEL)
    mod = _pad_rows(ada, 8)

    w_in_full = gw_in[:, :, :IN_SHARD].transpose(1, 0, 2).reshape(D_MODEL, N_IN)
    wqkv = w_in_full[:, :3 * ATTN_W]
    wft = w_in_full[:, 3 * ATTN_W:3 * ATTN_W + N_HEADS].T
    wag = w_in_full[:, 3 * ATTN_W + N_HEADS:]

    qkv, ag, ft, u = _in_proj(xs, mod, wqkv, wag, wft, tm)
    cum_t, sneg_t = _fgate(ft, b_forget.reshape(N_HEADS, 1))
    o, lse, gw_out, gw_ff1, gw_ff2 = _attn_fwd(
        qkv, cum_t, tq_att, tk_att, [w_out[0].astype(BF16), w_ff1[0].astype(BF16), w_ff2[0].astype(BF16)])
    w_out_full = gw_out.reshape(D_MODEL, D_MODEL)
    w2_full = gw_ff2.reshape(D_FF, D_MODEL)
    cvec = _pad_rows(jnp.concatenate([b_dw, gn_g, gn_b], axis=0), 8)
    grp = jnp.arange(CONV_W) // 64
    gmat = jnp.where(grp[:, None] == grp[None, :], 1.0 / 64.0, 0.0).astype(BF16)
    ypre, conv = _conv_fwd(ag, wdw_full, cvec, gmat, tm)
    gvec = _pad_rows(jnp.concatenate([g_attn_out, g_conv_out], axis=0), 8)
    lnv = _pad_rows(jnp.concatenate([ln1_g, ln1_b, ln2_g, ln2_b], axis=0), 8)
    cat, mixed, z1, x1 = _mix_fwd(o, conv, xs, w_out_full, gvec, mod, lnv, tm)
    tf = min(FF_TILE, s)
    u2, hid, dff, dxa, st_ff = _ff_fwd(x1, tgt, gw_ff1, w2_full, mod, lnv, min(2 * FF_TILE, s))

    dh, dxb, dmix, st_fb = _ff_bwd(dff, hid, gw_ff1, w2_full, dxa, x1, z1, mixed, mod, lnv, tf)
    do, dconv, st_mix = _mix_bwd(dmix, w_out_full, o, conv, gvec, tm)
    dag, st_conv, dwdw = _conv_bwd(dconv, ypre, ag, wdw_full, cvec, gmat, tm)

    tk = min(WGRAD_K_TILE, s)
    g_ff2 = _wgrad_tn(hid, dff, "wgrad_ff2", 512, 1024, tk,
                      jax.ShapeDtypeStruct((D_FF, D_MODEL), BF16),
                      pl.BlockSpec((512, 1024), lambda i, j, k: (i, j))).reshape(N_DEV, FF_SHARD, D_MODEL)
    g_ff1 = _wgrad_tn(u2, dh, "wgrad_ff1", 1024, 512, tk,
                      jax.ShapeDtypeStruct((N_DEV, D_MODEL, FF_SHARD), BF16),
                      pl.BlockSpec((None, 1024, 512), lambda i, j, k: (j, i, 0)))
    g_out = _wgrad_tn(cat, dmix, "wgrad_out", 512, 1024, tk,
                      jax.ShapeDtypeStruct((D_MODEL, D_MODEL), BF16),
                      pl.BlockSpec((512, 1024), lambda i, j, k: (i, j))).reshape(N_DEV, OUT_SHARD, D_MODEL)
    dq, dkt, dvt, dcum, drow, r_out, r_ff1, r_ff2 = _attn_bwd(
        qkv, cum_t, o, do, lse, tq_att, tk_att, [g_out, g_ff1, g_ff2])
    dft, dbf = _fgate_bwd(dcum[:, :2, :].reshape(N_HEADS, s), drow.transpose(0, 2, 1).reshape(N_HEADS, s), sneg_t)
    wkvt = wqkv[:, ATTN_W:3 * ATTN_W].T
    dkvt, grad_x, st_in = _in_proj_bwd(dq, dkt, dvt, dag, dft, wqkv, wkvt, wag, wft, xs, dxb, mod, tm)

    g_q = _wgrad_tn(u, dq, "wgrad_q", 1024, 512, tk,
                    jax.ShapeDtypeStruct((D_MODEL, ATTN_W), BF16),
                    pl.BlockSpec((1024, 512), lambda i, j, k: (i, j)))
    g_kvt = _wgrad_nn(dkvt, u, "wgrad_kv", 512, 1024, tk)
    g_qkv = jnp.concatenate([g_q, g_kvt.T], axis=1)
    g_ag = _wgrad_tn(u, dag, "wgrad_ag", 1024, 512, tk,
                     jax.ShapeDtypeStruct((D_MODEL, 2 * CONV_W), BF16),
                     pl.BlockSpec((1024, 512), lambda i, j, k: (i, j)))
    g_ft = _wgrad_f(dft, u, tk)
    g_in = jnp.concatenate([g_qkv, g_ft.T, g_ag], axis=1)
    g_in = jnp.pad(g_in.reshape(D_MODEL, N_DEV, IN_SHARD).transpose(1, 0, 2),
                   ((0, 0), (0, 0), (0, IN_SHARD_PAD - IN_SHARD)))

    dada = jnp.concatenate([st_in[1:2], st_in[0:1], st_fb[4:5], st_fb[1:2], st_fb[0:1], st_ff[2:3]], axis=1)
    loss_part = st_ff[3:4, 0:1]
    pack2 = jnp.concatenate([
        dada,
        _pad_cols(dbf[:, 0].reshape(1, N_HEADS), 128),
        dwdw[:CONV_K].reshape(1, CONV_K * CONV_W),
        st_conv[0:1], st_conv[1:2], st_conv[2:3], st_mix[0:1], st_mix[1:2],
        st_fb[2:3], st_fb[3:4], st_ff[0:1], st_ff[1:2],
        _pad_cols(loss_part, 128)], axis=1)
    r_in, g2 = _scatter_grads(g_in, pack2)
    g2 = g2[:, 0, :]
    o_bf = 6 * D_MODEL
    o_dw = o_bf + 128
    o_v = o_dw + CONV_K * CONV_W
    o_ln = o_v + 5 * CONV_W
    o_loss = o_ln + 4 * D_MODEL
    dw_parts = lax.dynamic_slice_in_dim(
        g2[:, o_dw:o_v].reshape(N_DEV, CONV_K, N_DEV, DW_SHARD), me, 1, axis=2).reshape(N_DEV, dw_n)
    parts = jnp.concatenate([
        g2[:, :o_dw], _pad_cols(dw_parts, 2048), g2[:, o_v:o_loss + 128]], axis=1)[:, None, :]

    def pack_w(b_ada_, b_forget_, w_dw_, smalls):
        return jnp.concatenate([b_ada_, _pad_cols(b_forget_, 128), _pad_cols(w_dw_[0].reshape(1, dw_n), 2048)]
                               + smalls + [jnp.zeros((1, 128), F32)], axis=1)

    pw = pack_w(b_ada, b_forget, w_dw, [b_dw, gn_g, gn_b, g_attn_out, g_conv_out, ln1_g, ln1_b, ln2_g, ln2_b])
    pm = pack_w(m_b_ada, m_b_forget, m_w_dw, [m_b_dw, m_gn_g, m_gn_b, m_g_attn_out, m_g_conv_out,
                                              m_ln1_g, m_ln1_b, m_ln2_g, m_ln2_b])
    pv = pack_w(v_b_ada, v_b_forget, v_w_dw, [v_b_dw, v_gn_g, v_gn_b, v_g_attn_out, v_g_conv_out,
                                              v_ln1_g, v_ln1_b, v_ln2_g, v_ln2_b])
    small = _adamw_small(parts, pw, pm, pv)

    p_bf = 6 * D_MODEL
    p_dw = p_bf + 128
    p_v = p_dw + 2048
    p_ln = p_v + 5 * CONV_W
    p_loss = p_ln + 4 * D_MODEL

    def unpack(t):
        outs = {"b_ada": t[:, :p_bf], "b_forget": t[:, p_bf:p_bf + N_HEADS],
                "w_dw": t[:, p_dw:p_dw + dw_n].reshape(1, CONV_K, 1, DW_SHARD)}
        for k, nm in enumerate(["b_dw", "gn_g", "gn_b", "g_attn_out", "g_conv_out"]):
            outs[nm] = t[:, p_v + k * CONV_W:p_v + (k + 1) * CONV_W]
        for k, nm in enumerate(["ln1_g", "ln1_b", "ln2_g", "ln2_b"]):
            outs[nm] = t[:, p_ln + k * D_MODEL:p_ln + (k + 1) * D_MODEL]
        return outs

    sm = [unpack(t) for t in small]
    loss = small[0][0, p_loss]

    dada_all = g2[:, :6 * D_MODEL]
    dada_shard = lax.dynamic_slice_in_dim(dada_all, me * ADA_SHARD, ADA_SHARD, axis=1)
    big = {
        "w_ada": _adamw_ada(c_all, dada_shard, w_ada[0], m_w_ada[0], v_w_ada[0], 256),
        "w_in": [t[:, :IN_SHARD] for t in _adamw_big(
            r_in, _pad_cols(w_in[0], IN_SHARD_PAD), _pad_cols(m_w_in[0], IN_SHARD_PAD),
            _pad_cols(v_w_in[0], IN_SHARD_PAD), "adamw_in", 256)],
        "w_out": _adamw_big(r_out, w_out[0], m_w_out[0], v_w_out[0], "adamw_out", 128),
        "w_ff1": _adamw_big(r_ff1, w_ff1[0], m_w_ff1[0], v_w_ff1[0], "adamw_ff1", 256),
        "w_ff2": _adamw_big(r_ff2, w_ff2[0], m_w_ff2[0], v_w_ff2[0], "adamw_ff2", 256),
    }

    names = ["w_ada", "b_ada", "w_in", "b_forget", "w_dw", "b_dw", "gn_g", "gn_b", "g_attn_out", "g_conv_out",
             "w_out", "ln1_g", "ln1_b", "w_ff1", "w_ff2", "ln2_g", "ln2_b"]

    def leaf(kind, nm):
        if nm in big:
            return big[nm][kind][None]
        return sm[kind][nm]

    outs = [loss, grad_x[None]]
    for kind in range(4):
        outs += [leaf(kind, nm) for nm in names]
    return tuple(outs)
```
